```python
import jax, jax.numpy as jnp
from jax import lax
import numpy as np

D_MODEL = 2048
BATCH = 4
SEQ = 2048
DEPTH = 2
DEC_BATCH = 128
DEC_SEQ = 8
PAST_LEN = 16384
PAGE_SIZE = 128

N_MIXERS = 2
N_LRU_LAYERS = (DEPTH + 1) // 2
N_CONV_LAYERS = DEPTH // 2
N_DENSE_LAYERS = (DEPTH + 1) // 2
N_MOE_LAYERS = DEPTH // 2

D_RNN = D_MODEL * 5 // 4
RNN_BLOCK = 256
N_RNN_BLOCKS = D_RNN // RNN_BLOCK
CONV_A_WIDTH = 4
RG_C = 8.0

D_CONV = D_MODEL
CONV_B_WIDTH = 3

D_FF = 3 * D_MODEL
N_EXPERTS = 8
TOP_K = 2
D_FF_EXPERT = 3 * D_MODEL // 2
MOE_BLOCK = 128
EPS = 1e-6

kernel_name = 'hawk_shortconv_moe_hybrid_step'


def rms_norm(x, g):
    xf = x.astype(jnp.float32)
    y = xf * lax.rsqrt(jnp.mean(xf * xf, axis=-1, keepdims=True) + EPS)
    return (y * g.astype(jnp.float32)).astype(x.dtype)


def causal_dwconv(u, buf, w):
    width = w.shape[0]
    t = u.shape[1]
    full = jnp.concatenate([buf.astype(u.dtype), u], axis=1)
    out = full[:, 0:t] * w[0]
    for k in range(1, width):
        out = out + full[:, k:k + t] * w[k]
    return out, full[:, full.shape[1] - (width - 1):]


def linear_scan(a, b, h0):
    b = b.at[:, 0].add(a[:, 0] * h0)

    def combine(left, right):
        a_l, b_l = left
        a_r, b_r = right
        return a_l * a_r, a_r * b_l + b_r

    _, h = lax.associative_scan(combine, (a, b), axis=1)
    return h


def rglru_mixer(x, h0, conv_buf, w_in, conv_w, conv_b, w_rgate, b_rgate, w_igate, b_igate, lam, w_out):
    bsz, t = x.shape[0], x.shape[1]
    proj = x @ w_in
    gate_branch, u = jnp.split(proj, 2, axis=-1)
    u, new_buf = causal_dwconv(u, conv_buf, conv_w)
    u = u + conv_b
    ub = u.reshape(bsz, t, N_RNN_BLOCKS, RNN_BLOCK)
    r_pre = jnp.einsum('bthi,hij->bthj', ub, w_rgate).reshape(bsz, t, D_RNN) + b_rgate
    i_pre = jnp.einsum('bthi,hij->bthj', ub, w_igate).reshape(bsz, t, D_RNN) + b_igate
    r = jax.nn.sigmoid(r_pre.astype(jnp.float32))
    i = jax.nn.sigmoid(i_pre.astype(jnp.float32))
    log_a = -RG_C * r * jax.nn.softplus(-lam.astype(jnp.float32))
    a = jnp.exp(log_a)
    mult = jnp.sqrt(-jnp.expm1(2.0 * log_a))
    h = linear_scan(a, mult * i * u.astype(jnp.float32), h0.astype(jnp.float32))
    y = (h.astype(x.dtype) * jax.nn.gelu(gate_branch)) @ w_out
    return y, h[:, -1].astype(h0.dtype), new_buf


def shortconv_mixer(x, conv_buf, w_in, conv_w, w_out):
    proj = x @ w_in
    b_gate, c_gate, v = jnp.split(proj, 3, axis=-1)
    z, new_buf = causal_dwconv(c_gate * v, conv_buf, conv_w)
    return (b_gate * z) @ w_out, new_buf


def swiglu(x, w_gate, w_up, w_down):
    return (jax.nn.silu(x @ w_gate) * (x @ w_up)) @ w_down


def moe_swiglu(x, w_router, w_gate, w_up, w_down):
    shp = x.shape
    xt = x.reshape(-1, D_MODEL)
    n_tok = xt.shape[0]
    logits = xt.astype(jnp.float32) @ w_router.astype(jnp.float32)
    top_logit, top_e = lax.top_k(logits, TOP_K)
    gates = jax.nn.softmax(top_logit, axis=-1)
    n_assign = n_tok * TOP_K
    flat_e = top_e.reshape(-1).astype(jnp.int32)
    flat_tok = jnp.arange(n_assign, dtype=jnp.int32) // TOP_K
    flat_g = gates.reshape(-1)
    order = jnp.argsort(flat_e, stable=True)
    sorted_e = flat_e[order]
    counts = jnp.zeros((N_EXPERTS,), jnp.int32).at[flat_e].add(1)
    padded = (counts + MOE_BLOCK - 1) // MOE_BLOCK * MOE_BLOCK
    pad_end = jnp.cumsum(padded)
    pad_start = pad_end - padded
    start = jnp.cumsum(counts) - counts
    rank = jnp.arange(n_assign, dtype=jnp.int32) - start[sorted_e]
    dest = pad_start[sorted_e] + rank
    n_rows = (-(-n_assign // MOE_BLOCK) + N_EXPERTS) * MOE_BLOCK
    n_blocks = n_rows // MOE_BLOCK
    row_tok = jnp.zeros((n_rows,), jnp.int32).at[dest].set(flat_tok[order])
    row_gate = jnp.zeros((n_rows,), jnp.float32).at[dest].set(flat_g[order])
    block_start = jnp.arange(n_blocks, dtype=jnp.int32) * MOE_BLOCK
    block_e = jnp.minimum(jnp.searchsorted(pad_end, block_start, side='right'), N_EXPERTS - 1)
    xs = xt[row_tok].reshape(n_blocks, MOE_BLOCK, D_MODEL)

    def expert_block(args):
        xb, e = args
        return swiglu(xb, w_gate[e], w_up[e], w_down[e])

    yb = lax.map(expert_block, (xs, block_e)).reshape(n_rows, D_MODEL)
    out = jnp.zeros_like(xt).at[row_tok].add(yb * row_gate[:, None].astype(x.dtype))
    return out.reshape(shp)


def run_trunk(x, lru_h, lru_conv, sconv, norm_mix, norm_ffn, norm_final,
              lru_w_in, lru_conv_w, lru_conv_b, lru_w_rgate, lru_b_rgate, lru_w_igate, lru_b_igate,
              lru_lambda, lru_w_out, sc_w_in, sc_conv_w, sc_w_out,
              ffn_w_gate, ffn_w_up, ffn_w_down, moe_w_router, moe_w_gate, moe_w_up, moe_w_down):
    new_h, new_lru_conv, new_sconv = [], [], []
    for i in range(DEPTH):
        l = i // N_MIXERS
        hn = rms_norm(x, norm_mix[i])
        if i % N_MIXERS == 0:
            y, h_last, buf = rglru_mixer(hn, lru_h[l], lru_conv[l], lru_w_in[l], lru_conv_w[l], lru_conv_b[l],
                                         lru_w_rgate[l], lru_b_rgate[l], lru_w_igate[l], lru_b_igate[l],
                                         lru_lambda[l], lru_w_out[l])
            new_h.append(h_last)
            new_lru_conv.append(buf)
        else:
            y, buf = shortconv_mixer(hn, sconv[l], sc_w_in[l], sc_conv_w[l], sc_w_out[l])
            new_sconv.append(buf)
        x = x + y
        hn = rms_norm(x, norm_ffn[i])
        if i % 2 == 0:
            x = x + swiglu(hn, ffn_w_gate[l], ffn_w_up[l], ffn_w_down[l])
        else:
            x = x + moe_swiglu(hn, moe_w_router[l], moe_w_gate[l], moe_w_up[l], moe_w_down[l])
    return rms_norm(x, norm_final), jnp.stack(new_h), jnp.stack(new_lru_conv), jnp.stack(new_sconv)


def setup_inputs(seed: int = 0) -> dict:
    key = jax.random.key(seed)
    ks = jax.random.split(key, 32)

    def nrm(k, shape, scale):
        return jax.random.normal(k, shape, jnp.float32) * scale

    u = jax.random.uniform(ks[10], (N_LRU_LAYERS, D_RNN), jnp.float32, minval=0.9, maxval=0.999)
    a_base = u ** (1.0 / RG_C)
    lam = jnp.log(a_base) - jnp.log1p(-a_base)
    return {
        'x_prompt': nrm(ks[0], (BATCH, SEQ, D_MODEL), 1.0),
        'x_sample': nrm(ks[1], (DEC_BATCH, DEC_SEQ, D_MODEL), 1.0),
        'state_lru_h': nrm(ks[2], (N_LRU_LAYERS, DEC_BATCH, D_RNN), 0.5),
        'state_lru_conv': nrm(ks[3], (N_LRU_LAYERS, DEC_BATCH, CONV_A_WIDTH - 1, D_RNN), 1.0),
        'state_sconv': nrm(ks[4], (N_CONV_LAYERS, DEC_BATCH, CONV_B_WIDTH - 1, D_CONV), 1.0),
        'norm_mix': 1.0 + nrm(ks[5], (DEPTH, D_MODEL), 0.02),
        'norm_ffn': 1.0 + nrm(ks[6], (DEPTH, D_MODEL), 0.02),
        'norm_final': 1.0 + nrm(ks[7], (D_MODEL,), 0.02),
        'lru_w_in': nrm(ks[8], (N_LRU_LAYERS, D_MODEL, 2 * D_RNN), D_MODEL ** -0.5),
        'lru_conv_w': nrm(ks[9], (N_LRU_LAYERS, CONV_A_WIDTH, D_RNN), CONV_A_WIDTH ** -0.5),
        'lru_conv_b': nrm(ks[11], (N_LRU_LAYERS, D_RNN), 0.01),
        'lru_w_rgate': nrm(ks[12], (N_LRU_LAYERS, N_RNN_BLOCKS, RNN_BLOCK, RNN_BLOCK), RNN_BLOCK ** -0.5),
        'lru_b_rgate': nrm(ks[13], (N_LRU_LAYERS, D_RNN), 0.01),
        'lru_w_igate': nrm(ks[14], (N_LRU_LAYERS, N_RNN_BLOCKS, RNN_BLOCK, RNN_BLOCK), RNN_BLOCK ** -0.5),
        'lru_b_igate': nrm(ks[15], (N_LRU_LAYERS, D_RNN), 0.01),
        'lru_lambda': lam,
        'lru_w_out': nrm(ks[16], (N_LRU_LAYERS, D_RNN, D_MODEL), D_RNN ** -0.5),
        'sc_w_in': nrm(ks[17], (N_CONV_LAYERS, D_MODEL, 3 * D_CONV), D_MODEL ** -0.5),
        'sc_conv_w': nrm(ks[18], (N_CONV_LAYERS, CONV_B_WIDTH, D_CONV), CONV_B_WIDTH ** -0.5),
        'sc_w_out': nrm(ks[19], (N_CONV_LAYERS, D_CONV, D_MODEL), D_CONV ** -0.5),
        'ffn_w_gate': nrm(ks[20], (N_DENSE_LAYERS, D_MODEL, D_FF), D_MODEL ** -0.5),
        'ffn_w_up': nrm(ks[21], (N_DENSE_LAYERS, D_MODEL, D_FF), D_MODEL ** -0.5),
        'ffn_w_down': nrm(ks[22], (N_DENSE_LAYERS, D_FF, D_MODEL), D_FF ** -0.5),
        'moe_w_router': nrm(ks[23], (N_MOE_LAYERS, D_MODEL, N_EXPERTS), D_MODEL ** -0.5),
        'moe_w_gate': nrm(ks[24], (N_MOE_LAYERS, N_EXPERTS, D_MODEL, D_FF_EXPERT), D_MODEL ** -0.5),
        'moe_w_up': nrm(ks[25], (N_MOE_LAYERS, N_EXPERTS, D_MODEL, D_FF_EXPERT), D_MODEL ** -0.5),
        'moe_w_down': nrm(ks[26], (N_MOE_LAYERS, N_EXPERTS, D_FF_EXPERT, D_MODEL), D_FF_EXPERT ** -0.5),
    }


def reference(x_prompt, x_sample, state_lru_h, state_lru_conv, state_sconv,
              norm_mix, norm_ffn, norm_final,
              lru_w_in, lru_conv_w, lru_conv_b, lru_w_rgate, lru_b_rgate, lru_w_igate, lru_b_igate,
              lru_lambda, lru_w_out, sc_w_in, sc_conv_w, sc_w_out,
              ffn_w_gate, ffn_w_up, ffn_w_down, moe_w_router, moe_w_gate, moe_w_up, moe_w_down):
    b = x_prompt.shape[0]
    dt = x_prompt.dtype
    zero_h = jnp.zeros((N_LRU_LAYERS, b, D_RNN), dt)
    zero_lru_conv = jnp.zeros((N_LRU_LAYERS, b, CONV_A_WIDTH - 1, D_RNN), dt)
    zero_sconv = jnp.zeros((N_CONV_LAYERS, b, CONV_B_WIDTH - 1, D_CONV), dt)
    y_prompt, lru_h_p, lru_conv_p, sconv_p = run_trunk(
        x_prompt, zero_h, zero_lru_conv, zero_sconv, norm_mix, norm_ffn, norm_final,
        lru_w_in, lru_conv_w, lru_conv_b, lru_w_rgate, lru_b_rgate, lru_w_igate, lru_b_igate,
        lru_lambda, lru_w_out, sc_w_in, sc_conv_w, sc_w_out,
        ffn_w_gate, ffn_w_up, ffn_w_down, moe_w_router, moe_w_gate, moe_w_up, moe_w_down)
    y_sample, lru_h_s, lru_conv_s, sconv_s = run_trunk(
        x_sample, state_lru_h, state_lru_conv, state_sconv, norm_mix, norm_ffn, norm_final,
        lru_w_in, lru_conv_w, lru_conv_b, lru_w_rgate, lru_b_rgate, lru_w_igate, lru_b_igate,
        lru_lambda, lru_w_out, sc_w_in, sc_conv_w, sc_w_out,
        ffn_w_gate, ffn_w_up, ffn_w_down, moe_w_router, moe_w_gate, moe_w_up, moe_w_down)
    return (y_prompt, y_sample, lru_h_p, lru_conv_p, sconv_p, lru_h_s, lru_conv_s, sconv_s)
```

```python
import functools

import jax
import jax.numpy as jnp
from jax import lax
from jax.experimental import pallas as pl
from jax.experimental.pallas import tpu as pltpu

F32 = jnp.float32
BF16 = jnp.bfloat16

EPS = 1e-6
RG_C = 8.0
TOP_K = 2

LANES = 128
SUBLANES = 8
ROW_TILE = 1024
FFN_ROW_TILE = 512
COL_TILE = 512
MOE_ROW_TILE = 512
COMBINE_ROW_TILE = 256
SCAN_CHUNK = 64
VMEM_LIMIT = 56 * 1024 * 1024


def _cparams(*sem):
    return pltpu.CompilerParams(dimension_semantics=sem, vmem_limit_bytes=VMEM_LIMIT)


def _rms_norm_f32(x, g):
    y = x * lax.rsqrt(jnp.mean(x * x, axis=-1, keepdims=True) + EPS)
    return y * g


def _norm_matmul_kernel(x_ref, g_ref, w_ref, o_ref, hn_ref):
    @pl.when(pl.program_id(1) == 0)
    def _():
        hn_ref[...] = _rms_norm_f32(x_ref[...], g_ref[...]).astype(BF16)

    o_ref[...] = jnp.dot(hn_ref[...], w_ref[...], preferred_element_type=F32)


def _norm_matmul(x, g, w):
    m, d = x.shape
    n = w.shape[1]
    return pl.pallas_call(
        _norm_matmul_kernel,
        grid=(m // ROW_TILE, n // COL_TILE),
        in_specs=[pl.BlockSpec((ROW_TILE, d), lambda i, j: (i, 0)),
                  pl.BlockSpec((1, d), lambda i, j: (0, 0)),
                  pl.BlockSpec((d, COL_TILE), lambda i, j: (0, j))],
        out_specs=pl.BlockSpec((ROW_TILE, COL_TILE), lambda i, j: (i, j)),
        out_shape=jax.ShapeDtypeStruct((m, n), F32),
        scratch_shapes=[pltpu.VMEM((ROW_TILE, d), BF16)],
        compiler_params=_cparams("arbitrary", "arbitrary"),
        name="norm_matmul",
    )(x, g, w)


def _sc_inproj_kernel(x_ref, g_ref, wb_ref, wc_ref, wv_ref, b_ref, cv_ref, hn_ref):
    @pl.when(pl.program_id(1) == 0)
    def _():
        hn_ref[...] = _rms_norm_f32(x_ref[...], g_ref[...]).astype(BF16)

    hn = hn_ref[...]
    b_ref[...] = jnp.dot(hn, wb_ref[...], preferred_element_type=F32)
    c = jnp.dot(hn, wc_ref[...], preferred_element_type=F32)
    v = jnp.dot(hn, wv_ref[...], preferred_element_type=F32)
    cv_ref[...] = c * v


def _sc_inproj(x, g, w):
    m, d = x.shape
    dc = w.shape[1] // 3
    nj = dc // COL_TILE
    wspec = lambda off: pl.BlockSpec((d, COL_TILE), lambda i, j: (0, off * nj + j))
    ospec = pl.BlockSpec((ROW_TILE, COL_TILE), lambda i, j: (i, j))
    return pl.pallas_call(
        _sc_inproj_kernel,
        grid=(m // ROW_TILE, nj),
        in_specs=[pl.BlockSpec((ROW_TILE, d), lambda i, j: (i, 0)),
                  pl.BlockSpec((1, d), lambda i, j: (0, 0)),
                  wspec(0), wspec(1), wspec(2)],
        out_specs=[ospec, ospec],
        out_shape=[jax.ShapeDtypeStruct((m, dc), F32)] * 2,
        scratch_shapes=[pltpu.VMEM((ROW_TILE, d), BF16)],
        compiler_params=_cparams("arbitrary", "arbitrary"),
        name="sc_inproj",
    )(x, g, w, w, w)


def _matmul_residual_kernel(y_ref, w_ref, r_ref, o_ref):
    o_ref[...] = r_ref[...] + jnp.dot(y_ref[...], w_ref[...], preferred_element_type=F32)


def _matmul_residual(y, w, res):
    m, k = y.shape
    n = w.shape[1]
    return pl.pallas_call(
        _matmul_residual_kernel,
        grid=(m // ROW_TILE, n // COL_TILE),
        in_specs=[pl.BlockSpec((ROW_TILE, k), lambda i, j: (i, 0)),
                  pl.BlockSpec((k, COL_TILE), lambda i, j: (0, j)),
                  pl.BlockSpec((ROW_TILE, COL_TILE), lambda i, j: (i, j))],
        out_specs=pl.BlockSpec((ROW_TILE, COL_TILE), lambda i, j: (i, j)),
        out_shape=jax.ShapeDtypeStruct((m, n), F32),
        compiler_params=_cparams("arbitrary", "arbitrary"),
        name="matmul_residual",
    )(y, w, res)


def _seq_pos(i, r0, rows, n_prompt_tiles, seq_p, seq_s):
    seqlen = jnp.where(i < n_prompt_tiles, seq_p, seq_s)
    grow = i * ROW_TILE + r0 + lax.broadcasted_iota(jnp.int32, (rows, 1), 0)
    return grow & (seqlen - 1)


def _rglru_kernel(n_prompt_tiles, seq_p, seq_s,
                  gate_ref, u_ref, p1_ref, p2_ref, p3_ref, h0_ref,
                  cw_ref, cb_ref, br_ref, bi_ref, lam_ref, wr_ref, wi_ref,
                  y_ref, h_ref, ubuf, uc_ref, rp_ref, ip_ref, hc_ref):
    i = pl.program_id(1)
    cblk = u_ref.shape[1]

    @pl.when(i == 0)
    def _():
        ubuf[0:SUBLANES, :] = jnp.zeros((SUBLANES, cblk), F32)
        hc_ref[...] = jnp.zeros_like(hc_ref)

    ubuf[SUBLANES:, :] = u_ref[...]
    cw = cw_ref[...]
    cb = cb_ref[...]
    rc = SCAN_CHUNK

    def conv_chunk(k, carry):
        r0 = pl.multiple_of(k * rc, rc)
        rows = pl.ds(r0, rc)
        t = _seq_pos(i, r0, rc, n_prompt_tiles, seq_p, seq_s)
        ext = ubuf[pl.ds(r0, rc + SUBLANES), :]

        def tap(s, p_ref):
            shifted = pltpu.roll(ext, s, axis=0)[SUBLANES:]
            return jnp.where(t >= s, shifted, p_ref[rows, :])

        conv = tap(3, p3_ref) * cw[0:1] + tap(2, p2_ref) * cw[1:2]
        conv = conv + tap(1, p1_ref) * cw[2:3]
        conv = conv + ext[SUBLANES:] * cw[3:4]
        uc_ref[rows, :] = conv + cb
        return carry

    lax.fori_loop(0, ROW_TILE // rc, conv_chunk, 0)

    ub = uc_ref[...].astype(BF16)
    rp_ref[...] = jnp.dot(ub, wr_ref[...], preferred_element_type=F32) + br_ref[...]
    ip_ref[...] = jnp.dot(ub, wi_ref[...], preferred_element_type=F32) + bi_ref[...]

    sp = jax.nn.softplus(-lam_ref[...])
    sub = lax.broadcasted_iota(jnp.int32, (1, SUBLANES, 1), 1)

    def scan_chunk(k, carry):
        r0 = pl.multiple_of(k * rc, rc)
        rows = pl.ds(r0, rc)
        t = _seq_pos(i, r0, rc, n_prompt_tiles, seq_p, seq_s)
        uc = uc_ref[rows, :]
        r = jax.nn.sigmoid(rp_ref[rows, :])
        ig = jax.nn.sigmoid(ip_ref[rows, :])
        log_a = (-RG_C * r) * sp
        a = jnp.exp(log_a)
        mult = jnp.sqrt(-jnp.tanh(log_a) * (a * a + 1.0))
        b = mult * ig * uc
        first = t == 0
        b = jnp.where(first, b + a * h0_ref[rows, :], b)
        a = jnp.where(first, 0.0, a)

        a3 = a.reshape(rc // SUBLANES, SUBLANES, cblk)
        b3 = b.reshape(rc // SUBLANES, SUBLANES, cblk)
        for s in (1, 2, 4):
            a_s = pltpu.roll(a3, s, axis=1)
            b_s = pltpu.roll(b3, s, axis=1)
            m = sub >= s
            b3 = jnp.where(m, a3 * b_s + b3, b3)
            a3 = jnp.where(m, a3 * a_s, a3)

        hc = hc_ref[...]
        hs = []
        for g in range(rc // SUBLANES):
            hg = a3[g] * hc + b3[g]
            hc = jnp.broadcast_to(hg[SUBLANES - 1:SUBLANES], (SUBLANES, cblk))
            hs.append(hg)
        hc_ref[...] = hc
        h = jnp.concatenate(hs, axis=0)
        h_ref[rows, :] = h
        y_ref[rows, :] = (h * jax.nn.gelu(gate_ref[rows, :])).astype(BF16)
        return carry

    lax.fori_loop(0, ROW_TILE // rc, scan_chunk, 0)
    ubuf[0:SUBLANES, :] = ubuf[ROW_TILE:ROW_TILE + SUBLANES, :]


def _rglru_core(proj, aux, conv_w, conv_b, b_r, b_i, lam, w_r, w_i, n_prompt_tiles, seq_p, seq_s):
    m = proj.shape[0]
    nb, cblk = w_r.shape[0], w_r.shape[1]
    c = nb * cblk
    n_tiles = m // ROW_TILE
    row_blk = lambda off: pl.BlockSpec((ROW_TILE, cblk), lambda cb, i: (i, off + cb))
    aux_blk = pl.BlockSpec(
        (ROW_TILE, cblk), lambda cb, i: (jnp.maximum(i - n_prompt_tiles + 1, 0), cb))
    vec_blk = lambda rows: pl.BlockSpec((rows, cblk), lambda cb, i: (0, cb))
    w_blk = pl.BlockSpec((None, cblk, cblk), lambda cb, i: (cb, 0, 0))
    out_blk = pl.BlockSpec((ROW_TILE, cblk), lambda cb, i: (i, cb))
    return pl.pallas_call(
        functools.partial(_rglru_kernel, n_prompt_tiles, seq_p, seq_s),
        grid=(nb, n_tiles),
        in_specs=[row_blk(0), row_blk(nb), aux_blk, aux_blk, aux_blk, aux_blk,
                  vec_blk(conv_w.shape[0]), vec_blk(1), vec_blk(1), vec_blk(1), vec_blk(1),
                  w_blk, w_blk],
        out_specs=[out_blk, out_blk],
        out_shape=[jax.ShapeDtypeStruct((m, c), BF16), jax.ShapeDtypeStruct((m, c), F32)],
        scratch_shapes=[pltpu.VMEM((ROW_TILE + SUBLANES, cblk), F32),
                        pltpu.VMEM((ROW_TILE, cblk), F32),
                        pltpu.VMEM((ROW_TILE, cblk), F32),
                        pltpu.VMEM((ROW_TILE, cblk), F32),
                        pltpu.VMEM((SUBLANES, cblk), F32)],
        compiler_params=_cparams("arbitrary", "arbitrary"),
        name="rglru_core",
    )(proj, proj, *aux, conv_w, conv_b, b_r, b_i, lam, w_r, w_i)


def _sconv_kernel(n_prompt_tiles, seq_p, seq_s,
                  b_ref, cv_ref, q1_ref, q2_ref, cw_ref, o_ref, cvbuf):
    i = pl.program_id(1)
    cblk = cv_ref.shape[1]

    @pl.when(i == 0)
    def _():
        cvbuf[0:SUBLANES, :] = jnp.zeros((SUBLANES, cblk), F32)

    cvbuf[SUBLANES:, :] = cv_ref[...]
    cw = cw_ref[...]
    rc = SCAN_CHUNK

    def conv_chunk(k, carry):
        r0 = pl.multiple_of(k * rc, rc)
        rows = pl.ds(r0, rc)
        t = _seq_pos(i, r0, rc, n_prompt_tiles, seq_p, seq_s)
        ext = cvbuf[pl.ds(r0, rc + SUBLANES), :]

        def tap(s, q_ref):
            shifted = pltpu.roll(ext, s, axis=0)[SUBLANES:]
            return jnp.where(t >= s, shifted, q_ref[rows, :])

        z = tap(2, q2_ref) * cw[0:1] + tap(1, q1_ref) * cw[1:2]
        z = z + ext[SUBLANES:] * cw[2:3]
        o_ref[rows, :] = (b_ref[rows, :] * z).astype(BF16)
        return carry

    lax.fori_loop(0, ROW_TILE // rc, conv_chunk, 0)
    cvbuf[0:SUBLANES, :] = cvbuf[ROW_TILE:ROW_TILE + SUBLANES, :]


def _sconv_core(bgate, cv, aux, conv_w, n_prompt_tiles, seq_p, seq_s):
    m, c = cv.shape
    cblk = COL_TILE
    blk = pl.BlockSpec((ROW_TILE, cblk), lambda cb, i: (i, cb))
    aux_blk = pl.BlockSpec(
        (ROW_TILE, cblk), lambda cb, i: (jnp.maximum(i - n_prompt_tiles + 1, 0), cb))
    return pl.pallas_call(
        functools.partial(_sconv_kernel, n_prompt_tiles, seq_p, seq_s),
        grid=(c // cblk, m // ROW_TILE),
        in_specs=[blk, blk, aux_blk, aux_blk,
                  pl.BlockSpec((conv_w.shape[0], cblk), lambda cb, i: (0, cb))],
        out_specs=blk,
        out_shape=jax.ShapeDtypeStruct((m, c), BF16),
        scratch_shapes=[pltpu.VMEM((ROW_TILE + SUBLANES, cblk), F32)],
        compiler_params=_cparams("arbitrary", "arbitrary"),
        name="sconv_core",
    )(bgate, cv, *aux, conv_w)


def _ffn_kernel(x_ref, g_ref, wg_ref, wu_ref, wd_ref, o_ref, hn_ref):
    @pl.when(pl.program_id(1) == 0)
    def _():
        x = x_ref[...]
        hn_ref[...] = _rms_norm_f32(x, g_ref[...]).astype(BF16)
        o_ref[...] = x

    hn = hn_ref[...]
    a = jnp.dot(hn, wg_ref[...], preferred_element_type=F32)
    b = jnp.dot(hn, wu_ref[...], preferred_element_type=F32)
    h = (jax.nn.silu(a) * b).astype(BF16)
    o_ref[...] += jnp.dot(h, wd_ref[...], preferred_element_type=F32)


def _ffn(x, g, wg, wu, wd):
    m, d = x.shape
    f = wg.shape[1]
    tm = FFN_ROW_TILE
    return pl.pallas_call(
        _ffn_kernel,
        grid=(m // tm, f // COL_TILE),
        in_specs=[pl.BlockSpec((tm, d), lambda i, j: (i, 0)),
                  pl.BlockSpec((1, d), lambda i, j: (0, 0)),
                  pl.BlockSpec((d, COL_TILE), lambda i, j: (0, j)),
                  pl.BlockSpec((d, COL_TILE), lambda i, j: (0, j)),
                  pl.BlockSpec((COL_TILE, d), lambda i, j: (j, 0))],
        out_specs=pl.BlockSpec((tm, d), lambda i, j: (i, 0)),
        out_shape=jax.ShapeDtypeStruct((m, d), F32),
        scratch_shapes=[pltpu.VMEM((tm, d), BF16)],
        compiler_params=_cparams("arbitrary", "arbitrary"),
        name="ffn",
    )(x, g, wg, wu, wd)


def _router_kernel(n_experts, x_ref, g_ref, wr_ref, hn_ref, meta_ref, cnt_ref, run_ref):
    @pl.when(pl.program_id(0) == 0)
    def _():
        run_ref[...] = jnp.zeros_like(run_ref)

    tm = x_ref.shape[0]
    hn = _rms_norm_f32(x_ref[...], g_ref[...])
    hn_ref[...] = hn
    logits = jnp.dot(hn, wr_ref[...], preferred_element_type=F32,
                     precision=lax.Precision.HIGHEST)
    lane = lax.broadcasted_iota(jnp.int32, (tm, LANES), 1).astype(F32)
    neg = jnp.float32(-jnp.inf)
    lg = jnp.where(lane < n_experts, logits, neg)
    m1 = jnp.max(lg, axis=1, keepdims=True)
    i1 = jnp.min(jnp.where(lg == m1, lane, float(LANES)), axis=1, keepdims=True)
    lg2 = jnp.where(lane == i1, neg, lg)
    m2 = jnp.max(lg2, axis=1, keepdims=True)
    i2 = jnp.min(jnp.where(lg2 == m2, lane, float(LANES)), axis=1, keepdims=True)
    ex = jnp.exp(m2 - m1)
    g1 = 1.0 / (1.0 + ex)
    g2 = ex / (1.0 + ex)

    sel1 = lane == i1
    sel2 = lane == i2
    onehot = jnp.where(sel1 | sel2, 1.0, 0.0)
    rr = lax.broadcasted_iota(jnp.int32, (tm, tm), 0)
    cc = lax.broadcasted_iota(jnp.int32, (tm, tm), 1)
    tri = jnp.where(rr > cc, 1.0, 0.0).astype(BF16)
    before = jnp.dot(tri, onehot.astype(BF16), preferred_element_type=F32) + run_ref[0:1, :]
    rank1 = jnp.sum(jnp.where(sel1, before, 0.0), axis=1, keepdims=True)
    rank2 = jnp.sum(jnp.where(sel2, before, 0.0), axis=1, keepdims=True)
    run = run_ref[0:1, :] + jnp.sum(onehot, axis=0, keepdims=True)
    run_ref[...] = jnp.broadcast_to(run, run_ref.shape)
    cnt_ref[...] = jnp.broadcast_to(run, cnt_ref.shape)

    meta = jnp.where(lane == 0, i1, 0.0)
    meta = jnp.where(lane == 1, i2, meta)
    meta = jnp.where(lane == 2, g1, meta)
    meta = jnp.where(lane == 3, g2, meta)
    meta = jnp.where(lane == 4, rank1, meta)
    meta = jnp.where(lane == 5, rank2, meta)
    meta_ref[...] = meta


def _router(x, g, w_router_padded, n_experts):
    m, d = x.shape
    tm = FFN_ROW_TILE
    return pl.pallas_call(
        functools.partial(_router_kernel, n_experts),
        grid=(m // tm,),
        in_specs=[pl.BlockSpec((tm, d), lambda i: (i, 0)),
                  pl.BlockSpec((1, d), lambda i: (0, 0)),
                  pl.BlockSpec((d, LANES), lambda i: (0, 0))],
        out_specs=[pl.BlockSpec((tm, d), lambda i: (i, 0)),
                   pl.BlockSpec((tm, LANES), lambda i: (i, 0)),
                   pl.BlockSpec((SUBLANES, LANES), lambda i: (0, 0))],
        out_shape=[jax.ShapeDtypeStruct((m, d), F32),
                   jax.ShapeDtypeStruct((m, LANES), F32),
                   jax.ShapeDtypeStruct((SUBLANES, LANES), F32)],
        scratch_shapes=[pltpu.VMEM((SUBLANES, LANES), F32)],
        compiler_params=_cparams("arbitrary"),
        name="router",
    )(x, g, w_router_padded)


def _row_copy(src_hbm, idx, dst_ref, r, sem):
    return pltpu.make_async_copy(src_hbm.at[pl.ds(idx, 1)], dst_ref.at[pl.ds(r, 1)], sem)


def _dispatch_kernel(nused_ref, tok_ref, hn_hbm, o_ref, buf, sem):
    t = pl.program_id(0)
    tr = buf.shape[0]

    @pl.when(t < nused_ref[0])
    def _():
        def issue(r, c):
            _row_copy(hn_hbm, tok_ref[0, 0, r], buf, r, sem).start()
            return c

        lax.fori_loop(0, tr, issue, 0, unroll=8)
        pltpu.make_async_copy(hn_hbm.at[pl.ds(0, tr)], buf, sem).wait()
        o_ref[...] = buf[...].astype(BF16)

    @pl.when(t >= nused_ref[0])
    def _():
        o_ref[...] = jnp.zeros_like(o_ref)


def _dispatch(hn, row_tok, n_used):
    m, d = hn.shape
    tr = MOE_ROW_TILE
    n_tiles = row_tok.shape[0] // tr
    return pl.pallas_call(
        _dispatch_kernel,
        grid_spec=pltpu.PrefetchScalarGridSpec(
            num_scalar_prefetch=1,
            grid=(n_tiles,),
            in_specs=[pl.BlockSpec((1, 1, tr), lambda t, nu: (t, 0, 0), memory_space=pltpu.SMEM),
                      pl.BlockSpec(memory_space=pl.ANY)],
            out_specs=pl.BlockSpec((tr, d), lambda t, nu: (t, 0)),
            scratch_shapes=[pltpu.VMEM((tr, d), F32), pltpu.SemaphoreType.DMA(())]),
        out_shape=jax.ShapeDtypeStruct((n_tiles * tr, d), BF16),
        compiler_params=_cparams("arbitrary"),
        name="moe_dispatch",
    )(n_used, row_tok.reshape(n_tiles, 1, tr), hn)


def _experts_kernel(be_ref, nused_ref, x_ref, wg_ref, wu_ref, wd_ref, o_ref):
    t = pl.program_id(0)
    j = pl.program_id(1)

    @pl.when(t < nused_ref[0])
    def _():
        x = x_ref[...]
        a = jnp.dot(x, wg_ref[...], preferred_element_type=F32)
        b = jnp.dot(x, wu_ref[...], preferred_element_type=F32)
        h = (jax.nn.silu(a) * b).astype(BF16)
        y = jnp.dot(h, wd_ref[...], preferred_element_type=F32)

        @pl.when(j == 0)
        def _():
            o_ref[...] = y

        @pl.when(j > 0)
        def _():
            o_ref[...] += y

    @pl.when((t >= nused_ref[0]) & (j == 0))
    def _():
        o_ref[...] = jnp.zeros_like(o_ref)


def _experts(xs, block_e, n_used, wg, wu, wd):
    n_rows, d = xs.shape
    f = wg.shape[2]
    tr = MOE_ROW_TILE
    n_tiles = n_rows // tr
    nj = f // COL_TILE

    def tile(t, nu):
        return jnp.minimum(t, nu[0] - 1)

    def col(t, j, nu):
        return jnp.where(t < nu[0], j, nj - 1)

    return pl.pallas_call(
        _experts_kernel,
        grid_spec=pltpu.PrefetchScalarGridSpec(
            num_scalar_prefetch=2,
            grid=(n_tiles, nj),
            in_specs=[pl.BlockSpec((tr, d), lambda t, j, be, nu: (tile(t, nu), 0)),
                      pl.BlockSpec((None, d, COL_TILE),
                                   lambda t, j, be, nu: (be[tile(t, nu)], 0, col(t, j, nu))),
                      pl.BlockSpec((None, d, COL_TILE),
                                   lambda t, j, be, nu: (be[tile(t, nu)], 0, col(t, j, nu))),
                      pl.BlockSpec((None, COL_TILE, d),
                                   lambda t, j, be, nu: (be[tile(t, nu)], col(t, j, nu), 0))],
            out_specs=pl.BlockSpec((tr, d), lambda t, j, be, nu: (t, 0))),
        out_shape=jax.ShapeDtypeStruct((n_rows, d), F32),
        compiler_params=_cparams("arbitrary", "arbitrary"),
        name="moe_experts",
    )(block_e, n_used, xs, wg, wu, wd)


def _combine_kernel(d0_ref, d1_ref, x_ref, meta_ref, g_ref, yb_hbm, o_ref, y0, y1, sem):
    tm = x_ref.shape[0]

    def issue(r, c):
        _row_copy(yb_hbm, d0_ref[0, 0, r], y0, r, sem.at[0]).start()
        _row_copy(yb_hbm, d1_ref[0, 0, r], y1, r, sem.at[1]).start()
        return c

    lax.fori_loop(0, tm, issue, 0, unroll=8)
    pltpu.make_async_copy(yb_hbm.at[pl.ds(0, tm)], y0, sem.at[0]).wait()
    pltpu.make_async_copy(yb_hbm.at[pl.ds(0, tm)], y1, sem.at[1]).wait()
    meta = meta_ref[...]
    x = x_ref[...] + (y0[...] * meta[:, 2:3] + y1[...] * meta[:, 3:4])
    o_ref[...] = _rms_norm_f32(x, g_ref[...])


def _combine(x, meta, g, yb, dest0, dest1):
    m, d = x.shape
    tm = COMBINE_ROW_TILE
    n_tiles = m // tm
    smem_blk = pl.BlockSpec((1, 1, tm), lambda i: (i, 0, 0), memory_space=pltpu.SMEM)
    return pl.pallas_call(
        _combine_kernel,
        grid=(n_tiles,),
        in_specs=[smem_blk, smem_blk,
                  pl.BlockSpec((tm, d), lambda i: (i, 0)),
                  pl.BlockSpec((tm, LANES), lambda i: (i, 0)),
                  pl.BlockSpec((1, d), lambda i: (0, 0)),
                  pl.BlockSpec(memory_space=pl.ANY)],
        out_specs=pl.BlockSpec((tm, d), lambda i: (i, 0)),
        out_shape=jax.ShapeDtypeStruct((m, d), F32),
        scratch_shapes=[pltpu.VMEM((tm, d), F32), pltpu.VMEM((tm, d), F32),
                        pltpu.SemaphoreType.DMA((2,))],
        compiler_params=_cparams("arbitrary"),
        name="moe_combine",
    )(dest0.reshape(n_tiles, 1, tm), dest1.reshape(n_tiles, 1, tm), x, meta, g, yb)


def _sample_aux(state, shift, dec_seq):
    b, w1, c = state.shape
    rows = [state[:, w1 + t - shift] if t < shift else jnp.zeros((b, c), state.dtype)
            for t in range(dec_seq)]
    aux = jnp.stack(rows, axis=1).reshape(b * dec_seq, c)
    return jnp.concatenate([jnp.zeros((ROW_TILE, c), state.dtype), aux], axis=0)


def kernel(x_prompt, x_sample, state_lru_h, state_lru_conv, state_sconv, norm_mix, norm_ffn, norm_final, lru_w_in, lru_conv_w, lru_conv_b, lru_w_rgate, lru_b_rgate, lru_w_igate, lru_b_igate, lru_lambda, lru_w_out, sc_w_in, sc_conv_w, sc_w_out, ffn_w_gate, ffn_w_up, ffn_w_down, moe_w_router, moe_w_gate, moe_w_up, moe_w_down):
    batch, seq, d = x_prompt.shape
    dec_batch, dec_seq, _ = x_sample.shape
    mp, ms = batch * seq, dec_batch * dec_seq
    m = mp + ms
    depth = norm_mix.shape[0]
    n_experts = moe_w_router.shape[2]
    conv_a = lru_conv_w.shape[1]
    conv_b = sc_conv_w.shape[1]
    assert depth == 2 and lru_w_in.shape[0] == 1 and sc_w_in.shape[0] == 1
    assert seq & (seq - 1) == 0 and dec_seq & (dec_seq - 1) == 0
    assert seq % ROW_TILE == 0 and ms == ROW_TILE and ROW_TILE % dec_seq == 0
    assert dec_seq >= conv_a - 1 and dec_seq >= conv_b - 1 and dec_seq == SUBLANES
    assert n_experts <= LANES and m % FFN_ROW_TILE == 0 and m % COMBINE_ROW_TILE == 0
    n_prompt_tiles = mp // ROW_TILE

    x = jnp.concatenate([x_prompt.reshape(mp, d), x_sample.reshape(ms, d)], axis=0)
    row = lambda v: v.reshape(1, -1)

    d_rnn = lru_w_out.shape[1]
    proj = _norm_matmul(x, row(norm_mix[0]), lru_w_in[0].astype(BF16))
    h0_rows = jnp.pad(state_lru_h[0][:, None, :], ((0, 0), (0, dec_seq - 1), (0, 0)))
    h0_rows = jnp.concatenate([jnp.zeros((ROW_TILE, d_rnn), F32), h0_rows.reshape(ms, d_rnn)], axis=0)
    aux = [_sample_aux(state_lru_conv[0], s, dec_seq) for s in (1, 2, 3)] + [h0_rows]
    y, h_all = _rglru_core(proj, aux, lru_conv_w[0], row(lru_conv_b[0]), row(lru_b_rgate[0]),
                           row(lru_b_igate[0]), row(lru_lambda[0]),
                           lru_w_rgate[0].astype(BF16), lru_w_igate[0].astype(BF16),
                           n_prompt_tiles, seq, dec_seq)
    x = _matmul_residual(y, lru_w_out[0].astype(BF16), x)

    x = _ffn(x, row(norm_ffn[0]), ffn_w_gate[0].astype(BF16), ffn_w_up[0].astype(BF16),
             ffn_w_down[0].astype(BF16))

    bgate, cv = _sc_inproj(x, row(norm_mix[1]), sc_w_in[0].astype(BF16))
    aux = [_sample_aux(state_sconv[0], s, dec_seq) for s in (1, 2)]
    zb = _sconv_core(bgate, cv, aux, sc_conv_w[0], n_prompt_tiles, seq, dec_seq)
    x = _matmul_residual(zb, sc_w_out[0].astype(BF16), x)

    w_router = jnp.pad(moe_w_router[0], ((0, 0), (0, LANES - n_experts)))
    hn, meta, counts = _router(x, row(norm_ffn[1]), w_router, n_experts)
    tr = MOE_ROW_TILE
    n_tiles = -(-(m * TOP_K) // tr) + n_experts
    counts = counts[0, :n_experts].astype(jnp.int32)
    padded = (counts + tr - 1) // tr * tr
    pad_end = jnp.cumsum(padded)
    pad_start = pad_end - padded
    e0 = meta[:, 0].astype(jnp.int32)
    e1 = meta[:, 1].astype(jnp.int32)
    dest0 = pad_start[e0] + meta[:, 4].astype(jnp.int32)
    dest1 = pad_start[e1] + meta[:, 5].astype(jnp.int32)
    tok = jnp.arange(m, dtype=jnp.int32)
    row_tok = jnp.zeros((n_tiles * tr,), jnp.int32).at[dest0].set(tok).at[dest1].set(tok)
    block_start = jnp.arange(n_tiles, dtype=jnp.int32) * tr
    block_e = jnp.minimum(jnp.searchsorted(pad_end, block_start, side='right'),
                          n_experts - 1).astype(jnp.int32)
    n_used = (pad_end[-1:] // tr).astype(jnp.int32)
    xs = _dispatch(hn, row_tok, n_used)
    yb = _experts(xs, block_e, n_used, moe_w_gate[0].astype(BF16), moe_w_up[0].astype(BF16),
                  moe_w_down[0].astype(BF16))
    y_final = _combine(x, meta, row(norm_final), yb, dest0, dest1)

    y_prompt = y_final[:mp].reshape(batch, seq, d)
    y_sample = y_final[mp:].reshape(dec_batch, dec_seq, d)

    def last_rows(a, width):
        ap = a[:mp].reshape(batch, seq, -1)[:, seq - width:]
        asmp = a[mp:].reshape(dec_batch, dec_seq, -1)[:, dec_seq - width:]
        return ap, asmp

    h_p, h_s = last_rows(h_all, 1)
    u_p, u_s = last_rows(proj[:, d_rnn:], conv_a - 1)
    cv_p, cv_s = last_rows(cv, conv_b - 1)
    return (y_prompt, y_sample, h_p[:, 0][None], u_p[None], cv_p[None],
            h_s[:, 0][None], u_s[None], cv_s[None])
```

```python
import functools

import jax
import jax.numpy as jnp
from jax import lax
from jax.experimental import pallas as pl
from jax.experimental.pallas import tpu as pltpu

F32 = jnp.float32
BF16 = jnp.bfloat16

EPS = 1e-6
RG_C = 8.0
TOP_K = 2

LANES = 128
SUBLANES = 8
ROW_TILE = 1024
OUT_ROW_TILE = 512
ROUTER_ROW_TILE = 512
COL_TILE = 512
W_COL_TILE = 256
MOE_ROW_TILE = 512
COMBINE_ROW_TILE = 256
SCAN_CHUNK = 64
VMEM_LIMIT = 56 * 1024 * 1024


def _cparams(*sem):
    return pltpu.CompilerParams(dimension_semantics=sem, vmem_limit_bytes=VMEM_LIMIT)


def _rms_norm_f32(x, g):
    y = x * lax.rsqrt(jnp.mean(x * x, axis=-1, keepdims=True) + EPS)
    return y * g


def _two_part_specs(n_first, block, col_of):
    first = pl.BlockSpec(block, lambda i, *a: (jnp.minimum(i, n_first - 1), col_of(*a)))
    second = pl.BlockSpec(block, lambda i, *a: (jnp.maximum(i - n_first, 0), col_of(*a)))
    return first, second


def _norm_matmul_kernel(n_first, xa_ref, xb_ref, g_ref, w_ref, o_ref, hn_ref):
    i = pl.program_id(0)
    j = pl.program_id(1)

    @pl.when((j == 0) & (i < n_first))
    def _():
        hn_ref[...] = _rms_norm_f32(xa_ref[...], g_ref[...]).astype(BF16)

    @pl.when((j == 0) & (i >= n_first))
    def _():
        hn_ref[...] = _rms_norm_f32(xb_ref[...], g_ref[...]).astype(BF16)

    o_ref[...] = jnp.dot(hn_ref[...], w_ref[...].astype(BF16), preferred_element_type=F32)


def _norm_matmul(xa, xb, g, w):
    d = xa.shape[1]
    m = xa.shape[0] + xb.shape[0]
    n = w.shape[1]
    xa_spec, xb_spec = _two_part_specs(xa.shape[0] // ROW_TILE, (ROW_TILE, d), lambda j: 0)
    return pl.pallas_call(
        functools.partial(_norm_matmul_kernel, xa.shape[0] // ROW_TILE),
        grid=(m // ROW_TILE, n // W_COL_TILE),
        in_specs=[xa_spec, xb_spec,
                  pl.BlockSpec((1, d), lambda i, j: (0, 0)),
                  pl.BlockSpec((d, W_COL_TILE), lambda i, j: (0, j))],
        out_specs=pl.BlockSpec((ROW_TILE, W_COL_TILE), lambda i, j: (i, j)),
        out_shape=jax.ShapeDtypeStruct((m, n), F32),
        scratch_shapes=[pltpu.VMEM((ROW_TILE, d), BF16)],
        compiler_params=_cparams("arbitrary", "arbitrary"),
        name="norm_matmul",
    )(xa, xb, g, w)


def _sc_inproj_kernel(x_ref, g_ref, wb_ref, wc_ref, wv_ref, b_ref, cv_ref, hn_ref):
    @pl.when(pl.program_id(1) == 0)
    def _():
        hn_ref[...] = _rms_norm_f32(x_ref[...], g_ref[...]).astype(BF16)

    hn = hn_ref[...]
    b_ref[...] = jnp.dot(hn, wb_ref[...].astype(BF16), preferred_element_type=F32)
    c = jnp.dot(hn, wc_ref[...].astype(BF16), preferred_element_type=F32)
    v = jnp.dot(hn, wv_ref[...].astype(BF16), preferred_element_type=F32)
    cv_ref[...] = c * v


def _sc_inproj(x, g, w):
    m, d = x.shape
    dc = w.shape[1] // 3
    nj = dc // W_COL_TILE
    wspec = lambda off: pl.BlockSpec((d, W_COL_TILE), lambda i, j: (0, off * nj + j))
    ospec = pl.BlockSpec((ROW_TILE, W_COL_TILE), lambda i, j: (i, j))
    return pl.pallas_call(
        _sc_inproj_kernel,
        grid=(m // ROW_TILE, nj),
        in_specs=[pl.BlockSpec((ROW_TILE, d), lambda i, j: (i, 0)),
                  pl.BlockSpec((1, d), lambda i, j: (0, 0)),
                  wspec(0), wspec(1), wspec(2)],
        out_specs=[ospec, ospec],
        out_shape=[jax.ShapeDtypeStruct((m, dc), F32)] * 2,
        scratch_shapes=[pltpu.VMEM((ROW_TILE, d), BF16)],
        compiler_params=_cparams("arbitrary", "arbitrary"),
        name="sc_inproj",
    )(x, g, w, w, w)


def _matmul_residual_kernel(n_first, y_ref, w_ref, ra_ref, rb_ref, o_ref):
    o_ref[...] = jnp.dot(y_ref[...], w_ref[...], preferred_element_type=F32)
    i = pl.program_id(0)

    @pl.when(i < n_first)
    def _():
        o_ref[...] += ra_ref[...]

    @pl.when(i >= n_first)
    def _():
        o_ref[...] += rb_ref[...]


def _matmul_residual(y, w, res_a, res_b):
    m, k = y.shape
    n = w.shape[1]
    tm = OUT_ROW_TILE
    ra_spec, rb_spec = _two_part_specs(res_a.shape[0] // tm, (tm, n), lambda: 0)
    return pl.pallas_call(
        functools.partial(_matmul_residual_kernel, res_a.shape[0] // tm),
        grid=(m // tm,),
        in_specs=[pl.BlockSpec((tm, k), lambda i: (i, 0)),
                  pl.BlockSpec((k, n), lambda i: (0, 0)),
                  ra_spec, rb_spec],
        out_specs=pl.BlockSpec((tm, n), lambda i: (i, 0)),
        out_shape=jax.ShapeDtypeStruct((m, n), F32),
        compiler_params=_cparams("arbitrary"),
        name="matmul_residual",
    )(y, w, res_a, res_b)


def _seq_pos(i, r0, rows, n_prompt_tiles, seq_p, seq_s):
    seqlen = jnp.where(i < n_prompt_tiles, seq_p, seq_s)
    grow = i * ROW_TILE + r0 + lax.broadcasted_iota(jnp.int32, (rows, 1), 0)
    return grow & (seqlen - 1)


def _rglru_kernel(n_prompt_tiles, seq_p, seq_s,
                  gate_ref, u_ref, p1_ref, p2_ref, p3_ref, h0_ref,
                  cw_ref, cb_ref, br_ref, bi_ref, lam_ref, wr_ref, wi_ref,
                  y_ref, h_ref, ubuf, uc_ref, rp_ref, ip_ref, hc_ref):
    i = pl.program_id(1)
    cblk = u_ref.shape[1]

    @pl.when(i == 0)
    def _():
        ubuf[0:SUBLANES, :] = jnp.zeros((SUBLANES, cblk), F32)
        hc_ref[...] = jnp.zeros_like(hc_ref)

    ubuf[SUBLANES:, :] = u_ref[...]
    is_sample = i >= n_prompt_tiles
    cw = cw_ref[...]
    cb = cb_ref[...]
    rc = SCAN_CHUNK

    def conv_chunk(k, carry):
        r0 = pl.multiple_of(k * rc, rc)
        rows = pl.ds(r0, rc)
        t = _seq_pos(i, r0, rc, n_prompt_tiles, seq_p, seq_s)
        ext = ubuf[pl.ds(r0, rc + SUBLANES), :]

        def tap(s, p_ref):
            shifted = pltpu.roll(ext, s, axis=0)[SUBLANES:]
            return jnp.where(t >= s, shifted, jnp.where(is_sample, p_ref[rows, :], 0.0))

        conv = tap(3, p3_ref) * cw[0:1] + tap(2, p2_ref) * cw[1:2]
        conv = conv + tap(1, p1_ref) * cw[2:3]
        conv = conv + ext[SUBLANES:] * cw[3:4]
        uc_ref[rows, :] = conv + cb
        return carry

    lax.fori_loop(0, ROW_TILE // rc, conv_chunk, 0)

    ub = uc_ref[...].astype(BF16)
    rp_ref[...] = jnp.dot(ub, wr_ref[...], preferred_element_type=F32) + br_ref[...]
    ip_ref[...] = jnp.dot(ub, wi_ref[...], preferred_element_type=F32) + bi_ref[...]

    sp = jax.nn.softplus(-lam_ref[...])
    sub = lax.broadcasted_iota(jnp.int32, (1, SUBLANES, 1), 1)

    def scan_chunk(k, carry):
        r0 = pl.multiple_of(k * rc, rc)
        rows = pl.ds(r0, rc)
        t = _seq_pos(i, r0, rc, n_prompt_tiles, seq_p, seq_s)
        uc = uc_ref[rows, :]
        r = jax.nn.sigmoid(rp_ref[rows, :])
        ig = jax.nn.sigmoid(ip_ref[rows, :])
        log_a = (-RG_C * r) * sp
        a = jnp.exp(log_a)
        mult = jnp.sqrt(-jnp.tanh(log_a) * (a * a + 1.0))
        b = mult * ig * uc
        first = t == 0
        b = jnp.where(first & is_sample, b + a * h0_ref[rows, :], b)
        a = jnp.where(first, 0.0, a)

        a3 = a.reshape(rc // SUBLANES, SUBLANES, cblk)
        b3 = b.reshape(rc // SUBLANES, SUBLANES, cblk)
        for s in (1, 2, 4):
            a_s = pltpu.roll(a3, s, axis=1)
            b_s = pltpu.roll(b3, s, axis=1)
            m = sub >= s
            b3 = jnp.where(m, a3 * b_s + b3, b3)
            a3 = jnp.where(m, a3 * a_s, a3)

        hc = hc_ref[...]
        hs = []
        for g in range(rc // SUBLANES):
            hg = a3[g] * hc + b3[g]
            hc = jnp.broadcast_to(hg[SUBLANES - 1:SUBLANES], (SUBLANES, cblk))
            hs.append(hg)
        hc_ref[...] = hc
        h = jnp.concatenate(hs, axis=0)
        h_ref[rows, :] = h
        y_ref[rows, :] = (h * jax.nn.gelu(gate_ref[rows, :])).astype(BF16)
        return carry

    lax.fori_loop(0, ROW_TILE // rc, scan_chunk, 0)
    ubuf[0:SUBLANES, :] = ubuf[ROW_TILE:ROW_TILE + SUBLANES, :]


def _rglru_core(proj, aux, conv_w, conv_b, b_r, b_i, lam, w_r, w_i, n_prompt_tiles, seq_p, seq_s):
    m = proj.shape[0]
    nb, cblk = w_r.shape[0], w_r.shape[1]
    c = nb * cblk
    n_tiles = m // ROW_TILE
    row_blk = lambda off: pl.BlockSpec((ROW_TILE, cblk), lambda cb, i: (i, off + cb))
    aux_blk = pl.BlockSpec((ROW_TILE, cblk), lambda cb, i: (0, cb))
    vec_blk = lambda rows: pl.BlockSpec((rows, cblk), lambda cb, i: (0, cb))
    w_blk = pl.BlockSpec((None, cblk, cblk), lambda cb, i: (cb, 0, 0))
    out_blk = pl.BlockSpec((ROW_TILE, cblk), lambda cb, i: (i, cb))
    return pl.pallas_call(
        functools.partial(_rglru_kernel, n_prompt_tiles, seq_p, seq_s),
        grid=(nb, n_tiles),
        in_specs=[row_blk(0), row_blk(nb), aux_blk, aux_blk, aux_blk, aux_blk,
                  vec_blk(conv_w.shape[0]), vec_blk(1), vec_blk(1), vec_blk(1), vec_blk(1),
                  w_blk, w_blk],
        out_specs=[out_blk, out_blk],
        out_shape=[jax.ShapeDtypeStruct((m, c), BF16), jax.ShapeDtypeStruct((m, c), F32)],
        scratch_shapes=[pltpu.VMEM((ROW_TILE + SUBLANES, cblk), F32),
                        pltpu.VMEM((ROW_TILE, cblk), F32),
                        pltpu.VMEM((ROW_TILE, cblk), F32),
                        pltpu.VMEM((ROW_TILE, cblk), F32),
                        pltpu.VMEM((SUBLANES, cblk), F32)],
        compiler_params=_cparams("arbitrary", "arbitrary"),
        name="rglru_core",
    )(proj, proj, *aux, conv_w, conv_b, b_r, b_i, lam, w_r, w_i)


def _sconv_kernel(n_prompt_tiles, seq_p, seq_s,
                  b_ref, cv_ref, q1_ref, q2_ref, cw_ref, o_ref, cvbuf):
    i = pl.program_id(1)
    cblk = cv_ref.shape[1]

    @pl.when(i == 0)
    def _():
        cvbuf[0:SUBLANES, :] = jnp.zeros((SUBLANES, cblk), F32)

    cvbuf[SUBLANES:, :] = cv_ref[...]
    is_sample = i >= n_prompt_tiles
    cw = cw_ref[...]
    rc = SCAN_CHUNK

    def conv_chunk(k, carry):
        r0 = pl.multiple_of(k * rc, rc)
        rows = pl.ds(r0, rc)
        t = _seq_pos(i, r0, rc, n_prompt_tiles, seq_p, seq_s)
        ext = cvbuf[pl.ds(r0, rc + SUBLANES), :]

        def tap(s, q_ref):
            shifted = pltpu.roll(ext, s, axis=0)[SUBLANES:]
            return jnp.where(t >= s, shifted, jnp.where(is_sample, q_ref[rows, :], 0.0))

        z = tap(2, q2_ref) * cw[0:1] + tap(1, q1_ref) * cw[1:2]
        z = z + ext[SUBLANES:] * cw[2:3]
        o_ref[rows, :] = (b_ref[rows, :] * z).astype(BF16)
        return carry

    lax.fori_loop(0, ROW_TILE // rc, conv_chunk, 0)
    cvbuf[0:SUBLANES, :] = cvbuf[ROW_TILE:ROW_TILE + SUBLANES, :]


def _sconv_core(bgate, cv, aux, conv_w, n_prompt_tiles, seq_p, seq_s):
    m, c = cv.shape
    cblk = COL_TILE
    blk = pl.BlockSpec((ROW_TILE, cblk), lambda cb, i: (i, cb))
    aux_blk = pl.BlockSpec((ROW_TILE, cblk), lambda cb, i: (0, cb))
    return pl.pallas_call(
        functools.partial(_sconv_kernel, n_prompt_tiles, seq_p, seq_s),
        grid=(c // cblk, m // ROW_TILE),
        in_specs=[blk, blk, aux_blk, aux_blk,
                  pl.BlockSpec((conv_w.shape[0], cblk), lambda cb, i: (0, cb))],
        out_specs=blk,
        out_shape=jax.ShapeDtypeStruct((m, c), BF16),
        scratch_shapes=[pltpu.VMEM((ROW_TILE + SUBLANES, cblk), F32)],
        compiler_params=_cparams("arbitrary", "arbitrary"),
        name="sconv_core",
    )(bgate, cv, *aux, conv_w)


def _ffn_kernel(x_ref, g_ref, wg_ref, wu_ref, wd_ref, o_ref, hn_ref):
    @pl.when(pl.program_id(1) == 0)
    def _():
        x = x_ref[...]
        hn_ref[...] = _rms_norm_f32(x, g_ref[...]).astype(BF16)
        o_ref[...] = x

    hn = hn_ref[...]
    a = jnp.dot(hn, wg_ref[...].astype(BF16), preferred_element_type=F32)
    b = jnp.dot(hn, wu_ref[...].astype(BF16), preferred_element_type=F32)
    h = (jax.nn.silu(a) * b).astype(BF16)
    o_ref[...] += jnp.dot(h, wd_ref[...].astype(BF16), preferred_element_type=F32)


def _ffn(x, g, wg, wu, wd):
    m, d = x.shape
    f = wg.shape[1]
    tm = ROW_TILE
    tf = W_COL_TILE
    return pl.pallas_call(
        _ffn_kernel,
        grid=(m // tm, f // tf),
        in_specs=[pl.BlockSpec((tm, d), lambda i, j: (i, 0)),
                  pl.BlockSpec((1, d), lambda i, j: (0, 0)),
                  pl.BlockSpec((d, tf), lambda i, j: (0, j)),
                  pl.BlockSpec((d, tf), lambda i, j: (0, j)),
                  pl.BlockSpec((tf, d), lambda i, j: (j, 0))],
        out_specs=pl.BlockSpec((tm, d), lambda i, j: (i, 0)),
        out_shape=jax.ShapeDtypeStruct((m, d), F32),
        scratch_shapes=[pltpu.VMEM((tm, d), BF16)],
        compiler_params=_cparams("arbitrary", "arbitrary"),
        name="ffn",
    )(x, g, wg, wu, wd)


def _router_kernel(n_experts, x_ref, g_ref, wr_ref, hn_ref, meta_ref, cnt_ref, run_ref):
    @pl.when(pl.program_id(0) == 0)
    def _():
        run_ref[...] = jnp.zeros_like(run_ref)

    tm = x_ref.shape[0]
    hn = _rms_norm_f32(x_ref[...], g_ref[...])
    hn_ref[...] = hn
    logits = jnp.dot(hn, wr_ref[...], preferred_element_type=F32,
                     precision=lax.Precision.HIGHEST)
    lane = lax.broadcasted_iota(jnp.int32, (tm, LANES), 1).astype(F32)
    neg = jnp.float32(-jnp.inf)
    lg = jnp.where(lane < n_experts, logits, neg)
    m1 = jnp.max(lg, axis=1, keepdims=True)
    i1 = jnp.min(jnp.where(lg == m1, lane, float(LANES)), axis=1, keepdims=True)
    lg2 = jnp.where(lane == i1, neg, lg)
    m2 = jnp.max(lg2, axis=1, keepdims=True)
    i2 = jnp.min(jnp.where(lg2 == m2, lane, float(LANES)), axis=1, keepdims=True)
    ex = jnp.exp(m2 - m1)
    g1 = 1.0 / (1.0 + ex)
    g2 = ex / (1.0 + ex)

    sel1 = lane == i1
    sel2 = lane == i2
    onehot = jnp.where(sel1 | sel2, 1.0, 0.0)
    rr = lax.broadcasted_iota(jnp.int32, (tm, tm), 0)
    cc = lax.broadcasted_iota(jnp.int32, (tm, tm), 1)
    tri = jnp.where(rr > cc, 1.0, 0.0).astype(BF16)
    before = jnp.dot(tri, onehot.astype(BF16), preferred_element_type=F32) + run_ref[0:1, :]
    rank1 = jnp.sum(jnp.where(sel1, before, 0.0), axis=1, keepdims=True)
    rank2 = jnp.sum(jnp.where(sel2, before, 0.0), axis=1, keepdims=True)
    run = run_ref[0:1, :] + jnp.sum(onehot, axis=0, keepdims=True)
    run_ref[...] = jnp.broadcast_to(run, run_ref.shape)
    cnt_ref[...] = jnp.broadcast_to(run, cnt_ref.shape)

    meta = jnp.where(lane == 0, i1, 0.0)
    meta = jnp.where(lane == 1, i2, meta)
    meta = jnp.where(lane == 2, g1, meta)
    meta = jnp.where(lane == 3, g2, meta)
    meta = jnp.where(lane == 4, rank1, meta)
    meta = jnp.where(lane == 5, rank2, meta)
    meta_ref[...] = meta


def _router(x, g, w_router_padded, n_experts):
    m, d = x.shape
    tm = ROUTER_ROW_TILE
    return pl.pallas_call(
        functools.partial(_router_kernel, n_experts),
        grid=(m // tm,),
        in_specs=[pl.BlockSpec((tm, d), lambda i: (i, 0)),
                  pl.BlockSpec((1, d), lambda i: (0, 0)),
                  pl.BlockSpec((d, LANES), lambda i: (0, 0))],
        out_specs=[pl.BlockSpec((tm, d), lambda i: (i, 0)),
                   pl.BlockSpec((tm, LANES), lambda i: (i, 0)),
                   pl.BlockSpec((SUBLANES, LANES), lambda i: (0, 0))],
        out_shape=[jax.ShapeDtypeStruct((m, d), F32),
                   jax.ShapeDtypeStruct((m, LANES), F32),
                   jax.ShapeDtypeStruct((SUBLANES, LANES), F32)],
        scratch_shapes=[pltpu.VMEM((SUBLANES, LANES), F32)],
        compiler_params=_cparams("arbitrary"),
        name="router",
    )(x, g, w_router_padded)


def _row_copy(src_hbm, idx, dst_ref, r, sem):
    return pltpu.make_async_copy(src_hbm.at[pl.ds(idx, 1)], dst_ref.at[pl.ds(r, 1)], sem)


def _dispatch_kernel(nused_ref, tok_ref, hn_hbm, o_ref, buf, sem):
    t = pl.program_id(0)
    tr = buf.shape[0]

    @pl.when(t < nused_ref[0])
    def _():
        def issue(r, c):
            _row_copy(hn_hbm, tok_ref[0, 0, r], buf, r, sem).start()
            return c

        lax.fori_loop(0, tr, issue, 0, unroll=8)
        pltpu.make_async_copy(hn_hbm.at[pl.ds(0, tr)], buf, sem).wait()
        o_ref[...] = buf[...].astype(BF16)

    @pl.when(t >= nused_ref[0])
    def _():
        o_ref[...] = jnp.zeros_like(o_ref)


def _dispatch(hn, row_tok, n_used):
    m, d = hn.shape
    tr = MOE_ROW_TILE
    n_tiles = row_tok.shape[0] // tr
    return pl.pallas_call(
        _dispatch_kernel,
        grid_spec=pltpu.PrefetchScalarGridSpec(
            num_scalar_prefetch=1,
            grid=(n_tiles,),
            in_specs=[pl.BlockSpec((1, 1, tr), lambda t, nu: (t, 0, 0), memory_space=pltpu.SMEM),
                      pl.BlockSpec(memory_space=pl.ANY)],
            out_specs=pl.BlockSpec((tr, d), lambda t, nu: (t, 0)),
            scratch_shapes=[pltpu.VMEM((tr, d), F32), pltpu.SemaphoreType.DMA(())]),
        out_shape=jax.ShapeDtypeStruct((n_tiles * tr, d), BF16),
        compiler_params=_cparams("arbitrary"),
        name="moe_dispatch",
    )(n_used, row_tok.reshape(n_tiles, 1, tr), hn)


def _experts_kernel(be_ref, nused_ref, x_ref, wg_ref, wu_ref, wd_ref, o_ref):
    t = pl.program_id(0)
    j = pl.program_id(1)

    @pl.when(j == 0)
    def _():
        o_ref[...] = jnp.zeros_like(o_ref)

    @pl.when(t < nused_ref[0])
    def _():
        x = x_ref[...]
        a = jnp.dot(x, wg_ref[...], preferred_element_type=F32)
        b = jnp.dot(x, wu_ref[...], preferred_element_type=F32)
        h = (jax.nn.silu(a) * b).astype(BF16)
        o_ref[...] += jnp.dot(h, wd_ref[...], preferred_element_type=F32)


def _experts(xs, block_e, n_used, wg, wu, wd):
    n_rows, d = xs.shape
    f = wg.shape[2]
    tr = MOE_ROW_TILE
    n_tiles = n_rows // tr
    nj = f // COL_TILE

    def tile(t, nu):
        return jnp.minimum(t, nu[0] - 1)

    def col(t, j, nu):
        return jnp.where(t < nu[0], j, nj - 1)

    return pl.pallas_call(
        _experts_kernel,
        grid_spec=pltpu.PrefetchScalarGridSpec(
            num_scalar_prefetch=2,
            grid=(n_tiles, nj),
            in_specs=[pl.BlockSpec((tr, d), lambda t, j, be, nu: (tile(t, nu), 0)),
                      pl.BlockSpec((None, d, COL_TILE),
                                   lambda t, j, be, nu: (be[tile(t, nu)], 0, col(t, j, nu))),
                      pl.BlockSpec((None, d, COL_TILE),
                                   lambda t, j, be, nu: (be[tile(t, nu)], 0, col(t, j, nu))),
                      pl.BlockSpec((None, COL_TILE, d),
                                   lambda t, j, be, nu: (be[tile(t, nu)], col(t, j, nu), 0))],
            out_specs=pl.BlockSpec((tr, d), lambda t, j, be, nu: (t, 0))),
        out_shape=jax.ShapeDtypeStruct((n_rows, d), F32),
        compiler_params=_cparams("arbitrary", "arbitrary"),
        name="moe_experts",
    )(block_e, n_used, xs, wg, wu, wd)


def _combine_kernel(n_first, d0_ref, d1_ref, x_ref, meta_ref, g_ref, yb_hbm, oa_ref, ob_ref,
                    y0, y1, sem):
    tm = x_ref.shape[0]
    i = pl.program_id(0)

    def issue(r, c):
        _row_copy(yb_hbm, d0_ref[0, 0, r], y0, r, sem.at[0]).start()
        _row_copy(yb_hbm, d1_ref[0, 0, r], y1, r, sem.at[1]).start()
        return c

    lax.fori_loop(0, tm, issue, 0, unroll=8)
    pltpu.make_async_copy(yb_hbm.at[pl.ds(0, tm)], y0, sem.at[0]).wait()
    pltpu.make_async_copy(yb_hbm.at[pl.ds(0, tm)], y1, sem.at[1]).wait()
    meta = meta_ref[...]
    x = x_ref[...] + (y0[...] * meta[:, 2:3] + y1[...] * meta[:, 3:4])

    @pl.when(i < n_first)
    def _():
        oa_ref[...] = _rms_norm_f32(x, g_ref[...])

    @pl.when(i >= n_first)
    def _():
        ob_ref[...] = _rms_norm_f32(x, g_ref[...])


def _combine(x, meta, g, yb, dest0, dest1, m_first):
    m, d = x.shape
    tm = COMBINE_ROW_TILE
    n_tiles = m // tm
    smem_blk = pl.BlockSpec((1, 1, tm), lambda i: (i, 0, 0), memory_space=pltpu.SMEM)
    oa_spec, ob_spec = _two_part_specs(m_first // tm, (tm, d), lambda: 0)
    return pl.pallas_call(
        functools.partial(_combine_kernel, m_first // tm),
        grid=(n_tiles,),
        in_specs=[smem_blk, smem_blk,
                  pl.BlockSpec((tm, d), lambda i: (i, 0)),
                  pl.BlockSpec((tm, LANES), lambda i: (i, 0)),
                  pl.BlockSpec((1, d), lambda i: (0, 0)),
                  pl.BlockSpec(memory_space=pl.ANY)],
        out_specs=[oa_spec, ob_spec],
        out_shape=[jax.ShapeDtypeStruct((m_first, d), F32),
                   jax.ShapeDtypeStruct((m - m_first, d), F32)],
        scratch_shapes=[pltpu.VMEM((tm, d), F32), pltpu.VMEM((tm, d), F32),
                        pltpu.SemaphoreType.DMA((2,))],
        compiler_params=_cparams("arbitrary"),
        name="moe_combine",
    )(dest0.reshape(n_tiles, 1, tm), dest1.reshape(n_tiles, 1, tm), x, meta, g, yb)


def _sample_aux(state, shift, dec_seq):
    b, w1, c = state.shape
    rows = [state[:, w1 + t - shift] if t < shift else jnp.zeros((b, c), state.dtype)
            for t in range(dec_seq)]
    return jnp.stack(rows, axis=1).reshape(b * dec_seq, c)


def _seq_tails(a, col0, ncol, width, batch, seq, dec_batch, dec_seq):
    mp = batch * seq
    tails = [lax.slice(a, ((b + 1) * seq - width, col0), ((b + 1) * seq, col0 + ncol))
             for b in range(batch)]
    smp = lax.slice(a, (mp, col0), (mp + dec_batch * dec_seq, col0 + ncol))
    smp = smp.reshape(dec_batch, dec_seq, ncol)[:, dec_seq - width:]
    return jnp.stack(tails, axis=0), smp


def kernel(x_prompt, x_sample, state_lru_h, state_lru_conv, state_sconv, norm_mix, norm_ffn, norm_final, lru_w_in, lru_conv_w, lru_conv_b, lru_w_rgate, lru_b_rgate, lru_w_igate, lru_b_igate, lru_lambda, lru_w_out, sc_w_in, sc_conv_w, sc_w_out, ffn_w_gate, ffn_w_up, ffn_w_down, moe_w_router, moe_w_gate, moe_w_up, moe_w_down):
    batch, seq, d = x_prompt.shape
    dec_batch, dec_seq, _ = x_sample.shape
    mp, ms = batch * seq, dec_batch * dec_seq
    m = mp + ms
    depth = norm_mix.shape[0]
    n_experts = moe_w_router.shape[2]
    conv_a = lru_conv_w.shape[1]
    conv_b = sc_conv_w.shape[1]
    assert depth == 2 and lru_w_in.shape[0] == 1 and sc_w_in.shape[0] == 1
    assert seq & (seq - 1) == 0 and dec_seq & (dec_seq - 1) == 0
    assert seq % ROW_TILE == 0 and ms == ROW_TILE and ROW_TILE % dec_seq == 0
    assert dec_seq >= conv_a - 1 and dec_seq >= conv_b - 1 and dec_seq == SUBLANES
    assert n_experts <= LANES and m % ROUTER_ROW_TILE == 0 and m % COMBINE_ROW_TILE == 0
    n_prompt_tiles = mp // ROW_TILE

    xp = x_prompt.reshape(mp, d)
    xs_in = x_sample.reshape(ms, d)
    row = lambda v: v.reshape(1, -1)

    d_rnn = lru_w_out.shape[1]
    proj = _norm_matmul(xp, xs_in, row(norm_mix[0]), lru_w_in[0])
    h0_rows = jnp.pad(state_lru_h[0][:, None, :], ((0, 0), (0, dec_seq - 1), (0, 0)))
    aux = [_sample_aux(state_lru_conv[0], s, dec_seq) for s in (1, 2, 3)]
    aux.append(h0_rows.reshape(ms, d_rnn))
    y, h_all = _rglru_core(proj, aux, lru_conv_w[0], row(lru_conv_b[0]), row(lru_b_rgate[0]),
                           row(lru_b_igate[0]), row(lru_lambda[0]),
                           lru_w_rgate[0].astype(BF16), lru_w_igate[0].astype(BF16),
                           n_prompt_tiles, seq, dec_seq)
    x = _matmul_residual(y, lru_w_out[0].astype(BF16), xp, xs_in)

    x = _ffn(x, row(norm_ffn[0]), ffn_w_gate[0], ffn_w_up[0], ffn_w_down[0])

    bgate, cv = _sc_inproj(x, row(norm_mix[1]), sc_w_in[0])
    aux = [_sample_aux(state_sconv[0], s, dec_seq) for s in (1, 2)]
    zb = _sconv_core(bgate, cv, aux, sc_conv_w[0], n_prompt_tiles, seq, dec_seq)
    x = _matmul_residual(zb, sc_w_out[0].astype(BF16), x, x)

    w_router = jnp.pad(moe_w_router[0], ((0, 0), (0, LANES - n_experts)))
    hn, meta, counts = _router(x, row(norm_ffn[1]), w_router, n_experts)
    tr = MOE_ROW_TILE
    n_tiles = -(-(m * TOP_K) // tr) + n_experts
    counts = counts[0, :n_experts].astype(jnp.int32)
    padded = (counts + tr - 1) // tr * tr
    pad_end = jnp.cumsum(padded)
    pad_start = pad_end - padded
    e0 = meta[:, 0].astype(jnp.int32)
    e1 = meta[:, 1].astype(jnp.int32)
    dest0 = pad_start[e0] + meta[:, 4].astype(jnp.int32)
    dest1 = pad_start[e1] + meta[:, 5].astype(jnp.int32)
    tok = jnp.arange(m, dtype=jnp.int32)
    row_tok = jnp.zeros((n_tiles * tr,), jnp.int32).at[jnp.concatenate([dest0, dest1])].set(
        jnp.concatenate([tok, tok]), unique_indices=True, mode='promise_in_bounds')
    block_start = jnp.arange(n_tiles, dtype=jnp.int32) * tr
    block_e = jnp.minimum(jnp.searchsorted(pad_end, block_start, side='right'),
                          n_experts - 1).astype(jnp.int32)
    n_used = (pad_end[-1:] // tr).astype(jnp.int32)
    xs = _dispatch(hn, row_tok, n_used)
    yb = _experts(xs, block_e, n_used, moe_w_gate[0].astype(BF16), moe_w_up[0].astype(BF16),
                  moe_w_down[0].astype(BF16))
    y_prompt, y_sample = _combine(x, meta, row(norm_final), yb, dest0, dest1, mp)

    tails = functools.partial(_seq_tails, batch=batch, seq=seq, dec_batch=dec_batch, dec_seq=dec_seq)
    h_p, h_s = tails(h_all, 0, d_rnn, 1)
    u_p, u_s = tails(proj, d_rnn, d_rnn, conv_a - 1)
    cv_p, cv_s = tails(cv, 0, cv.shape[1], conv_b - 1)
    return (y_prompt.reshape(batch, seq, d), y_sample.reshape(dec_batch, dec_seq, d),
            h_p[:, 0][None], u_p[None], cv_p[None], h_s[:, 0][None], u_s[None], cv_s[None])
```

```python
import functools

import jax
import jax.numpy as jnp
from jax import lax
from jax.experimental import pallas as pl
from jax.experimental.pallas import tpu as pltpu

F32 = jnp.float32
BF16 = jnp.bfloat16

EPS = 1e-6
RG_C = 8.0
TOP_K = 2

LANES = 128
SUBLANES = 8
ROW_TILE = 1024
OUT_ROW_TILE = 512
ROUTER_ROW_TILE = 512
COL_TILE = 512
W_COL_TILE = 256
MOE_ROW_TILE = 1024
COMBINE_ROW_TILE = 256
SCAN_CHUNK = 64
VMEM_LIMIT = 56 * 1024 * 1024


def _cparams(*sem):
    return pltpu.CompilerParams(dimension_semantics=sem, vmem_limit_bytes=VMEM_LIMIT)


def _rms_norm_f32(x, g):
    y = x * lax.rsqrt(jnp.mean(x * x, axis=-1, keepdims=True) + EPS)
    return y * g


def _two_part_specs(n_first, block, col_of):
    first = pl.BlockSpec(block, lambda i, *a: (jnp.minimum(i, n_first - 1), col_of(*a)))
    second = pl.BlockSpec(block, lambda i, *a: (jnp.maximum(i - n_first, 0), col_of(*a)))
    return first, second


def _norm_matmul_kernel(n_first, xa_ref, xb_ref, g_ref, w_ref, o_ref, hn_ref):
    i = pl.program_id(0)
    j = pl.program_id(1)

    @pl.when((j == 0) & (i < n_first))
    def _():
        hn_ref[...] = _rms_norm_f32(xa_ref[...], g_ref[...]).astype(BF16)

    @pl.when((j == 0) & (i >= n_first))
    def _():
        hn_ref[...] = _rms_norm_f32(xb_ref[...], g_ref[...]).astype(BF16)

    o_ref[...] = jnp.dot(hn_ref[...], w_ref[...].astype(BF16), preferred_element_type=F32)


def _norm_matmul(xa, xb, g, w):
    d = xa.shape[1]
    m = xa.shape[0] + xb.shape[0]
    n = w.shape[1]
    xa_spec, xb_spec = _two_part_specs(xa.shape[0] // ROW_TILE, (ROW_TILE, d), lambda j: 0)
    return pl.pallas_call(
        functools.partial(_norm_matmul_kernel, xa.shape[0] // ROW_TILE),
        grid=(m // ROW_TILE, n // W_COL_TILE),
        in_specs=[xa_spec, xb_spec,
                  pl.BlockSpec((1, d), lambda i, j: (0, 0)),
                  pl.BlockSpec((d, W_COL_TILE), lambda i, j: (0, j))],
        out_specs=pl.BlockSpec((ROW_TILE, W_COL_TILE), lambda i, j: (i, j)),
        out_shape=jax.ShapeDtypeStruct((m, n), F32),
        scratch_shapes=[pltpu.VMEM((ROW_TILE, d), BF16)],
        compiler_params=_cparams("arbitrary", "arbitrary"),
        name="norm_matmul",
    )(xa, xb, g, w)


def _sc_inproj_kernel(x_ref, g_ref, wb_ref, wc_ref, wv_ref, b_ref, cv_ref, hn_ref):
    @pl.when(pl.program_id(1) == 0)
    def _():
        hn_ref[...] = _rms_norm_f32(x_ref[...], g_ref[...]).astype(BF16)

    hn = hn_ref[...]
    b_ref[...] = jnp.dot(hn, wb_ref[...].astype(BF16), preferred_element_type=F32)
    c = jnp.dot(hn, wc_ref[...].astype(BF16), preferred_element_type=F32)
    v = jnp.dot(hn, wv_ref[...].astype(BF16), preferred_element_type=F32)
    cv_ref[...] = c * v


def _sc_inproj(x, g, w):
    m, d = x.shape
    dc = w.shape[1] // 3
    nj = dc // W_COL_TILE
    wspec = lambda off: pl.BlockSpec((d, W_COL_TILE), lambda i, j: (0, off * nj + j))
    ospec = pl.BlockSpec((ROW_TILE, W_COL_TILE), lambda i, j: (i, j))
    return pl.pallas_call(
        _sc_inproj_kernel,
        grid=(m // ROW_TILE, nj),
        in_specs=[pl.BlockSpec((ROW_TILE, d), lambda i, j: (i, 0)),
                  pl.BlockSpec((1, d), lambda i, j: (0, 0)),
                  wspec(0), wspec(1), wspec(2)],
        out_specs=[ospec, ospec],
        out_shape=[jax.ShapeDtypeStruct((m, dc), F32)] * 2,
        scratch_shapes=[pltpu.VMEM((ROW_TILE, d), BF16)],
        compiler_params=_cparams("arbitrary", "arbitrary"),
        name="sc_inproj",
    )(x, g, w, w, w)


def _matmul_residual_kernel(n_first, y_ref, w_ref, ra_ref, rb_ref, o_ref):
    o_ref[...] = jnp.dot(y_ref[...], w_ref[...], preferred_element_type=F32)
    i = pl.program_id(0)

    @pl.when(i < n_first)
    def _():
        o_ref[...] += ra_ref[...]

    @pl.when(i >= n_first)
    def _():
        o_ref[...] += rb_ref[...]


def _matmul_residual(y, w, res_a, res_b):
    m, k = y.shape
    n = w.shape[1]
    tm = OUT_ROW_TILE
    ra_spec, rb_spec = _two_part_specs(res_a.shape[0] // tm, (tm, n), lambda: 0)
    return pl.pallas_call(
        functools.partial(_matmul_residual_kernel, res_a.shape[0] // tm),
        grid=(m // tm,),
        in_specs=[pl.BlockSpec((tm, k), lambda i: (i, 0)),
                  pl.BlockSpec((k, n), lambda i: (0, 0)),
                  ra_spec, rb_spec],
        out_specs=pl.BlockSpec((tm, n), lambda i: (i, 0)),
        out_shape=jax.ShapeDtypeStruct((m, n), F32),
        compiler_params=_cparams("arbitrary"),
        name="matmul_residual",
    )(y, w, res_a, res_b)


def _seq_pos(i, r0, rows, n_prompt_tiles, seq_p, seq_s):
    seqlen = jnp.where(i < n_prompt_tiles, seq_p, seq_s)
    grow = i * ROW_TILE + r0 + lax.broadcasted_iota(jnp.int32, (rows, 1), 0)
    return grow & (seqlen - 1)


def _rglru_kernel(n_prompt_tiles, seq_p, seq_s,
                  gate_ref, u_ref, p1_ref, p2_ref, p3_ref, h0_ref,
                  cw_ref, cb_ref, br_ref, bi_ref, lam_ref, wr_ref, wi_ref,
                  y_ref, h_ref, ubuf, uc_ref, rp_ref, ip_ref, hc_ref):
    i = pl.program_id(1)
    cblk = u_ref.shape[1]

    @pl.when(i == 0)
    def _():
        ubuf[0:SUBLANES, :] = jnp.zeros((SUBLANES, cblk), F32)
        hc_ref[...] = jnp.zeros_like(hc_ref)

    ubuf[SUBLANES:, :] = u_ref[...]
    is_sample = i >= n_prompt_tiles
    cw = cw_ref[...]
    cb = cb_ref[...]
    rc = SCAN_CHUNK

    def conv_chunk(k, carry):
        r0 = pl.multiple_of(k * rc, rc)
        rows = pl.ds(r0, rc)
        t = _seq_pos(i, r0, rc, n_prompt_tiles, seq_p, seq_s)
        ext = ubuf[pl.ds(r0, rc + SUBLANES), :]

        def tap(s, p_ref):
            shifted = pltpu.roll(ext, s, axis=0)[SUBLANES:]
            return jnp.where(t >= s, shifted, jnp.where(is_sample, p_ref[rows, :], 0.0))

        conv = tap(3, p3_ref) * cw[0:1] + tap(2, p2_ref) * cw[1:2]
        conv = conv + tap(1, p1_ref) * cw[2:3]
        conv = conv + ext[SUBLANES:] * cw[3:4]
        uc_ref[rows, :] = conv + cb
        return carry

    lax.fori_loop(0, ROW_TILE // rc, conv_chunk, 0)

    ub = uc_ref[...].astype(BF16)
    rp_ref[...] = jnp.dot(ub, wr_ref[...], preferred_element_type=F32) + br_ref[...]
    ip_ref[...] = jnp.dot(ub, wi_ref[...], preferred_element_type=F32) + bi_ref[...]

    sp = jax.nn.softplus(-lam_ref[...])
    sub = lax.broadcasted_iota(jnp.int32, (1, SUBLANES, 1), 1)

    def scan_chunk(k, carry):
        r0 = pl.multiple_of(k * rc, rc)
        rows = pl.ds(r0, rc)
        t = _seq_pos(i, r0, rc, n_prompt_tiles, seq_p, seq_s)
        uc = uc_ref[rows, :]
        r = jax.nn.sigmoid(rp_ref[rows, :])
        ig = jax.nn.sigmoid(ip_ref[rows, :])
        log_a = (-RG_C * r) * sp
        a = jnp.exp(log_a)
        mult = jnp.sqrt(-jnp.tanh(log_a) * (a * a + 1.0))
        b = mult * ig * uc
        first = t == 0
        b = jnp.where(first & is_sample, b + a * h0_ref[rows, :], b)
        a = jnp.where(first, 0.0, a)

        a3 = a.reshape(rc // SUBLANES, SUBLANES, cblk)
        b3 = b.reshape(rc // SUBLANES, SUBLANES, cblk)
        for s in (1, 2, 4):
            a_s = pltpu.roll(a3, s, axis=1)
            b_s = pltpu.roll(b3, s, axis=1)
            m = sub >= s
            b3 = jnp.where(m, a3 * b_s + b3, b3)
            a3 = jnp.where(m, a3 * a_s, a3)

        hc = hc_ref[...]
        hs = []
        for g in range(rc // SUBLANES):
            hg = a3[g] * hc + b3[g]
            hc = jnp.broadcast_to(hg[SUBLANES - 1:SUBLANES], (SUBLANES, cblk))
            hs.append(hg)
        hc_ref[...] = hc
        h = jnp.concatenate(hs, axis=0)
        h_ref[rows, :] = h
        y_ref[rows, :] = (h * jax.nn.gelu(gate_ref[rows, :])).astype(BF16)
        return carry

    lax.fori_loop(0, ROW_TILE // rc, scan_chunk, 0)
    ubuf[0:SUBLANES, :] = ubuf[ROW_TILE:ROW_TILE + SUBLANES, :]


def _rglru_core(proj, aux, conv_w, conv_b, b_r, b_i, lam, w_r, w_i, n_prompt_tiles, seq_p, seq_s):
    m = proj.shape[0]
    nb, cblk = w_r.shape[0], w_r.shape[1]
    c = nb * cblk
    n_tiles = m // ROW_TILE
    row_blk = lambda off: pl.BlockSpec((ROW_TILE, cblk), lambda cb, i: (i, off + cb))
    aux_blk = pl.BlockSpec((ROW_TILE, cblk), lambda cb, i: (0, cb))
    vec_blk = lambda rows: pl.BlockSpec((rows, cblk), lambda cb, i: (0, cb))
    w_blk = pl.BlockSpec((None, cblk, cblk), lambda cb, i: (cb, 0, 0))
    out_blk = pl.BlockSpec((ROW_TILE, cblk), lambda cb, i: (i, cb))
    return pl.pallas_call(
        functools.partial(_rglru_kernel, n_prompt_tiles, seq_p, seq_s),
        grid=(nb, n_tiles),
        in_specs=[row_blk(0), row_blk(nb), aux_blk, aux_blk, aux_blk, aux_blk,
                  vec_blk(conv_w.shape[0]), vec_blk(1), vec_blk(1), vec_blk(1), vec_blk(1),
                  w_blk, w_blk],
        out_specs=[out_blk, out_blk],
        out_shape=[jax.ShapeDtypeStruct((m, c), BF16), jax.ShapeDtypeStruct((m, c), F32)],
        scratch_shapes=[pltpu.VMEM((ROW_TILE + SUBLANES, cblk), F32),
                        pltpu.VMEM((ROW_TILE, cblk), F32),
                        pltpu.VMEM((ROW_TILE, cblk), F32),
                        pltpu.VMEM((ROW_TILE, cblk), F32),
                        pltpu.VMEM((SUBLANES, cblk), F32)],
        compiler_params=_cparams("arbitrary", "arbitrary"),
        name="rglru_core",
    )(proj, proj, *aux, conv_w, conv_b, b_r, b_i, lam, w_r, w_i)


def _sconv_kernel(n_prompt_tiles, seq_p, seq_s,
                  b_ref, cv_ref, q1_ref, q2_ref, cw_ref, o_ref, cvbuf):
    i = pl.program_id(1)
    cblk = cv_ref.shape[1]

    @pl.when(i == 0)
    def _():
        cvbuf[0:SUBLANES, :] = jnp.zeros((SUBLANES, cblk), F32)

    cvbuf[SUBLANES:, :] = cv_ref[...]
    is_sample = i >= n_prompt_tiles
    cw = cw_ref[...]
    rc = SCAN_CHUNK

    def conv_chunk(k, carry):
        r0 = pl.multiple_of(k * rc, rc)
        rows = pl.ds(r0, rc)
        t = _seq_pos(i, r0, rc, n_prompt_tiles, seq_p, seq_s)
        ext = cvbuf[pl.ds(r0, rc + SUBLANES), :]

        def tap(s, q_ref):
            shifted = pltpu.roll(ext, s, axis=0)[SUBLANES:]
            return jnp.where(t >= s, shifted, jnp.where(is_sample, q_ref[rows, :], 0.0))

        z = tap(2, q2_ref) * cw[0:1] + tap(1, q1_ref) * cw[1:2]
        z = z + ext[SUBLANES:] * cw[2:3]
        o_ref[rows, :] = (b_ref[rows, :] * z).astype(BF16)
        return carry

    lax.fori_loop(0, ROW_TILE // rc, conv_chunk, 0)
    cvbuf[0:SUBLANES, :] = cvbuf[ROW_TILE:ROW_TILE + SUBLANES, :]


def _sconv_core(bgate, cv, aux, conv_w, n_prompt_tiles, seq_p, seq_s):
    m, c = cv.shape
    cblk = COL_TILE
    blk = pl.BlockSpec((ROW_TILE, cblk), lambda cb, i: (i, cb))
    aux_blk = pl.BlockSpec((ROW_TILE, cblk), lambda cb, i: (0, cb))
    return pl.pallas_call(
        functools.partial(_sconv_kernel, n_prompt_tiles, seq_p, seq_s),
        grid=(c // cblk, m // ROW_TILE),
        in_specs=[blk, blk, aux_blk, aux_blk,
                  pl.BlockSpec((conv_w.shape[0], cblk), lambda cb, i: (0, cb))],
        out_specs=blk,
        out_shape=jax.ShapeDtypeStruct((m, c), BF16),
        scratch_shapes=[pltpu.VMEM((ROW_TILE + SUBLANES, cblk), F32)],
        compiler_params=_cparams("arbitrary", "arbitrary"),
        name="sconv_core",
    )(bgate, cv, *aux, conv_w)


def _ffn_kernel(x_ref, g_ref, wg_ref, wu_ref, wd_ref, o_ref, hn_ref):
    @pl.when(pl.program_id(1) == 0)
    def _():
        x = x_ref[...]
        hn_ref[...] = _rms_norm_f32(x, g_ref[...]).astype(BF16)
        o_ref[...] = x

    hn = hn_ref[...]
    a = jnp.dot(hn, wg_ref[...].astype(BF16), preferred_element_type=F32)
    b = jnp.dot(hn, wu_ref[...].astype(BF16), preferred_element_type=F32)
    h = (jax.nn.silu(a) * b).astype(BF16)
    o_ref[...] += jnp.dot(h, wd_ref[...].astype(BF16), preferred_element_type=F32)


def _ffn(x, g, wg, wu, wd):
    m, d = x.shape
    f = wg.shape[1]
    tm = ROW_TILE
    tf = W_COL_TILE
    return pl.pallas_call(
        _ffn_kernel,
        grid=(m // tm, f // tf),
        in_specs=[pl.BlockSpec((tm, d), lambda i, j: (i, 0)),
                  pl.BlockSpec((1, d), lambda i, j: (0, 0)),
                  pl.BlockSpec((d, tf), lambda i, j: (0, j)),
                  pl.BlockSpec((d, tf), lambda i, j: (0, j)),
                  pl.BlockSpec((tf, d), lambda i, j: (j, 0))],
        out_specs=pl.BlockSpec((tm, d), lambda i, j: (i, 0)),
        out_shape=jax.ShapeDtypeStruct((m, d), F32),
        scratch_shapes=[pltpu.VMEM((tm, d), BF16)],
        compiler_params=_cparams("arbitrary", "arbitrary"),
        name="ffn",
    )(x, g, wg, wu, wd)


def _router_kernel(n_experts, x_ref, g_ref, wr_ref, hn_ref, meta_ref, cnt_ref, run_ref):
    @pl.when(pl.program_id(0) == 0)
    def _():
        run_ref[...] = jnp.zeros_like(run_ref)

    tm = x_ref.shape[0]
    hn = _rms_norm_f32(x_ref[...], g_ref[...])
    hn_ref[...] = hn
    logits = jnp.dot(hn, wr_ref[...], preferred_element_type=F32,
                     precision=lax.Precision.HIGHEST)
    lane = lax.broadcasted_iota(jnp.int32, (tm, LANES), 1).astype(F32)
    neg = jnp.float32(-jnp.inf)
    lg = jnp.where(lane < n_experts, logits, neg)
    m1 = jnp.max(lg, axis=1, keepdims=True)
    i1 = jnp.min(jnp.where(lg == m1, lane, float(LANES)), axis=1, keepdims=True)
    lg2 = jnp.where(lane == i1, neg, lg)
    m2 = jnp.max(lg2, axis=1, keepdims=True)
    i2 = jnp.min(jnp.where(lg2 == m2, lane, float(LANES)), axis=1, keepdims=True)
    ex = jnp.exp(m2 - m1)
    g1 = 1.0 / (1.0 + ex)
    g2 = ex / (1.0 + ex)

    sel1 = lane == i1
    sel2 = lane == i2
    onehot = jnp.where(sel1 | sel2, 1.0, 0.0)
    rr = lax.broadcasted_iota(jnp.int32, (tm, tm), 0)
    cc = lax.broadcasted_iota(jnp.int32, (tm, tm), 1)
    tri = jnp.where(rr > cc, 1.0, 0.0).astype(BF16)
    before = jnp.dot(tri, onehot.astype(BF16), preferred_element_type=F32) + run_ref[0:1, :]
    rank1 = jnp.sum(jnp.where(sel1, before, 0.0), axis=1, keepdims=True)
    rank2 = jnp.sum(jnp.where(sel2, before, 0.0), axis=1, keepdims=True)
    run = run_ref[0:1, :] + jnp.sum(onehot, axis=0, keepdims=True)
    run_ref[...] = jnp.broadcast_to(run, run_ref.shape)
    cnt_ref[...] = jnp.broadcast_to(run, cnt_ref.shape)

    meta = jnp.where(lane == 0, i1, 0.0)
    meta = jnp.where(lane == 1, i2, meta)
    meta = jnp.where(lane == 2, g1, meta)
    meta = jnp.where(lane == 3, g2, meta)
    meta = jnp.where(lane == 4, rank1, meta)
    meta = jnp.where(lane == 5, rank2, meta)
    meta_ref[...] = meta


def _router(x, g, w_router_padded, n_experts):
    m, d = x.shape
    tm = ROUTER_ROW_TILE
    return pl.pallas_call(
        functools.partial(_router_kernel, n_experts),
        grid=(m // tm,),
        in_specs=[pl.BlockSpec((tm, d), lambda i: (i, 0)),
                  pl.BlockSpec((1, d), lambda i: (0, 0)),
                  pl.BlockSpec((d, LANES), lambda i: (0, 0))],
        out_specs=[pl.BlockSpec((tm, d), lambda i: (i, 0)),
                   pl.BlockSpec((tm, LANES), lambda i: (i, 0)),
                   pl.BlockSpec((SUBLANES, LANES), lambda i: (0, 0))],
        out_shape=[jax.ShapeDtypeStruct((m, d), F32),
                   jax.ShapeDtypeStruct((m, LANES), F32),
                   jax.ShapeDtypeStruct((SUBLANES, LANES), F32)],
        scratch_shapes=[pltpu.VMEM((SUBLANES, LANES), F32)],
        compiler_params=_cparams("arbitrary"),
        name="router",
    )(x, g, w_router_padded)


def _row_copy(src_hbm, idx, dst_ref, r, sem):
    return pltpu.make_async_copy(src_hbm.at[pl.ds(idx, 1)], dst_ref.at[pl.ds(r, 1)], sem)


def _experts_kernel(rows_per_step, be_ref, nv_ref, nused_ref,
                    tok0_ref, tok_next_ref, hn_hbm, wg_ref, wu_ref, wd_ref, o_ref,
                    xbuf, xb_ref, wgb_ref, wub_ref, wdb_ref, sem):
    t = pl.program_id(0)
    j = pl.program_id(1)
    nused = nused_ref[0]
    tr = xb_ref.shape[0]
    gr = xbuf.shape[0]
    half = tr // 2

    @pl.when(j == 0)
    def _():
        o_ref[...] = jnp.zeros_like(o_ref)

    @pl.when((t == 0) & (j == 0))
    def _():
        def issue(r, c):
            _row_copy(hn_hbm, tok0_ref[0, 0, r], xbuf, r, sem).start()
            return c

        lax.fori_loop(0, gr, issue, 0, unroll=8)

    @pl.when((j == 0) & (t <= nused))
    def _():
        pltpu.make_async_copy(hn_hbm.at[pl.ds(0, gr)], xbuf, sem).wait()

    @pl.when((j == 0) & (t < nused))
    def _():
        xb_ref[...] = xbuf[0:tr, :].astype(BF16)

    @pl.when(t < nused)
    def _():
        base = j * rows_per_step
        for k in range(rows_per_step):
            _row_copy(hn_hbm, tok_next_ref[0, 0, base + k], xbuf, base + k, sem).start()

        wgb_ref[...] = wg_ref[...].astype(BF16)
        wub_ref[...] = wu_ref[...].astype(BF16)
        wdb_ref[...] = wd_ref[...].astype(BF16)

        def block(rows):
            x = xb_ref[rows, :]
            a = jnp.dot(x, wgb_ref[...], preferred_element_type=F32)
            b = jnp.dot(x, wub_ref[...], preferred_element_type=F32)
            h = (jax.nn.silu(a) * b).astype(BF16)
            o_ref[rows, :] += jnp.dot(h, wdb_ref[...], preferred_element_type=F32)

        block(pl.ds(0, half))

        @pl.when(nv_ref[t] > half)
        def _():
            block(pl.ds(half, half))


def _experts(hn, row_tok, block_e, n_valid, n_used, wg, wu, wd):
    d = hn.shape[1]
    f = wg.shape[2]
    tr = MOE_ROW_TILE
    tf = W_COL_TILE
    n_tiles = row_tok.shape[0]
    nj = f // tf
    rows_per_step = -(-tr // nj)
    gr = rows_per_step * nj
    tok = jnp.pad(row_tok, ((0, 1), (0, gr - tr))).reshape(n_tiles + 1, 1, gr)

    def tile(t, nu):
        return jnp.minimum(t, jnp.maximum(nu[0] - 1, 0))

    def col(t, j, nu):
        return jnp.where(t < nu[0], j, nj - 1)

    smem = lambda imap: pl.BlockSpec((1, 1, gr), imap, memory_space=pltpu.SMEM)
    return pl.pallas_call(
        functools.partial(_experts_kernel, rows_per_step),
        grid_spec=pltpu.PrefetchScalarGridSpec(
            num_scalar_prefetch=3,
            grid=(n_tiles + 1, nj),
            in_specs=[smem(lambda t, j, be, nv, nu: (0, 0, 0)),
                      smem(lambda t, j, be, nv, nu: (jnp.minimum(t + 1, n_tiles), 0, 0)),
                      pl.BlockSpec(memory_space=pl.ANY),
                      pl.BlockSpec((None, d, tf),
                                   lambda t, j, be, nv, nu: (be[tile(t, nu)], 0, col(t, j, nu))),
                      pl.BlockSpec((None, d, tf),
                                   lambda t, j, be, nv, nu: (be[tile(t, nu)], 0, col(t, j, nu))),
                      pl.BlockSpec((None, tf, d),
                                   lambda t, j, be, nv, nu: (be[tile(t, nu)], col(t, j, nu), 0))],
            out_specs=pl.BlockSpec((tr, d), lambda t, j, be, nv, nu: (t, 0)),
            scratch_shapes=[pltpu.VMEM((gr, d), F32), pltpu.VMEM((tr, d), BF16),
                            pltpu.VMEM((d, tf), BF16), pltpu.VMEM((d, tf), BF16),
                            pltpu.VMEM((tf, d), BF16), pltpu.SemaphoreType.DMA(())]),
        out_shape=jax.ShapeDtypeStruct(((n_tiles + 1) * tr, d), F32),
        compiler_params=_cparams("arbitrary", "arbitrary"),
        name="moe_experts",
    )(block_e, n_valid, n_used, tok, tok, hn, wg, wu, wd)


def _combine_kernel(n_first, d0_ref, d1_ref, d0n_ref, d1n_ref, x_ref, meta_ref, g_ref, yb_hbm,
                    oa_ref, ob_ref, y0, y1, sem):
    tm = x_ref.shape[0]
    i = pl.program_id(0)
    slot = lax.rem(i, 2)

    def issue_tile(da_ref, db_ref, s):
        def issue(r, c):
            _row_copy(yb_hbm, da_ref[0, 0, r], y0.at[s], r, sem.at[0, s]).start()
            _row_copy(yb_hbm, db_ref[0, 0, r], y1.at[s], r, sem.at[1, s]).start()
            return c

        lax.fori_loop(0, tm, issue, 0, unroll=8)

    @pl.when(i == 0)
    def _():
        issue_tile(d0_ref, d1_ref, 0)

    @pl.when(i + 1 < pl.num_programs(0))
    def _():
        issue_tile(d0n_ref, d1n_ref, 1 - slot)

    pltpu.make_async_copy(yb_hbm.at[pl.ds(0, tm)], y0.at[slot], sem.at[0, slot]).wait()
    pltpu.make_async_copy(yb_hbm.at[pl.ds(0, tm)], y1.at[slot], sem.at[1, slot]).wait()
    meta = meta_ref[...]
    x = x_ref[...] + (y0[slot] * meta[:, 2:3] + y1[slot] * meta[:, 3:4])

    @pl.when(i < n_first)
    def _():
        oa_ref[...] = _rms_norm_f32(x, g_ref[...])

    @pl.when(i >= n_first)
    def _():
        ob_ref[...] = _rms_norm_f32(x, g_ref[...])


def _combine(x, meta, g, yb, dest0, dest1, m_first):
    m, d = x.shape
    tm = COMBINE_ROW_TILE
    n_tiles = m // tm
    smem_blk = pl.BlockSpec((1, 1, tm), lambda i: (i, 0, 0), memory_space=pltpu.SMEM)
    smem_next = pl.BlockSpec((1, 1, tm), lambda i: (jnp.minimum(i + 1, n_tiles - 1), 0, 0),
                             memory_space=pltpu.SMEM)
    oa_spec, ob_spec = _two_part_specs(m_first // tm, (tm, d), lambda: 0)
    d0 = dest0.reshape(n_tiles, 1, tm)
    d1 = dest1.reshape(n_tiles, 1, tm)
    return pl.pallas_call(
        functools.partial(_combine_kernel, m_first // tm),
        grid=(n_tiles,),
        in_specs=[smem_blk, smem_blk, smem_next, smem_next,
                  pl.BlockSpec((tm, d), lambda i: (i, 0)),
                  pl.BlockSpec((tm, LANES), lambda i: (i, 0)),
                  pl.BlockSpec((1, d), lambda i: (0, 0)),
                  pl.BlockSpec(memory_space=pl.ANY)],
        out_specs=[oa_spec, ob_spec],
        out_shape=[jax.ShapeDtypeStruct((m_first, d), F32),
                   jax.ShapeDtypeStruct((m - m_first, d), F32)],
        scratch_shapes=[pltpu.VMEM((2, tm, d), F32), pltpu.VMEM((2, tm, d), F32),
                        pltpu.SemaphoreType.DMA((2, 2))],
        compiler_params=_cparams("arbitrary"),
        name="moe_combine",
    )(d0, d1, d0, d1, x, meta, g, yb)


def _sample_aux(state, shift, dec_seq):
    b, w1, c = state.shape
    rows = [state[:, w1 + t - shift] if t < shift else jnp.zeros((b, c), state.dtype)
            for t in range(dec_seq)]
    return jnp.stack(rows, axis=1).reshape(b * dec_seq, c)


def _seq_tails(a, col0, ncol, width, batch, seq, dec_batch, dec_seq):
    mp = batch * seq
    tails = [lax.slice(a, ((b + 1) * seq - width, col0), ((b + 1) * seq, col0 + ncol))
             for b in range(batch)]
    smp = lax.slice(a, (mp, col0), (mp + dec_batch * dec_seq, col0 + ncol))
    smp = smp.reshape(dec_batch, dec_seq, ncol)[:, dec_seq - width:]
    return jnp.stack(tails, axis=0), smp


def kernel(x_prompt, x_sample, state_lru_h, state_lru_conv, state_sconv, norm_mix, norm_ffn, norm_final, lru_w_in, lru_conv_w, lru_conv_b, lru_w_rgate, lru_b_rgate, lru_w_igate, lru_b_igate, lru_lambda, lru_w_out, sc_w_in, sc_conv_w, sc_w_out, ffn_w_gate, ffn_w_up, ffn_w_down, moe_w_router, moe_w_gate, moe_w_up, moe_w_down):
    batch, seq, d = x_prompt.shape
    dec_batch, dec_seq, _ = x_sample.shape
    mp, ms = batch * seq, dec_batch * dec_seq
    m = mp + ms
    depth = norm_mix.shape[0]
    n_experts = moe_w_router.shape[2]
    conv_a = lru_conv_w.shape[1]
    conv_b = sc_conv_w.shape[1]
    assert depth == 2 and lru_w_in.shape[0] == 1 and sc_w_in.shape[0] == 1
    assert seq & (seq - 1) == 0 and dec_seq & (dec_seq - 1) == 0
    assert seq % ROW_TILE == 0 and ms == ROW_TILE and ROW_TILE % dec_seq == 0
    assert dec_seq >= conv_a - 1 and dec_seq >= conv_b - 1 and dec_seq == SUBLANES
    assert n_experts <= LANES and m % ROUTER_ROW_TILE == 0 and m % COMBINE_ROW_TILE == 0
    n_prompt_tiles = mp // ROW_TILE

    xp = x_prompt.reshape(mp, d)
    xs_in = x_sample.reshape(ms, d)
    row = lambda v: v.reshape(1, -1)

    d_rnn = lru_w_out.shape[1]
    proj = _norm_matmul(xp, xs_in, row(norm_mix[0]), lru_w_in[0])
    h0_rows = jnp.pad(state_lru_h[0][:, None, :], ((0, 0), (0, dec_seq - 1), (0, 0)))
    aux = [_sample_aux(state_lru_conv[0], s, dec_seq) for s in (1, 2, 3)]
    aux.append(h0_rows.reshape(ms, d_rnn))
    y, h_all = _rglru_core(proj, aux, lru_conv_w[0], row(lru_conv_b[0]), row(lru_b_rgate[0]),
                           row(lru_b_igate[0]), row(lru_lambda[0]),
                           lru_w_rgate[0].astype(BF16), lru_w_igate[0].astype(BF16),
                           n_prompt_tiles, seq, dec_seq)
    x = _matmul_residual(y, lru_w_out[0].astype(BF16), xp, xs_in)

    x = _ffn(x, row(norm_ffn[0]), ffn_w_gate[0], ffn_w_up[0], ffn_w_down[0])

    bgate, cv = _sc_inproj(x, row(norm_mix[1]), sc_w_in[0])
    aux = [_sample_aux(state_sconv[0], s, dec_seq) for s in (1, 2)]
    zb = _sconv_core(bgate, cv, aux, sc_conv_w[0], n_prompt_tiles, seq, dec_seq)
    x = _matmul_residual(zb, sc_w_out[0].astype(BF16), x, x)

    w_router = jnp.pad(moe_w_router[0], ((0, 0), (0, LANES - n_experts)))
    hn, meta, counts = _router(x, row(norm_ffn[1]), w_router, n_experts)
    tr = MOE_ROW_TILE
    n_tiles = -(-(m * TOP_K) // tr) + n_experts
    counts = counts[0, :n_experts].astype(jnp.int32)
    padded = (counts + tr - 1) // tr * tr
    pad_end = jnp.cumsum(padded)
    pad_start = pad_end - padded
    e0 = meta[:, 0].astype(jnp.int32)
    e1 = meta[:, 1].astype(jnp.int32)
    dest0 = pad_start[e0] + meta[:, 4].astype(jnp.int32)
    dest1 = pad_start[e1] + meta[:, 5].astype(jnp.int32)
    tok = jnp.arange(m, dtype=jnp.int32)
    row_tok = jnp.zeros((n_tiles * tr,), jnp.int32).at[jnp.concatenate([dest0, dest1])].set(
        jnp.concatenate([tok, tok]), unique_indices=True, mode='promise_in_bounds')
    block_start = jnp.arange(n_tiles, dtype=jnp.int32) * tr
    block_e = jnp.sum((pad_end[None, :] <= block_start[:, None]).astype(jnp.int32), axis=1)
    block_e = jnp.minimum(block_e, n_experts - 1)
    n_valid = jnp.clip(counts[block_e] - (block_start - pad_start[block_e]), 0, tr)
    n_used = (pad_end[-1:] // tr).astype(jnp.int32)
    yb = _experts(hn, row_tok.reshape(n_tiles, tr), block_e, n_valid.astype(jnp.int32), n_used,
                  moe_w_gate[0], moe_w_up[0], moe_w_down[0])
    y_prompt, y_sample = _combine(x, meta, row(norm_final), yb, dest0, dest1, mp)

    tails = functools.partial(_seq_tails, batch=batch, seq=seq, dec_batch=dec_batch, dec_seq=dec_seq)
    h_p, h_s = tails(h_all, 0, d_rnn, 1)
    u_p, u_s = tails(proj, d_rnn, d_rnn, conv_a - 1)
    cv_p, cv_s = tails(cv, 0, cv.shape[1], conv_b - 1)
    return (y_prompt.reshape(batch, seq, d), y_sample.reshape(dec_batch, dec_seq, d),
            h_p[:, 0][None], u_p[None], cv_p[None], h_s[:, 0][None], u_s[None], cv_s[None])
```

```python
import functools

import jax
import jax.numpy as jnp
from jax import lax
from jax.experimental import pallas as pl
from jax.experimental.pallas import tpu as pltpu

F32 = jnp.float32
BF16 = jnp.bfloat16

EPS = 1e-6
RG_C = 8.0
TOP_K = 2

LANES = 128
SUBLANES = 8
ROW_TILE = 1024
OUT_ROW_TILE = 512
ROUTER_ROW_TILE = 512
COL_TILE = 512
W_COL_TILE = 256
MOE_ROW_TILE = 1024
COMBINE_ROW_TILE = 256
SCAN_CHUNK = 64
VMEM_LIMIT = 60 * 1024 * 1024


def _cparams(*sem):
    return pltpu.CompilerParams(dimension_semantics=sem, vmem_limit_bytes=VMEM_LIMIT)


def _rms_norm_f32(x, g):
    y = x * lax.rsqrt(jnp.mean(x * x, axis=-1, keepdims=True) + EPS)
    return y * g


def _two_part_specs(n_first, block, col_of, second_mode=None):
    first = pl.BlockSpec(block, lambda i, *a: (jnp.minimum(i, n_first - 1), col_of(*a)))
    second = pl.BlockSpec(block, lambda i, *a: (jnp.maximum(i - n_first, 0), col_of(*a)),
                          pipeline_mode=second_mode)
    return first, second


def _norm_matmul_kernel(n_first, xa_ref, xb_ref, g_ref, w_ref, o_ref, hn_ref):
    i = pl.program_id(0)
    j = pl.program_id(1)

    @pl.when((j == 0) & (i < n_first))
    def _():
        hn_ref[...] = _rms_norm_f32(xa_ref[...], g_ref[...]).astype(BF16)

    @pl.when((j == 0) & (i >= n_first))
    def _():
        hn_ref[...] = _rms_norm_f32(xb_ref[...], g_ref[...]).astype(BF16)

    o_ref[...] = jnp.dot(hn_ref[...], w_ref[...].astype(BF16), preferred_element_type=F32)


def _norm_matmul(xa, xb, g, w):
    d = xa.shape[1]
    m = xa.shape[0] + xb.shape[0]
    n = w.shape[1]
    assert xb.shape[0] == ROW_TILE
    xa_spec, xb_spec = _two_part_specs(xa.shape[0] // ROW_TILE, (ROW_TILE, d), lambda j: 0,
                                       second_mode=pl.Buffered(1))
    return pl.pallas_call(
        functools.partial(_norm_matmul_kernel, xa.shape[0] // ROW_TILE),
        grid=(m // ROW_TILE, n // COL_TILE),
        in_specs=[xa_spec, xb_spec,
                  pl.BlockSpec((1, d), lambda i, j: (0, 0)),
                  pl.BlockSpec((d, COL_TILE), lambda i, j: (0, j))],
        out_specs=pl.BlockSpec((ROW_TILE, COL_TILE), lambda i, j: (i, j)),
        out_shape=jax.ShapeDtypeStruct((m, n), F32),
        scratch_shapes=[pltpu.VMEM((ROW_TILE, d), BF16)],
        compiler_params=_cparams("arbitrary", "arbitrary"),
        name="norm_matmul",
    )(xa, xb, g, w)


def _sc_inproj_kernel(x_ref, g_ref, wb_ref, wc_ref, wv_ref, b_ref, cv_ref, hn_ref):
    @pl.when(pl.program_id(1) == 0)
    def _():
        hn_ref[...] = _rms_norm_f32(x_ref[...], g_ref[...]).astype(BF16)

    hn = hn_ref[...]
    b_ref[...] = jnp.dot(hn, wb_ref[...].astype(BF16), preferred_element_type=F32)
    c = jnp.dot(hn, wc_ref[...].astype(BF16), preferred_element_type=F32)
    v = jnp.dot(hn, wv_ref[...].astype(BF16), preferred_element_type=F32)
    cv_ref[...] = c * v


def _sc_inproj(x, g, w):
    m, d = x.shape
    dc = w.shape[1] // 3
    nj = dc // COL_TILE
    wspec = lambda off: pl.BlockSpec((d, COL_TILE), lambda i, j: (0, off * nj + j))
    ospec = pl.BlockSpec((ROW_TILE, COL_TILE), lambda i, j: (i, j))
    return pl.pallas_call(
        _sc_inproj_kernel,
        grid=(m // ROW_TILE, nj),
        in_specs=[pl.BlockSpec((ROW_TILE, d), lambda i, j: (i, 0)),
                  pl.BlockSpec((1, d), lambda i, j: (0, 0)),
                  wspec(0), wspec(1), wspec(2)],
        out_specs=[ospec, ospec],
        out_shape=[jax.ShapeDtypeStruct((m, dc), F32)] * 2,
        scratch_shapes=[pltpu.VMEM((ROW_TILE, d), BF16)],
        compiler_params=_cparams("arbitrary", "arbitrary"),
        name="sc_inproj",
    )(x, g, w, w, w)


def _matmul_residual_kernel(n_first, y_ref, w_ref, ra_ref, rb_ref, o_ref):
    o_ref[...] = jnp.dot(y_ref[...], w_ref[...], preferred_element_type=F32)
    i = pl.program_id(0)

    @pl.when(i < n_first)
    def _():
        o_ref[...] += ra_ref[...]

    @pl.when(i >= n_first)
    def _():
        o_ref[...] += rb_ref[...]


def _matmul_residual(y, w, res_a, res_b):
    m, k = y.shape
    n = w.shape[1]
    tm = OUT_ROW_TILE
    ra_spec, rb_spec = _two_part_specs(res_a.shape[0] // tm, (tm, n), lambda: 0)
    return pl.pallas_call(
        functools.partial(_matmul_residual_kernel, res_a.shape[0] // tm),
        grid=(m // tm,),
        in_specs=[pl.BlockSpec((tm, k), lambda i: (i, 0)),
                  pl.BlockSpec((k, n), lambda i: (0, 0)),
                  ra_spec, rb_spec],
        out_specs=pl.BlockSpec((tm, n), lambda i: (i, 0)),
        out_shape=jax.ShapeDtypeStruct((m, n), F32),
        compiler_params=_cparams("arbitrary"),
        name="matmul_residual",
    )(y, w, res_a, res_b)


def _seq_pos(i, r0, rows, n_prompt_tiles, seq_p, seq_s):
    seqlen = jnp.where(i < n_prompt_tiles, seq_p, seq_s)
    grow = i * ROW_TILE + r0 + lax.broadcasted_iota(jnp.int32, (rows, 1), 0)
    return grow & (seqlen - 1)


def _rglru_kernel(n_prompt_tiles, seq_p, seq_s,
                  gate_ref, u_ref, p1_ref, p2_ref, p3_ref, h0_ref,
                  cw_ref, cb_ref, br_ref, bi_ref, lam_ref, wr_ref, wi_ref,
                  y_ref, h_ref, ubuf, uc_ref, rp_ref, ip_ref, hc_ref):
    i = pl.program_id(1)
    cblk = u_ref.shape[1]

    @pl.when(i == 0)
    def _():
        ubuf[0:SUBLANES, :] = jnp.zeros((SUBLANES, cblk), F32)
        hc_ref[...] = jnp.zeros_like(hc_ref)

    ubuf[SUBLANES:, :] = u_ref[...]
    is_sample = i >= n_prompt_tiles
    cw = cw_ref[...]
    cb = cb_ref[...]
    rc = SCAN_CHUNK

    def conv_chunk(k, carry):
        r0 = pl.multiple_of(k * rc, rc)
        rows = pl.ds(r0, rc)
        t = _seq_pos(i, r0, rc, n_prompt_tiles, seq_p, seq_s)
        ext = ubuf[pl.ds(r0, rc + SUBLANES), :]

        def tap(s, p_ref):
            shifted = pltpu.roll(ext, s, axis=0)[SUBLANES:]
            return jnp.where(t >= s, shifted, jnp.where(is_sample, p_ref[rows, :], 0.0))

        conv = tap(3, p3_ref) * cw[0:1] + tap(2, p2_ref) * cw[1:2]
        conv = conv + tap(1, p1_ref) * cw[2:3]
        conv = conv + ext[SUBLANES:] * cw[3:4]
        uc_ref[rows, :] = conv + cb
        return carry

    lax.fori_loop(0, ROW_TILE // rc, conv_chunk, 0)

    ub = uc_ref[...].astype(BF16)
    rp_ref[...] = jnp.dot(ub, wr_ref[...], preferred_element_type=F32) + br_ref[...]
    ip_ref[...] = jnp.dot(ub, wi_ref[...], preferred_element_type=F32) + bi_ref[...]

    sp = jax.nn.softplus(-lam_ref[...])
    sub = lax.broadcasted_iota(jnp.int32, (1, SUBLANES, 1), 1)

    def scan_chunk(k, carry):
        r0 = pl.multiple_of(k * rc, rc)
        rows = pl.ds(r0, rc)
        t = _seq_pos(i, r0, rc, n_prompt_tiles, seq_p, seq_s)
        uc = uc_ref[rows, :]
        r = jax.nn.sigmoid(rp_ref[rows, :])
        ig = jax.nn.sigmoid(ip_ref[rows, :])
        log_a = (-RG_C * r) * sp
        a = jnp.exp(log_a)
        mult = jnp.sqrt(-jnp.tanh(log_a) * (a * a + 1.0))
        b = mult * ig * uc
        first = t == 0
        b = jnp.where(first & is_sample, b + a * h0_ref[rows, :], b)
        a = jnp.where(first, 0.0, a)

        a3 = a.reshape(rc // SUBLANES, SUBLANES, cblk)
        b3 = b.reshape(rc // SUBLANES, SUBLANES, cblk)
        for s in (1, 2, 4):
            a_s = pltpu.roll(a3, s, axis=1)
            b_s = pltpu.roll(b3, s, axis=1)
            m = sub >= s
            b3 = jnp.where(m, a3 * b_s + b3, b3)
            a3 = jnp.where(m, a3 * a_s, a3)

        hc = hc_ref[...]
        hs = []
        for g in range(rc // SUBLANES):
            hg = a3[g] * hc + b3[g]
            hc = jnp.broadcast_to(hg[SUBLANES - 1:SUBLANES], (SUBLANES, cblk))
            hs.append(hg)
        hc_ref[...] = hc
        h = jnp.concatenate(hs, axis=0)
        h_ref[rows, :] = h
        y_ref[rows, :] = (h * jax.nn.gelu(gate_ref[rows, :])).astype(BF16)
        return carry

    lax.fori_loop(0, ROW_TILE // rc, scan_chunk, 0)
    ubuf[0:SUBLANES, :] = ubuf[ROW_TILE:ROW_TILE + SUBLANES, :]


def _rglru_core(proj, aux, conv_w, conv_b, b_r, b_i, lam, w_r, w_i, n_prompt_tiles, seq_p, seq_s):
    m = proj.shape[0]
    nb, cblk = w_r.shape[0], w_r.shape[1]
    c = nb * cblk
    n_tiles = m // ROW_TILE
    row_blk = lambda off: pl.BlockSpec((ROW_TILE, cblk), lambda cb, i: (i, off + cb))
    aux_blk = pl.BlockSpec((ROW_TILE, cblk), lambda cb, i: (0, cb))
    vec_blk = lambda rows: pl.BlockSpec((rows, cblk), lambda cb, i: (0, cb))
    w_blk = pl.BlockSpec((None, cblk, cblk), lambda cb, i: (cb, 0, 0))
    out_blk = pl.BlockSpec((ROW_TILE, cblk), lambda cb, i: (i, cb))
    return pl.pallas_call(
        functools.partial(_rglru_kernel, n_prompt_tiles, seq_p, seq_s),
        grid=(nb, n_tiles),
        in_specs=[row_blk(0), row_blk(nb), aux_blk, aux_blk, aux_blk, aux_blk,
                  vec_blk(conv_w.shape[0]), vec_blk(1), vec_blk(1), vec_blk(1), vec_blk(1),
                  w_blk, w_blk],
        out_specs=[out_blk, out_blk],
        out_shape=[jax.ShapeDtypeStruct((m, c), BF16), jax.ShapeDtypeStruct((m, c), F32)],
        scratch_shapes=[pltpu.VMEM((ROW_TILE + SUBLANES, cblk), F32),
                        pltpu.VMEM((ROW_TILE, cblk), F32),
                        pltpu.VMEM((ROW_TILE, cblk), F32),
                        pltpu.VMEM((ROW_TILE, cblk), F32),
                        pltpu.VMEM((SUBLANES, cblk), F32)],
        compiler_params=_cparams("arbitrary", "arbitrary"),
        name="rglru_core",
    )(proj, proj, *aux, conv_w, conv_b, b_r, b_i, lam, w_r, w_i)


def _sconv_kernel(n_prompt_tiles, seq_p, seq_s,
                  b_ref, cv_ref, q1_ref, q2_ref, cw_ref, o_ref, cvbuf):
    i = pl.program_id(1)
    cblk = cv_ref.shape[1]

    @pl.when(i == 0)
    def _():
        cvbuf[0:SUBLANES, :] = jnp.zeros((SUBLANES, cblk), F32)

    cvbuf[SUBLANES:, :] = cv_ref[...]
    is_sample = i >= n_prompt_tiles
    cw = cw_ref[...]
    rc = SCAN_CHUNK

    def conv_chunk(k, carry):
        r0 = pl.multiple_of(k * rc, rc)
        rows = pl.ds(r0, rc)
        t = _seq_pos(i, r0, rc, n_prompt_tiles, seq_p, seq_s)
        ext = cvbuf[pl.ds(r0, rc + SUBLANES), :]

        def tap(s, q_ref):
            shifted = pltpu.roll(ext, s, axis=0)[SUBLANES:]
            return jnp.where(t >= s, shifted, jnp.where(is_sample, q_ref[rows, :], 0.0))

        z = tap(2, q2_ref) * cw[0:1] + tap(1, q1_ref) * cw[1:2]
        z = z + ext[SUBLANES:] * cw[2:3]
        o_ref[rows, :] = (b_ref[rows, :] * z).astype(BF16)
        return carry

    lax.fori_loop(0, ROW_TILE // rc, conv_chunk, 0)
    cvbuf[0:SUBLANES, :] = cvbuf[ROW_TILE:ROW_TILE + SUBLANES, :]


def _sconv_core(bgate, cv, aux, conv_w, n_prompt_tiles, seq_p, seq_s):
    m, c = cv.shape
    cblk = COL_TILE
    blk = pl.BlockSpec((ROW_TILE, cblk), lambda cb, i: (i, cb))
    aux_blk = pl.BlockSpec((ROW_TILE, cblk), lambda cb, i: (0, cb))
    return pl.pallas_call(
        functools.partial(_sconv_kernel, n_prompt_tiles, seq_p, seq_s),
        grid=(c // cblk, m // ROW_TILE),
        in_specs=[blk, blk, aux_blk, aux_blk,
                  pl.BlockSpec((conv_w.shape[0], cblk), lambda cb, i: (0, cb))],
        out_specs=blk,
        out_shape=jax.ShapeDtypeStruct((m, c), BF16),
        scratch_shapes=[pltpu.VMEM((ROW_TILE + SUBLANES, cblk), F32)],
        compiler_params=_cparams("arbitrary", "arbitrary"),
        name="sconv_core",
    )(bgate, cv, *aux, conv_w)


def _ffn_kernel(x_ref, g_ref, wg_ref, wu_ref, wd_ref, o_ref, hn_ref):
    @pl.when(pl.program_id(1) == 0)
    def _():
        x = x_ref[...]
        hn_ref[...] = _rms_norm_f32(x, g_ref[...]).astype(BF16)
        o_ref[...] = x

    hn = hn_ref[...]
    a = jnp.dot(hn, wg_ref[...].astype(BF16), preferred_element_type=F32)
    b = jnp.dot(hn, wu_ref[...].astype(BF16), preferred_element_type=F32)
    h = (jax.nn.silu(a) * b).astype(BF16)
    o_ref[...] += jnp.dot(h, wd_ref[...].astype(BF16), preferred_element_type=F32)


def _ffn(x, g, wg, wu, wd):
    m, d = x.shape
    f = wg.shape[1]
    tm = ROW_TILE
    tf = W_COL_TILE
    return pl.pallas_call(
        _ffn_kernel,
        grid=(m // tm, f // tf),
        in_specs=[pl.BlockSpec((tm, d), lambda i, j: (i, 0)),
                  pl.BlockSpec((1, d), lambda i, j: (0, 0)),
                  pl.BlockSpec((d, tf), lambda i, j: (0, j)),
                  pl.BlockSpec((d, tf), lambda i, j: (0, j)),
                  pl.BlockSpec((tf, d), lambda i, j: (j, 0))],
        out_specs=pl.BlockSpec((tm, d), lambda i, j: (i, 0)),
        out_shape=jax.ShapeDtypeStruct((m, d), F32),
        scratch_shapes=[pltpu.VMEM((tm, d), BF16)],
        compiler_params=_cparams("arbitrary", "arbitrary"),
        name="ffn",
    )(x, g, wg, wu, wd)


def _router_kernel(n_experts, x_ref, g_ref, wr_ref, hn_ref, meta_ref, cnt_ref, run_ref):
    @pl.when(pl.program_id(0) == 0)
    def _():
        run_ref[...] = jnp.zeros_like(run_ref)

    tm = x_ref.shape[0]
    hn = _rms_norm_f32(x_ref[...], g_ref[...])
    bits = pltpu.bitcast(hn.astype(BF16).astype(F32), jnp.uint32)
    half = hn.shape[1] // 2
    hn_ref[...] = (bits[:, half:] & jnp.uint32(0xFFFF0000)) | (bits[:, :half] >> 16)
    logits = jnp.dot(hn, wr_ref[...], preferred_element_type=F32,
                     precision=lax.Precision.HIGHEST)
    lane = lax.broadcasted_iota(jnp.int32, (tm, LANES), 1).astype(F32)
    neg = jnp.float32(-jnp.inf)
    lg = jnp.where(lane < n_experts, logits, neg)
    m1 = jnp.max(lg, axis=1, keepdims=True)
    i1 = jnp.min(jnp.where(lg == m1, lane, float(LANES)), axis=1, keepdims=True)
    lg2 = jnp.where(lane == i1, neg, lg)
    m2 = jnp.max(lg2, axis=1, keepdims=True)
    i2 = jnp.min(jnp.where(lg2 == m2, lane, float(LANES)), axis=1, keepdims=True)
    ex = jnp.exp(m2 - m1)
    g1 = 1.0 / (1.0 + ex)
    g2 = ex / (1.0 + ex)

    sel1 = lane == i1
    sel2 = lane == i2
    onehot = jnp.where(sel1 | sel2, 1.0, 0.0)
    rr = lax.broadcasted_iota(jnp.int32, (tm, tm), 0)
    cc = lax.broadcasted_iota(jnp.int32, (tm, tm), 1)
    tri = jnp.where(rr > cc, 1.0, 0.0).astype(BF16)
    before = jnp.dot(tri, onehot.astype(BF16), preferred_element_type=F32) + run_ref[0:1, :]
    rank1 = jnp.sum(jnp.where(sel1, before, 0.0), axis=1, keepdims=True)
    rank2 = jnp.sum(jnp.where(sel2, before, 0.0), axis=1, keepdims=True)
    run = run_ref[0:1, :] + jnp.sum(onehot, axis=0, keepdims=True)
    run_ref[...] = jnp.broadcast_to(run, run_ref.shape)
    cnt_ref[...] = jnp.broadcast_to(run, cnt_ref.shape)

    meta = jnp.where(lane == 0, i1, 0.0)
    meta = jnp.where(lane == 1, i2, meta)
    meta = jnp.where(lane == 2, g1, meta)
    meta = jnp.where(lane == 3, g2, meta)
    meta = jnp.where(lane == 4, rank1, meta)
    meta = jnp.where(lane == 5, rank2, meta)
    meta_ref[...] = meta


def _router(x, g, w_router_padded, n_experts):
    m, d = x.shape
    tm = ROUTER_ROW_TILE
    return pl.pallas_call(
        functools.partial(_router_kernel, n_experts),
        grid=(m // tm,),
        in_specs=[pl.BlockSpec((tm, d), lambda i: (i, 0)),
                  pl.BlockSpec((1, d), lambda i: (0, 0)),
                  pl.BlockSpec((d, LANES), lambda i: (0, 0))],
        out_specs=[pl.BlockSpec((tm, d // 2), lambda i: (i, 0)),
                   pl.BlockSpec((tm, LANES), lambda i: (i, 0)),
                   pl.BlockSpec((SUBLANES, LANES), lambda i: (0, 0))],
        out_shape=[jax.ShapeDtypeStruct((m, d // 2), jnp.uint32),
                   jax.ShapeDtypeStruct((m, LANES), F32),
                   jax.ShapeDtypeStruct((SUBLANES, LANES), F32)],
        scratch_shapes=[pltpu.VMEM((SUBLANES, LANES), F32)],
        compiler_params=_cparams("arbitrary"),
        name="router",
    )(x, g, w_router_padded)


def _row_copy(src_hbm, idx, dst_ref, r, sem):
    return pltpu.make_async_copy(src_hbm.at[pl.ds(idx, 1)], dst_ref.at[pl.ds(r, 1)], sem)


def _experts_kernel(rows_per_step, be_ref, nv_ref, nused_ref, off_ref, stok_ref,
                    hn_hbm, wg_ref, wu_ref, wd_ref, o_ref,
                    xbuf, xb_ref, wgb_ref, wub_ref, wdb_ref, sem):
    t = pl.program_id(0)
    j = pl.program_id(1)
    nused = nused_ref[0]
    tr = xb_ref.shape[0]
    gr = xbuf.shape[0]
    half = tr // 2
    dh = xbuf.shape[1]

    @pl.when(j == 0)
    def _():
        o_ref[...] = jnp.zeros_like(o_ref)

    @pl.when((t == 0) & (j == 0))
    def _():
        off0 = off_ref[0]

        def issue(r, c):
            _row_copy(hn_hbm, stok_ref[off0 + r], xbuf, r, sem).start()
            return c

        lax.fori_loop(0, gr, issue, 0, unroll=8)

    @pl.when((j == 0) & (t <= nused))
    def _():
        pltpu.make_async_copy(hn_hbm.at[pl.ds(0, gr)], xbuf, sem).wait()

    @pl.when((j == 0) & (t < nused))
    def _():
        w = xbuf[0:tr, :]
        xb_ref[:, 0:dh] = pltpu.bitcast(w << 16, F32).astype(BF16)
        xb_ref[:, dh:] = pltpu.bitcast(w & jnp.uint32(0xFFFF0000), F32).astype(BF16)

    @pl.when(t < nused)
    def _():
        base = off_ref[t + 1] + j * rows_per_step
        for k in range(rows_per_step):
            _row_copy(hn_hbm, stok_ref[base + k], xbuf, j * rows_per_step + k, sem).start()

        wgb_ref[...] = wg_ref[...].astype(BF16)
        wub_ref[...] = wu_ref[...].astype(BF16)
        wdb_ref[...] = wd_ref[...].astype(BF16)

        def block(rows):
            x = xb_ref[rows, :]
            a = jnp.dot(x, wgb_ref[...], preferred_element_type=F32)
            b = jnp.dot(x, wub_ref[...], preferred_element_type=F32)
            h = (jax.nn.silu(a) * b).astype(BF16)
            o_ref[rows, :] += jnp.dot(h, wdb_ref[...], preferred_element_type=F32)

        block(pl.ds(0, half))

        @pl.when(nv_ref[t] > half)
        def _():
            block(pl.ds(half, half))


def _experts(hn_packed, sorted_tok, tile_off, block_e, n_valid, n_used, wg, wu, wd):
    dh = hn_packed.shape[1]
    d = 2 * dh
    f = wg.shape[2]
    tr = MOE_ROW_TILE
    tf = W_COL_TILE
    n_tiles = block_e.shape[0]
    nj = f // tf
    rows_per_step = -(-tr // nj)
    gr = rows_per_step * nj
    stok = jnp.pad(sorted_tok, (0, gr))
    offs = jnp.pad(jnp.clip(tile_off, 0, sorted_tok.shape[0]), (0, 1))

    def tile(t, nu):
        return jnp.minimum(t, jnp.maximum(nu[0] - 1, 0))

    def col(t, j, nu):
        return jnp.where(t < nu[0], j, nj - 1)

    wspec = lambda shape, imap: pl.BlockSpec(
        shape, lambda t, j, be, nv, nu, off, st: imap(be[tile(t, nu)], col(t, j, nu)))
    return pl.pallas_call(
        functools.partial(_experts_kernel, rows_per_step),
        grid_spec=pltpu.PrefetchScalarGridSpec(
            num_scalar_prefetch=5,
            grid=(n_tiles + 1, nj),
            in_specs=[pl.BlockSpec(memory_space=pl.ANY),
                      wspec((None, d, tf), lambda e, c: (e, 0, c)),
                      wspec((None, d, tf), lambda e, c: (e, 0, c)),
                      wspec((None, tf, d), lambda e, c: (e, c, 0))],
            out_specs=pl.BlockSpec((tr, d), lambda t, j, *_: (t, 0)),
            scratch_shapes=[pltpu.VMEM((gr, dh), jnp.uint32), pltpu.VMEM((tr, d), BF16),
                            pltpu.VMEM((d, tf), BF16), pltpu.VMEM((d, tf), BF16),
                            pltpu.VMEM((tf, d), BF16), pltpu.SemaphoreType.DMA(())]),
        out_shape=jax.ShapeDtypeStruct(((n_tiles + 1) * tr, d), F32),
        compiler_params=_cparams("arbitrary", "arbitrary"),
        name="moe_experts",
    )(block_e, n_valid, n_used, offs, stok, hn_packed, wg, wu, wd)


def _combine_kernel(n_first, d0_ref, d1_ref, d0n_ref, d1n_ref, x_ref, meta_ref, g_ref, yb_hbm,
                    oa_ref, ob_ref, y0, y1, sem):
    tm = x_ref.shape[0]
    i = pl.program_id(0)
    slot = lax.rem(i, 2)

    def issue_tile(da_ref, db_ref, s):
        def issue(r, c):
            _row_copy(yb_hbm, da_ref[0, 0, r], y0.at[s], r, sem.at[0, s]).start()
            _row_copy(yb_hbm, db_ref[0, 0, r], y1.at[s], r, sem.at[1, s]).start()
            return c

        lax.fori_loop(0, tm, issue, 0, unroll=8)

    @pl.when(i == 0)
    def _():
        issue_tile(d0_ref, d1_ref, 0)

    @pl.when(i + 1 < pl.num_programs(0))
    def _():
        issue_tile(d0n_ref, d1n_ref, 1 - slot)

    pltpu.make_async_copy(yb_hbm.at[pl.ds(0, tm)], y0.at[slot], sem.at[0, slot]).wait()
    pltpu.make_async_copy(yb_hbm.at[pl.ds(0, tm)], y1.at[slot], sem.at[1, slot]).wait()
    meta = meta_ref[...]
    x = x_ref[...] + (y0[slot] * meta[:, 2:3] + y1[slot] * meta[:, 3:4])

    @pl.when(i < n_first)
    def _():
        oa_ref[...] = _rms_norm_f32(x, g_ref[...])

    @pl.when(i >= n_first)
    def _():
        ob_ref[...] = _rms_norm_f32(x, g_ref[...])


def _combine(x, meta, g, yb, dest0, dest1, m_first):
    m, d = x.shape
    tm = COMBINE_ROW_TILE
    n_tiles = m // tm
    smem_blk = pl.BlockSpec((1, 1, tm), lambda i: (i, 0, 0), memory_space=pltpu.SMEM)
    smem_next = pl.BlockSpec((1, 1, tm), lambda i: (jnp.minimum(i + 1, n_tiles - 1), 0, 0),
                             memory_space=pltpu.SMEM)
    oa_spec, ob_spec = _two_part_specs(m_first // tm, (tm, d), lambda: 0)
    d0 = dest0.reshape(n_tiles, 1, tm)
    d1 = dest1.reshape(n_tiles, 1, tm)
    return pl.pallas_call(
        functools.partial(_combine_kernel, m_first // tm),
        grid=(n_tiles,),
        in_specs=[smem_blk, smem_blk, smem_next, smem_next,
                  pl.BlockSpec((tm, d), lambda i: (i, 0)),
                  pl.BlockSpec((tm, LANES), lambda i: (i, 0)),
                  pl.BlockSpec((1, d), lambda i: (0, 0)),
                  pl.BlockSpec(memory_space=pl.ANY)],
        out_specs=[oa_spec, ob_spec],
        out_shape=[jax.ShapeDtypeStruct((m_first, d), F32),
                   jax.ShapeDtypeStruct((m - m_first, d), F32)],
        scratch_shapes=[pltpu.VMEM((2, tm, d), F32), pltpu.VMEM((2, tm, d), F32),
                        pltpu.SemaphoreType.DMA((2, 2))],
        compiler_params=_cparams("arbitrary"),
        name="moe_combine",
    )(d0, d1, d0, d1, x, meta, g, yb)


def _sample_aux(state, shift, dec_seq):
    b, w1, c = state.shape
    rows = [state[:, w1 + t - shift] if t < shift else jnp.zeros((b, c), state.dtype)
            for t in range(dec_seq)]
    return jnp.stack(rows, axis=1).reshape(b * dec_seq, c)


def _seq_tails(a, col0, ncol, width, batch, seq, dec_batch, dec_seq):
    mp = batch * seq
    tails = [lax.slice(a, ((b + 1) * seq - width, col0), ((b + 1) * seq, col0 + ncol))
             for b in range(batch)]
    smp = lax.slice(a, (mp, col0), (mp + dec_batch * dec_seq, col0 + ncol))
    smp = smp.reshape(dec_batch, dec_seq, ncol)[:, dec_seq - width:]
    return jnp.stack(tails, axis=0), smp


def kernel(x_prompt, x_sample, state_lru_h, state_lru_conv, state_sconv, norm_mix, norm_ffn, norm_final, lru_w_in, lru_conv_w, lru_conv_b, lru_w_rgate, lru_b_rgate, lru_w_igate, lru_b_igate, lru_lambda, lru_w_out, sc_w_in, sc_conv_w, sc_w_out, ffn_w_gate, ffn_w_up, ffn_w_down, moe_w_router, moe_w_gate, moe_w_up, moe_w_down):
    batch, seq, d = x_prompt.shape
    dec_batch, dec_seq, _ = x_sample.shape
    mp, ms = batch * seq, dec_batch * dec_seq
    m = mp + ms
    depth = norm_mix.shape[0]
    n_experts = moe_w_router.shape[2]
    conv_a = lru_conv_w.shape[1]
    conv_b = sc_conv_w.shape[1]
    assert depth == 2 and lru_w_in.shape[0] == 1 and sc_w_in.shape[0] == 1
    assert seq & (seq - 1) == 0 and dec_seq & (dec_seq - 1) == 0
    assert seq % ROW_TILE == 0 and ms == ROW_TILE and ROW_TILE % dec_seq == 0
    assert dec_seq >= conv_a - 1 and dec_seq >= conv_b - 1 and dec_seq == SUBLANES
    assert n_experts <= LANES and m % ROUTER_ROW_TILE == 0 and m % COMBINE_ROW_TILE == 0
    n_prompt_tiles = mp // ROW_TILE

    xp = x_prompt.reshape(mp, d)
    xs_in = x_sample.reshape(ms, d)
    row = lambda v: v.reshape(1, -1)

    d_rnn = lru_w_out.shape[1]
    proj = _norm_matmul(xp, xs_in, row(norm_mix[0]), lru_w_in[0])
    h0_rows = jnp.pad(state_lru_h[0][:, None, :], ((0, 0), (0, dec_seq - 1), (0, 0)))
    aux = [_sample_aux(state_lru_conv[0], s, dec_seq) for s in (1, 2, 3)]
    aux.append(h0_rows.reshape(ms, d_rnn))
    y, h_all = _rglru_core(proj, aux, lru_conv_w[0], row(lru_conv_b[0]), row(lru_b_rgate[0]),
                           row(lru_b_igate[0]), row(lru_lambda[0]),
                           lru_w_rgate[0].astype(BF16), lru_w_igate[0].astype(BF16),
                           n_prompt_tiles, seq, dec_seq)
    x = _matmul_residual(y, lru_w_out[0].astype(BF16), xp, xs_in)

    x = _ffn(x, row(norm_ffn[0]), ffn_w_gate[0], ffn_w_up[0], ffn_w_down[0])

    bgate, cv = _sc_inproj(x, row(norm_mix[1]), sc_w_in[0])
    aux = [_sample_aux(state_sconv[0], s, dec_seq) for s in (1, 2)]
    zb = _sconv_core(bgate, cv, aux, sc_conv_w[0], n_prompt_tiles, seq, dec_seq)
    x = _matmul_residual(zb, sc_w_out[0].astype(BF16), x, x)

    w_router = jnp.pad(moe_w_router[0], ((0, 0), (0, LANES - n_experts)))
    hn, meta, counts = _router(x, row(norm_ffn[1]), w_router, n_experts)
    tr = MOE_ROW_TILE
    n_tiles = -(-(m * TOP_K) // tr) + n_experts
    counts = counts[0, :n_experts].astype(jnp.int32)
    padded = (counts + tr - 1) // tr * tr
    pad_end = jnp.cumsum(padded)
    pad_start = pad_end - padded
    e0 = meta[:, 0].astype(jnp.int32)
    e1 = meta[:, 1].astype(jnp.int32)
    dest0 = pad_start[e0] + meta[:, 4].astype(jnp.int32)
    dest1 = pad_start[e1] + meta[:, 5].astype(jnp.int32)
    order = jnp.argsort(jnp.stack([e0, e1], axis=1).reshape(-1), stable=True)
    sorted_tok = (order // TOP_K).astype(jnp.int32)
    start = jnp.cumsum(counts) - counts
    block_start = jnp.arange(n_tiles, dtype=jnp.int32) * tr
    block_e = jnp.sum((pad_end[None, :] <= block_start[:, None]).astype(jnp.int32), axis=1)
    block_e = jnp.minimum(block_e, n_experts - 1)
    in_expert = block_start - pad_start[block_e]
    n_valid = jnp.clip(counts[block_e] - in_expert, 0, tr).astype(jnp.int32)
    n_used = (pad_end[-1:] // tr).astype(jnp.int32)
    yb = _experts(hn, sorted_tok, (start[block_e] + in_expert).astype(jnp.int32), block_e,
                  n_valid, n_used, moe_w_gate[0], moe_w_up[0], moe_w_down[0])
    y_prompt, y_sample = _combine(x, meta, row(norm_final), yb, dest0, dest1, mp)

    tails = functools.partial(_seq_tails, batch=batch, seq=seq, dec_batch=dec_batch, dec_seq=dec_seq)
    h_p, h_s = tails(h_all, 0, d_rnn, 1)
    u_p, u_s = tails(proj, d_rnn, d_rnn, conv_a - 1)
    cv_p, cv_s = tails(cv, 0, cv.shape[1], conv_b - 1)
    return (y_prompt.reshape(batch, seq, d), y_sample.reshape(dec_batch, dec_seq, d),
            h_p[:, 0][None], u_p[None], cv_p[None], h_s[:, 0][None], u_s[None], cv_s[None])
```

```python
import functools

import jax
import jax.numpy as jnp
from jax import lax
from jax.experimental import pallas as pl
from jax.experimental.pallas import tpu as pltpu

F32 = jnp.float32
BF16 = jnp.bfloat16

EPS = 1e-6
RG_C = 8.0
TOP_K = 2

LANES = 128
SUBLANES = 8
ROW_TILE = 1024
OUT_ROW_TILE = 512
ROUTER_ROW_TILE = 512
COL_TILE = 512
W_COL_TILE = 256
MOE_ROW_TILE = 1024
COMBINE_ROW_TILE = 256
SCAN_CHUNK = 64
VMEM_LIMIT = 60 * 1024 * 1024


def _cparams(*sem):
    return pltpu.CompilerParams(dimension_semantics=sem, vmem_limit_bytes=VMEM_LIMIT)


def _rms_norm_f32(x, g):
    y = x * lax.rsqrt(jnp.mean(x * x, axis=-1, keepdims=True) + EPS)
    return y * g


def _two_part_specs(n_first, block, col_of, second_mode=None):
    first = pl.BlockSpec(block, lambda i, *a: (jnp.minimum(i, n_first - 1), col_of(*a)))
    second = pl.BlockSpec(block, lambda i, *a: (jnp.maximum(i - n_first, 0), col_of(*a)),
                          pipeline_mode=second_mode)
    return first, second


def _norm_matmul_kernel(n_first, n_gelu, xa_ref, xb_ref, g_ref, w_ref, o_ref, hn_ref):
    i = pl.program_id(0)
    j = pl.program_id(1)

    def project():
        return jnp.dot(hn_ref[...], w_ref[...].astype(BF16), preferred_element_type=F32)

    @pl.when((j == 0) & (i < n_first))
    def _():
        hn_ref[...] = _rms_norm_f32(xa_ref[...], g_ref[...]).astype(BF16)

    @pl.when((j == 0) & (i >= n_first))
    def _():
        hn_ref[...] = _rms_norm_f32(xb_ref[...], g_ref[...]).astype(BF16)

    @pl.when(j < n_gelu)
    def _():
        o_ref[...] = jax.nn.gelu(project())

    @pl.when(j >= n_gelu)
    def _():
        o_ref[...] = project()


def _norm_matmul(xa, xb, g, w, n_gelu_cols):
    d = xa.shape[1]
    m = xa.shape[0] + xb.shape[0]
    n = w.shape[1]
    assert xb.shape[0] == ROW_TILE
    xa_spec, xb_spec = _two_part_specs(xa.shape[0] // ROW_TILE, (ROW_TILE, d), lambda j: 0,
                                       second_mode=pl.Buffered(1))
    return pl.pallas_call(
        functools.partial(_norm_matmul_kernel, xa.shape[0] // ROW_TILE, n_gelu_cols // COL_TILE),
        grid=(m // ROW_TILE, n // COL_TILE),
        in_specs=[xa_spec, xb_spec,
                  pl.BlockSpec((1, d), lambda i, j: (0, 0)),
                  pl.BlockSpec((d, COL_TILE), lambda i, j: (0, j))],
        out_specs=pl.BlockSpec((ROW_TILE, COL_TILE), lambda i, j: (i, j)),
        out_shape=jax.ShapeDtypeStruct((m, n), F32),
        scratch_shapes=[pltpu.VMEM((ROW_TILE, d), BF16)],
        compiler_params=_cparams("arbitrary", "arbitrary"),
        name="norm_matmul",
    )(xa, xb, g, w)


def _sc_inproj_kernel(x_ref, g_ref, wb_ref, wc_ref, wv_ref, b_ref, cv_ref, hn_ref):
    @pl.when(pl.program_id(1) == 0)
    def _():
        hn_ref[...] = _rms_norm_f32(x_ref[...], g_ref[...]).astype(BF16)

    hn = hn_ref[...]
    b_ref[...] = jnp.dot(hn, wb_ref[...].astype(BF16), preferred_element_type=F32)
    c = jnp.dot(hn, wc_ref[...].astype(BF16), preferred_element_type=F32)
    v = jnp.dot(hn, wv_ref[...].astype(BF16), preferred_element_type=F32)
    cv_ref[...] = c * v


def _sc_inproj(x, g, w):
    m, d = x.shape
    dc = w.shape[1] // 3
    nj = dc // COL_TILE
    wspec = lambda off: pl.BlockSpec((d, COL_TILE), lambda i, j: (0, off * nj + j))
    ospec = pl.BlockSpec((ROW_TILE, COL_TILE), lambda i, j: (i, j))
    return pl.pallas_call(
        _sc_inproj_kernel,
        grid=(m // ROW_TILE, nj),
        in_specs=[pl.BlockSpec((ROW_TILE, d), lambda i, j: (i, 0)),
                  pl.BlockSpec((1, d), lambda i, j: (0, 0)),
                  wspec(0), wspec(1), wspec(2)],
        out_specs=[ospec, ospec],
        out_shape=[jax.ShapeDtypeStruct((m, dc), F32)] * 2,
        scratch_shapes=[pltpu.VMEM((ROW_TILE, d), BF16)],
        compiler_params=_cparams("arbitrary", "arbitrary"),
        name="sc_inproj",
    )(x, g, w, w, w)


def _matmul_residual_kernel(n_first, y_ref, w_ref, ra_ref, rb_ref, o_ref):
    o_ref[...] = jnp.dot(y_ref[...], w_ref[...], preferred_element_type=F32)
    i = pl.program_id(0)

    @pl.when(i < n_first)
    def _():
        o_ref[...] += ra_ref[...]

    @pl.when(i >= n_first)
    def _():
        o_ref[...] += rb_ref[...]


def _matmul_residual(y, w, res_a, res_b):
    m, k = y.shape
    n = w.shape[1]
    tm = OUT_ROW_TILE
    ra_spec, rb_spec = _two_part_specs(res_a.shape[0] // tm, (tm, n), lambda: 0)
    return pl.pallas_call(
        functools.partial(_matmul_residual_kernel, res_a.shape[0] // tm),
        grid=(m // tm,),
        in_specs=[pl.BlockSpec((tm, k), lambda i: (i, 0)),
                  pl.BlockSpec((k, n), lambda i: (0, 0)),
                  ra_spec, rb_spec],
        out_specs=pl.BlockSpec((tm, n), lambda i: (i, 0)),
        out_shape=jax.ShapeDtypeStruct((m, n), F32),
        compiler_params=_cparams("arbitrary"),
        name="matmul_residual",
    )(y, w, res_a, res_b)


def _pos_in_seq(r0, rows, seq_len):
    return (r0 + lax.broadcasted_iota(jnp.int32, (rows, 1), 0)) & (seq_len - 1)


def _sigmoid(x):
    return 0.5 * jnp.tanh(0.5 * x) + 0.5


def _run_chunks(is_sample, n_chunks, body):
    @pl.when(is_sample)
    def _():
        lax.fori_loop(0, n_chunks, lambda k, c: body(True, k) or c, 0)

    @pl.when(jnp.logical_not(is_sample))
    def _():
        lax.fori_loop(0, n_chunks, lambda k, c: body(False, k) or c, 0)


def _rglru_kernel(n_prompt_tiles, tiles_per_seq, seq_s,
                  gate_ref, u_ref, p1_ref, p2_ref, p3_ref, h0_ref,
                  cw_ref, cb_ref, br_ref, bi_ref, lam_ref, wr_ref, wi_ref,
                  y_ref, h_ref, ubuf, uc_ref, rp_ref, ip_ref, hc_ref):
    i = pl.program_id(1)
    cblk = u_ref.shape[1]
    is_sample = i >= n_prompt_tiles

    @pl.when(is_sample | (lax.rem(i, tiles_per_seq) == 0))
    def _():
        ubuf[0:SUBLANES, :] = jnp.zeros((SUBLANES, cblk), F32)
        hc_ref[...] = jnp.zeros_like(hc_ref)

    ubuf[SUBLANES:, :] = u_ref[...]
    cw = cw_ref[...]
    cb = cb_ref[...]
    rc = SCAN_CHUNK

    def conv_chunk(masked, k):
        r0 = pl.multiple_of(k * rc, rc)
        rows = pl.ds(r0, rc)
        ext = ubuf[pl.ds(r0, rc + SUBLANES), :]
        if masked:
            t = _pos_in_seq(r0, rc, seq_s)

        def tap(s, p_ref):
            shifted = pltpu.roll(ext, s, axis=0)[SUBLANES:]
            return jnp.where(t >= s, shifted, p_ref[rows, :]) if masked else shifted

        conv = tap(3, p3_ref) * cw[0:1] + tap(2, p2_ref) * cw[1:2]
        conv = conv + tap(1, p1_ref) * cw[2:3]
        conv = conv + ext[SUBLANES:] * cw[3:4]
        uc_ref[rows, :] = conv + cb

    _run_chunks(is_sample, ROW_TILE // rc, conv_chunk)

    ub = uc_ref[...].astype(BF16)
    rp_ref[...] = jnp.dot(ub, wr_ref[...], preferred_element_type=F32) + br_ref[...]
    ip_ref[...] = jnp.dot(ub, wi_ref[...], preferred_element_type=F32) + bi_ref[...]

    sp = jax.nn.softplus(-lam_ref[...])
    sub = lax.broadcasted_iota(jnp.int32, (1, SUBLANES, 1), 1)

    def scan_chunk(masked, k):
        r0 = pl.multiple_of(k * rc, rc)
        rows = pl.ds(r0, rc)
        uc = uc_ref[rows, :]
        r = _sigmoid(rp_ref[rows, :])
        ig = _sigmoid(ip_ref[rows, :])
        log_a = (-RG_C * r) * sp
        a = jnp.exp(log_a)
        mult = jnp.sqrt(-jnp.tanh(log_a) * (a * a + 1.0))
        b = mult * ig * uc
        if masked:
            first = _pos_in_seq(r0, rc, seq_s) == 0
            b = jnp.where(first, b + a * h0_ref[rows, :], b)
            a = jnp.where(first, 0.0, a)

        a3 = a.reshape(rc // SUBLANES, SUBLANES, cblk)
        b3 = b.reshape(rc // SUBLANES, SUBLANES, cblk)
        for s in (1, 2, 4):
            a_s = pltpu.roll(a3, s, axis=1)
            b_s = pltpu.roll(b3, s, axis=1)
            m = sub >= s
            b3 = jnp.where(m, a3 * b_s + b3, b3)
            a3 = jnp.where(m, a3 * a_s, a3)

        hc = hc_ref[...]
        hs = []
        for g in range(rc // SUBLANES):
            hg = a3[g] * hc + b3[g]
            hc = jnp.broadcast_to(hg[SUBLANES - 1:SUBLANES], (SUBLANES, cblk))
            hs.append(hg)
        hc_ref[...] = hc
        h = jnp.concatenate(hs, axis=0)
        h_ref[rows, :] = h
        y_ref[rows, :] = (h * gate_ref[rows, :]).astype(BF16)

    _run_chunks(is_sample, ROW_TILE // rc, scan_chunk)
    ubuf[0:SUBLANES, :] = ubuf[ROW_TILE:ROW_TILE + SUBLANES, :]


def _rglru_core(proj, aux, conv_w, conv_b, b_r, b_i, lam, w_r, w_i, n_prompt_tiles,
                tiles_per_seq, seq_s):
    m = proj.shape[0]
    nb, cblk = w_r.shape[0], w_r.shape[1]
    c = nb * cblk
    n_tiles = m // ROW_TILE
    row_blk = lambda off: pl.BlockSpec((ROW_TILE, cblk), lambda cb, i: (i, off + cb))
    aux_blk = pl.BlockSpec((ROW_TILE, cblk), lambda cb, i: (0, cb))
    vec_blk = lambda rows: pl.BlockSpec((rows, cblk), lambda cb, i: (0, cb))
    w_blk = pl.BlockSpec((None, cblk, cblk), lambda cb, i: (cb, 0, 0))
    out_blk = pl.BlockSpec((ROW_TILE, cblk), lambda cb, i: (i, cb))
    return pl.pallas_call(
        functools.partial(_rglru_kernel, n_prompt_tiles, tiles_per_seq, seq_s),
        grid=(nb, n_tiles),
        in_specs=[row_blk(0), row_blk(nb), aux_blk, aux_blk, aux_blk, aux_blk,
                  vec_blk(conv_w.shape[0]), vec_blk(1), vec_blk(1), vec_blk(1), vec_blk(1),
                  w_blk, w_blk],
        out_specs=[out_blk, out_blk],
        out_shape=[jax.ShapeDtypeStruct((m, c), BF16), jax.ShapeDtypeStruct((m, c), F32)],
        scratch_shapes=[pltpu.VMEM((ROW_TILE + SUBLANES, cblk), F32),
                        pltpu.VMEM((ROW_TILE, cblk), F32),
                        pltpu.VMEM((ROW_TILE, cblk), F32),
                        pltpu.VMEM((ROW_TILE, cblk), F32),
                        pltpu.VMEM((SUBLANES, cblk), F32)],
        compiler_params=_cparams("arbitrary", "arbitrary"),
        name="rglru_core",
    )(proj, proj, *aux, conv_w, conv_b, b_r, b_i, lam, w_r, w_i)


def _sconv_kernel(n_prompt_tiles, tiles_per_seq, seq_s,
                  b_ref, cv_ref, q1_ref, q2_ref, cw_ref, o_ref, cvbuf):
    i = pl.program_id(1)
    cblk = cv_ref.shape[1]
    is_sample = i >= n_prompt_tiles

    @pl.when(is_sample | (lax.rem(i, tiles_per_seq) == 0))
    def _():
        cvbuf[0:SUBLANES, :] = jnp.zeros((SUBLANES, cblk), F32)

    cvbuf[SUBLANES:, :] = cv_ref[...]
    cw = cw_ref[...]
    rc = SCAN_CHUNK

    def conv_chunk(masked, k):
        r0 = pl.multiple_of(k * rc, rc)
        rows = pl.ds(r0, rc)
        ext = cvbuf[pl.ds(r0, rc + SUBLANES), :]
        if masked:
            t = _pos_in_seq(r0, rc, seq_s)

        def tap(s, q_ref):
            shifted = pltpu.roll(ext, s, axis=0)[SUBLANES:]
            return jnp.where(t >= s, shifted, q_ref[rows, :]) if masked else shifted

        z = tap(2, q2_ref) * cw[0:1] + tap(1, q1_ref) * cw[1:2]
        z = z + ext[SUBLANES:] * cw[2:3]
        o_ref[rows, :] = (b_ref[rows, :] * z).astype(BF16)

    _run_chunks(is_sample, ROW_TILE // rc, conv_chunk)
    cvbuf[0:SUBLANES, :] = cvbuf[ROW_TILE:ROW_TILE + SUBLANES, :]


def _sconv_core(bgate, cv, aux, conv_w, n_prompt_tiles, tiles_per_seq, seq_s):
    m, c = cv.shape
    cblk = COL_TILE
    blk = pl.BlockSpec((ROW_TILE, cblk), lambda cb, i: (i, cb))
    aux_blk = pl.BlockSpec((ROW_TILE, cblk), lambda cb, i: (0, cb))
    return pl.pallas_call(
        functools.partial(_sconv_kernel, n_prompt_tiles, tiles_per_seq, seq_s),
        grid=(c // cblk, m // ROW_TILE),
        in_specs=[blk, blk, aux_blk, aux_blk,
                  pl.BlockSpec((conv_w.shape[0], cblk), lambda cb, i: (0, cb))],
        out_specs=blk,
        out_shape=jax.ShapeDtypeStruct((m, c), BF16),
        scratch_shapes=[pltpu.VMEM((ROW_TILE + SUBLANES, cblk), F32)],
        compiler_params=_cparams("arbitrary", "arbitrary"),
        name="sconv_core",
    )(bgate, cv, *aux, conv_w)


def _ffn_kernel(x_ref, g_ref, wg_ref, wu_ref, wd_ref, o_ref, hn_ref):
    @pl.when(pl.program_id(1) == 0)
    def _():
        x = x_ref[...]
        hn_ref[...] = _rms_norm_f32(x, g_ref[...]).astype(BF16)
        o_ref[...] = x

    hn = hn_ref[...]
    a = jnp.dot(hn, wg_ref[...].astype(BF16), preferred_element_type=F32)
    b = jnp.dot(hn, wu_ref[...].astype(BF16), preferred_element_type=F32)
    h = (jax.nn.silu(a) * b).astype(BF16)
    o_ref[...] += jnp.dot(h, wd_ref[...].astype(BF16), preferred_element_type=F32)


def _ffn(x, g, wg, wu, wd):
    m, d = x.shape
    f = wg.shape[1]
    tm = ROW_TILE
    tf = W_COL_TILE
    return pl.pallas_call(
        _ffn_kernel,
        grid=(m // tm, f // tf),
        in_specs=[pl.BlockSpec((tm, d), lambda i, j: (i, 0)),
                  pl.BlockSpec((1, d), lambda i, j: (0, 0)),
                  pl.BlockSpec((d, tf), lambda i, j: (0, j)),
                  pl.BlockSpec((d, tf), lambda i, j: (0, j)),
                  pl.BlockSpec((tf, d), lambda i, j: (j, 0))],
        out_specs=pl.BlockSpec((tm, d), lambda i, j: (i, 0)),
        out_shape=jax.ShapeDtypeStruct((m, d), F32),
        scratch_shapes=[pltpu.VMEM((tm, d), BF16)],
        compiler_params=_cparams("arbitrary", "arbitrary"),
        name="ffn",
    )(x, g, wg, wu, wd)


def _router_kernel(n_experts, x_ref, g_ref, whi_ref, wlo_ref, hn_ref, meta_ref, cnt_ref, run_ref):
    @pl.when(pl.program_id(0) == 0)
    def _():
        run_ref[...] = jnp.zeros_like(run_ref)

    tm = x_ref.shape[0]
    hn = _rms_norm_f32(x_ref[...], g_ref[...])
    hn_hi = hn.astype(BF16)
    hn_hi32 = hn_hi.astype(F32)
    bits = pltpu.bitcast(hn_hi32, jnp.uint32)
    half = hn.shape[1] // 2
    hn_ref[...] = (bits[:, half:] & jnp.uint32(0xFFFF0000)) | (bits[:, :half] >> 16)
    hn_lo = (hn - hn_hi32).astype(BF16)
    logits = (jnp.dot(hn_hi, whi_ref[...], preferred_element_type=F32)
              + (jnp.dot(hn_lo, whi_ref[...], preferred_element_type=F32)
                 + jnp.dot(hn_hi, wlo_ref[...], preferred_element_type=F32)))
    lane = lax.broadcasted_iota(jnp.int32, (tm, LANES), 1).astype(F32)
    neg = jnp.float32(-jnp.inf)
    lg = jnp.where(lane < n_experts, logits, neg)
    m1 = jnp.max(lg, axis=1, keepdims=True)
    i1 = jnp.min(jnp.where(lg == m1, lane, float(LANES)), axis=1, keepdims=True)
    lg2 = jnp.where(lane == i1, neg, lg)
    m2 = jnp.max(lg2, axis=1, keepdims=True)
    i2 = jnp.min(jnp.where(lg2 == m2, lane, float(LANES)), axis=1, keepdims=True)
    ex = jnp.exp(m2 - m1)
    g1 = 1.0 / (1.0 + ex)
    g2 = ex / (1.0 + ex)

    sel1 = lane == i1
    sel2 = lane == i2
    onehot = jnp.where(sel1 | sel2, 1.0, 0.0)
    rr = lax.broadcasted_iota(jnp.int32, (tm, tm), 0)
    cc = lax.broadcasted_iota(jnp.int32, (tm, tm), 1)
    tri = jnp.where(rr > cc, 1.0, 0.0).astype(BF16)
    before = jnp.dot(tri, onehot.astype(BF16), preferred_element_type=F32) + run_ref[0:1, :]
    rank1 = jnp.sum(jnp.where(sel1, before, 0.0), axis=1, keepdims=True)
    rank2 = jnp.sum(jnp.where(sel2, before, 0.0), axis=1, keepdims=True)
    run = run_ref[0:1, :] + jnp.sum(onehot, axis=0, keepdims=True)
    run_ref[...] = jnp.broadcast_to(run, run_ref.shape)
    cnt_ref[...] = jnp.broadcast_to(run, cnt_ref.shape)

    meta = jnp.where(lane == 0, i1, 0.0)
    meta = jnp.where(lane == 1, i2, meta)
    meta = jnp.where(lane == 2, g1, meta)
    meta = jnp.where(lane == 3, g2, meta)
    meta = jnp.where(lane == 4, rank1, meta)
    meta = jnp.where(lane == 5, rank2, meta)
    meta_ref[...] = meta


def _router(x, g, w_router_padded, n_experts):
    m, d = x.shape
    tm = ROUTER_ROW_TILE
    w_hi = w_router_padded.astype(BF16)
    w_lo = (w_router_padded - w_hi.astype(F32)).astype(BF16)
    return pl.pallas_call(
        functools.partial(_router_kernel, n_experts),
        grid=(m // tm,),
        in_specs=[pl.BlockSpec((tm, d), lambda i: (i, 0)),
                  pl.BlockSpec((1, d), lambda i: (0, 0)),
                  pl.BlockSpec((d, LANES), lambda i: (0, 0)),
                  pl.BlockSpec((d, LANES), lambda i: (0, 0))],
        out_specs=[pl.BlockSpec((tm, d // 2), lambda i: (i, 0)),
                   pl.BlockSpec((tm, LANES), lambda i: (i, 0)),
                   pl.BlockSpec((SUBLANES, LANES), lambda i: (0, 0))],
        out_shape=[jax.ShapeDtypeStruct((m, d // 2), jnp.uint32),
                   jax.ShapeDtypeStruct((m, LANES), F32),
                   jax.ShapeDtypeStruct((SUBLANES, LANES), F32)],
        scratch_shapes=[pltpu.VMEM((SUBLANES, LANES), F32)],
        compiler_params=_cparams("arbitrary"),
        name="router",
    )(x, g, w_hi, w_lo)


def _row_copy(src_hbm, idx, dst_ref, r, sem):
    return pltpu.make_async_copy(src_hbm.at[pl.ds(idx, 1)], dst_ref.at[pl.ds(r, 1)], sem)


def _experts_kernel(rows_per_step, be_ref, nv_ref, nused_ref, off_ref, stok_ref,
                    hn_hbm, wg_ref, wu_ref, wd_ref, o_ref,
                    xbuf, xb_ref, sem):
    t = pl.program_id(0)
    j = pl.program_id(1)
    nused = nused_ref[0]
    tr = xb_ref.shape[0]
    gr = xbuf.shape[0]
    half = tr // 2
    dh = xbuf.shape[1]

    @pl.when(j == 0)
    def _():
        o_ref[...] = jnp.zeros_like(o_ref)

    @pl.when((t == 0) & (j == 0))
    def _():
        off0 = off_ref[0]

        def issue(r, c):
            _row_copy(hn_hbm, stok_ref[off0 + r], xbuf, r, sem).start()
            return c

        lax.fori_loop(0, gr, issue, 0, unroll=8)

    @pl.when((j == 0) & (t <= nused))
    def _():
        pltpu.make_async_copy(hn_hbm.at[pl.ds(0, gr)], xbuf, sem).wait()

    @pl.when((j == 0) & (t < nused))
    def _():
        w = xbuf[0:tr, :]
        xb_ref[:, 0:dh] = pltpu.bitcast(w << 16, F32).astype(BF16)
        xb_ref[:, dh:] = pltpu.bitcast(w & jnp.uint32(0xFFFF0000), F32).astype(BF16)

    def step(n_rows):
        base = off_ref[t + 1] + j * rows_per_step
        for k in range(rows_per_step):
            _row_copy(hn_hbm, stok_ref[base + k], xbuf, j * rows_per_step + k, sem).start()
        rows = pl.ds(0, n_rows)
        x = xb_ref[rows, :]
        a = jnp.dot(x, wg_ref[...].astype(BF16), preferred_element_type=F32)
        b = jnp.dot(x, wu_ref[...].astype(BF16), preferred_element_type=F32)
        h = (jax.nn.silu(a) * b).astype(BF16)
        o_ref[rows, :] += jnp.dot(h, wd_ref[...].astype(BF16), preferred_element_type=F32)

    @pl.when((t < nused) & (nv_ref[t] > half))
    def _():
        step(tr)

    @pl.when((t < nused) & (nv_ref[t] <= half))
    def _():
        step(half)


def _experts(hn_packed, sorted_tok, tile_off, block_e, n_valid, n_used, wg, wu, wd):
    dh = hn_packed.shape[1]
    d = 2 * dh
    f = wg.shape[2]
    tr = MOE_ROW_TILE
    tf = W_COL_TILE
    n_tiles = block_e.shape[0]
    nj = f // tf
    rows_per_step = -(-tr // nj)
    gr = rows_per_step * nj
    stok = jnp.pad(sorted_tok, (0, gr))
    offs = jnp.pad(jnp.clip(tile_off, 0, sorted_tok.shape[0]), (0, 1))

    def tile(t, nu):
        return jnp.minimum(t, jnp.maximum(nu[0] - 1, 0))

    def col(t, j, nu):
        return jnp.where(t < nu[0], j, nj - 1)

    wspec = lambda shape, imap: pl.BlockSpec(
        shape, lambda t, j, be, nv, nu, off, st: imap(be[tile(t, nu)], col(t, j, nu)))
    return pl.pallas_call(
        functools.partial(_experts_kernel, rows_per_step),
        grid_spec=pltpu.PrefetchScalarGridSpec(
            num_scalar_prefetch=5,
            grid=(n_tiles + 1, nj),
            in_specs=[pl.BlockSpec(memory_space=pl.ANY),
                      wspec((None, d, tf), lambda e, c: (e, 0, c)),
                      wspec((None, d, tf), lambda e, c: (e, 0, c)),
                      wspec((None, tf, d), lambda e, c: (e, c, 0))],
            out_specs=pl.BlockSpec((tr, d), lambda t, j, *_: (t, 0)),
            scratch_shapes=[pltpu.VMEM((gr, dh), jnp.uint32), pltpu.VMEM((tr, d), BF16),
                            pltpu.SemaphoreType.DMA(())]),
        out_shape=jax.ShapeDtypeStruct(((n_tiles + 1) * tr, d), F32),
        compiler_params=_cparams("arbitrary", "arbitrary"),
        name="moe_experts",
    )(block_e, jnp.pad(n_valid, (0, 1)), n_used, offs, stok, hn_packed, wg, wu, wd)


def _combine_kernel(n_first, d0_ref, d1_ref, d0n_ref, d1n_ref, x_ref, meta_ref, g_ref, yb_hbm,
                    oa_ref, ob_ref, y0, y1, sem):
    tm = x_ref.shape[0]
    i = pl.program_id(0)
    slot = lax.rem(i, 2)

    def issue_tile(da_ref, db_ref, s):
        def issue(r, c):
            _row_copy(yb_hbm, da_ref[0, 0, r], y0.at[s], r, sem.at[0, s]).start()
            _row_copy(yb_hbm, db_ref[0, 0, r], y1.at[s], r, sem.at[1, s]).start()
            return c

        lax.fori_loop(0, tm, issue, 0, unroll=8)

    @pl.when(i == 0)
    def _():
        issue_tile(d0_ref, d1_ref, 0)

    @pl.when(i + 1 < pl.num_programs(0))
    def _():
        issue_tile(d0n_ref, d1n_ref, 1 - slot)

    pltpu.make_async_copy(yb_hbm.at[pl.ds(0, tm)], y0.at[slot], sem.at[0, slot]).wait()
    pltpu.make_async_copy(yb_hbm.at[pl.ds(0, tm)], y1.at[slot], sem.at[1, slot]).wait()
    meta = meta_ref[...]
    x = x_ref[...] + (y0[slot] * meta[:, 2:3] + y1[slot] * meta[:, 3:4])

    @pl.when(i < n_first)
    def _():
        oa_ref[...] = _rms_norm_f32(x, g_ref[...])

    @pl.when(i >= n_first)
    def _():
        ob_ref[...] = _rms_norm_f32(x, g_ref[...])


def _combine(x, meta, g, yb, dest0, dest1, m_first):
    m, d = x.shape
    tm = COMBINE_ROW_TILE
    n_tiles = m // tm
    smem_blk = pl.BlockSpec((1, 1, tm), lambda i: (i, 0, 0), memory_space=pltpu.SMEM)
    smem_next = pl.BlockSpec((1, 1, tm), lambda i: (jnp.minimum(i + 1, n_tiles - 1), 0, 0),
                             memory_space=pltpu.SMEM)
    oa_spec, ob_spec = _two_part_specs(m_first // tm, (tm, d), lambda: 0)
    d0 = dest0.reshape(n_tiles, 1, tm)
    d1 = dest1.reshape(n_tiles, 1, tm)
    return pl.pallas_call(
        functools.partial(_combine_kernel, m_first // tm),
        grid=(n_tiles,),
        in_specs=[smem_blk, smem_blk, smem_next, smem_next,
                  pl.BlockSpec((tm, d), lambda i: (i, 0)),
                  pl.BlockSpec((tm, LANES), lambda i: (i, 0)),
                  pl.BlockSpec((1, d), lambda i: (0, 0)),
                  pl.BlockSpec(memory_space=pl.ANY)],
        out_specs=[oa_spec, ob_spec],
        out_shape=[jax.ShapeDtypeStruct((m_first, d), F32),
                   jax.ShapeDtypeStruct((m - m_first, d), F32)],
        scratch_shapes=[pltpu.VMEM((2, tm, d), F32), pltpu.VMEM((2, tm, d), F32),
                        pltpu.SemaphoreType.DMA((2, 2))],
        compiler_params=_cparams("arbitrary"),
        name="moe_combine",
    )(d0, d1, d0, d1, x, meta, g, yb)


def _sample_aux(state, shift, dec_seq):
    b, w1, c = state.shape
    rows = [state[:, w1 + t - shift] if t < shift else jnp.zeros((b, c), state.dtype)
            for t in range(dec_seq)]
    return jnp.stack(rows, axis=1).reshape(b * dec_seq, c)


def _seq_tails(a, col0, ncol, width, batch, seq, dec_batch, dec_seq):
    mp = batch * seq
    tails = [lax.slice(a, ((b + 1) * seq - width, col0), ((b + 1) * seq, col0 + ncol))
             for b in range(batch)]
    smp = lax.slice(a, (mp, col0), (mp + dec_batch * dec_seq, col0 + ncol))
    smp = smp.reshape(dec_batch, dec_seq, ncol)[:, dec_seq - width:]
    return jnp.stack(tails, axis=0), smp


def kernel(x_prompt, x_sample, state_lru_h, state_lru_conv, state_sconv, norm_mix, norm_ffn, norm_final, lru_w_in, lru_conv_w, lru_conv_b, lru_w_rgate, lru_b_rgate, lru_w_igate, lru_b_igate, lru_lambda, lru_w_out, sc_w_in, sc_conv_w, sc_w_out, ffn_w_gate, ffn_w_up, ffn_w_down, moe_w_router, moe_w_gate, moe_w_up, moe_w_down):
    batch, seq, d = x_prompt.shape
    dec_batch, dec_seq, _ = x_sample.shape
    mp, ms = batch * seq, dec_batch * dec_seq
    m = mp + ms
    depth = norm_mix.shape[0]
    n_experts = moe_w_router.shape[2]
    conv_a = lru_conv_w.shape[1]
    conv_b = sc_conv_w.shape[1]
    assert depth == 2 and lru_w_in.shape[0] == 1 and sc_w_in.shape[0] == 1
    assert seq & (seq - 1) == 0 and dec_seq & (dec_seq - 1) == 0
    assert seq % ROW_TILE == 0 and ms == ROW_TILE and ROW_TILE % dec_seq == 0
    assert dec_seq >= conv_a - 1 and dec_seq >= conv_b - 1 and dec_seq == SUBLANES
    assert n_experts <= LANES and m % ROUTER_ROW_TILE == 0 and m % COMBINE_ROW_TILE == 0
    n_prompt_tiles = mp // ROW_TILE

    xp = x_prompt.reshape(mp, d)
    xs_in = x_sample.reshape(ms, d)
    row = lambda v: v.reshape(1, -1)

    d_rnn = lru_w_out.shape[1]
    assert d_rnn % COL_TILE == 0
    proj = _norm_matmul(xp, xs_in, row(norm_mix[0]), lru_w_in[0], n_gelu_cols=d_rnn)
    h0_rows = jnp.pad(state_lru_h[0][:, None, :], ((0, 0), (0, dec_seq - 1), (0, 0)))
    aux = [_sample_aux(state_lru_conv[0], s, dec_seq) for s in (1, 2, 3)]
    aux.append(h0_rows.reshape(ms, d_rnn))
    y, h_all = _rglru_core(proj, aux, lru_conv_w[0], row(lru_conv_b[0]), row(lru_b_rgate[0]),
                           row(lru_b_igate[0]), row(lru_lambda[0]),
                           lru_w_rgate[0].astype(BF16), lru_w_igate[0].astype(BF16),
                           n_prompt_tiles, seq // ROW_TILE, dec_seq)
    x = _matmul_residual(y, lru_w_out[0].astype(BF16), xp, xs_in)

    x = _ffn(x, row(norm_ffn[0]), ffn_w_gate[0], ffn_w_up[0], ffn_w_down[0])

    bgate, cv = _sc_inproj(x, row(norm_mix[1]), sc_w_in[0])
    aux = [_sample_aux(state_sconv[0], s, dec_seq) for s in (1, 2)]
    zb = _sconv_core(bgate, cv, aux, sc_conv_w[0], n_prompt_tiles, seq // ROW_TILE, dec_seq)
    x = _matmul_residual(zb, sc_w_out[0].astype(BF16), x, x)

    w_router = jnp.pad(moe_w_router[0], ((0, 0), (0, LANES - n_experts)))
    hn, meta, counts = _router(x, row(norm_ffn[1]), w_router, n_experts)
    tr = MOE_ROW_TILE
    n_tiles = -(-(m * TOP_K) // tr) + n_experts
    counts = counts[0, :n_experts].astype(jnp.int32)
    padded = (counts + tr - 1) // tr * tr
    pad_end = jnp.cumsum(padded)
    pad_start = pad_end - padded
    e0 = meta[:, 0].astype(jnp.int32)
    e1 = meta[:, 1].astype(jnp.int32)
    dest0 = pad_start[e0] + meta[:, 4].astype(jnp.int32)
    dest1 = pad_start[e1] + meta[:, 5].astype(jnp.int32)
    order = jnp.argsort(jnp.stack([e0, e1], axis=1).reshape(-1), stable=True)
    sorted_tok = (order // TOP_K).astype(jnp.int32)
    start = jnp.cumsum(counts) - counts
    block_start = jnp.arange(n_tiles, dtype=jnp.int32) * tr
    block_e = jnp.sum((pad_end[None, :] <= block_start[:, None]).astype(jnp.int32), axis=1)
    block_e = jnp.minimum(block_e, n_experts - 1)
    in_expert = block_start - pad_start[block_e]
    n_valid = jnp.clip(counts[block_e] - in_expert, 0, tr).astype(jnp.int32)
    n_used = (pad_end[-1:] // tr).astype(jnp.int32)
    yb = _experts(hn, sorted_tok, (start[block_e] + in_expert).astype(jnp.int32), block_e,
                  n_valid, n_used, moe_w_gate[0], moe_w_up[0], moe_w_down[0])
    y_prompt, y_sample = _combine(x, meta, row(norm_final), yb, dest0, dest1, mp)

    tails = functools.partial(_seq_tails, batch=batch, seq=seq, dec_batch=dec_batch, dec_seq=dec_seq)
    h_p, h_s = tails(h_all, 0, d_rnn, 1)
    u_p, u_s = tails(proj, d_rnn, d_rnn, conv_a - 1)
    cv_p, cv_s = tails(cv, 0, cv.shape[1], conv_b - 1)
    return (y_prompt.reshape(batch, seq, d), y_sample.reshape(dec_batch, dec_seq, d),
            h_p[:, 0][None], u_p[None], cv_p[None], h_s[:, 0][None], u_s[None], cv_s[None])
```

```python
import functools

import jax
import jax.numpy as jnp
from jax import lax
from jax.experimental import pallas as pl
from jax.experimental.pallas import tpu as pltpu

F32 = jnp.float32
BF16 = jnp.bfloat16

EPS = 1e-6
RG_C = 8.0
TOP_K = 2

LANES = 128
SUBLANES = 8
ROW_TILE = 1024
OUT_ROW_TILE = 512
ROUTER_ROW_TILE = 512
COL_TILE = 512
W_COL_TILE = 256
MOE_ROW_TILE = 1024
COMBINE_ROW_TILE = 256
SCAN_CHUNK = 64
VMEM_LIMIT = 60 * 1024 * 1024


def _cparams(*sem):
    return pltpu.CompilerParams(dimension_semantics=sem, vmem_limit_bytes=VMEM_LIMIT)


def _rms_norm_f32(x, g):
    y = x * lax.rsqrt(jnp.mean(x * x, axis=-1, keepdims=True) + EPS)
    return y * g


def _two_part_specs(n_first, block, col_of, second_mode=None):
    first = pl.BlockSpec(block, lambda i, *a: (jnp.minimum(i, n_first - 1), col_of(*a)))
    second = pl.BlockSpec(block, lambda i, *a: (jnp.maximum(i - n_first, 0), col_of(*a)),
                          pipeline_mode=second_mode)
    return first, second


def _norm_matmul_kernel(n_first, n_gelu, xa_ref, xb_ref, g_ref, w_ref, o_ref, hn_ref):
    i = pl.program_id(0)
    j = pl.program_id(1)

    def project():
        return jnp.dot(hn_ref[...], w_ref[...].astype(BF16), preferred_element_type=F32)

    @pl.when((j == 0) & (i < n_first))
    def _():
        hn_ref[...] = _rms_norm_f32(xa_ref[...], g_ref[...]).astype(BF16)

    @pl.when((j == 0) & (i >= n_first))
    def _():
        hn_ref[...] = _rms_norm_f32(xb_ref[...], g_ref[...]).astype(BF16)

    @pl.when(j < n_gelu)
    def _():
        o_ref[...] = jax.nn.gelu(project())

    @pl.when(j >= n_gelu)
    def _():
        o_ref[...] = project()


def _norm_matmul(xa, xb, g, w, n_gelu_cols):
    d = xa.shape[1]
    m = xa.shape[0] + xb.shape[0]
    n = w.shape[1]
    assert xb.shape[0] == ROW_TILE
    xa_spec, xb_spec = _two_part_specs(xa.shape[0] // ROW_TILE, (ROW_TILE, d), lambda j: 0,
                                       second_mode=pl.Buffered(1))
    return pl.pallas_call(
        functools.partial(_norm_matmul_kernel, xa.shape[0] // ROW_TILE, n_gelu_cols // COL_TILE),
        grid=(m // ROW_TILE, n // COL_TILE),
        in_specs=[xa_spec, xb_spec,
                  pl.BlockSpec((1, d), lambda i, j: (0, 0)),
                  pl.BlockSpec((d, COL_TILE), lambda i, j: (0, j))],
        out_specs=pl.BlockSpec((ROW_TILE, COL_TILE), lambda i, j: (i, j)),
        out_shape=jax.ShapeDtypeStruct((m, n), F32),
        scratch_shapes=[pltpu.VMEM((ROW_TILE, d), BF16)],
        compiler_params=_cparams("arbitrary", "arbitrary"),
        name="norm_matmul",
    )(xa, xb, g, w)


def _sc_mixer_kernel(n_prompt_tiles, tiles_per_seq, seq_s,
                     x_ref, g_ref, wb_ref, wc_ref, wv_ref, cw_ref, q1_ref, q2_ref,
                     zb_ref, tail_ref, cvs_ref, hn_ref, carry_ref):
    i = pl.program_id(0)
    j = pl.program_id(1)
    t_rows = x_ref.shape[0]
    is_sample = i >= n_prompt_tiles
    starts_seq = is_sample | (lax.rem(i, tiles_per_seq) == 0)

    @pl.when(j == 0)
    def _():
        hn_ref[...] = _rms_norm_f32(x_ref[...], g_ref[...]).astype(BF16)

    @pl.when((i == 0) & (j == 0))
    def _():
        carry_ref[...] = jnp.zeros_like(carry_ref)

    def mixer(masked):
        hn = hn_ref[...]
        cw = cw_ref[...]
        c = jnp.dot(hn, wc_ref[...].astype(BF16), preferred_element_type=F32)
        v = jnp.dot(hn, wv_ref[...].astype(BF16), preferred_element_type=F32)
        cv = c * v
        prev = jnp.where(starts_seq, 0.0, carry_ref[j])
        ext = jnp.concatenate([prev, cv], axis=0)
        if masked:
            t = _pos_in_seq(0, t_rows, seq_s)

        def tap(s, q_ref):
            shifted = pltpu.roll(ext, s, axis=0)[SUBLANES:]
            return jnp.where(t >= s, shifted, q_ref[...]) if masked else shifted

        z = tap(2, q2_ref) * cw[0:1] + tap(1, q1_ref) * cw[1:2]
        z = z + cv * cw[2:3]
        b = jnp.dot(hn, wb_ref[...].astype(BF16), preferred_element_type=F32)
        zb_ref[...] = (b * z).astype(BF16)
        last = cv[t_rows - SUBLANES:, :]
        carry_ref[j] = last
        tail_ref[...] = last
        return cv

    @pl.when(is_sample)
    def _():
        cvs_ref[...] = mixer(True)

    @pl.when(jnp.logical_not(is_sample))
    def _():
        mixer(False)


def _sc_mixer(x, g, w, conv_w, aux, n_prompt_tiles, tiles_per_seq, seq_s):
    m, d = x.shape
    dc = w.shape[1] // 3
    tn = W_COL_TILE
    nj = dc // tn
    n_tiles = m // ROW_TILE
    ms = aux[0].shape[0]
    assert ms == ROW_TILE
    wspec = lambda off: pl.BlockSpec((d, tn), lambda i, j: (0, off * nj + j))
    sample_col = lambda i, j: (0, jnp.where(i >= n_prompt_tiles, j, 0))
    return pl.pallas_call(
        functools.partial(_sc_mixer_kernel, n_prompt_tiles, tiles_per_seq, seq_s),
        grid=(n_tiles, nj),
        in_specs=[pl.BlockSpec((ROW_TILE, d), lambda i, j: (i, 0)),
                  pl.BlockSpec((1, d), lambda i, j: (0, 0)),
                  wspec(0), wspec(1), wspec(2),
                  pl.BlockSpec((conv_w.shape[0], tn), lambda i, j: (0, j)),
                  pl.BlockSpec((ROW_TILE, tn), sample_col),
                  pl.BlockSpec((ROW_TILE, tn), sample_col)],
        out_specs=[pl.BlockSpec((ROW_TILE, tn), lambda i, j: (i, j)),
                   pl.BlockSpec((SUBLANES, tn), lambda i, j: (i, j)),
                   pl.BlockSpec((ROW_TILE, tn), sample_col)],
        out_shape=[jax.ShapeDtypeStruct((m, dc), BF16),
                   jax.ShapeDtypeStruct((n_tiles * SUBLANES, dc), F32),
                   jax.ShapeDtypeStruct((ms, dc), F32)],
        scratch_shapes=[pltpu.VMEM((ROW_TILE, d), BF16),
                        pltpu.VMEM((nj, SUBLANES, tn), F32)],
        compiler_params=_cparams("arbitrary", "arbitrary"),
        name="sc_mixer",
    )(x, g, w, w, w, conv_w, *aux)


def _matmul_residual_kernel(n_first, y_ref, w_ref, ra_ref, rb_ref, o_ref):
    o_ref[...] = jnp.dot(y_ref[...], w_ref[...], preferred_element_type=F32)
    i = pl.program_id(0)

    @pl.when(i < n_first)
    def _():
        o_ref[...] += ra_ref[...]

    @pl.when(i >= n_first)
    def _():
        o_ref[...] += rb_ref[...]


def _matmul_residual(y, w, res_a, res_b):
    m, k = y.shape
    n = w.shape[1]
    tm = OUT_ROW_TILE
    ra_spec, rb_spec = _two_part_specs(res_a.shape[0] // tm, (tm, n), lambda: 0)
    return pl.pallas_call(
        functools.partial(_matmul_residual_kernel, res_a.shape[0] // tm),
        grid=(m // tm,),
        in_specs=[pl.BlockSpec((tm, k), lambda i: (i, 0)),
                  pl.BlockSpec((k, n), lambda i: (0, 0)),
                  ra_spec, rb_spec],
        out_specs=pl.BlockSpec((tm, n), lambda i: (i, 0)),
        out_shape=jax.ShapeDtypeStruct((m, n), F32),
        compiler_params=_cparams("arbitrary"),
        name="matmul_residual",
    )(y, w, res_a, res_b)


def _pos_in_seq(r0, rows, seq_len):
    return (r0 + lax.broadcasted_iota(jnp.int32, (rows, 1), 0)) & (seq_len - 1)


def _sigmoid(x):
    return 0.5 * jnp.tanh(0.5 * x) + 0.5


def _run_chunks(is_sample, n_chunks, body):
    @pl.when(is_sample)
    def _():
        lax.fori_loop(0, n_chunks, lambda k, c: body(True, k) or c, 0)

    @pl.when(jnp.logical_not(is_sample))
    def _():
        lax.fori_loop(0, n_chunks, lambda k, c: body(False, k) or c, 0)


def _rglru_kernel(n_prompt_tiles, tiles_per_seq, seq_s,
                  gate_ref, u_ref, p1_ref, p2_ref, p3_ref, h0_ref,
                  cw_ref, cb_ref, br_ref, bi_ref, lam_ref, wr_ref, wi_ref,
                  y_ref, h_ref, ubuf, uc_ref, rp_ref, ip_ref, hc_ref):
    i = pl.program_id(1)
    cblk = u_ref.shape[1]
    is_sample = i >= n_prompt_tiles

    @pl.when(is_sample | (lax.rem(i, tiles_per_seq) == 0))
    def _():
        ubuf[0:SUBLANES, :] = jnp.zeros((SUBLANES, cblk), F32)
        hc_ref[...] = jnp.zeros_like(hc_ref)

    ubuf[SUBLANES:, :] = u_ref[...]
    cw = cw_ref[...]
    cb = cb_ref[...]
    rc = SCAN_CHUNK

    def conv_chunk(masked, k):
        r0 = pl.multiple_of(k * rc, rc)
        rows = pl.ds(r0, rc)
        ext = ubuf[pl.ds(r0, rc + SUBLANES), :]
        if masked:
            t = _pos_in_seq(r0, rc, seq_s)

        def tap(s, p_ref):
            shifted = pltpu.roll(ext, s, axis=0)[SUBLANES:]
            return jnp.where(t >= s, shifted, p_ref[rows, :]) if masked else shifted

        conv = tap(3, p3_ref) * cw[0:1] + tap(2, p2_ref) * cw[1:2]
        conv = conv + tap(1, p1_ref) * cw[2:3]
        conv = conv + ext[SUBLANES:] * cw[3:4]
        uc_ref[rows, :] = conv + cb

    _run_chunks(is_sample, ROW_TILE // rc, conv_chunk)

    ub = uc_ref[...].astype(BF16)
    rp_ref[...] = jnp.dot(ub, wr_ref[...], preferred_element_type=F32) + br_ref[...]
    ip_ref[...] = jnp.dot(ub, wi_ref[...], preferred_element_type=F32) + bi_ref[...]

    sp = jax.nn.softplus(-lam_ref[...])
    sub = lax.broadcasted_iota(jnp.int32, (1, SUBLANES, 1), 1)

    def scan_chunk(masked, k):
        r0 = pl.multiple_of(k * rc, rc)
        rows = pl.ds(r0, rc)
        uc = uc_ref[rows, :]
        r = _sigmoid(rp_ref[rows, :])
        ig = _sigmoid(ip_ref[rows, :])
        log_a = (-RG_C * r) * sp
        a = jnp.exp(log_a)
        mult = jnp.sqrt(-jnp.tanh(log_a) * (a * a + 1.0))
        b = mult * ig * uc
        if masked:
            first = _pos_in_seq(r0, rc, seq_s) == 0
            b = jnp.where(first, b + a * h0_ref[rows, :], b)
            a = jnp.where(first, 0.0, a)

        a3 = a.reshape(rc // SUBLANES, SUBLANES, cblk)
        b3 = b.reshape(rc // SUBLANES, SUBLANES, cblk)
        for s in (1, 2, 4):
            a_s = pltpu.roll(a3, s, axis=1)
            b_s = pltpu.roll(b3, s, axis=1)
            m = sub >= s
            b3 = jnp.where(m, a3 * b_s + b3, b3)
            a3 = jnp.where(m, a3 * a_s, a3)

        hc = hc_ref[...]
        hs = []
        for g in range(rc // SUBLANES):
            hg = a3[g] * hc + b3[g]
            hc = jnp.broadcast_to(hg[SUBLANES - 1:SUBLANES], (SUBLANES, cblk))
            hs.append(hg)
        hc_ref[...] = hc
        h = jnp.concatenate(hs, axis=0)
        h_ref[rows, :] = h
        y_ref[rows, :] = (h * gate_ref[rows, :]).astype(BF16)

    _run_chunks(is_sample, ROW_TILE // rc, scan_chunk)
    ubuf[0:SUBLANES, :] = ubuf[ROW_TILE:ROW_TILE + SUBLANES, :]


def _rglru_core(proj, aux, conv_w, conv_b, b_r, b_i, lam, w_r, w_i, n_prompt_tiles,
                tiles_per_seq, seq_s):
    m = proj.shape[0]
    nb, cblk = w_r.shape[0], w_r.shape[1]
    c = nb * cblk
    n_tiles = m // ROW_TILE
    row_blk = lambda off: pl.BlockSpec((ROW_TILE, cblk), lambda cb, i: (i, off + cb))
    aux_blk = pl.BlockSpec((ROW_TILE, cblk), lambda cb, i: (0, cb))
    vec_blk = lambda rows: pl.BlockSpec((rows, cblk), lambda cb, i: (0, cb))
    w_blk = pl.BlockSpec((None, cblk, cblk), lambda cb, i: (cb, 0, 0))
    out_blk = pl.BlockSpec((ROW_TILE, cblk), lambda cb, i: (i, cb))
    return pl.pallas_call(
        functools.partial(_rglru_kernel, n_prompt_tiles, tiles_per_seq, seq_s),
        grid=(nb, n_tiles),
        in_specs=[row_blk(0), row_blk(nb), aux_blk, aux_blk, aux_blk, aux_blk,
                  vec_blk(conv_w.shape[0]), vec_blk(1), vec_blk(1), vec_blk(1), vec_blk(1),
                  w_blk, w_blk],
        out_specs=[out_blk, out_blk],
        out_shape=[jax.ShapeDtypeStruct((m, c), BF16), jax.ShapeDtypeStruct((m, c), F32)],
        scratch_shapes=[pltpu.VMEM((ROW_TILE + SUBLANES, cblk), F32),
                        pltpu.VMEM((ROW_TILE, cblk), F32),
                        pltpu.VMEM((ROW_TILE, cblk), F32),
                        pltpu.VMEM((ROW_TILE, cblk), F32),
                        pltpu.VMEM((SUBLANES, cblk), F32)],
        compiler_params=_cparams("arbitrary", "arbitrary"),
        name="rglru_core",
    )(proj, proj, *aux, conv_w, conv_b, b_r, b_i, lam, w_r, w_i)


def _ffn_kernel(x_ref, g_ref, wg_ref, wu_ref, wd_ref, o_ref, hn_ref):
    @pl.when(pl.program_id(1) == 0)
    def _():
        x = x_ref[...]
        hn_ref[...] = _rms_norm_f32(x, g_ref[...]).astype(BF16)
        o_ref[...] = x

    hn = hn_ref[...]
    a = jnp.dot(hn, wg_ref[...].astype(BF16), preferred_element_type=F32)
    b = jnp.dot(hn, wu_ref[...].astype(BF16), preferred_element_type=F32)
    h = (jax.nn.silu(a) * b).astype(BF16)
    o_ref[...] += jnp.dot(h, wd_ref[...].astype(BF16), preferred_element_type=F32)


def _ffn(x, g, wg, wu, wd):
    m, d = x.shape
    f = wg.shape[1]
    tm = ROW_TILE
    tf = W_COL_TILE
    return pl.pallas_call(
        _ffn_kernel,
        grid=(m // tm, f // tf),
        in_specs=[pl.BlockSpec((tm, d), lambda i, j: (i, 0)),
                  pl.BlockSpec((1, d), lambda i, j: (0, 0)),
                  pl.BlockSpec((d, tf), lambda i, j: (0, j)),
                  pl.BlockSpec((d, tf), lambda i, j: (0, j)),
                  pl.BlockSpec((tf, d), lambda i, j: (j, 0))],
        out_specs=pl.BlockSpec((tm, d), lambda i, j: (i, 0)),
        out_shape=jax.ShapeDtypeStruct((m, d), F32),
        scratch_shapes=[pltpu.VMEM((tm, d), BF16)],
        compiler_params=_cparams("arbitrary", "arbitrary"),
        name="ffn",
    )(x, g, wg, wu, wd)


def _router_kernel(n_experts, x_ref, g_ref, whi_ref, wlo_ref, hn_ref, meta_ref, cnt_ref, run_ref):
    @pl.when(pl.program_id(0) == 0)
    def _():
        run_ref[...] = jnp.zeros_like(run_ref)

    tm = x_ref.shape[0]
    hn = _rms_norm_f32(x_ref[...], g_ref[...])
    hn_hi = hn.astype(BF16)
    hn_hi32 = hn_hi.astype(F32)
    bits = pltpu.bitcast(hn_hi32, jnp.uint32)
    half = hn.shape[1] // 2
    hn_ref[...] = (bits[:, half:] & jnp.uint32(0xFFFF0000)) | (bits[:, :half] >> 16)
    hn_lo = (hn - hn_hi32).astype(BF16)
    logits = (jnp.dot(hn_hi, whi_ref[...], preferred_element_type=F32)
              + (jnp.dot(hn_lo, whi_ref[...], preferred_element_type=F32)
                 + jnp.dot(hn_hi, wlo_ref[...], preferred_element_type=F32)))
    lane = lax.broadcasted_iota(jnp.int32, (tm, LANES), 1).astype(F32)
    neg = jnp.float32(-jnp.inf)
    lg = jnp.where(lane < n_experts, logits, neg)
    m1 = jnp.max(lg, axis=1, keepdims=True)
    i1 = jnp.min(jnp.where(lg == m1, lane, float(LANES)), axis=1, keepdims=True)
    lg2 = jnp.where(lane == i1, neg, lg)
    m2 = jnp.max(lg2, axis=1, keepdims=True)
    i2 = jnp.min(jnp.where(lg2 == m2, lane, float(LANES)), axis=1, keepdims=True)
    ex = jnp.exp(m2 - m1)
    g1 = 1.0 / (1.0 + ex)
    g2 = ex / (1.0 + ex)

    sel1 = lane == i1
    sel2 = lane == i2
    onehot = jnp.where(sel1 | sel2, 1.0, 0.0)
    rr = lax.broadcasted_iota(jnp.int32, (tm, tm), 0)
    cc = lax.broadcasted_iota(jnp.int32, (tm, tm), 1)
    tri = jnp.where(rr > cc, 1.0, 0.0).astype(BF16)
    before = jnp.dot(tri, onehot.astype(BF16), preferred_element_type=F32) + run_ref[0:1, :]
    rank1 = jnp.sum(jnp.where(sel1, before, 0.0), axis=1, keepdims=True)
    rank2 = jnp.sum(jnp.where(sel2, before, 0.0), axis=1, keepdims=True)
    run = run_ref[0:1, :] + jnp.sum(onehot, axis=0, keepdims=True)
    run_ref[...] = jnp.broadcast_to(run, run_ref.shape)
    cnt_ref[...] = jnp.broadcast_to(run, cnt_ref.shape)

    meta = jnp.where(lane == 0, i1, 0.0)
    meta = jnp.where(lane == 1, i2, meta)
    meta = jnp.where(lane == 2, g1, meta)
    meta = jnp.where(lane == 3, g2, meta)
    meta = jnp.where(lane == 4, rank1, meta)
    meta = jnp.where(lane == 5, rank2, meta)
    meta_ref[...] = meta


def _router(x, g, w_router_padded, n_experts):
    m, d = x.shape
    tm = ROUTER_ROW_TILE
    w_hi = w_router_padded.astype(BF16)
    w_lo = (w_router_padded - w_hi.astype(F32)).astype(BF16)
    return pl.pallas_call(
        functools.partial(_router_kernel, n_experts),
        grid=(m // tm,),
        in_specs=[pl.BlockSpec((tm, d), lambda i: (i, 0)),
                  pl.BlockSpec((1, d), lambda i: (0, 0)),
                  pl.BlockSpec((d, LANES), lambda i: (0, 0)),
                  pl.BlockSpec((d, LANES), lambda i: (0, 0))],
        out_specs=[pl.BlockSpec((tm, d // 2), lambda i: (i, 0)),
                   pl.BlockSpec((tm, LANES), lambda i: (i, 0)),
                   pl.BlockSpec((SUBLANES, LANES), lambda i: (0, 0))],
        out_shape=[jax.ShapeDtypeStruct((m, d // 2), jnp.uint32),
                   jax.ShapeDtypeStruct((m, LANES), F32),
                   jax.ShapeDtypeStruct((SUBLANES, LANES), F32)],
        scratch_shapes=[pltpu.VMEM((SUBLANES, LANES), F32)],
        compiler_params=_cparams("arbitrary"),
        name="router",
    )(x, g, w_hi, w_lo)


def _row_copy(src_hbm, idx, dst_ref, r, sem):
    return pltpu.make_async_copy(src_hbm.at[pl.ds(idx, 1)], dst_ref.at[pl.ds(r, 1)], sem)


def _experts_kernel(rows_per_step, be_ref, nv_ref, nused_ref, off_ref, stok_ref,
                    hn_hbm, wg_ref, wu_ref, wd_ref, o_ref,
                    xbuf, xb_ref, sem):
    t = pl.program_id(0)
    j = pl.program_id(1)
    nused = nused_ref[0]
    tr = xb_ref.shape[0]
    gr = xbuf.shape[0]
    half = tr // 2
    dh = xbuf.shape[1]

    @pl.when(j == 0)
    def _():
        o_ref[...] = jnp.zeros_like(o_ref)

    @pl.when((t == 0) & (j == 0))
    def _():
        off0 = off_ref[0]

        def issue(r, c):
            _row_copy(hn_hbm, stok_ref[off0 + r], xbuf, r, sem).start()
            return c

        lax.fori_loop(0, gr, issue, 0, unroll=8)

    @pl.when((j == 0) & (t <= nused))
    def _():
        pltpu.make_async_copy(hn_hbm.at[pl.ds(0, gr)], xbuf, sem).wait()

    @pl.when((j == 0) & (t < nused))
    def _():
        w = xbuf[0:tr, :]
        xb_ref[:, 0:dh] = pltpu.bitcast(w << 16, F32).astype(BF16)
        xb_ref[:, dh:] = pltpu.bitcast(w & jnp.uint32(0xFFFF0000), F32).astype(BF16)

    def step(n_rows):
        base = off_ref[t + 1] + j * rows_per_step
        for k in range(rows_per_step):
            _row_copy(hn_hbm, stok_ref[base + k], xbuf, j * rows_per_step + k, sem).start()
        rows = pl.ds(0, n_rows)
        x = xb_ref[rows, :]
        a = jnp.dot(x, wg_ref[...].astype(BF16), preferred_element_type=F32)
        b = jnp.dot(x, wu_ref[...].astype(BF16), preferred_element_type=F32)
        h = (jax.nn.silu(a) * b).astype(BF16)
        o_ref[rows, :] += jnp.dot(h, wd_ref[...].astype(BF16), preferred_element_type=F32)

    @pl.when((t < nused) & (nv_ref[t] > half))
    def _():
        step(tr)

    @pl.when((t < nused) & (nv_ref[t] <= half))
    def _():
        step(half)


def _experts(hn_packed, sorted_tok, tile_off, block_e, n_valid, n_used, wg, wu, wd):
    dh = hn_packed.shape[1]
    d = 2 * dh
    f = wg.shape[2]
    tr = MOE_ROW_TILE
    tf = W_COL_TILE
    n_tiles = block_e.shape[0]
    nj = f // tf
    rows_per_step = -(-tr // nj)
    gr = rows_per_step * nj
    stok = jnp.pad(sorted_tok, (0, gr))
    offs = jnp.pad(jnp.clip(tile_off, 0, sorted_tok.shape[0]), (0, 1))

    def tile(t, nu):
        return jnp.minimum(t, jnp.maximum(nu[0] - 1, 0))

    def col(t, j, nu):
        return jnp.where(t < nu[0], j, nj - 1)

    wspec = lambda shape, imap: pl.BlockSpec(
        shape, lambda t, j, be, nv, nu, off, st: imap(be[tile(t, nu)], col(t, j, nu)))
    return pl.pallas_call(
        functools.partial(_experts_kernel, rows_per_step),
        grid_spec=pltpu.PrefetchScalarGridSpec(
            num_scalar_prefetch=5,
            grid=(n_tiles + 1, nj),
            in_specs=[pl.BlockSpec(memory_space=pl.ANY),
                      wspec((None, d, tf), lambda e, c: (e, 0, c)),
                      wspec((None, d, tf), lambda e, c: (e, 0, c)),
                      wspec((None, tf, d), lambda e, c: (e, c, 0))],
            out_specs=pl.BlockSpec((tr, d), lambda t, j, *_: (t, 0)),
            scratch_shapes=[pltpu.VMEM((gr, dh), jnp.uint32), pltpu.VMEM((tr, d), BF16),
                            pltpu.SemaphoreType.DMA(())]),
        out_shape=jax.ShapeDtypeStruct(((n_tiles + 1) * tr, d), F32),
        compiler_params=_cparams("arbitrary", "arbitrary"),
        name="moe_experts",
    )(block_e, jnp.pad(n_valid, (0, 1)), n_used, offs, stok, hn_packed, wg, wu, wd)


def _combine_kernel(n_first, d0_ref, d1_ref, d0n_ref, d1n_ref, x_ref, meta_ref, g_ref, yb_hbm,
                    oa_ref, ob_ref, y0, y1, sem):
    tm = x_ref.shape[0]
    i = pl.program_id(0)
    slot = lax.rem(i, 2)

    def issue_tile(da_ref, db_ref, s):
        def issue(r, c):
            _row_copy(yb_hbm, da_ref[0, 0, r], y0.at[s], r, sem.at[0, s]).start()
            _row_copy(yb_hbm, db_ref[0, 0, r], y1.at[s], r, sem.at[1, s]).start()
            return c

        lax.fori_loop(0, tm, issue, 0, unroll=8)

    @pl.when(i == 0)
    def _():
        issue_tile(d0_ref, d1_ref, 0)

    @pl.when(i + 1 < pl.num_programs(0))
    def _():
        issue_tile(d0n_ref, d1n_ref, 1 - slot)

    pltpu.make_async_copy(yb_hbm.at[pl.ds(0, tm)], y0.at[slot], sem.at[0, slot]).wait()
    pltpu.make_async_copy(yb_hbm.at[pl.ds(0, tm)], y1.at[slot], sem.at[1, slot]).wait()
    meta = meta_ref[...]
    x = x_ref[...] + (y0[slot] * meta[:, 2:3] + y1[slot] * meta[:, 3:4])

    @pl.when(i < n_first)
    def _():
        oa_ref[...] = _rms_norm_f32(x, g_ref[...])

    @pl.when(i >= n_first)
    def _():
        ob_ref[...] = _rms_norm_f32(x, g_ref[...])


def _combine(x, meta, g, yb, dest0, dest1, m_first):
    m, d = x.shape
    tm = COMBINE_ROW_TILE
    n_tiles = m // tm
    smem_blk = pl.BlockSpec((1, 1, tm), lambda i: (i, 0, 0), memory_space=pltpu.SMEM)
    smem_next = pl.BlockSpec((1, 1, tm), lambda i: (jnp.minimum(i + 1, n_tiles - 1), 0, 0),
                             memory_space=pltpu.SMEM)
    oa_spec, ob_spec = _two_part_specs(m_first // tm, (tm, d), lambda: 0)
    d0 = dest0.reshape(n_tiles, 1, tm)
    d1 = dest1.reshape(n_tiles, 1, tm)
    return pl.pallas_call(
        functools.partial(_combine_kernel, m_first // tm),
        grid=(n_tiles,),
        in_specs=[smem_blk, smem_blk, smem_next, smem_next,
                  pl.BlockSpec((tm, d), lambda i: (i, 0)),
                  pl.BlockSpec((tm, LANES), lambda i: (i, 0)),
                  pl.BlockSpec((1, d), lambda i: (0, 0)),
                  pl.BlockSpec(memory_space=pl.ANY)],
        out_specs=[oa_spec, ob_spec],
        out_shape=[jax.ShapeDtypeStruct((m_first, d), F32),
                   jax.ShapeDtypeStruct((m - m_first, d), F32)],
        scratch_shapes=[pltpu.VMEM((2, tm, d), F32), pltpu.VMEM((2, tm, d), F32),
                        pltpu.SemaphoreType.DMA((2, 2))],
        compiler_params=_cparams("arbitrary"),
        name="moe_combine",
    )(d0, d1, d0, d1, x, meta, g, yb)


def _sample_aux(state, shift, dec_seq):
    b, w1, c = state.shape
    rows = [state[:, w1 + t - shift] if t < shift else jnp.zeros((b, c), state.dtype)
            for t in range(dec_seq)]
    return jnp.stack(rows, axis=1).reshape(b * dec_seq, c)


def _seq_tails(a, col0, ncol, width, batch, seq, dec_batch, dec_seq):
    mp = batch * seq
    tails = [lax.slice(a, ((b + 1) * seq - width, col0), ((b + 1) * seq, col0 + ncol))
             for b in range(batch)]
    smp = lax.slice(a, (mp, col0), (mp + dec_batch * dec_seq, col0 + ncol))
    smp = smp.reshape(dec_batch, dec_seq, ncol)[:, dec_seq - width:]
    return jnp.stack(tails, axis=0), smp


def kernel(x_prompt, x_sample, state_lru_h, state_lru_conv, state_sconv, norm_mix, norm_ffn, norm_final, lru_w_in, lru_conv_w, lru_conv_b, lru_w_rgate, lru_b_rgate, lru_w_igate, lru_b_igate, lru_lambda, lru_w_out, sc_w_in, sc_conv_w, sc_w_out, ffn_w_gate, ffn_w_up, ffn_w_down, moe_w_router, moe_w_gate, moe_w_up, moe_w_down):
    batch, seq, d = x_prompt.shape
    dec_batch, dec_seq, _ = x_sample.shape
    mp, ms = batch * seq, dec_batch * dec_seq
    m = mp + ms
    depth = norm_mix.shape[0]
    n_experts = moe_w_router.shape[2]
    conv_a = lru_conv_w.shape[1]
    conv_b = sc_conv_w.shape[1]
    assert depth == 2 and lru_w_in.shape[0] == 1 and sc_w_in.shape[0] == 1
    assert seq & (seq - 1) == 0 and dec_seq & (dec_seq - 1) == 0
    assert seq % ROW_TILE == 0 and ms == ROW_TILE and ROW_TILE % dec_seq == 0
    assert dec_seq >= conv_a - 1 and dec_seq >= conv_b - 1 and dec_seq == SUBLANES
    assert n_experts <= LANES and m % ROUTER_ROW_TILE == 0 and m % COMBINE_ROW_TILE == 0
    n_prompt_tiles = mp // ROW_TILE

    xp = x_prompt.reshape(mp, d)
    xs_in = x_sample.reshape(ms, d)
    row = lambda v: v.reshape(1, -1)

    d_rnn = lru_w_out.shape[1]
    assert d_rnn % COL_TILE == 0
    proj = _norm_matmul(xp, xs_in, row(norm_mix[0]), lru_w_in[0], n_gelu_cols=d_rnn)
    h0_rows = jnp.pad(state_lru_h[0][:, None, :], ((0, 0), (0, dec_seq - 1), (0, 0)))
    aux = [_sample_aux(state_lru_conv[0], s, dec_seq) for s in (1, 2, 3)]
    aux.append(h0_rows.reshape(ms, d_rnn))
    y, h_all = _rglru_core(proj, aux, lru_conv_w[0], row(lru_conv_b[0]), row(lru_b_rgate[0]),
                           row(lru_b_igate[0]), row(lru_lambda[0]),
                           lru_w_rgate[0].astype(BF16), lru_w_igate[0].astype(BF16),
                           n_prompt_tiles, seq // ROW_TILE, dec_seq)
    x = _matmul_residual(y, lru_w_out[0].astype(BF16), xp, xs_in)

    x = _ffn(x, row(norm_ffn[0]), ffn_w_gate[0], ffn_w_up[0], ffn_w_down[0])

    aux = [_sample_aux(state_sconv[0], s, dec_seq) for s in (1, 2)]
    zb, cv_tile_tails, cv_smp = _sc_mixer(x, row(norm_mix[1]), sc_w_in[0], sc_conv_w[0], aux,
                                          n_prompt_tiles, seq // ROW_TILE, dec_seq)
    x = _matmul_residual(zb, sc_w_out[0].astype(BF16), x, x)

    w_router = jnp.pad(moe_w_router[0], ((0, 0), (0, LANES - n_experts)))
    hn, meta, counts = _router(x, row(norm_ffn[1]), w_router, n_experts)
    tr = MOE_ROW_TILE
    n_tiles = -(-(m * TOP_K) // tr) + n_experts
    counts = counts[0, :n_experts].astype(jnp.int32)
    padded = (counts + tr - 1) // tr * tr
    pad_end = jnp.cumsum(padded)
    pad_start = pad_end - padded
    e0 = meta[:, 0].astype(jnp.int32)
    e1 = meta[:, 1].astype(jnp.int32)
    dest0 = pad_start[e0] + meta[:, 4].astype(jnp.int32)
    dest1 = pad_start[e1] + meta[:, 5].astype(jnp.int32)
    order = jnp.argsort(jnp.stack([e0, e1], axis=1).reshape(-1), stable=True)
    sorted_tok = (order // TOP_K).astype(jnp.int32)
    start = jnp.cumsum(counts) - counts
    block_start = jnp.arange(n_tiles, dtype=jnp.int32) * tr
    block_e = jnp.sum((pad_end[None, :] <= block_start[:, None]).astype(jnp.int32), axis=1)
    block_e = jnp.minimum(block_e, n_experts - 1)
    in_expert = block_start - pad_start[block_e]
    n_valid = jnp.clip(counts[block_e] - in_expert, 0, tr).astype(jnp.int32)
    n_used = (pad_end[-1:] // tr).astype(jnp.int32)
    yb = _experts(hn, sorted_tok, (start[block_e] + in_expert).astype(jnp.int32), block_e,
                  n_valid, n_used, moe_w_gate[0], moe_w_up[0], moe_w_down[0])
    y_prompt, y_sample = _combine(x, meta, row(norm_final), yb, dest0, dest1, mp)

    tails = functools.partial(_seq_tails, batch=batch, seq=seq, dec_batch=dec_batch, dec_seq=dec_seq)
    h_p, h_s = tails(h_all, 0, d_rnn, 1)
    u_p, u_s = tails(proj, d_rnn, d_rnn, conv_a - 1)
    tiles_per_seq = seq // ROW_TILE
    cv_p = cv_tile_tails.reshape(m // ROW_TILE, SUBLANES, -1)[
        tiles_per_seq - 1:n_prompt_tiles:tiles_per_seq, SUBLANES - (conv_b - 1):]
    cv_s = cv_smp.reshape(dec_batch, dec_seq, -1)[:, dec_seq - (conv_b - 1):]
    return (y_prompt.reshape(batch, seq, d), y_sample.reshape(dec_batch, dec_seq, d),
            h_p[:, 0][None], u_p[None], cv_p[None], h_s[:, 0][None], u_s[None], cv_s[None])
```

```python
import functools

import jax
import jax.numpy as jnp
from jax import lax
from jax.experimental import pallas as pl
from jax.experimental.pallas import tpu as pltpu

F32 = jnp.float32
BF16 = jnp.bfloat16

EPS = 1e-6
RG_C = 8.0
TOP_K = 2

LANES = 128
SUBLANES = 8
ROW_TILE = 1024
OUT_ROW_TILE = 512
ROUTER_ROW_TILE = 512
COL_TILE = 512
W_COL_TILE = 256
MOE_ROW_TILE = 1024
COMBINE_ROW_TILE = 256
SCAN_CHUNK = 64
VMEM_LIMIT = 60 * 1024 * 1024


def _cparams(*sem):
    return pltpu.CompilerParams(dimension_semantics=sem, vmem_limit_bytes=VMEM_LIMIT)


def _rms_norm_f32(x, g):
    y = x * lax.rsqrt(jnp.mean(x * x, axis=-1, keepdims=True) + EPS)
    return y * g


def _two_part_specs(n_first, block, col_of, second_mode=None):
    first = pl.BlockSpec(block, lambda i, *a: (jnp.minimum(i, n_first - 1), col_of(*a)))
    second = pl.BlockSpec(block, lambda i, *a: (jnp.maximum(i - n_first, 0), col_of(*a)),
                          pipeline_mode=second_mode)
    return first, second


def _norm_matmul_kernel(n_first, n_gelu, xa_ref, xb_ref, g_ref, w_ref, o_ref, hn_ref):
    i = pl.program_id(0)
    j = pl.program_id(1)

    def project():
        return jnp.dot(hn_ref[...], w_ref[...].astype(BF16), preferred_element_type=F32)

    @pl.when((j == 0) & (i < n_first))
    def _():
        hn_ref[...] = _rms_norm_f32(xa_ref[...], g_ref[...]).astype(BF16)

    @pl.when((j == 0) & (i >= n_first))
    def _():
        hn_ref[...] = _rms_norm_f32(xb_ref[...], g_ref[...]).astype(BF16)

    @pl.when(j < n_gelu)
    def _():
        o_ref[...] = jax.nn.gelu(project())

    @pl.when(j >= n_gelu)
    def _():
        o_ref[...] = project()


def _norm_matmul(xa, xb, g, w, n_gelu_cols):
    d = xa.shape[1]
    m = xa.shape[0] + xb.shape[0]
    n = w.shape[1]
    assert xb.shape[0] == ROW_TILE
    xa_spec, xb_spec = _two_part_specs(xa.shape[0] // ROW_TILE, (ROW_TILE, d), lambda j: 0,
                                       second_mode=pl.Buffered(1))
    return pl.pallas_call(
        functools.partial(_norm_matmul_kernel, xa.shape[0] // ROW_TILE, n_gelu_cols // COL_TILE),
        grid=(m // ROW_TILE, n // COL_TILE),
        in_specs=[xa_spec, xb_spec,
                  pl.BlockSpec((1, d), lambda i, j: (0, 0)),
                  pl.BlockSpec((d, COL_TILE), lambda i, j: (0, j))],
        out_specs=pl.BlockSpec((ROW_TILE, COL_TILE), lambda i, j: (i, j)),
        out_shape=jax.ShapeDtypeStruct((m, n), F32),
        scratch_shapes=[pltpu.VMEM((ROW_TILE, d), BF16)],
        compiler_params=_cparams("arbitrary", "arbitrary"),
        name="norm_matmul",
    )(xa, xb, g, w)


def _sc_mixer_kernel(n_prompt_tiles, tiles_per_seq, seq_s,
                     x_ref, g_ref, wb_ref, wc_ref, wv_ref, cw_ref, q1_ref, q2_ref,
                     zb_ref, tail_ref, cvs_ref, hn_ref, carry_ref):
    i = pl.program_id(0)
    j = pl.program_id(1)
    t_rows = x_ref.shape[0]
    is_sample = i >= n_prompt_tiles
    starts_seq = is_sample | (lax.rem(i, tiles_per_seq) == 0)

    @pl.when(j == 0)
    def _():
        hn_ref[...] = _rms_norm_f32(x_ref[...], g_ref[...]).astype(BF16)

    @pl.when((i == 0) & (j == 0))
    def _():
        carry_ref[...] = jnp.zeros_like(carry_ref)

    def mixer(masked):
        hn = hn_ref[...]
        cw = cw_ref[...]
        c = jnp.dot(hn, wc_ref[...].astype(BF16), preferred_element_type=F32)
        v = jnp.dot(hn, wv_ref[...].astype(BF16), preferred_element_type=F32)
        cv = c * v
        prev = jnp.where(starts_seq, 0.0, carry_ref[j])
        ext = jnp.concatenate([prev, cv], axis=0)
        if masked:
            t = _pos_in_seq(0, t_rows, seq_s)

        def tap(s, q_ref):
            shifted = pltpu.roll(ext, s, axis=0)[SUBLANES:]
            return jnp.where(t >= s, shifted, q_ref[...]) if masked else shifted

        z = tap(2, q2_ref) * cw[0:1] + tap(1, q1_ref) * cw[1:2]
        z = z + cv * cw[2:3]
        b = jnp.dot(hn, wb_ref[...].astype(BF16), preferred_element_type=F32)
        zb_ref[...] = (b * z).astype(BF16)
        last = cv[t_rows - SUBLANES:, :]
        carry_ref[j] = last
        tail_ref[...] = last
        return cv

    @pl.when(is_sample)
    def _():
        cvs_ref[...] = mixer(True)

    @pl.when(jnp.logical_not(is_sample))
    def _():
        mixer(False)


def _sc_mixer(x, g, w, conv_w, aux, n_prompt_tiles, tiles_per_seq, seq_s):
    m, d = x.shape
    dc = w.shape[1] // 3
    tn = W_COL_TILE
    nj = dc // tn
    n_tiles = m // ROW_TILE
    ms = aux[0].shape[0]
    assert ms == ROW_TILE
    wspec = lambda off: pl.BlockSpec((d, tn), lambda i, j: (0, off * nj + j))
    sample_col = lambda i, j: (0, jnp.where(i >= n_prompt_tiles, j, 0))
    return pl.pallas_call(
        functools.partial(_sc_mixer_kernel, n_prompt_tiles, tiles_per_seq, seq_s),
        grid=(n_tiles, nj),
        in_specs=[pl.BlockSpec((ROW_TILE, d), lambda i, j: (i, 0)),
                  pl.BlockSpec((1, d), lambda i, j: (0, 0)),
                  wspec(0), wspec(1), wspec(2),
                  pl.BlockSpec((conv_w.shape[0], tn), lambda i, j: (0, j)),
                  pl.BlockSpec((ROW_TILE, tn), sample_col),
                  pl.BlockSpec((ROW_TILE, tn), sample_col)],
        out_specs=[pl.BlockSpec((ROW_TILE, tn), lambda i, j: (i, j)),
                   pl.BlockSpec((SUBLANES, tn), lambda i, j: (i, j)),
                   pl.BlockSpec((ROW_TILE, tn), sample_col)],
        out_shape=[jax.ShapeDtypeStruct((m, dc), BF16),
                   jax.ShapeDtypeStruct((n_tiles * SUBLANES, dc), F32),
                   jax.ShapeDtypeStruct((ms, dc), F32)],
        scratch_shapes=[pltpu.VMEM((ROW_TILE, d), BF16),
                        pltpu.VMEM((nj, SUBLANES, tn), F32)],
        compiler_params=_cparams("arbitrary", "arbitrary"),
        name="sc_mixer",
    )(x, g, w, w, w, conv_w, *aux)


def _matmul_residual_kernel(n_first, y_ref, w_ref, ra_ref, rb_ref, o_ref):
    o_ref[...] = jnp.dot(y_ref[...], w_ref[...], preferred_element_type=F32)
    i = pl.program_id(0)

    @pl.when(i < n_first)
    def _():
        o_ref[...] += ra_ref[...]

    @pl.when(i >= n_first)
    def _():
        o_ref[...] += rb_ref[...]


def _matmul_residual(y, w, res_a, res_b):
    m, k = y.shape
    n = w.shape[1]
    tm = OUT_ROW_TILE
    ra_spec, rb_spec = _two_part_specs(res_a.shape[0] // tm, (tm, n), lambda: 0)
    return pl.pallas_call(
        functools.partial(_matmul_residual_kernel, res_a.shape[0] // tm),
        grid=(m // tm,),
        in_specs=[pl.BlockSpec((tm, k), lambda i: (i, 0)),
                  pl.BlockSpec((k, n), lambda i: (0, 0)),
                  ra_spec, rb_spec],
        out_specs=pl.BlockSpec((tm, n), lambda i: (i, 0)),
        out_shape=jax.ShapeDtypeStruct((m, n), F32),
        compiler_params=_cparams("arbitrary"),
        name="matmul_residual",
    )(y, w, res_a, res_b)


def _pos_in_seq(r0, rows, seq_len):
    return (r0 + lax.broadcasted_iota(jnp.int32, (rows, 1), 0)) & (seq_len - 1)


def _sigmoid(x):
    return 0.5 * jnp.tanh(0.5 * x) + 0.5


def _run_chunks(is_sample, n_chunks, body):
    @pl.when(is_sample)
    def _():
        lax.fori_loop(0, n_chunks, lambda k, c: body(True, k) or c, 0)

    @pl.when(jnp.logical_not(is_sample))
    def _():
        lax.fori_loop(0, n_chunks, lambda k, c: body(False, k) or c, 0)


def _rglru_kernel(n_prompt_tiles, tiles_per_seq, seq_s,
                  gate_ref, u_ref, p1_ref, p2_ref, p3_ref, h0_ref,
                  cw_ref, cb_ref, br_ref, bi_ref, lam_ref, wr_ref, wi_ref,
                  y_ref, h_ref, ubuf, uc_ref, rp_ref, ip_ref, hc_ref):
    i = pl.program_id(1)
    cblk = u_ref.shape[1]
    is_sample = i >= n_prompt_tiles

    @pl.when(is_sample | (lax.rem(i, tiles_per_seq) == 0))
    def _():
        ubuf[0:SUBLANES, :] = jnp.zeros((SUBLANES, cblk), F32)
        hc_ref[...] = jnp.zeros_like(hc_ref)

    ubuf[SUBLANES:, :] = u_ref[...]
    cw = cw_ref[...]
    cb = cb_ref[...]
    rc = SCAN_CHUNK

    def conv_chunk(masked, k):
        r0 = pl.multiple_of(k * rc, rc)
        rows = pl.ds(r0, rc)
        ext = ubuf[pl.ds(r0, rc + SUBLANES), :]
        if masked:
            t = _pos_in_seq(r0, rc, seq_s)

        def tap(s, p_ref):
            shifted = pltpu.roll(ext, s, axis=0)[SUBLANES:]
            return jnp.where(t >= s, shifted, p_ref[rows, :]) if masked else shifted

        conv = tap(3, p3_ref) * cw[0:1] + tap(2, p2_ref) * cw[1:2]
        conv = conv + tap(1, p1_ref) * cw[2:3]
        conv = conv + ext[SUBLANES:] * cw[3:4]
        uc_ref[rows, :] = conv + cb

    _run_chunks(is_sample, ROW_TILE // rc, conv_chunk)

    ub = uc_ref[...].astype(BF16)
    rp_ref[...] = jnp.dot(ub, wr_ref[...], preferred_element_type=F32) + br_ref[...]
    ip_ref[...] = jnp.dot(ub, wi_ref[...], preferred_element_type=F32) + bi_ref[...]

    sp = jax.nn.softplus(-lam_ref[...])
    sub = lax.broadcasted_iota(jnp.int32, (1, SUBLANES, 1), 1)

    def scan_chunk(masked, k):
        r0 = pl.multiple_of(k * rc, rc)
        rows = pl.ds(r0, rc)
        uc = uc_ref[rows, :]
        r = _sigmoid(rp_ref[rows, :])
        ig = _sigmoid(ip_ref[rows, :])
        log_a = (-RG_C * r) * sp
        a = jnp.exp(log_a)
        mult = jnp.sqrt(-jnp.tanh(log_a) * (a * a + 1.0))
        b = mult * ig * uc
        if masked:
            first = _pos_in_seq(r0, rc, seq_s) == 0
            b = jnp.where(first, b + a * h0_ref[rows, :], b)
            a = jnp.where(first, 0.0, a)

        a3 = a.reshape(rc // SUBLANES, SUBLANES, cblk)
        b3 = b.reshape(rc // SUBLANES, SUBLANES, cblk)
        for s in (1, 2, 4):
            a_s = pltpu.roll(a3, s, axis=1)
            b_s = pltpu.roll(b3, s, axis=1)
            m = sub >= s
            b3 = jnp.where(m, a3 * b_s + b3, b3)
            a3 = jnp.where(m, a3 * a_s, a3)

        hc = hc_ref[...]
        hs = []
        for g in range(rc // SUBLANES):
            hg = a3[g] * hc + b3[g]
            hc = jnp.broadcast_to(hg[SUBLANES - 1:SUBLANES], (SUBLANES, cblk))
            hs.append(hg)
        hc_ref[...] = hc
        h = jnp.concatenate(hs, axis=0)
        h_ref[rows, :] = h
        y_ref[rows, :] = (h * gate_ref[rows, :]).astype(BF16)

    _run_chunks(is_sample, ROW_TILE // rc, scan_chunk)
    ubuf[0:SUBLANES, :] = ubuf[ROW_TILE:ROW_TILE + SUBLANES, :]


def _rglru_core(proj, aux, conv_w, conv_b, b_r, b_i, lam, w_r, w_i, n_prompt_tiles,
                tiles_per_seq, seq_s):
    m = proj.shape[0]
    nb, cblk = w_r.shape[0], w_r.shape[1]
    c = nb * cblk
    n_tiles = m // ROW_TILE
    row_blk = lambda off: pl.BlockSpec((ROW_TILE, cblk), lambda cb, i: (i, off + cb))
    aux_blk = pl.BlockSpec((ROW_TILE, cblk), lambda cb, i: (0, cb))
    vec_blk = lambda rows: pl.BlockSpec((rows, cblk), lambda cb, i: (0, cb))
    w_blk = pl.BlockSpec((None, cblk, cblk), lambda cb, i: (cb, 0, 0))
    out_blk = pl.BlockSpec((ROW_TILE, cblk), lambda cb, i: (i, cb))
    return pl.pallas_call(
        functools.partial(_rglru_kernel, n_prompt_tiles, tiles_per_seq, seq_s),
        grid=(nb, n_tiles),
        in_specs=[row_blk(0), row_blk(nb), aux_blk, aux_blk, aux_blk, aux_blk,
                  vec_blk(conv_w.shape[0]), vec_blk(1), vec_blk(1), vec_blk(1), vec_blk(1),
                  w_blk, w_blk],
        out_specs=[out_blk, out_blk],
        out_shape=[jax.ShapeDtypeStruct((m, c), BF16), jax.ShapeDtypeStruct((m, c), F32)],
        scratch_shapes=[pltpu.VMEM((ROW_TILE + SUBLANES, cblk), F32),
                        pltpu.VMEM((ROW_TILE, cblk), F32),
                        pltpu.VMEM((ROW_TILE, cblk), F32),
                        pltpu.VMEM((ROW_TILE, cblk), F32),
                        pltpu.VMEM((SUBLANES, cblk), F32)],
        compiler_params=_cparams("arbitrary", "arbitrary"),
        name="rglru_core",
    )(proj, proj, *aux, conv_w, conv_b, b_r, b_i, lam, w_r, w_i)


def _ffn_kernel(x_ref, g_ref, wg_ref, wu_ref, wd_ref, o_ref, hn_ref):
    @pl.when(pl.program_id(1) == 0)
    def _():
        x = x_ref[...]
        hn_ref[...] = _rms_norm_f32(x, g_ref[...]).astype(BF16)
        o_ref[...] = x

    hn = hn_ref[...]
    a = jnp.dot(hn, wg_ref[...].astype(BF16), preferred_element_type=F32)
    b = jnp.dot(hn, wu_ref[...].astype(BF16), preferred_element_type=F32)
    h = (jax.nn.silu(a) * b).astype(BF16)
    o_ref[...] += jnp.dot(h, wd_ref[...].astype(BF16), preferred_element_type=F32)


def _ffn(x, g, wg, wu, wd):
    m, d = x.shape
    f = wg.shape[1]
    tm = ROW_TILE
    tf = W_COL_TILE
    return pl.pallas_call(
        _ffn_kernel,
        grid=(m // tm, f // tf),
        in_specs=[pl.BlockSpec((tm, d), lambda i, j: (i, 0)),
                  pl.BlockSpec((1, d), lambda i, j: (0, 0)),
                  pl.BlockSpec((d, tf), lambda i, j: (0, j)),
                  pl.BlockSpec((d, tf), lambda i, j: (0, j)),
                  pl.BlockSpec((tf, d), lambda i, j: (j, 0))],
        out_specs=pl.BlockSpec((tm, d), lambda i, j: (i, 0)),
        out_shape=jax.ShapeDtypeStruct((m, d), F32),
        scratch_shapes=[pltpu.VMEM((tm, d), BF16)],
        compiler_params=_cparams("arbitrary", "arbitrary"),
        name="ffn",
    )(x, g, wg, wu, wd)


def _router_kernel(n_experts, x_ref, g_ref, whi_ref, wlo_ref, hn_ref, meta_ref, cnt_ref, run_ref):
    @pl.when(pl.program_id(0) == 0)
    def _():
        run_ref[...] = jnp.zeros_like(run_ref)

    tm = x_ref.shape[0]
    hn = _rms_norm_f32(x_ref[...], g_ref[...])
    hn_hi = hn.astype(BF16)
    hn_hi32 = hn_hi.astype(F32)
    bits = pltpu.bitcast(hn_hi32, jnp.uint32)
    half = hn.shape[1] // 2
    hn_ref[...] = (bits[:, half:] & jnp.uint32(0xFFFF0000)) | (bits[:, :half] >> 16)
    hn_lo = (hn - hn_hi32).astype(BF16)
    logits = (jnp.dot(hn_hi, whi_ref[...], preferred_element_type=F32)
              + (jnp.dot(hn_lo, whi_ref[...], preferred_element_type=F32)
                 + jnp.dot(hn_hi, wlo_ref[...], preferred_element_type=F32)))
    lane = lax.broadcasted_iota(jnp.int32, (tm, LANES), 1).astype(F32)
    neg = jnp.float32(-jnp.inf)
    lg = jnp.where(lane < n_experts, logits, neg)
    m1 = jnp.max(lg, axis=1, keepdims=True)
    i1 = jnp.min(jnp.where(lg == m1, lane, float(LANES)), axis=1, keepdims=True)
    lg2 = jnp.where(lane == i1, neg, lg)
    m2 = jnp.max(lg2, axis=1, keepdims=True)
    i2 = jnp.min(jnp.where(lg2 == m2, lane, float(LANES)), axis=1, keepdims=True)
    ex = jnp.exp(m2 - m1)
    g1 = 1.0 / (1.0 + ex)
    g2 = ex / (1.0 + ex)

    sel1 = lane == i1
    sel2 = lane == i2
    onehot = jnp.where(sel1 | sel2, 1.0, 0.0)
    rr = lax.broadcasted_iota(jnp.int32, (tm, tm), 0)
    cc = lax.broadcasted_iota(jnp.int32, (tm, tm), 1)
    tri = jnp.where(rr > cc, 1.0, 0.0).astype(BF16)
    before = jnp.dot(tri, onehot.astype(BF16), preferred_element_type=F32) + run_ref[0:1, :]
    rank1 = jnp.sum(jnp.where(sel1, before, 0.0), axis=1, keepdims=True)
    rank2 = jnp.sum(jnp.where(sel2, before, 0.0), axis=1, keepdims=True)
    run = run_ref[0:1, :] + jnp.sum(onehot, axis=0, keepdims=True)
    run_ref[...] = jnp.broadcast_to(run, run_ref.shape)
    cnt_ref[...] = jnp.broadcast_to(run, cnt_ref.shape)

    meta = jnp.where(lane == 0, i1, 0.0)
    meta = jnp.where(lane == 1, i2, meta)
    meta = jnp.where(lane == 2, g1, meta)
    meta = jnp.where(lane == 3, g2, meta)
    meta = jnp.where(lane == 4, rank1, meta)
    meta = jnp.where(lane == 5, rank2, meta)
    meta_ref[...] = meta


def _router(x, g, w_router_padded, n_experts):
    m, d = x.shape
    tm = ROUTER_ROW_TILE
    w_hi = w_router_padded.astype(BF16)
    w_lo = (w_router_padded - w_hi.astype(F32)).astype(BF16)
    return pl.pallas_call(
        functools.partial(_router_kernel, n_experts),
        grid=(m // tm,),
        in_specs=[pl.BlockSpec((tm, d), lambda i: (i, 0)),
                  pl.BlockSpec((1, d), lambda i: (0, 0)),
                  pl.BlockSpec((d, LANES), lambda i: (0, 0)),
                  pl.BlockSpec((d, LANES), lambda i: (0, 0))],
        out_specs=[pl.BlockSpec((tm, d // 2), lambda i: (i, 0)),
                   pl.BlockSpec((tm, LANES), lambda i: (i, 0)),
                   pl.BlockSpec((SUBLANES, LANES), lambda i: (0, 0))],
        out_shape=[jax.ShapeDtypeStruct((m, d // 2), jnp.uint32),
                   jax.ShapeDtypeStruct((m, LANES), F32),
                   jax.ShapeDtypeStruct((SUBLANES, LANES), F32)],
        scratch_shapes=[pltpu.VMEM((SUBLANES, LANES), F32)],
        compiler_params=_cparams("arbitrary"),
        name="router",
    )(x, g, w_hi, w_lo)


def _row_copy(src_hbm, idx, dst_ref, r, sem):
    return pltpu.make_async_copy(src_hbm.at[pl.ds(idx, 1)], dst_ref.at[pl.ds(r, 1)], sem)


def _experts_kernel(rows_per_step, be_ref, nv_ref, nused_ref, off_ref, stok_ref,
                    hn_hbm, wg_ref, wu_ref, wd_ref, o_ref,
                    xbuf, xb_ref, sem):
    t = pl.program_id(0)
    j = pl.program_id(1)
    nused = nused_ref[0]
    tr = xb_ref.shape[0]
    gr = xbuf.shape[0]
    dh = xbuf.shape[1]

    @pl.when(j == 0)
    def _():
        o_ref[...] = jnp.zeros_like(o_ref)

    @pl.when((t == 0) & (j == 0))
    def _():
        off0 = off_ref[0]

        def issue(r, c):
            _row_copy(hn_hbm, stok_ref[off0 + r], xbuf, r, sem).start()
            return c

        lax.fori_loop(0, gr, issue, 0, unroll=8)

    @pl.when((j == 0) & (t <= nused))
    def _():
        pltpu.make_async_copy(hn_hbm.at[pl.ds(0, gr)], xbuf, sem).wait()

    @pl.when((j == 0) & (t < nused))
    def _():
        w = xbuf[0:tr, :]
        xb_ref[:, 0:dh] = pltpu.bitcast(w << 16, F32).astype(BF16)
        xb_ref[:, dh:] = pltpu.bitcast(w & jnp.uint32(0xFFFF0000), F32).astype(BF16)

    def step(n_rows):
        base = off_ref[t + 1] + j * rows_per_step
        for k in range(rows_per_step):
            _row_copy(hn_hbm, stok_ref[base + k], xbuf, j * rows_per_step + k, sem).start()
        rows = pl.ds(0, n_rows)
        x = xb_ref[rows, :]
        a = jnp.dot(x, wg_ref[...].astype(BF16), preferred_element_type=F32)
        b = jnp.dot(x, wu_ref[...].astype(BF16), preferred_element_type=F32)
        h = (jax.nn.silu(a) * b).astype(BF16)
        o_ref[rows, :] += jnp.dot(h, wd_ref[...].astype(BF16), preferred_element_type=F32)

    quarter = tr // 4
    n_quarters = (nv_ref[t] + (quarter - 1)) // quarter
    for q in range(1, 5):
        @pl.when((t < nused) & (jnp.maximum(n_quarters, 1) == q))
        def _(q=q):
            step(q * quarter)


def _experts(hn_packed, sorted_tok, tile_off, block_e, n_valid, n_used, wg, wu, wd):
    dh = hn_packed.shape[1]
    d = 2 * dh
    f = wg.shape[2]
    tr = MOE_ROW_TILE
    tf = W_COL_TILE
    n_tiles = block_e.shape[0]
    nj = f // tf
    rows_per_step = -(-tr // nj)
    gr = rows_per_step * nj
    stok = jnp.pad(sorted_tok, (0, gr))
    offs = jnp.pad(jnp.clip(tile_off, 0, sorted_tok.shape[0]), (0, 1))

    def tile(t, nu):
        return jnp.minimum(t, jnp.maximum(nu[0] - 1, 0))

    def col(t, j, nu):
        return jnp.where(t < nu[0], j, nj - 1)

    wspec = lambda shape, imap: pl.BlockSpec(
        shape, lambda t, j, be, nv, nu, off, st: imap(be[tile(t, nu)], col(t, j, nu)))
    return pl.pallas_call(
        functools.partial(_experts_kernel, rows_per_step),
        grid_spec=pltpu.PrefetchScalarGridSpec(
            num_scalar_prefetch=5,
            grid=(n_tiles + 1, nj),
            in_specs=[pl.BlockSpec(memory_space=pl.ANY),
                      wspec((None, d, tf), lambda e, c: (e, 0, c)),
                      wspec((None, d, tf), lambda e, c: (e, 0, c)),
                      wspec((None, tf, d), lambda e, c: (e, c, 0))],
            out_specs=pl.BlockSpec((tr, d), lambda t, j, *_: (t, 0)),
            scratch_shapes=[pltpu.VMEM((gr, dh), jnp.uint32), pltpu.VMEM((tr, d), BF16),
                            pltpu.SemaphoreType.DMA(())]),
        out_shape=jax.ShapeDtypeStruct(((n_tiles + 1) * tr, d), F32),
        compiler_params=_cparams("arbitrary", "arbitrary"),
        name="moe_experts",
    )(block_e, jnp.pad(n_valid, (0, 1)), n_used, offs, stok, hn_packed, wg, wu, wd)


def _combine_kernel(n_first, d0_ref, d1_ref, d0n_ref, d1n_ref, x_ref, meta_ref, g_ref, yb_hbm,
                    oa_ref, ob_ref, y0, y1, sem):
    tm = x_ref.shape[0]
    i = pl.program_id(0)
    slot = lax.rem(i, 2)

    def issue_tile(da_ref, db_ref, s):
        def issue(r, c):
            _row_copy(yb_hbm, da_ref[0, 0, r], y0.at[s], r, sem.at[0, s]).start()
            _row_copy(yb_hbm, db_ref[0, 0, r], y1.at[s], r, sem.at[1, s]).start()
            return c

        lax.fori_loop(0, tm, issue, 0, unroll=8)

    @pl.when(i == 0)
    def _():
        issue_tile(d0_ref, d1_ref, 0)

    @pl.when(i + 1 < pl.num_programs(0))
    def _():
        issue_tile(d0n_ref, d1n_ref, 1 - slot)

    pltpu.make_async_copy(yb_hbm.at[pl.ds(0, tm)], y0.at[slot], sem.at[0, slot]).wait()
    pltpu.make_async_copy(yb_hbm.at[pl.ds(0, tm)], y1.at[slot], sem.at[1, slot]).wait()
    meta = meta_ref[...]
    x = x_ref[...] + (y0[slot] * meta[:, 2:3] + y1[slot] * meta[:, 3:4])

    @pl.when(i < n_first)
    def _():
        oa_ref[...] = _rms_norm_f32(x, g_ref[...])

    @pl.when(i >= n_first)
    def _():
        ob_ref[...] = _rms_norm_f32(x, g_ref[...])


def _combine(x, meta, g, yb, dest0, dest1, m_first):
    m, d = x.shape
    tm = COMBINE_ROW_TILE
    n_tiles = m // tm
    smem_blk = pl.BlockSpec((1, 1, tm), lambda i: (i, 0, 0), memory_space=pltpu.SMEM)
    smem_next = pl.BlockSpec((1, 1, tm), lambda i: (jnp.minimum(i + 1, n_tiles - 1), 0, 0),
                             memory_space=pltpu.SMEM)
    oa_spec, ob_spec = _two_part_specs(m_first // tm, (tm, d), lambda: 0)
    d0 = dest0.reshape(n_tiles, 1, tm)
    d1 = dest1.reshape(n_tiles, 1, tm)
    return pl.pallas_call(
        functools.partial(_combine_kernel, m_first // tm),
        grid=(n_tiles,),
        in_specs=[smem_blk, smem_blk, smem_next, smem_next,
                  pl.BlockSpec((tm, d), lambda i: (i, 0)),
                  pl.BlockSpec((tm, LANES), lambda i: (i, 0)),
                  pl.BlockSpec((1, d), lambda i: (0, 0)),
                  pl.BlockSpec(memory_space=pl.ANY)],
        out_specs=[oa_spec, ob_spec],
        out_shape=[jax.ShapeDtypeStruct((m_first, d), F32),
                   jax.ShapeDtypeStruct((m - m_first, d), F32)],
        scratch_shapes=[pltpu.VMEM((2, tm, d), F32), pltpu.VMEM((2, tm, d), F32),
                        pltpu.SemaphoreType.DMA((2, 2))],
        compiler_params=_cparams("arbitrary"),
        name="moe_combine",
    )(d0, d1, d0, d1, x, meta, g, yb)


def _sample_aux(state, shift, dec_seq):
    b, w1, c = state.shape
    rows = [state[:, w1 + t - shift] if t < shift else jnp.zeros((b, c), state.dtype)
            for t in range(dec_seq)]
    return jnp.stack(rows, axis=1).reshape(b * dec_seq, c)


def _seq_tails(a, col0, ncol, width, batch, seq, dec_batch, dec_seq):
    mp = batch * seq
    tails = [lax.slice(a, ((b + 1) * seq - width, col0), ((b + 1) * seq, col0 + ncol))
             for b in range(batch)]
    smp = lax.slice(a, (mp, col0), (mp + dec_batch * dec_seq, col0 + ncol))
    smp = smp.reshape(dec_batch, dec_seq, ncol)[:, dec_seq - width:]
    return jnp.stack(tails, axis=0), smp


def kernel(x_prompt, x_sample, state_lru_h, state_lru_conv, state_sconv, norm_mix, norm_ffn, norm_final, lru_w_in, lru_conv_w, lru_conv_b, lru_w_rgate, lru_b_rgate, lru_w_igate, lru_b_igate, lru_lambda, lru_w_out, sc_w_in, sc_conv_w, sc_w_out, ffn_w_gate, ffn_w_up, ffn_w_down, moe_w_router, moe_w_gate, moe_w_up, moe_w_down):
    batch, seq, d = x_prompt.shape
    dec_batch, dec_seq, _ = x_sample.shape
    mp, ms = batch * seq, dec_batch * dec_seq
    m = mp + ms
    depth = norm_mix.shape[0]
    n_experts = moe_w_router.shape[2]
    conv_a = lru_conv_w.shape[1]
    conv_b = sc_conv_w.shape[1]
    assert depth == 2 and lru_w_in.shape[0] == 1 and sc_w_in.shape[0] == 1
    assert seq & (seq - 1) == 0 and dec_seq & (dec_seq - 1) == 0
    assert seq % ROW_TILE == 0 and ms == ROW_TILE and ROW_TILE % dec_seq == 0
    assert dec_seq >= conv_a - 1 and dec_seq >= conv_b - 1 and dec_seq == SUBLANES
    assert n_experts <= LANES and m % ROUTER_ROW_TILE == 0 and m % COMBINE_ROW_TILE == 0
    n_prompt_tiles = mp // ROW_TILE

    xp = x_prompt.reshape(mp, d)
    xs_in = x_sample.reshape(ms, d)
    row = lambda v: v.reshape(1, -1)

    d_rnn = lru_w_out.shape[1]
    assert d_rnn % COL_TILE == 0
    proj = _norm_matmul(xp, xs_in, row(norm_mix[0]), lru_w_in[0], n_gelu_cols=d_rnn)
    h0_rows = jnp.pad(state_lru_h[0][:, None, :], ((0, 0), (0, dec_seq - 1), (0, 0)))
    aux = [_sample_aux(state_lru_conv[0], s, dec_seq) for s in (1, 2, 3)]
    aux.append(h0_rows.reshape(ms, d_rnn))
    y, h_all = _rglru_core(proj, aux, lru_conv_w[0], row(lru_conv_b[0]), row(lru_b_rgate[0]),
                           row(lru_b_igate[0]), row(lru_lambda[0]),
                           lru_w_rgate[0].astype(BF16), lru_w_igate[0].astype(BF16),
                           n_prompt_tiles, seq // ROW_TILE, dec_seq)
    x = _matmul_residual(y, lru_w_out[0].astype(BF16), xp, xs_in)

    x = _ffn(x, row(norm_ffn[0]), ffn_w_gate[0], ffn_w_up[0], ffn_w_down[0])

    aux = [_sample_aux(state_sconv[0], s, dec_seq) for s in (1, 2)]
    zb, cv_tile_tails, cv_smp = _sc_mixer(x, row(norm_mix[1]), sc_w_in[0], sc_conv_w[0], aux,
                                          n_prompt_tiles, seq // ROW_TILE, dec_seq)
    x = _matmul_residual(zb, sc_w_out[0].astype(BF16), x, x)

    w_router = jnp.pad(moe_w_router[0], ((0, 0), (0, LANES - n_experts)))
    hn, meta, counts = _router(x, row(norm_ffn[1]), w_router, n_experts)
    tr = MOE_ROW_TILE
    n_tiles = -(-(m * TOP_K) // tr) + n_experts
    counts = counts[0, :n_experts].astype(jnp.int32)
    k_tiles = (counts + tr - 1) // tr
    share = jnp.maximum((counts + k_tiles - 1) // jnp.maximum(k_tiles, 1), 1)
    padded = k_tiles * tr
    pad_end = jnp.cumsum(padded)
    pad_start = pad_end - padded
    e0 = meta[:, 0].astype(jnp.int32)
    e1 = meta[:, 1].astype(jnp.int32)

    def sorted_row(e, rank):
        tile_in_expert = rank // share[e]
        return pad_start[e] + tile_in_expert * tr + (rank - tile_in_expert * share[e])

    dest0 = sorted_row(e0, meta[:, 4].astype(jnp.int32))
    dest1 = sorted_row(e1, meta[:, 5].astype(jnp.int32))
    order = jnp.argsort(jnp.stack([e0, e1], axis=1).reshape(-1), stable=True)
    sorted_tok = (order // TOP_K).astype(jnp.int32)
    start = jnp.cumsum(counts) - counts
    block_start = jnp.arange(n_tiles, dtype=jnp.int32) * tr
    block_e = jnp.sum((pad_end[None, :] <= block_start[:, None]).astype(jnp.int32), axis=1)
    block_e = jnp.minimum(block_e, n_experts - 1)
    in_expert = (block_start - pad_start[block_e]) // tr * share[block_e]
    n_valid = jnp.clip(counts[block_e] - in_expert, 0, share[block_e]).astype(jnp.int32)
    n_used = (pad_end[-1:] // tr).astype(jnp.int32)
    yb = _experts(hn, sorted_tok, (start[block_e] + in_expert).astype(jnp.int32), block_e,
                  n_valid, n_used, moe_w_gate[0], moe_w_up[0], moe_w_down[0])
    y_prompt, y_sample = _combine(x, meta, row(norm_final), yb, dest0, dest1, mp)

    tails = functools.partial(_seq_tails, batch=batch, seq=seq, dec_batch=dec_batch, dec_seq=dec_seq)
    h_p, h_s = tails(h_all, 0, d_rnn, 1)
    u_p, u_s = tails(proj, d_rnn, d_rnn, conv_a - 1)
    tiles_per_seq = seq // ROW_TILE
    cv_p = cv_tile_tails.reshape(m // ROW_TILE, SUBLANES, -1)[
        tiles_per_seq - 1:n_prompt_tiles:tiles_per_seq, SUBLANES - (conv_b - 1):]
    cv_s = cv_smp.reshape(dec_batch, dec_seq, -1)[:, dec_seq - (conv_b - 1):]
    return (y_prompt.reshape(batch, seq, d), y_sample.reshape(dec_batch, dec_seq, d),
            h_p[:, 0][None], u_p[None], cv_p[None], h_s[:, 0][None], u_s[None], cv_s[None])
```

```python
import functools

import jax
import jax.numpy as jnp
from jax import lax
from jax.experimental import pallas as pl
from jax.experimental.pallas import tpu as pltpu

F32 = jnp.float32
BF16 = jnp.bfloat16

EPS = 1e-6
RG_C = 8.0
TOP_K = 2

LANES = 128
SUBLANES = 8
ROW_TILE = 1024
OUT_ROW_TILE = 512
ROUTER_ROW_TILE = 512
COL_TILE = 512
W_COL_TILE = 256
MOE_ROW_TILE = 768
MOE_COL_TILE = 512
COMBINE_ROW_TILE = 256
SCAN_CHUNK = 64
VMEM_LIMIT = 60 * 1024 * 1024


def _cparams(*sem):
    return pltpu.CompilerParams(dimension_semantics=sem, vmem_limit_bytes=VMEM_LIMIT)


def _rms_norm_f32(x, g):
    y = x * lax.rsqrt(jnp.mean(x * x, axis=-1, keepdims=True) + EPS)
    return y * g


def _two_part_specs(n_first, block, col_of, second_mode=None):
    first = pl.BlockSpec(block, lambda i, *a: (jnp.minimum(i, n_first - 1), col_of(*a)))
    second = pl.BlockSpec(block, lambda i, *a: (jnp.maximum(i - n_first, 0), col_of(*a)),
                          pipeline_mode=second_mode)
    return first, second


def _norm_matmul_kernel(n_first, n_gelu, xa_ref, xb_ref, g_ref, w_ref, o_ref, hn_ref):
    i = pl.program_id(0)
    j = pl.program_id(1)

    def project():
        return jnp.dot(hn_ref[...], w_ref[...].astype(BF16), preferred_element_type=F32)

    @pl.when((j == 0) & (i < n_first))
    def _():
        hn_ref[...] = _rms_norm_f32(xa_ref[...], g_ref[...]).astype(BF16)

    @pl.when((j == 0) & (i >= n_first))
    def _():
        hn_ref[...] = _rms_norm_f32(xb_ref[...], g_ref[...]).astype(BF16)

    @pl.when(j < n_gelu)
    def _():
        o_ref[...] = jax.nn.gelu(project())

    @pl.when(j >= n_gelu)
    def _():
        o_ref[...] = project()


def _norm_matmul(xa, xb, g, w, n_gelu_cols):
    d = xa.shape[1]
    m = xa.shape[0] + xb.shape[0]
    n = w.shape[1]
    assert xb.shape[0] == ROW_TILE
    xa_spec, xb_spec = _two_part_specs(xa.shape[0] // ROW_TILE, (ROW_TILE, d), lambda j: 0,
                                       second_mode=pl.Buffered(1))
    return pl.pallas_call(
        functools.partial(_norm_matmul_kernel, xa.shape[0] // ROW_TILE, n_gelu_cols // COL_TILE),
        grid=(m // ROW_TILE, n // COL_TILE),
        in_specs=[xa_spec, xb_spec,
                  pl.BlockSpec((1, d), lambda i, j: (0, 0)),
                  pl.BlockSpec((d, COL_TILE), lambda i, j: (0, j))],
        out_specs=pl.BlockSpec((ROW_TILE, COL_TILE), lambda i, j: (i, j)),
        out_shape=jax.ShapeDtypeStruct((m, n), F32),
        scratch_shapes=[pltpu.VMEM((ROW_TILE, d), BF16)],
        compiler_params=_cparams("arbitrary", "arbitrary"),
        name="norm_matmul",
    )(xa, xb, g, w)


def _sc_mixer_kernel(n_prompt_tiles, tiles_per_seq, seq_s,
                     x_ref, g_ref, wb_ref, wc_ref, wv_ref, cw_ref, q1_ref, q2_ref,
                     zb_ref, tail_ref, cvs_ref, hn_ref, carry_ref):
    i = pl.program_id(0)
    j = pl.program_id(1)
    t_rows = x_ref.shape[0]
    is_sample = i >= n_prompt_tiles
    starts_seq = is_sample | (lax.rem(i, tiles_per_seq) == 0)

    @pl.when(j == 0)
    def _():
        hn_ref[...] = _rms_norm_f32(x_ref[...], g_ref[...]).astype(BF16)

    @pl.when((i == 0) & (j == 0))
    def _():
        carry_ref[...] = jnp.zeros_like(carry_ref)

    def mixer(masked):
        hn = hn_ref[...]
        cw = cw_ref[...]
        c = jnp.dot(hn, wc_ref[...].astype(BF16), preferred_element_type=F32)
        v = jnp.dot(hn, wv_ref[...].astype(BF16), preferred_element_type=F32)
        cv = c * v
        prev = jnp.where(starts_seq, 0.0, carry_ref[j])
        ext = jnp.concatenate([prev, cv], axis=0)
        if masked:
            t = _pos_in_seq(0, t_rows, seq_s)

        def tap(s, q_ref):
            shifted = pltpu.roll(ext, s, axis=0)[SUBLANES:]
            return jnp.where(t >= s, shifted, q_ref[...]) if masked else shifted

        z = tap(2, q2_ref) * cw[0:1] + tap(1, q1_ref) * cw[1:2]
        z = z + cv * cw[2:3]
        b = jnp.dot(hn, wb_ref[...].astype(BF16), preferred_element_type=F32)
        zb_ref[...] = (b * z).astype(BF16)
        last = cv[t_rows - SUBLANES:, :]
        carry_ref[j] = last
        tail_ref[...] = last
        return cv

    @pl.when(is_sample)
    def _():
        cvs_ref[...] = mixer(True)

    @pl.when(jnp.logical_not(is_sample))
    def _():
        mixer(False)


def _sc_mixer(x, g, w, conv_w, aux, n_prompt_tiles, tiles_per_seq, seq_s):
    m, d = x.shape
    dc = w.shape[1] // 3
    tn = W_COL_TILE
    nj = dc // tn
    n_tiles = m // ROW_TILE
    ms = aux[0].shape[0]
    assert ms == ROW_TILE
    wspec = lambda off: pl.BlockSpec((d, tn), lambda i, j: (0, off * nj + j))
    sample_col = lambda i, j: (0, jnp.where(i >= n_prompt_tiles, j, 0))
    return pl.pallas_call(
        functools.partial(_sc_mixer_kernel, n_prompt_tiles, tiles_per_seq, seq_s),
        grid=(n_tiles, nj),
        in_specs=[pl.BlockSpec((ROW_TILE, d), lambda i, j: (i, 0)),
                  pl.BlockSpec((1, d), lambda i, j: (0, 0)),
                  wspec(0), wspec(1), wspec(2),
                  pl.BlockSpec((conv_w.shape[0], tn), lambda i, j: (0, j)),
                  pl.BlockSpec((ROW_TILE, tn), sample_col),
                  pl.BlockSpec((ROW_TILE, tn), sample_col)],
        out_specs=[pl.BlockSpec((ROW_TILE, tn), lambda i, j: (i, j)),
                   pl.BlockSpec((SUBLANES, tn), lambda i, j: (i, j)),
                   pl.BlockSpec((ROW_TILE, tn), sample_col)],
        out_shape=[jax.ShapeDtypeStruct((m, dc), BF16),
                   jax.ShapeDtypeStruct((n_tiles * SUBLANES, dc), F32),
                   jax.ShapeDtypeStruct((ms, dc), F32)],
        scratch_shapes=[pltpu.VMEM((ROW_TILE, d), BF16),
                        pltpu.VMEM((nj, SUBLANES, tn), F32)],
        compiler_params=_cparams("arbitrary", "arbitrary"),
        name="sc_mixer",
    )(x, g, w, w, w, conv_w, *aux)


def _matmul_residual_kernel(n_first, y_ref, w_ref, ra_ref, rb_ref, o_ref):
    o_ref[...] = jnp.dot(y_ref[...], w_ref[...], preferred_element_type=F32)
    i = pl.program_id(0)

    @pl.when(i < n_first)
    def _():
        o_ref[...] += ra_ref[...]

    @pl.when(i >= n_first)
    def _():
        o_ref[...] += rb_ref[...]


def _matmul_residual(y, w, res_a, res_b):
    m, k = y.shape
    n = w.shape[1]
    tm = OUT_ROW_TILE
    ra_spec, rb_spec = _two_part_specs(res_a.shape[0] // tm, (tm, n), lambda: 0)
    return pl.pallas_call(
        functools.partial(_matmul_residual_kernel, res_a.shape[0] // tm),
        grid=(m // tm,),
        in_specs=[pl.BlockSpec((tm, k), lambda i: (i, 0)),
                  pl.BlockSpec((k, n), lambda i: (0, 0)),
                  ra_spec, rb_spec],
        out_specs=pl.BlockSpec((tm, n), lambda i: (i, 0)),
        out_shape=jax.ShapeDtypeStruct((m, n), F32),
        compiler_params=_cparams("arbitrary"),
        name="matmul_residual",
    )(y, w, res_a, res_b)


def _pos_in_seq(r0, rows, seq_len):
    return (r0 + lax.broadcasted_iota(jnp.int32, (rows, 1), 0)) & (seq_len - 1)


def _sigmoid(x):
    return 0.5 * jnp.tanh(0.5 * x) + 0.5


def _run_chunks(is_sample, n_chunks, body):
    @pl.when(is_sample)
    def _():
        lax.fori_loop(0, n_chunks, lambda k, c: body(True, k) or c, 0)

    @pl.when(jnp.logical_not(is_sample))
    def _():
        lax.fori_loop(0, n_chunks, lambda k, c: body(False, k) or c, 0)


def _rglru_kernel(n_prompt_tiles, tiles_per_seq, seq_s,
                  gate_ref, u_ref, p1_ref, p2_ref, p3_ref, h0_ref,
                  cw_ref, cb_ref, br_ref, bi_ref, lam_ref, wr_ref, wi_ref,
                  y_ref, h_ref, ubuf, uc_ref, rp_ref, ip_ref, hc_ref):
    i = pl.program_id(1)
    cblk = u_ref.shape[1]
    is_sample = i >= n_prompt_tiles

    @pl.when(is_sample | (lax.rem(i, tiles_per_seq) == 0))
    def _():
        ubuf[0:SUBLANES, :] = jnp.zeros((SUBLANES, cblk), F32)
        hc_ref[...] = jnp.zeros_like(hc_ref)

    ubuf[SUBLANES:, :] = u_ref[...]
    cw = cw_ref[...]
    cb = cb_ref[...]
    rc = SCAN_CHUNK

    def conv_chunk(masked, k):
        r0 = pl.multiple_of(k * rc, rc)
        rows = pl.ds(r0, rc)
        ext = ubuf[pl.ds(r0, rc + SUBLANES), :]
        if masked:
            t = _pos_in_seq(r0, rc, seq_s)

        def tap(s, p_ref):
            shifted = pltpu.roll(ext, s, axis=0)[SUBLANES:]
            return jnp.where(t >= s, shifted, p_ref[rows, :]) if masked else shifted

        conv = tap(3, p3_ref) * cw[0:1] + tap(2, p2_ref) * cw[1:2]
        conv = conv + tap(1, p1_ref) * cw[2:3]
        conv = conv + ext[SUBLANES:] * cw[3:4]
        uc_ref[rows, :] = conv + cb

    _run_chunks(is_sample, ROW_TILE // rc, conv_chunk)

    ub = uc_ref[...].astype(BF16)
    rp_ref[...] = jnp.dot(ub, wr_ref[...], preferred_element_type=F32) + br_ref[...]
    ip_ref[...] = jnp.dot(ub, wi_ref[...], preferred_element_type=F32) + bi_ref[...]

    sp = jax.nn.softplus(-lam_ref[...])
    sub = lax.broadcasted_iota(jnp.int32, (1, SUBLANES, 1), 1)

    def scan_chunk(masked, k):
        r0 = pl.multiple_of(k * rc, rc)
        rows = pl.ds(r0, rc)
        uc = uc_ref[rows, :]
        r = _sigmoid(rp_ref[rows, :])
        ig = _sigmoid(ip_ref[rows, :])
        log_a = (-RG_C * r) * sp
        a = jnp.exp(log_a)
        mult = jnp.sqrt(-jnp.tanh(log_a) * (a * a + 1.0))
        b = mult * ig * uc
        if masked:
            first = _pos_in_seq(r0, rc, seq_s) == 0
            b = jnp.where(first, b + a * h0_ref[rows, :], b)
            a = jnp.where(first, 0.0, a)

        a3 = a.reshape(rc // SUBLANES, SUBLANES, cblk)
        b3 = b.reshape(rc // SUBLANES, SUBLANES, cblk)
        for s in (1, 2, 4):
            a_s = pltpu.roll(a3, s, axis=1)
            b_s = pltpu.roll(b3, s, axis=1)
            m = sub >= s
            b3 = jnp.where(m, a3 * b_s + b3, b3)
            a3 = jnp.where(m, a3 * a_s, a3)

        hc = hc_ref[...]
        hs = []
        for g in range(rc // SUBLANES):
            hg = a3[g] * hc + b3[g]
            hc = jnp.broadcast_to(hg[SUBLANES - 1:SUBLANES], (SUBLANES, cblk))
            hs.append(hg)
        hc_ref[...] = hc
        h = jnp.concatenate(hs, axis=0)
        h_ref[rows, :] = h
        y_ref[rows, :] = (h * gate_ref[rows, :]).astype(BF16)

    _run_chunks(is_sample, ROW_TILE // rc, scan_chunk)
    ubuf[0:SUBLANES, :] = ubuf[ROW_TILE:ROW_TILE + SUBLANES, :]


def _rglru_core(proj, aux, conv_w, conv_b, b_r, b_i, lam, w_r, w_i, n_prompt_tiles,
                tiles_per_seq, seq_s):
    m = proj.shape[0]
    nb, cblk = w_r.shape[0], w_r.shape[1]
    c = nb * cblk
    n_tiles = m // ROW_TILE
    row_blk = lambda off: pl.BlockSpec((ROW_TILE, cblk), lambda cb, i: (i, off + cb))
    aux_blk = pl.BlockSpec((ROW_TILE, cblk), lambda cb, i: (0, cb))
    vec_blk = lambda rows: pl.BlockSpec((rows, cblk), lambda cb, i: (0, cb))
    w_blk = pl.BlockSpec((None, cblk, cblk), lambda cb, i: (cb, 0, 0))
    out_blk = pl.BlockSpec((ROW_TILE, cblk), lambda cb, i: (i, cb))
    return pl.pallas_call(
        functools.partial(_rglru_kernel, n_prompt_tiles, tiles_per_seq, seq_s),
        grid=(nb, n_tiles),
        in_specs=[row_blk(0), row_blk(nb), aux_blk, aux_blk, aux_blk, aux_blk,
                  vec_blk(conv_w.shape[0]), vec_blk(1), vec_blk(1), vec_blk(1), vec_blk(1),
                  w_blk, w_blk],
        out_specs=[out_blk, out_blk],
        out_shape=[jax.ShapeDtypeStruct((m, c), BF16), jax.ShapeDtypeStruct((m, c), F32)],
        scratch_shapes=[pltpu.VMEM((ROW_TILE + SUBLANES, cblk), F32),
                        pltpu.VMEM((ROW_TILE, cblk), F32),
                        pltpu.VMEM((ROW_TILE, cblk), F32),
                        pltpu.VMEM((ROW_TILE, cblk), F32),
                        pltpu.VMEM((SUBLANES, cblk), F32)],
        compiler_params=_cparams("arbitrary", "arbitrary"),
        name="rglru_core",
    )(proj, proj, *aux, conv_w, conv_b, b_r, b_i, lam, w_r, w_i)


def _ffn_kernel(x_ref, g_ref, wg_ref, wu_ref, wd_ref, o_ref, hn_ref):
    @pl.when(pl.program_id(1) == 0)
    def _():
        x = x_ref[...]
        hn_ref[...] = _rms_norm_f32(x, g_ref[...]).astype(BF16)
        o_ref[...] = x

    hn = hn_ref[...]
    a = jnp.dot(hn, wg_ref[...].astype(BF16), preferred_element_type=F32)
    b = jnp.dot(hn, wu_ref[...].astype(BF16), preferred_element_type=F32)
    h = (jax.nn.silu(a) * b).astype(BF16)
    o_ref[...] += jnp.dot(h, wd_ref[...].astype(BF16), preferred_element_type=F32)


def _ffn(x, g, wg, wu, wd):
    m, d = x.shape
    f = wg.shape[1]
    tm = ROW_TILE
    tf = W_COL_TILE
    return pl.pallas_call(
        _ffn_kernel,
        grid=(m // tm, f // tf),
        in_specs=[pl.BlockSpec((tm, d), lambda i, j: (i, 0)),
                  pl.BlockSpec((1, d), lambda i, j: (0, 0)),
                  pl.BlockSpec((d, tf), lambda i, j: (0, j)),
                  pl.BlockSpec((d, tf), lambda i, j: (0, j)),
                  pl.BlockSpec((tf, d), lambda i, j: (j, 0))],
        out_specs=pl.BlockSpec((tm, d), lambda i, j: (i, 0)),
        out_shape=jax.ShapeDtypeStruct((m, d), F32),
        scratch_shapes=[pltpu.VMEM((tm, d), BF16)],
        compiler_params=_cparams("arbitrary", "arbitrary"),
        name="ffn",
    )(x, g, wg, wu, wd)


def _router_kernel(n_experts, x_ref, g_ref, whi_ref, wlo_ref, hn_ref, meta_ref, cnt_ref, run_ref):
    @pl.when(pl.program_id(0) == 0)
    def _():
        run_ref[...] = jnp.zeros_like(run_ref)

    tm = x_ref.shape[0]
    hn = _rms_norm_f32(x_ref[...], g_ref[...])
    hn_hi = hn.astype(BF16)
    hn_hi32 = hn_hi.astype(F32)
    bits = pltpu.bitcast(hn_hi32, jnp.uint32)
    half = hn.shape[1] // 2
    hn_ref[...] = (bits[:, half:] & jnp.uint32(0xFFFF0000)) | (bits[:, :half] >> 16)
    hn_lo = (hn - hn_hi32).astype(BF16)
    logits = (jnp.dot(hn_hi, whi_ref[...], preferred_element_type=F32)
              + (jnp.dot(hn_lo, whi_ref[...], preferred_element_type=F32)
                 + jnp.dot(hn_hi, wlo_ref[...], preferred_element_type=F32)))
    lane = lax.broadcasted_iota(jnp.int32, (tm, LANES), 1).astype(F32)
    neg = jnp.float32(-jnp.inf)
    lg = jnp.where(lane < n_experts, logits, neg)
    m1 = jnp.max(lg, axis=1, keepdims=True)
    i1 = jnp.min(jnp.where(lg == m1, lane, float(LANES)), axis=1, keepdims=True)
    lg2 = jnp.where(lane == i1, neg, lg)
    m2 = jnp.max(lg2, axis=1, keepdims=True)
    i2 = jnp.min(jnp.where(lg2 == m2, lane, float(LANES)), axis=1, keepdims=True)
    ex = jnp.exp(m2 - m1)
    g1 = 1.0 / (1.0 + ex)
    g2 = ex / (1.0 + ex)

    sel1 = lane == i1
    sel2 = lane == i2
    onehot = jnp.where(sel1 | sel2, 1.0, 0.0)
    rr = lax.broadcasted_iota(jnp.int32, (tm, tm), 0)
    cc = lax.broadcasted_iota(jnp.int32, (tm, tm), 1)
    tri = jnp.where(rr > cc, 1.0, 0.0).astype(BF16)
    before = jnp.dot(tri, onehot.astype(BF16), preferred_element_type=F32) + run_ref[0:1, :]
    rank1 = jnp.sum(jnp.where(sel1, before, 0.0), axis=1, keepdims=True)
    rank2 = jnp.sum(jnp.where(sel2, before, 0.0), axis=1, keepdims=True)
    run = run_ref[0:1, :] + jnp.sum(onehot, axis=0, keepdims=True)
    run_ref[...] = jnp.broadcast_to(run, run_ref.shape)
    cnt_ref[...] = jnp.broadcast_to(run, cnt_ref.shape)

    meta = jnp.where(lane == 0, i1, 0.0)
    meta = jnp.where(lane == 1, i2, meta)
    meta = jnp.where(lane == 2, g1, meta)
    meta = jnp.where(lane == 3, g2, meta)
    meta = jnp.where(lane == 4, rank1, meta)
    meta = jnp.where(lane == 5, rank2, meta)
    meta_ref[...] = meta


def _router(x, g, w_router_padded, n_experts):
    m, d = x.shape
    tm = ROUTER_ROW_TILE
    w_hi = w_router_padded.astype(BF16)
    w_lo = (w_router_padded - w_hi.astype(F32)).astype(BF16)
    return pl.pallas_call(
        functools.partial(_router_kernel, n_experts),
        grid=(m // tm,),
        in_specs=[pl.BlockSpec((tm, d), lambda i: (i, 0)),
                  pl.BlockSpec((1, d), lambda i: (0, 0)),
                  pl.BlockSpec((d, LANES), lambda i: (0, 0)),
                  pl.BlockSpec((d, LANES), lambda i: (0, 0))],
        out_specs=[pl.BlockSpec((tm, d // 2), lambda i: (i, 0)),
                   pl.BlockSpec((tm, LANES), lambda i: (i, 0)),
                   pl.BlockSpec((SUBLANES, LANES), lambda i: (0, 0))],
        out_shape=[jax.ShapeDtypeStruct((m, d // 2), jnp.uint32),
                   jax.ShapeDtypeStruct((m, LANES), F32),
                   jax.ShapeDtypeStruct((SUBLANES, LANES), F32)],
        scratch_shapes=[pltpu.VMEM((SUBLANES, LANES), F32)],
        compiler_params=_cparams("arbitrary"),
        name="router",
    )(x, g, w_hi, w_lo)


def _row_copy(src_hbm, idx, dst_ref, r, sem):
    return pltpu.make_async_copy(src_hbm.at[pl.ds(idx, 1)], dst_ref.at[pl.ds(r, 1)], sem)


def _experts_kernel(rows_per_step, be_ref, nv_ref, nused_ref, off_ref, stok_ref,
                    hn_hbm, wg_ref, wu_ref, wd_ref, o_ref,
                    xbuf, xb_ref, sem):
    t = pl.program_id(0)
    j = pl.program_id(1)
    nused = nused_ref[0]
    tr = xb_ref.shape[0]
    gr = xbuf.shape[0]
    dh = xbuf.shape[1]

    @pl.when(j == 0)
    def _():
        o_ref[...] = jnp.zeros_like(o_ref)

    @pl.when((t == 0) & (j == 0))
    def _():
        off0 = off_ref[0]

        def issue(r, c):
            _row_copy(hn_hbm, stok_ref[off0 + r], xbuf, r, sem).start()
            return c

        lax.fori_loop(0, gr, issue, 0, unroll=8)

    @pl.when((j == 0) & (t <= nused))
    def _():
        pltpu.make_async_copy(hn_hbm.at[pl.ds(0, gr)], xbuf, sem).wait()

    @pl.when((j == 0) & (t < nused))
    def _():
        w = xbuf[0:tr, :]
        xb_ref[:, 0:dh] = pltpu.bitcast(w << 16, F32).astype(BF16)
        xb_ref[:, dh:] = pltpu.bitcast(w & jnp.uint32(0xFFFF0000), F32).astype(BF16)

    def step(n_rows):
        base = off_ref[t + 1] + j * rows_per_step
        for k in range(rows_per_step):
            _row_copy(hn_hbm, stok_ref[base + k], xbuf, j * rows_per_step + k, sem).start()
        rows = pl.ds(0, n_rows)
        x = xb_ref[rows, :]
        a = jnp.dot(x, wg_ref[...].astype(BF16), preferred_element_type=F32)
        b = jnp.dot(x, wu_ref[...].astype(BF16), preferred_element_type=F32)
        h = (jax.nn.silu(a) * b).astype(BF16)
        o_ref[rows, :] += jnp.dot(h, wd_ref[...].astype(BF16), preferred_element_type=F32)

    quarter = tr // 4
    n_quarters = (nv_ref[t] + (quarter - 1)) // quarter
    for q in range(1, 5):
        @pl.when((t < nused) & (jnp.maximum(n_quarters, 1) == q))
        def _(q=q):
            step(q * quarter)


def _experts(hn_packed, sorted_tok, tile_off, block_e, n_valid, n_used, wg, wu, wd):
    dh = hn_packed.shape[1]
    d = 2 * dh
    f = wg.shape[2]
    tr = MOE_ROW_TILE
    tf = MOE_COL_TILE
    assert tr % (4 * 2 * SUBLANES) == 0 and f % tf == 0
    n_tiles = block_e.shape[0]
    nj = f // tf
    rows_per_step = -(-tr // nj)
    gr = rows_per_step * nj
    stok = jnp.pad(sorted_tok, (0, gr))
    offs = jnp.pad(jnp.clip(tile_off, 0, sorted_tok.shape[0]), (0, 1))

    def tile(t, nu):
        return jnp.minimum(t, jnp.maximum(nu[0] - 1, 0))

    def col(t, j, nu):
        return jnp.where(t < nu[0], j, nj - 1)

    wspec = lambda shape, imap: pl.BlockSpec(
        shape, lambda t, j, be, nv, nu, off, st: imap(be[tile(t, nu)], col(t, j, nu)))
    return pl.pallas_call(
        functools.partial(_experts_kernel, rows_per_step),
        grid_spec=pltpu.PrefetchScalarGridSpec(
            num_scalar_prefetch=5,
            grid=(n_tiles + 1, nj),
            in_specs=[pl.BlockSpec(memory_space=pl.ANY),
                      wspec((None, d, tf), lambda e, c: (e, 0, c)),
                      wspec((None, d, tf), lambda e, c: (e, 0, c)),
                      wspec((None, tf, d), lambda e, c: (e, c, 0))],
            out_specs=pl.BlockSpec((tr, d), lambda t, j, *_: (t, 0)),
            scratch_shapes=[pltpu.VMEM((gr, dh), jnp.uint32), pltpu.VMEM((tr, d), BF16),
                            pltpu.SemaphoreType.DMA(())]),
        out_shape=jax.ShapeDtypeStruct(((n_tiles + 1) * tr, d), F32),
        compiler_params=_cparams("arbitrary", "arbitrary"),
        name="moe_experts",
    )(block_e, jnp.pad(n_valid, (0, 1)), n_used, offs, stok, hn_packed, wg, wu, wd)


def _combine_kernel(n_first, d0_ref, d1_ref, d0n_ref, d1n_ref, x_ref, meta_ref, g_ref, yb_hbm,
                    oa_ref, ob_ref, y0, y1, sem):
    tm = x_ref.shape[0]
    i = pl.program_id(0)
    slot = lax.rem(i, 2)

    def issue_tile(da_ref, db_ref, s):
        def issue(r, c):
            _row_copy(yb_hbm, da_ref[0, 0, r], y0.at[s], r, sem.at[0, s]).start()
            _row_copy(yb_hbm, db_ref[0, 0, r], y1.at[s], r, sem.at[1, s]).start()
            return c

        lax.fori_loop(0, tm, issue, 0, unroll=8)

    @pl.when(i == 0)
    def _():
        issue_tile(d0_ref, d1_ref, 0)

    @pl.when(i + 1 < pl.num_programs(0))
    def _():
        issue_tile(d0n_ref, d1n_ref, 1 - slot)

    pltpu.make_async_copy(yb_hbm.at[pl.ds(0, tm)], y0.at[slot], sem.at[0, slot]).wait()
    pltpu.make_async_copy(yb_hbm.at[pl.ds(0, tm)], y1.at[slot], sem.at[1, slot]).wait()
    meta = meta_ref[...]
    x = x_ref[...] + (y0[slot] * meta[:, 2:3] + y1[slot] * meta[:, 3:4])

    @pl.when(i < n_first)
    def _():
        oa_ref[...] = _rms_norm_f32(x, g_ref[...])

    @pl.when(i >= n_first)
    def _():
        ob_ref[...] = _rms_norm_f32(x, g_ref[...])


def _combine(x, meta, g, yb, dest0, dest1, m_first):
    m, d = x.shape
    tm = COMBINE_ROW_TILE
    n_tiles = m // tm
    smem_blk = pl.BlockSpec((1, 1, tm), lambda i: (i, 0, 0), memory_space=pltpu.SMEM)
    smem_next = pl.BlockSpec((1, 1, tm), lambda i: (jnp.minimum(i + 1, n_tiles - 1), 0, 0),
                             memory_space=pltpu.SMEM)
    oa_spec, ob_spec = _two_part_specs(m_first // tm, (tm, d), lambda: 0)
    d0 = dest0.reshape(n_tiles, 1, tm)
    d1 = dest1.reshape(n_tiles, 1, tm)
    return pl.pallas_call(
        functools.partial(_combine_kernel, m_first // tm),
        grid=(n_tiles,),
        in_specs=[smem_blk, smem_blk, smem_next, smem_next,
                  pl.BlockSpec((tm, d), lambda i: (i, 0)),
                  pl.BlockSpec((tm, LANES), lambda i: (i, 0)),
                  pl.BlockSpec((1, d), lambda i: (0, 0)),
                  pl.BlockSpec(memory_space=pl.ANY)],
        out_specs=[oa_spec, ob_spec],
        out_shape=[jax.ShapeDtypeStruct((m_first, d), F32),
                   jax.ShapeDtypeStruct((m - m_first, d), F32)],
        scratch_shapes=[pltpu.VMEM((2, tm, d), F32), pltpu.VMEM((2, tm, d), F32),
                        pltpu.SemaphoreType.DMA((2, 2))],
        compiler_params=_cparams("arbitrary"),
        name="moe_combine",
    )(d0, d1, d0, d1, x, meta, g, yb)


def _sample_aux(state, shift, dec_seq):
    b, w1, c = state.shape
    rows = [state[:, w1 + t - shift] if t < shift else jnp.zeros((b, c), state.dtype)
            for t in range(dec_seq)]
    return jnp.stack(rows, axis=1).reshape(b * dec_seq, c)


def _seq_tails(a, col0, ncol, width, batch, seq, dec_batch, dec_seq):
    mp = batch * seq
    tails = [lax.slice(a, ((b + 1) * seq - width, col0), ((b + 1) * seq, col0 + ncol))
             for b in range(batch)]
    smp = lax.slice(a, (mp, col0), (mp + dec_batch * dec_seq, col0 + ncol))
    smp = smp.reshape(dec_batch, dec_seq, ncol)[:, dec_seq - width:]
    return jnp.stack(tails, axis=0), smp


def kernel(x_prompt, x_sample, state_lru_h, state_lru_conv, state_sconv, norm_mix, norm_ffn, norm_final, lru_w_in, lru_conv_w, lru_conv_b, lru_w_rgate, lru_b_rgate, lru_w_igate, lru_b_igate, lru_lambda, lru_w_out, sc_w_in, sc_conv_w, sc_w_out, ffn_w_gate, ffn_w_up, ffn_w_down, moe_w_router, moe_w_gate, moe_w_up, moe_w_down):
    batch, seq, d = x_prompt.shape
    dec_batch, dec_seq, _ = x_sample.shape
    mp, ms = batch * seq, dec_batch * dec_seq
    m = mp + ms
    depth = norm_mix.shape[0]
    n_experts = moe_w_router.shape[2]
    conv_a = lru_conv_w.shape[1]
    conv_b = sc_conv_w.shape[1]
    assert depth == 2 and lru_w_in.shape[0] == 1 and sc_w_in.shape[0] == 1
    assert seq & (seq - 1) == 0 and dec_seq & (dec_seq - 1) == 0
    assert seq % ROW_TILE == 0 and ms == ROW_TILE and ROW_TILE % dec_seq == 0
    assert dec_seq >= conv_a - 1 and dec_seq >= conv_b - 1 and dec_seq == SUBLANES
    assert n_experts <= LANES and m % ROUTER_ROW_TILE == 0 and m % COMBINE_ROW_TILE == 0
    n_prompt_tiles = mp // ROW_TILE

    xp = x_prompt.reshape(mp, d)
    xs_in = x_sample.reshape(ms, d)
    row = lambda v: v.reshape(1, -1)

    d_rnn = lru_w_out.shape[1]
    assert d_rnn % COL_TILE == 0
    proj = _norm_matmul(xp, xs_in, row(norm_mix[0]), lru_w_in[0], n_gelu_cols=d_rnn)
    h0_rows = jnp.pad(state_lru_h[0][:, None, :], ((0, 0), (0, dec_seq - 1), (0, 0)))
    aux = [_sample_aux(state_lru_conv[0], s, dec_seq) for s in (1, 2, 3)]
    aux.append(h0_rows.reshape(ms, d_rnn))
    y, h_all = _rglru_core(proj, aux, lru_conv_w[0], row(lru_conv_b[0]), row(lru_b_rgate[0]),
                           row(lru_b_igate[0]), row(lru_lambda[0]),
                           lru_w_rgate[0].astype(BF16), lru_w_igate[0].astype(BF16),
                           n_prompt_tiles, seq // ROW_TILE, dec_seq)
    x = _matmul_residual(y, lru_w_out[0].astype(BF16), xp, xs_in)

    x = _ffn(x, row(norm_ffn[0]), ffn_w_gate[0], ffn_w_up[0], ffn_w_down[0])

    aux = [_sample_aux(state_sconv[0], s, dec_seq) for s in (1, 2)]
    zb, cv_tile_tails, cv_smp = _sc_mixer(x, row(norm_mix[1]), sc_w_in[0], sc_conv_w[0], aux,
                                          n_prompt_tiles, seq // ROW_TILE, dec_seq)
    x = _matmul_residual(zb, sc_w_out[0].astype(BF16), x, x)

    w_router = jnp.pad(moe_w_router[0], ((0, 0), (0, LANES - n_experts)))
    hn, meta, counts = _router(x, row(norm_ffn[1]), w_router, n_experts)
    tr = MOE_ROW_TILE
    n_tiles = -(-(m * TOP_K) // tr) + n_experts
    counts = counts[0, :n_experts].astype(jnp.int32)
    k_tiles = (counts + tr - 1) // tr
    share = jnp.maximum((counts + k_tiles - 1) // jnp.maximum(k_tiles, 1), 1)
    padded = k_tiles * tr
    pad_end = jnp.cumsum(padded)
    pad_start = pad_end - padded
    e0 = meta[:, 0].astype(jnp.int32)
    e1 = meta[:, 1].astype(jnp.int32)

    def sorted_row(e, rank):
        tile_in_expert = rank // share[e]
        return pad_start[e] + tile_in_expert * tr + (rank - tile_in_expert * share[e])

    dest0 = sorted_row(e0, meta[:, 4].astype(jnp.int32))
    dest1 = sorted_row(e1, meta[:, 5].astype(jnp.int32))
    order = jnp.argsort(jnp.stack([e0, e1], axis=1).reshape(-1), stable=True)
    sorted_tok = (order // TOP_K).astype(jnp.int32)
    start = jnp.cumsum(counts) - counts
    block_start = jnp.arange(n_tiles, dtype=jnp.int32) * tr
    block_e = jnp.sum((pad_end[None, :] <= block_start[:, None]).astype(jnp.int32), axis=1)
    block_e = jnp.minimum(block_e, n_experts - 1)
    in_expert = (block_start - pad_start[block_e]) // tr * share[block_e]
    n_valid = jnp.clip(counts[block_e] - in_expert, 0, share[block_e]).astype(jnp.int32)
    n_used = (pad_end[-1:] // tr).astype(jnp.int32)
    yb = _experts(hn, sorted_tok, (start[block_e] + in_expert).astype(jnp.int32), block_e,
                  n_valid, n_used, moe_w_gate[0], moe_w_up[0], moe_w_down[0])
    y_prompt, y_sample = _combine(x, meta, row(norm_final), yb, dest0, dest1, mp)

    tails = functools.partial(_seq_tails, batch=batch, seq=seq, dec_batch=dec_batch, dec_seq=dec_seq)
    h_p, h_s = tails(h_all, 0, d_rnn, 1)
    u_p, u_s = tails(proj, d_rnn, d_rnn, conv_a - 1)
    tiles_per_seq = seq // ROW_TILE
    cv_p = cv_tile_tails.reshape(m // ROW_TILE, SUBLANES, -1)[
        tiles_per_seq - 1:n_prompt_tiles:tiles_per_seq, SUBLANES - (conv_b - 1):]
    cv_s = cv_smp.reshape(dec_batch, dec_seq, -1)[:, dec_seq - (conv_b - 1):]
    return (y_prompt.reshape(batch, seq, d), y_sample.reshape(dec_batch, dec_seq, d),
            h_p[:, 0][None], u_p[None], cv_p[None], h_s[:, 0][None], u_s[None], cv_s[None])
```

```python
import functools

import jax
import jax.numpy as jnp
from jax import lax
from jax.experimental import pallas as pl
from jax.experimental.pallas import tpu as pltpu

F32 = jnp.float32
BF16 = jnp.bfloat16

EPS = 1e-6
RG_C = 8.0
TOP_K = 2

LANES = 128
SUBLANES = 8
ROW_TILE = 1024
OUT_ROW_TILE = 512
ROUTER_ROW_TILE = 512
COL_TILE = 512
W_COL_TILE = 256
MOE_ROW_TILE = 960
MOE_ROW_SLABS = 6
MOE_COL_TILE = 512
COMBINE_ROW_TILE = 256
SCAN_CHUNK = 64
VMEM_LIMIT = 60 * 1024 * 1024


def _cparams(*sem):
    return pltpu.CompilerParams(dimension_semantics=sem, vmem_limit_bytes=VMEM_LIMIT)


def _rms_norm_f32(x, g):
    y = x * lax.rsqrt(jnp.mean(x * x, axis=-1, keepdims=True) + EPS)
    return y * g


def _two_part_specs(n_first, block, col_of, second_mode=None):
    first = pl.BlockSpec(block, lambda i, *a: (jnp.minimum(i, n_first - 1), col_of(*a)))
    second = pl.BlockSpec(block, lambda i, *a: (jnp.maximum(i - n_first, 0), col_of(*a)),
                          pipeline_mode=second_mode)
    return first, second


def _norm_matmul_kernel(n_first, n_gelu, xa_ref, xb_ref, g_ref, w_ref, o_ref, hn_ref):
    i = pl.program_id(0)
    j = pl.program_id(1)

    def project():
        return jnp.dot(hn_ref[...], w_ref[...].astype(BF16), preferred_element_type=F32)

    @pl.when((j == 0) & (i < n_first))
    def _():
        hn_ref[...] = _rms_norm_f32(xa_ref[...], g_ref[...]).astype(BF16)

    @pl.when((j == 0) & (i >= n_first))
    def _():
        hn_ref[...] = _rms_norm_f32(xb_ref[...], g_ref[...]).astype(BF16)

    @pl.when(j < n_gelu)
    def _():
        o_ref[...] = jax.nn.gelu(project())

    @pl.when(j >= n_gelu)
    def _():
        o_ref[...] = project()


def _norm_matmul(xa, xb, g, w, n_gelu_cols):
    d = xa.shape[1]
    m = xa.shape[0] + xb.shape[0]
    n = w.shape[1]
    assert xb.shape[0] == ROW_TILE
    xa_spec, xb_spec = _two_part_specs(xa.shape[0] // ROW_TILE, (ROW_TILE, d), lambda j: 0,
                                       second_mode=pl.Buffered(1))
    return pl.pallas_call(
        functools.partial(_norm_matmul_kernel, xa.shape[0] // ROW_TILE, n_gelu_cols // COL_TILE),
        grid=(m // ROW_TILE, n // COL_TILE),
        in_specs=[xa_spec, xb_spec,
                  pl.BlockSpec((1, d), lambda i, j: (0, 0)),
                  pl.BlockSpec((d, COL_TILE), lambda i, j: (0, j))],
        out_specs=pl.BlockSpec((ROW_TILE, COL_TILE), lambda i, j: (i, j)),
        out_shape=jax.ShapeDtypeStruct((m, n), F32),
        scratch_shapes=[pltpu.VMEM((ROW_TILE, d), BF16)],
        compiler_params=_cparams("arbitrary", "arbitrary"),
        name="norm_matmul",
    )(xa, xb, g, w)


def _sc_mixer_kernel(n_prompt_tiles, tiles_per_seq, seq_s,
                     x_ref, g_ref, wb_ref, wc_ref, wv_ref, cw_ref, q1_ref, q2_ref,
                     zb_ref, tail_ref, cvs_ref, hn_ref, carry_ref):
    i = pl.program_id(0)
    j = pl.program_id(1)
    t_rows = x_ref.shape[0]
    is_sample = i >= n_prompt_tiles
    starts_seq = is_sample | (lax.rem(i, tiles_per_seq) == 0)

    @pl.when(j == 0)
    def _():
        hn_ref[...] = _rms_norm_f32(x_ref[...], g_ref[...]).astype(BF16)

    @pl.when((i == 0) & (j == 0))
    def _():
        carry_ref[...] = jnp.zeros_like(carry_ref)

    def mixer(masked):
        hn = hn_ref[...]
        cw = cw_ref[...]
        c = jnp.dot(hn, wc_ref[...].astype(BF16), preferred_element_type=F32)
        v = jnp.dot(hn, wv_ref[...].astype(BF16), preferred_element_type=F32)
        cv = c * v
        prev = jnp.where(starts_seq, 0.0, carry_ref[j])
        ext = jnp.concatenate([prev, cv], axis=0)
        if masked:
            t = _pos_in_seq(0, t_rows, seq_s)

        def tap(s, q_ref):
            shifted = pltpu.roll(ext, s, axis=0)[SUBLANES:]
            return jnp.where(t >= s, shifted, q_ref[...]) if masked else shifted

        z = tap(2, q2_ref) * cw[0:1] + tap(1, q1_ref) * cw[1:2]
        z = z + cv * cw[2:3]
        b = jnp.dot(hn, wb_ref[...].astype(BF16), preferred_element_type=F32)
        zb_ref[...] = (b * z).astype(BF16)
        last = cv[t_rows - SUBLANES:, :]
        carry_ref[j] = last
        tail_ref[...] = last
        return cv

    @pl.when(is_sample)
    def _():
        cvs_ref[...] = mixer(True)

    @pl.when(jnp.logical_not(is_sample))
    def _():
        mixer(False)


def _sc_mixer(x, g, w, conv_w, aux, n_prompt_tiles, tiles_per_seq, seq_s):
    m, d = x.shape
    dc = w.shape[1] // 3
    tn = W_COL_TILE
    nj = dc // tn
    n_tiles = m // ROW_TILE
    ms = aux[0].shape[0]
    assert ms == ROW_TILE
    wspec = lambda off: pl.BlockSpec((d, tn), lambda i, j: (0, off * nj + j))
    sample_col = lambda i, j: (0, jnp.where(i >= n_prompt_tiles, j, 0))
    return pl.pallas_call(
        functools.partial(_sc_mixer_kernel, n_prompt_tiles, tiles_per_seq, seq_s),
        grid=(n_tiles, nj),
        in_specs=[pl.BlockSpec((ROW_TILE, d), lambda i, j: (i, 0)),
                  pl.BlockSpec((1, d), lambda i, j: (0, 0)),
                  wspec(0), wspec(1), wspec(2),
                  pl.BlockSpec((conv_w.shape[0], tn), lambda i, j: (0, j)),
                  pl.BlockSpec((ROW_TILE, tn), sample_col),
                  pl.BlockSpec((ROW_TILE, tn), sample_col)],
        out_specs=[pl.BlockSpec((ROW_TILE, tn), lambda i, j: (i, j)),
                   pl.BlockSpec((SUBLANES, tn), lambda i, j: (i, j)),
                   pl.BlockSpec((ROW_TILE, tn), sample_col)],
        out_shape=[jax.ShapeDtypeStruct((m, dc), BF16),
                   jax.ShapeDtypeStruct((n_tiles * SUBLANES, dc), F32),
                   jax.ShapeDtypeStruct((ms, dc), F32)],
        scratch_shapes=[pltpu.VMEM((ROW_TILE, d), BF16),
                        pltpu.VMEM((nj, SUBLANES, tn), F32)],
        compiler_params=_cparams("arbitrary", "arbitrary"),
        name="sc_mixer",
    )(x, g, w, w, w, conv_w, *aux)


def _matmul_residual_kernel(n_first, y_ref, w_ref, ra_ref, rb_ref, o_ref):
    o_ref[...] = jnp.dot(y_ref[...], w_ref[...], preferred_element_type=F32)
    i = pl.program_id(0)

    @pl.when(i < n_first)
    def _():
        o_ref[...] += ra_ref[...]

    @pl.when(i >= n_first)
    def _():
        o_ref[...] += rb_ref[...]


def _matmul_residual(y, w, res_a, res_b):
    m, k = y.shape
    n = w.shape[1]
    tm = OUT_ROW_TILE
    ra_spec, rb_spec = _two_part_specs(res_a.shape[0] // tm, (tm, n), lambda: 0)
    return pl.pallas_call(
        functools.partial(_matmul_residual_kernel, res_a.shape[0] // tm),
        grid=(m // tm,),
        in_specs=[pl.BlockSpec((tm, k), lambda i: (i, 0)),
                  pl.BlockSpec((k, n), lambda i: (0, 0)),
                  ra_spec, rb_spec],
        out_specs=pl.BlockSpec((tm, n), lambda i: (i, 0)),
        out_shape=jax.ShapeDtypeStruct((m, n), F32),
        compiler_params=_cparams("arbitrary"),
        name="matmul_residual",
    )(y, w, res_a, res_b)


def _pos_in_seq(r0, rows, seq_len):
    return (r0 + lax.broadcasted_iota(jnp.int32, (rows, 1), 0)) & (seq_len - 1)


def _sigmoid(x):
    return 0.5 * jnp.tanh(0.5 * x) + 0.5


def _run_chunks(is_sample, n_chunks, body):
    @pl.when(is_sample)
    def _():
        lax.fori_loop(0, n_chunks, lambda k, c: body(True, k) or c, 0)

    @pl.when(jnp.logical_not(is_sample))
    def _():
        lax.fori_loop(0, n_chunks, lambda k, c: body(False, k) or c, 0)


def _rglru_kernel(n_prompt_tiles, tiles_per_seq, seq_s,
                  gate_ref, u_ref, p1_ref, p2_ref, p3_ref, h0_ref,
                  cw_ref, cb_ref, br_ref, bi_ref, lam_ref, wr_ref, wi_ref,
                  y_ref, h_ref, ubuf, uc_ref, rp_ref, ip_ref, hc_ref):
    i = pl.program_id(1)
    cblk = u_ref.shape[1]
    is_sample = i >= n_prompt_tiles

    @pl.when(is_sample | (lax.rem(i, tiles_per_seq) == 0))
    def _():
        ubuf[0:SUBLANES, :] = jnp.zeros((SUBLANES, cblk), F32)
        hc_ref[...] = jnp.zeros_like(hc_ref)

    ubuf[SUBLANES:, :] = u_ref[...]
    cw = cw_ref[...]
    cb = cb_ref[...]
    rc = SCAN_CHUNK

    def conv_chunk(masked, k):
        r0 = pl.multiple_of(k * rc, rc)
        rows = pl.ds(r0, rc)
        ext = ubuf[pl.ds(r0, rc + SUBLANES), :]
        if masked:
            t = _pos_in_seq(r0, rc, seq_s)

        def tap(s, p_ref):
            shifted = pltpu.roll(ext, s, axis=0)[SUBLANES:]
            return jnp.where(t >= s, shifted, p_ref[rows, :]) if masked else shifted

        conv = tap(3, p3_ref) * cw[0:1] + tap(2, p2_ref) * cw[1:2]
        conv = conv + tap(1, p1_ref) * cw[2:3]
        conv = conv + ext[SUBLANES:] * cw[3:4]
        uc_ref[rows, :] = conv + cb

    _run_chunks(is_sample, ROW_TILE // rc, conv_chunk)

    ub = uc_ref[...].astype(BF16)
    rp_ref[...] = jnp.dot(ub, wr_ref[...], preferred_element_type=F32) + br_ref[...]
    ip_ref[...] = jnp.dot(ub, wi_ref[...], preferred_element_type=F32) + bi_ref[...]

    sp = jax.nn.softplus(-lam_ref[...])
    sub = lax.broadcasted_iota(jnp.int32, (1, SUBLANES, 1), 1)

    def scan_chunk(masked, k):
        r0 = pl.multiple_of(k * rc, rc)
        rows = pl.ds(r0, rc)
        uc = uc_ref[rows, :]
        r = _sigmoid(rp_ref[rows, :])
        ig = _sigmoid(ip_ref[rows, :])
        log_a = (-RG_C * r) * sp
        a = jnp.exp(log_a)
        mult = jnp.sqrt(-jnp.tanh(log_a) * (a * a + 1.0))
        b = mult * ig * uc
        if masked:
            first = _pos_in_seq(r0, rc, seq_s) == 0
            b = jnp.where(first, b + a * h0_ref[rows, :], b)
            a = jnp.where(first, 0.0, a)

        a3 = a.reshape(rc // SUBLANES, SUBLANES, cblk)
        b3 = b.reshape(rc // SUBLANES, SUBLANES, cblk)
        for s in (1, 2, 4):
            a_s = pltpu.roll(a3, s, axis=1)
            b_s = pltpu.roll(b3, s, axis=1)
            m = sub >= s
            b3 = jnp.where(m, a3 * b_s + b3, b3)
            a3 = jnp.where(m, a3 * a_s, a3)

        hc = hc_ref[...]
        hs = []
        for g in range(rc // SUBLANES):
            hg = a3[g] * hc + b3[g]
            hc = jnp.broadcast_to(hg[SUBLANES - 1:SUBLANES], (SUBLANES, cblk))
            hs.append(hg)
        hc_ref[...] = hc
        h = jnp.concatenate(hs, axis=0)
        h_ref[rows, :] = h
        y_ref[rows, :] = (h * gate_ref[rows, :]).astype(BF16)

    _run_chunks(is_sample, ROW_TILE // rc, scan_chunk)
    ubuf[0:SUBLANES, :] = ubuf[ROW_TILE:ROW_TILE + SUBLANES, :]


def _rglru_core(proj, aux, conv_w, conv_b, b_r, b_i, lam, w_r, w_i, n_prompt_tiles,
                tiles_per_seq, seq_s):
    m = proj.shape[0]
    nb, cblk = w_r.shape[0], w_r.shape[1]
    c = nb * cblk
    n_tiles = m // ROW_TILE
    row_blk = lambda off: pl.BlockSpec((ROW_TILE, cblk), lambda cb, i: (i, off + cb))
    aux_blk = pl.BlockSpec((ROW_TILE, cblk), lambda cb, i: (0, cb))
    vec_blk = lambda rows: pl.BlockSpec((rows, cblk), lambda cb, i: (0, cb))
    w_blk = pl.BlockSpec((None, cblk, cblk), lambda cb, i: (cb, 0, 0))
    out_blk = pl.BlockSpec((ROW_TILE, cblk), lambda cb, i: (i, cb))
    return pl.pallas_call(
        functools.partial(_rglru_kernel, n_prompt_tiles, tiles_per_seq, seq_s),
        grid=(nb, n_tiles),
        in_specs=[row_blk(0), row_blk(nb), aux_blk, aux_blk, aux_blk, aux_blk,
                  vec_blk(conv_w.shape[0]), vec_blk(1), vec_blk(1), vec_blk(1), vec_blk(1),
                  w_blk, w_blk],
        out_specs=[out_blk, out_blk],
        out_shape=[jax.ShapeDtypeStruct((m, c), BF16), jax.ShapeDtypeStruct((m, c), F32)],
        scratch_shapes=[pltpu.VMEM((ROW_TILE + SUBLANES, cblk), F32),
                        pltpu.VMEM((ROW_TILE, cblk), F32),
                        pltpu.VMEM((ROW_TILE, cblk), F32),
                        pltpu.VMEM((ROW_TILE, cblk), F32),
                        pltpu.VMEM((SUBLANES, cblk), F32)],
        compiler_params=_cparams("arbitrary", "arbitrary"),
        name="rglru_core",
    )(proj, proj, *aux, conv_w, conv_b, b_r, b_i, lam, w_r, w_i)


def _ffn_kernel(x_ref, g_ref, wg_ref, wu_ref, wd_ref, o_ref, hn_ref):
    @pl.when(pl.program_id(1) == 0)
    def _():
        x = x_ref[...]
        hn_ref[...] = _rms_norm_f32(x, g_ref[...]).astype(BF16)
        o_ref[...] = x

    hn = hn_ref[...]
    a = jnp.dot(hn, wg_ref[...].astype(BF16), preferred_element_type=F32)
    b = jnp.dot(hn, wu_ref[...].astype(BF16), preferred_element_type=F32)
    h = (jax.nn.silu(a) * b).astype(BF16)
    o_ref[...] += jnp.dot(h, wd_ref[...].astype(BF16), preferred_element_type=F32)


def _ffn(x, g, wg, wu, wd):
    m, d = x.shape
    f = wg.shape[1]
    tm = ROW_TILE
    tf = W_COL_TILE
    return pl.pallas_call(
        _ffn_kernel,
        grid=(m // tm, f // tf),
        in_specs=[pl.BlockSpec((tm, d), lambda i, j: (i, 0)),
                  pl.BlockSpec((1, d), lambda i, j: (0, 0)),
                  pl.BlockSpec((d, tf), lambda i, j: (0, j)),
                  pl.BlockSpec((d, tf), lambda i, j: (0, j)),
                  pl.BlockSpec((tf, d), lambda i, j: (j, 0))],
        out_specs=pl.BlockSpec((tm, d), lambda i, j: (i, 0)),
        out_shape=jax.ShapeDtypeStruct((m, d), F32),
        scratch_shapes=[pltpu.VMEM((tm, d), BF16)],
        compiler_params=_cparams("arbitrary", "arbitrary"),
        name="ffn",
    )(x, g, wg, wu, wd)


def _router_kernel(n_experts, x_ref, g_ref, whi_ref, wlo_ref, hn_ref, meta_ref, cnt_ref, run_ref):
    @pl.when(pl.program_id(0) == 0)
    def _():
        run_ref[...] = jnp.zeros_like(run_ref)

    tm = x_ref.shape[0]
    hn = _rms_norm_f32(x_ref[...], g_ref[...])
    hn_hi = hn.astype(BF16)
    hn_hi32 = hn_hi.astype(F32)
    bits = pltpu.bitcast(hn_hi32, jnp.uint32)
    half = hn.shape[1] // 2
    hn_ref[...] = (bits[:, half:] & jnp.uint32(0xFFFF0000)) | (bits[:, :half] >> 16)
    hn_lo = (hn - hn_hi32).astype(BF16)
    logits = (jnp.dot(hn_hi, whi_ref[...], preferred_element_type=F32)
              + (jnp.dot(hn_lo, whi_ref[...], preferred_element_type=F32)
                 + jnp.dot(hn_hi, wlo_ref[...], preferred_element_type=F32)))
    lane = lax.broadcasted_iota(jnp.int32, (tm, LANES), 1).astype(F32)
    neg = jnp.float32(-jnp.inf)
    lg = jnp.where(lane < n_experts, logits, neg)
    m1 = jnp.max(lg, axis=1, keepdims=True)
    i1 = jnp.min(jnp.where(lg == m1, lane, float(LANES)), axis=1, keepdims=True)
    lg2 = jnp.where(lane == i1, neg, lg)
    m2 = jnp.max(lg2, axis=1, keepdims=True)
    i2 = jnp.min(jnp.where(lg2 == m2, lane, float(LANES)), axis=1, keepdims=True)
    ex = jnp.exp(m2 - m1)
    g1 = 1.0 / (1.0 + ex)
    g2 = ex / (1.0 + ex)

    sel1 = lane == i1
    sel2 = lane == i2
    onehot = jnp.where(sel1 | sel2, 1.0, 0.0)
    rr = lax.broadcasted_iota(jnp.int32, (tm, tm), 0)
    cc = lax.broadcasted_iota(jnp.int32, (tm, tm), 1)
    tri = jnp.where(rr > cc, 1.0, 0.0).astype(BF16)
    before = jnp.dot(tri, onehot.astype(BF16), preferred_element_type=F32) + run_ref[0:1, :]
    rank1 = jnp.sum(jnp.where(sel1, before, 0.0), axis=1, keepdims=True)
    rank2 = jnp.sum(jnp.where(sel2, before, 0.0), axis=1, keepdims=True)
    run = run_ref[0:1, :] + jnp.sum(onehot, axis=0, keepdims=True)
    run_ref[...] = jnp.broadcast_to(run, run_ref.shape)
    cnt_ref[...] = jnp.broadcast_to(run, cnt_ref.shape)

    meta = jnp.where(lane == 0, i1, 0.0)
    meta = jnp.where(lane == 1, i2, meta)
    meta = jnp.where(lane == 2, g1, meta)
    meta = jnp.where(lane == 3, g2, meta)
    meta = jnp.where(lane == 4, rank1, meta)
    meta = jnp.where(lane == 5, rank2, meta)
    meta_ref[...] = meta


def _router(x, g, w_router_padded, n_experts):
    m, d = x.shape
    tm = ROUTER_ROW_TILE
    w_hi = w_router_padded.astype(BF16)
    w_lo = (w_router_padded - w_hi.astype(F32)).astype(BF16)
    return pl.pallas_call(
        functools.partial(_router_kernel, n_experts),
        grid=(m // tm,),
        in_specs=[pl.BlockSpec((tm, d), lambda i: (i, 0)),
                  pl.BlockSpec((1, d), lambda i: (0, 0)),
                  pl.BlockSpec((d, LANES), lambda i: (0, 0)),
                  pl.BlockSpec((d, LANES), lambda i: (0, 0))],
        out_specs=[pl.BlockSpec((tm, d // 2), lambda i: (i, 0)),
                   pl.BlockSpec((tm, LANES), lambda i: (i, 0)),
                   pl.BlockSpec((SUBLANES, LANES), lambda i: (0, 0))],
        out_shape=[jax.ShapeDtypeStruct((m, d // 2), jnp.uint32),
                   jax.ShapeDtypeStruct((m, LANES), F32),
                   jax.ShapeDtypeStruct((SUBLANES, LANES), F32)],
        scratch_shapes=[pltpu.VMEM((SUBLANES, LANES), F32)],
        compiler_params=_cparams("arbitrary"),
        name="router",
    )(x, g, w_hi, w_lo)


def _row_copy(src_hbm, idx, dst_ref, r, sem):
    return pltpu.make_async_copy(src_hbm.at[pl.ds(idx, 1)], dst_ref.at[pl.ds(r, 1)], sem)


def _experts_kernel(rows_per_step, be_ref, nv_ref, nused_ref, off_ref, stok_ref,
                    hn_hbm, wg_ref, wu_ref, wd_ref, o_ref,
                    xbuf, xb_ref, sem):
    t = pl.program_id(0)
    j = pl.program_id(1)
    nused = nused_ref[0]
    tr = xb_ref.shape[0]
    gr = xbuf.shape[0]
    dh = xbuf.shape[1]

    @pl.when(j == 0)
    def _():
        o_ref[...] = jnp.zeros_like(o_ref)

    @pl.when((t == 0) & (j == 0))
    def _():
        off0 = off_ref[0]

        def issue(r, c):
            _row_copy(hn_hbm, stok_ref[off0 + r], xbuf, r, sem).start()
            return c

        lax.fori_loop(0, gr, issue, 0, unroll=8)

    @pl.when((j == 0) & (t <= nused))
    def _():
        pltpu.make_async_copy(hn_hbm.at[pl.ds(0, gr)], xbuf, sem).wait()

    @pl.when((j == 0) & (t < nused))
    def _():
        w = xbuf[0:tr, :]
        xb_ref[:, 0:dh] = pltpu.bitcast(w << 16, F32).astype(BF16)
        xb_ref[:, dh:] = pltpu.bitcast(w & jnp.uint32(0xFFFF0000), F32).astype(BF16)

    def step(n_rows):
        base = off_ref[t + 1] + j * rows_per_step
        for k in range(rows_per_step):
            _row_copy(hn_hbm, stok_ref[base + k], xbuf, j * rows_per_step + k, sem).start()
        rows = pl.ds(0, n_rows)
        x = xb_ref[rows, :]
        a = jnp.dot(x, wg_ref[...].astype(BF16), preferred_element_type=F32)
        b = jnp.dot(x, wu_ref[...].astype(BF16), preferred_element_type=F32)
        h = (jax.nn.silu(a) * b).astype(BF16)
        o_ref[rows, :] += jnp.dot(h, wd_ref[...].astype(BF16), preferred_element_type=F32)

    slab = tr // MOE_ROW_SLABS
    n_slabs = (nv_ref[t] + (slab - 1)) // slab
    for q in range(1, MOE_ROW_SLABS + 1):
        @pl.when((t < nused) & (jnp.maximum(n_slabs, 1) == q))
        def _(q=q):
            step(q * slab)


def _experts(hn_packed, sorted_tok, tile_off, block_e, n_valid, n_used, wg, wu, wd):
    dh = hn_packed.shape[1]
    d = 2 * dh
    f = wg.shape[2]
    tr = MOE_ROW_TILE
    tf = MOE_COL_TILE
    assert tr % (MOE_ROW_SLABS * 2 * SUBLANES) == 0 and f % tf == 0
    n_tiles = block_e.shape[0]
    nj = f // tf
    rows_per_step = -(-tr // nj)
    gr = rows_per_step * nj
    stok = jnp.pad(sorted_tok, (0, gr))
    offs = jnp.pad(jnp.clip(tile_off, 0, sorted_tok.shape[0]), (0, 1))

    def tile(t, nu):
        return jnp.minimum(t, jnp.maximum(nu[0] - 1, 0))

    def col(t, j, nu):
        return jnp.where(t < nu[0], j, nj - 1)

    wspec = lambda shape, imap: pl.BlockSpec(
        shape, lambda t, j, be, nv, nu, off, st: imap(be[tile(t, nu)], col(t, j, nu)))
    return pl.pallas_call(
        functools.partial(_experts_kernel, rows_per_step),
        grid_spec=pltpu.PrefetchScalarGridSpec(
            num_scalar_prefetch=5,
            grid=(n_tiles + 1, nj),
            in_specs=[pl.BlockSpec(memory_space=pl.ANY),
                      wspec((None, d, tf), lambda e, c: (e, 0, c)),
                      wspec((None, d, tf), lambda e, c: (e, 0, c)),
                      wspec((None, tf, d), lambda e, c: (e, c, 0))],
            out_specs=pl.BlockSpec((tr, d), lambda t, j, *_: (t, 0)),
            scratch_shapes=[pltpu.VMEM((gr, dh), jnp.uint32), pltpu.VMEM((tr, d), BF16),
                            pltpu.SemaphoreType.DMA(())]),
        out_shape=jax.ShapeDtypeStruct(((n_tiles + 1) * tr, d), F32),
        compiler_params=_cparams("arbitrary", "arbitrary"),
        name="moe_experts",
    )(block_e, jnp.pad(n_valid, (0, 1)), n_used, offs, stok, hn_packed, wg, wu, wd)


def _combine_kernel(n_first, d0_ref, d1_ref, d0n_ref, d1n_ref, x_ref, meta_ref, g_ref, yb_hbm,
                    oa_ref, ob_ref, y0, y1, sem):
    tm = x_ref.shape[0]
    i = pl.program_id(0)
    slot = lax.rem(i, 2)

    def issue_tile(da_ref, db_ref, s):
        def issue(r, c):
            _row_copy(yb_hbm, da_ref[0, 0, r], y0.at[s], r, sem.at[0, s]).start()
            _row_copy(yb_hbm, db_ref[0, 0, r], y1.at[s], r, sem.at[1, s]).start()
            return c

        lax.fori_loop(0, tm, issue, 0, unroll=8)

    @pl.when(i == 0)
    def _():
        issue_tile(d0_ref, d1_ref, 0)

    @pl.when(i + 1 < pl.num_programs(0))
    def _():
        issue_tile(d0n_ref, d1n_ref, 1 - slot)

    pltpu.make_async_copy(yb_hbm.at[pl.ds(0, tm)], y0.at[slot], sem.at[0, slot]).wait()
    pltpu.make_async_copy(yb_hbm.at[pl.ds(0, tm)], y1.at[slot], sem.at[1, slot]).wait()
    meta = meta_ref[...]
    x = x_ref[...] + (y0[slot] * meta[:, 2:3] + y1[slot] * meta[:, 3:4])

    @pl.when(i < n_first)
    def _():
        oa_ref[...] = _rms_norm_f32(x, g_ref[...])

    @pl.when(i >= n_first)
    def _():
        ob_ref[...] = _rms_norm_f32(x, g_ref[...])


def _combine(x, meta, g, yb, dest0, dest1, m_first):
    m, d = x.shape
    tm = COMBINE_ROW_TILE
    n_tiles = m // tm
    smem_blk = pl.BlockSpec((1, 1, tm), lambda i: (i, 0, 0), memory_space=pltpu.SMEM)
    smem_next = pl.BlockSpec((1, 1, tm), lambda i: (jnp.minimum(i + 1, n_tiles - 1), 0, 0),
                             memory_space=pltpu.SMEM)
    oa_spec, ob_spec = _two_part_specs(m_first // tm, (tm, d), lambda: 0)
    d0 = dest0.reshape(n_tiles, 1, tm)
    d1 = dest1.reshape(n_tiles, 1, tm)
    return pl.pallas_call(
        functools.partial(_combine_kernel, m_first // tm),
        grid=(n_tiles,),
        in_specs=[smem_blk, smem_blk, smem_next, smem_next,
                  pl.BlockSpec((tm, d), lambda i: (i, 0)),
                  pl.BlockSpec((tm, LANES), lambda i: (i, 0)),
                  pl.BlockSpec((1, d), lambda i: (0, 0)),
                  pl.BlockSpec(memory_space=pl.ANY)],
        out_specs=[oa_spec, ob_spec],
        out_shape=[jax.ShapeDtypeStruct((m_first, d), F32),
                   jax.ShapeDtypeStruct((m - m_first, d), F32)],
        scratch_shapes=[pltpu.VMEM((2, tm, d), F32), pltpu.VMEM((2, tm, d), F32),
                        pltpu.SemaphoreType.DMA((2, 2))],
        compiler_params=_cparams("arbitrary"),
        name="moe_combine",
    )(d0, d1, d0, d1, x, meta, g, yb)


def _sample_aux(state, shift, dec_seq):
    b, w1, c = state.shape
    rows = [state[:, w1 + t - shift] if t < shift else jnp.zeros((b, c), state.dtype)
            for t in range(dec_seq)]
    return jnp.stack(rows, axis=1).reshape(b * dec_seq, c)


def _seq_tails(a, col0, ncol, width, batch, seq, dec_batch, dec_seq):
    mp = batch * seq
    tails = [lax.slice(a, ((b + 1) * seq - width, col0), ((b + 1) * seq, col0 + ncol))
             for b in range(batch)]
    smp = lax.slice(a, (mp, col0), (mp + dec_batch * dec_seq, col0 + ncol))
    smp = smp.reshape(dec_batch, dec_seq, ncol)[:, dec_seq - width:]
    return jnp.stack(tails, axis=0), smp


def kernel(x_prompt, x_sample, state_lru_h, state_lru_conv, state_sconv, norm_mix, norm_ffn, norm_final, lru_w_in, lru_conv_w, lru_conv_b, lru_w_rgate, lru_b_rgate, lru_w_igate, lru_b_igate, lru_lambda, lru_w_out, sc_w_in, sc_conv_w, sc_w_out, ffn_w_gate, ffn_w_up, ffn_w_down, moe_w_router, moe_w_gate, moe_w_up, moe_w_down):
    batch, seq, d = x_prompt.shape
    dec_batch, dec_seq, _ = x_sample.shape
    mp, ms = batch * seq, dec_batch * dec_seq
    m = mp + ms
    depth = norm_mix.shape[0]
    n_experts = moe_w_router.shape[2]
    conv_a = lru_conv_w.shape[1]
    conv_b = sc_conv_w.shape[1]
    assert depth == 2 and lru_w_in.shape[0] == 1 and sc_w_in.shape[0] == 1
    assert seq & (seq - 1) == 0 and dec_seq & (dec_seq - 1) == 0
    assert seq % ROW_TILE == 0 and ms == ROW_TILE and ROW_TILE % dec_seq == 0
    assert dec_seq >= conv_a - 1 and dec_seq >= conv_b - 1 and dec_seq == SUBLANES
    assert n_experts <= LANES and m % ROUTER_ROW_TILE == 0 and m % COMBINE_ROW_TILE == 0
    n_prompt_tiles = mp // ROW_TILE

    xp = x_prompt.reshape(mp, d)
    xs_in = x_sample.reshape(ms, d)
    row = lambda v: v.reshape(1, -1)

    d_rnn = lru_w_out.shape[1]
    assert d_rnn % COL_TILE == 0
    proj = _norm_matmul(xp, xs_in, row(norm_mix[0]), lru_w_in[0], n_gelu_cols=d_rnn)
    h0_rows = jnp.pad(state_lru_h[0][:, None, :], ((0, 0), (0, dec_seq - 1), (0, 0)))
    aux = [_sample_aux(state_lru_conv[0], s, dec_seq) for s in (1, 2, 3)]
    aux.append(h0_rows.reshape(ms, d_rnn))
    y, h_all = _rglru_core(proj, aux, lru_conv_w[0], row(lru_conv_b[0]), row(lru_b_rgate[0]),
                           row(lru_b_igate[0]), row(lru_lambda[0]),
                           lru_w_rgate[0].astype(BF16), lru_w_igate[0].astype(BF16),
                           n_prompt_tiles, seq // ROW_TILE, dec_seq)
    x = _matmul_residual(y, lru_w_out[0].astype(BF16), xp, xs_in)

    x = _ffn(x, row(norm_ffn[0]), ffn_w_gate[0], ffn_w_up[0], ffn_w_down[0])

    aux = [_sample_aux(state_sconv[0], s, dec_seq) for s in (1, 2)]
    zb, cv_tile_tails, cv_smp = _sc_mixer(x, row(norm_mix[1]), sc_w_in[0], sc_conv_w[0], aux,
                                          n_prompt_tiles, seq // ROW_TILE, dec_seq)
    x = _matmul_residual(zb, sc_w_out[0].astype(BF16), x, x)

    w_router = jnp.pad(moe_w_router[0], ((0, 0), (0, LANES - n_experts)))
    hn, meta, counts = _router(x, row(norm_ffn[1]), w_router, n_experts)
    tr = MOE_ROW_TILE
    n_tiles = -(-(m * TOP_K) // tr) + n_experts
    counts = counts[0, :n_experts].astype(jnp.int32)
    k_tiles = (counts + tr - 1) // tr
    share = jnp.maximum((counts + k_tiles - 1) // jnp.maximum(k_tiles, 1), 1)
    padded = k_tiles * tr
    pad_end = jnp.cumsum(padded)
    pad_start = pad_end - padded
    e0 = meta[:, 0].astype(jnp.int32)
    e1 = meta[:, 1].astype(jnp.int32)

    def sorted_row(e, rank):
        tile_in_expert = rank // share[e]
        return pad_start[e] + tile_in_expert * tr + (rank - tile_in_expert * share[e])

    dest0 = sorted_row(e0, meta[:, 4].astype(jnp.int32))
    dest1 = sorted_row(e1, meta[:, 5].astype(jnp.int32))
    order = jnp.argsort(jnp.stack([e0, e1], axis=1).reshape(-1), stable=True)
    sorted_tok = (order // TOP_K).astype(jnp.int32)
    start = jnp.cumsum(counts) - counts
    block_start = jnp.arange(n_tiles, dtype=jnp.int32) * tr
    block_e = jnp.sum((pad_end[None, :] <= block_start[:, None]).astype(jnp.int32), axis=1)
    block_e = jnp.minimum(block_e, n_experts - 1)
    in_expert = (block_start - pad_start[block_e]) // tr * share[block_e]
    n_valid = jnp.clip(counts[block_e] - in_expert, 0, share[block_e]).astype(jnp.int32)
    n_used = (pad_end[-1:] // tr).astype(jnp.int32)
    yb = _experts(hn, sorted_tok, (start[block_e] + in_expert).astype(jnp.int32), block_e,
                  n_valid, n_used, moe_w_gate[0], moe_w_up[0], moe_w_down[0])
    y_prompt, y_sample = _combine(x, meta, row(norm_final), yb, dest0, dest1, mp)

    tails = functools.partial(_seq_tails, batch=batch, seq=seq, dec_batch=dec_batch, dec_seq=dec_seq)
    h_p, h_s = tails(h_all, 0, d_rnn, 1)
    u_p, u_s = tails(proj, d_rnn, d_rnn, conv_a - 1)
    tiles_per_seq = seq // ROW_TILE
    cv_p = cv_tile_tails.reshape(m // ROW_TILE, SUBLANES, -1)[
        tiles_per_seq - 1:n_prompt_tiles:tiles_per_seq, SUBLANES - (conv_b - 1):]
    cv_s = cv_smp.reshape(dec_batch, dec_seq, -1)[:, dec_seq - (conv_b - 1):]
    return (y_prompt.reshape(batch, seq, d), y_sample.reshape(dec_batch, dec_seq, d),
            h_p[:, 0][None], u_p[None], cv_p[None], h_s[:, 0][None], u_s[None], cv_s[None])
```

```python
import functools

import jax
import jax.numpy as jnp
from jax import lax
from jax.experimental import pallas as pl
from jax.experimental.pallas import tpu as pltpu

F32 = jnp.float32
BF16 = jnp.bfloat16

EPS = 1e-6
RG_C = 8.0
TOP_K = 2

LANES = 128
SUBLANES = 8
ROW_TILE = 1024
OUT_ROW_TILE = 512
ROUTER_ROW_TILE = 512
COL_TILE = 512
W_COL_TILE = 256
MOE_ROW_TILE = 768
MOE_ROW_SLABS = 4
MOE_COL_TILE = 512
COMBINE_ROW_TILE = 256
SCAN_CHUNK = 64
VMEM_LIMIT = 60 * 1024 * 1024


def _cparams(*sem):
    return pltpu.CompilerParams(dimension_semantics=sem, vmem_limit_bytes=VMEM_LIMIT)


def _rms_norm_f32(x, g):
    y = x * lax.rsqrt(jnp.mean(x * x, axis=-1, keepdims=True) + EPS)
    return y * g


def _two_part_specs(n_first, block, col_of, second_mode=None):
    first = pl.BlockSpec(block, lambda i, *a: (jnp.minimum(i, n_first - 1), col_of(*a)))
    second = pl.BlockSpec(block, lambda i, *a: (jnp.maximum(i - n_first, 0), col_of(*a)),
                          pipeline_mode=second_mode)
    return first, second


def _norm_matmul_kernel(n_first, n_gelu, xa_ref, xb_ref, g_ref, w_ref, o_ref, hn_ref):
    i = pl.program_id(0)
    j = pl.program_id(1)

    def project():
        return jnp.dot(hn_ref[...], w_ref[...].astype(BF16), preferred_element_type=F32)

    @pl.when((j == 0) & (i < n_first))
    def _():
        hn_ref[...] = _rms_norm_f32(xa_ref[...], g_ref[...]).astype(BF16)

    @pl.when((j == 0) & (i >= n_first))
    def _():
        hn_ref[...] = _rms_norm_f32(xb_ref[...], g_ref[...]).astype(BF16)

    @pl.when(j < n_gelu)
    def _():
        o_ref[...] = jax.nn.gelu(project())

    @pl.when(j >= n_gelu)
    def _():
        o_ref[...] = project()


def _norm_matmul(xa, xb, g, w, n_gelu_cols):
    d = xa.shape[1]
    m = xa.shape[0] + xb.shape[0]
    n = w.shape[1]
    assert xb.shape[0] == ROW_TILE
    xa_spec, xb_spec = _two_part_specs(xa.shape[0] // ROW_TILE, (ROW_TILE, d), lambda j: 0,
                                       second_mode=pl.Buffered(1))
    return pl.pallas_call(
        functools.partial(_norm_matmul_kernel, xa.shape[0] // ROW_TILE, n_gelu_cols // COL_TILE),
        grid=(m // ROW_TILE, n // COL_TILE),
        in_specs=[xa_spec, xb_spec,
                  pl.BlockSpec((1, d), lambda i, j: (0, 0)),
                  pl.BlockSpec((d, COL_TILE), lambda i, j: (0, j))],
        out_specs=pl.BlockSpec((ROW_TILE, COL_TILE), lambda i, j: (i, j)),
        out_shape=jax.ShapeDtypeStruct((m, n), F32),
        scratch_shapes=[pltpu.VMEM((ROW_TILE, d), BF16)],
        compiler_params=_cparams("arbitrary", "arbitrary"),
        name="norm_matmul",
    )(xa, xb, g, w)


def _sc_mixer_kernel(n_prompt_tiles, tiles_per_seq, seq_s,
                     x_ref, g_ref, wb_ref, wc_ref, wv_ref, cw_ref, hist_ref,
                     zb_ref, tail_ref, cvs_ref, hn_ref, carry_ref):
    i = pl.program_id(0)
    j = pl.program_id(1)
    t_rows = x_ref.shape[0]
    is_sample = i >= n_prompt_tiles
    starts_seq = is_sample | (lax.rem(i, tiles_per_seq) == 0)

    @pl.when(j == 0)
    def _():
        hn_ref[...] = _rms_norm_f32(x_ref[...], g_ref[...]).astype(BF16)

    @pl.when((i == 0) & (j == 0))
    def _():
        carry_ref[...] = jnp.zeros_like(carry_ref)

    def mixer(masked):
        hn = hn_ref[...]
        cw = cw_ref[...]
        c = jnp.dot(hn, wc_ref[...].astype(BF16), preferred_element_type=F32)
        v = jnp.dot(hn, wv_ref[...].astype(BF16), preferred_element_type=F32)
        cv = c * v
        prev = jnp.where(starts_seq, 0.0, carry_ref[j])
        ext = jnp.concatenate([prev, cv], axis=0)
        w1 = cw.shape[0] - 1
        if masked:
            t = _pos_in_seq(0, t_rows, seq_s)
            hist = hist_ref[...]

        def tap(s):
            shifted = pltpu.roll(ext, s, axis=0)[SUBLANES:]
            if not masked:
                return shifted
            return jnp.where(t >= s, shifted, _history_rows(hist, w1 - s))

        z = tap(2) * cw[0:1] + tap(1) * cw[1:2]
        z = z + cv * cw[2:3]
        b = jnp.dot(hn, wb_ref[...].astype(BF16), preferred_element_type=F32)
        zb_ref[...] = (b * z).astype(BF16)
        last = cv[t_rows - SUBLANES:, :]
        carry_ref[j] = last
        tail_ref[...] = last
        return cv

    @pl.when(is_sample)
    def _():
        cvs_ref[...] = mixer(True)

    @pl.when(jnp.logical_not(is_sample))
    def _():
        mixer(False)


def _sc_mixer(x, g, w, conv_w, hist, n_prompt_tiles, tiles_per_seq, seq_s):
    m, d = x.shape
    dc = w.shape[1] // 3
    tn = W_COL_TILE
    nj = dc // tn
    n_tiles = m // ROW_TILE
    ms = hist.shape[0]
    assert ms == ROW_TILE
    wspec = lambda off: pl.BlockSpec((d, tn), lambda i, j: (0, off * nj + j))
    sample_col = lambda i, j: (0, jnp.where(i >= n_prompt_tiles, j, 0))
    return pl.pallas_call(
        functools.partial(_sc_mixer_kernel, n_prompt_tiles, tiles_per_seq, seq_s),
        grid=(n_tiles, nj),
        in_specs=[pl.BlockSpec((ROW_TILE, d), lambda i, j: (i, 0)),
                  pl.BlockSpec((1, d), lambda i, j: (0, 0)),
                  wspec(0), wspec(1), wspec(2),
                  pl.BlockSpec((conv_w.shape[0], tn), lambda i, j: (0, j)),
                  pl.BlockSpec((ROW_TILE, tn), sample_col)],
        out_specs=[pl.BlockSpec((ROW_TILE, tn), lambda i, j: (i, j)),
                   pl.BlockSpec((SUBLANES, tn), lambda i, j: (i, j)),
                   pl.BlockSpec((ROW_TILE, tn), sample_col)],
        out_shape=[jax.ShapeDtypeStruct((m, dc), BF16),
                   jax.ShapeDtypeStruct((n_tiles * SUBLANES, dc), F32),
                   jax.ShapeDtypeStruct((ms, dc), F32)],
        scratch_shapes=[pltpu.VMEM((ROW_TILE, d), BF16),
                        pltpu.VMEM((nj, SUBLANES, tn), F32)],
        compiler_params=_cparams("arbitrary", "arbitrary"),
        name="sc_mixer",
    )(x, g, w, w, w, conv_w, hist)


def _matmul_residual_kernel(n_first, y_ref, w_ref, ra_ref, rb_ref, o_ref):
    o_ref[...] = jnp.dot(y_ref[...], w_ref[...], preferred_element_type=F32)
    i = pl.program_id(0)

    @pl.when(i < n_first)
    def _():
        o_ref[...] += ra_ref[...]

    @pl.when(i >= n_first)
    def _():
        o_ref[...] += rb_ref[...]


def _matmul_residual(y, w, res_a, res_b):
    m, k = y.shape
    n = w.shape[1]
    tm = OUT_ROW_TILE
    ra_spec, rb_spec = _two_part_specs(res_a.shape[0] // tm, (tm, n), lambda: 0)
    return pl.pallas_call(
        functools.partial(_matmul_residual_kernel, res_a.shape[0] // tm),
        grid=(m // tm,),
        in_specs=[pl.BlockSpec((tm, k), lambda i: (i, 0)),
                  pl.BlockSpec((k, n), lambda i: (0, 0)),
                  ra_spec, rb_spec],
        out_specs=pl.BlockSpec((tm, n), lambda i: (i, 0)),
        out_shape=jax.ShapeDtypeStruct((m, n), F32),
        compiler_params=_cparams("arbitrary"),
        name="matmul_residual",
    )(y, w, res_a, res_b)


def _pos_in_seq(r0, rows, seq_len):
    return (r0 + lax.broadcasted_iota(jnp.int32, (rows, 1), 0)) & (seq_len - 1)


def _sigmoid(x):
    return 0.5 * jnp.tanh(0.5 * x) + 0.5


def _run_chunks(is_sample, n_chunks, body):
    @pl.when(is_sample)
    def _():
        lax.fori_loop(0, n_chunks, lambda k, c: body(True, k) or c, 0)

    @pl.when(jnp.logical_not(is_sample))
    def _():
        lax.fori_loop(0, n_chunks, lambda k, c: body(False, k) or c, 0)


def _history_rows(hist_chunk, back):
    n = hist_chunk.shape[0]
    return hist_chunk if back == 0 else pltpu.roll(hist_chunk, n - back, axis=0)


def _rglru_kernel(n_prompt_tiles, tiles_per_seq, seq_s,
                  gate_ref, u_ref, hist_ref,
                  cw_ref, cb_ref, br_ref, bi_ref, lam_ref, wr_ref, wi_ref,
                  y_ref, hl_ref, ubuf, uc_ref, rp_ref, ip_ref, hc_ref):
    i = pl.program_id(1)
    cblk = u_ref.shape[1]
    is_sample = i >= n_prompt_tiles
    w1 = cw_ref.shape[0] - 1

    @pl.when(is_sample | (lax.rem(i, tiles_per_seq) == 0))
    def _():
        ubuf[0:SUBLANES, :] = jnp.zeros((SUBLANES, cblk), F32)
        hc_ref[...] = jnp.zeros_like(hc_ref)

    ubuf[SUBLANES:, :] = u_ref[...]
    cw = cw_ref[...]
    cb = cb_ref[...]
    rc = SCAN_CHUNK

    def conv_chunk(masked, k):
        r0 = pl.multiple_of(k * rc, rc)
        rows = pl.ds(r0, rc)
        ext = ubuf[pl.ds(r0, rc + SUBLANES), :]
        if masked:
            t = _pos_in_seq(r0, rc, seq_s)
            hist = hist_ref[rows, :]

        def tap(s):
            shifted = pltpu.roll(ext, s, axis=0)[SUBLANES:]
            if not masked:
                return shifted
            return jnp.where(t >= s, shifted, _history_rows(hist, w1 - s))

        conv = tap(3) * cw[0:1] + tap(2) * cw[1:2]
        conv = conv + tap(1) * cw[2:3]
        conv = conv + ext[SUBLANES:] * cw[3:4]
        uc_ref[rows, :] = conv + cb

    _run_chunks(is_sample, ROW_TILE // rc, conv_chunk)

    ub = uc_ref[...].astype(BF16)
    rp_ref[...] = jnp.dot(ub, wr_ref[...], preferred_element_type=F32) + br_ref[...]
    ip_ref[...] = jnp.dot(ub, wi_ref[...], preferred_element_type=F32) + bi_ref[...]

    sp = jax.nn.softplus(-lam_ref[...])
    sub = lax.broadcasted_iota(jnp.int32, (1, SUBLANES, 1), 1)

    def scan_chunk(masked, k):
        r0 = pl.multiple_of(k * rc, rc)
        rows = pl.ds(r0, rc)
        uc = uc_ref[rows, :]
        r = _sigmoid(rp_ref[rows, :])
        ig = _sigmoid(ip_ref[rows, :])
        log_a = (-RG_C * r) * sp
        a = jnp.exp(log_a)
        mult = jnp.sqrt(-jnp.tanh(log_a) * (a * a + 1.0))
        b = mult * ig * uc
        if masked:
            first = _pos_in_seq(r0, rc, seq_s) == 0
            b = jnp.where(first, b + a * _history_rows(hist_ref[rows, :], w1), b)
            a = jnp.where(first, 0.0, a)

        a3 = a.reshape(rc // SUBLANES, SUBLANES, cblk)
        b3 = b.reshape(rc // SUBLANES, SUBLANES, cblk)
        for s in (1, 2, 4):
            a_s = pltpu.roll(a3, s, axis=1)
            b_s = pltpu.roll(b3, s, axis=1)
            m = sub >= s
            b3 = jnp.where(m, a3 * b_s + b3, b3)
            a3 = jnp.where(m, a3 * a_s, a3)

        hc = hc_ref[...]
        hs, lasts = [], []
        for g in range(rc // SUBLANES):
            hg = a3[g] * hc + b3[g]
            lasts.append(hg[SUBLANES - 1:SUBLANES])
            hc = jnp.broadcast_to(lasts[-1], (SUBLANES, cblk))
            hs.append(hg)
        hc_ref[...] = hc
        h = jnp.concatenate(hs, axis=0)
        groups = rc // SUBLANES
        hl_ref[pl.ds(pl.multiple_of(k * groups, groups), groups), :] = jnp.concatenate(lasts, axis=0)
        y_ref[rows, :] = (h * gate_ref[rows, :]).astype(BF16)

    _run_chunks(is_sample, ROW_TILE // rc, scan_chunk)
    ubuf[0:SUBLANES, :] = ubuf[ROW_TILE:ROW_TILE + SUBLANES, :]


def _rglru_core(proj, hist, conv_w, conv_b, b_r, b_i, lam, w_r, w_i, n_prompt_tiles,
                tiles_per_seq, seq_s):
    m = proj.shape[0]
    assert SCAN_CHUNK % (SUBLANES * SUBLANES) == 0
    nb, cblk = w_r.shape[0], w_r.shape[1]
    c = nb * cblk
    n_tiles = m // ROW_TILE
    row_blk = lambda off: pl.BlockSpec((ROW_TILE, cblk), lambda cb, i: (i, off + cb))
    aux_blk = pl.BlockSpec((ROW_TILE, cblk), lambda cb, i: (0, cb))
    vec_blk = lambda rows: pl.BlockSpec((rows, cblk), lambda cb, i: (0, cb))
    w_blk = pl.BlockSpec((None, cblk, cblk), lambda cb, i: (cb, 0, 0))
    out_blk = pl.BlockSpec((ROW_TILE, cblk), lambda cb, i: (i, cb))
    return pl.pallas_call(
        functools.partial(_rglru_kernel, n_prompt_tiles, tiles_per_seq, seq_s),
        grid=(nb, n_tiles),
        in_specs=[row_blk(0), row_blk(nb), aux_blk,
                  vec_blk(conv_w.shape[0]), vec_blk(1), vec_blk(1), vec_blk(1), vec_blk(1),
                  w_blk, w_blk],
        out_specs=[out_blk,
                   pl.BlockSpec((ROW_TILE // SUBLANES, cblk), lambda cb, i: (i, cb))],
        out_shape=[jax.ShapeDtypeStruct((m, c), BF16),
                   jax.ShapeDtypeStruct((m // SUBLANES, c), F32)],
        scratch_shapes=[pltpu.VMEM((ROW_TILE + SUBLANES, cblk), F32),
                        pltpu.VMEM((ROW_TILE, cblk), F32),
                        pltpu.VMEM((ROW_TILE, cblk), F32),
                        pltpu.VMEM((ROW_TILE, cblk), F32),
                        pltpu.VMEM((SUBLANES, cblk), F32)],
        compiler_params=_cparams("arbitrary", "arbitrary"),
        name="rglru_core",
    )(proj, proj, hist, conv_w, conv_b, b_r, b_i, lam, w_r, w_i)


def _ffn_kernel(x_ref, g_ref, wg_ref, wu_ref, wd_ref, o_ref, hn_ref):
    @pl.when(pl.program_id(1) == 0)
    def _():
        x = x_ref[...]
        hn_ref[...] = _rms_norm_f32(x, g_ref[...]).astype(BF16)
        o_ref[...] = x

    hn = hn_ref[...]
    a = jnp.dot(hn, wg_ref[...].astype(BF16), preferred_element_type=F32)
    b = jnp.dot(hn, wu_ref[...].astype(BF16), preferred_element_type=F32)
    h = (jax.nn.silu(a) * b).astype(BF16)
    o_ref[...] += jnp.dot(h, wd_ref[...].astype(BF16), preferred_element_type=F32)


def _ffn(x, g, wg, wu, wd):
    m, d = x.shape
    f = wg.shape[1]
    tm = ROW_TILE
    tf = W_COL_TILE
    return pl.pallas_call(
        _ffn_kernel,
        grid=(m // tm, f // tf),
        in_specs=[pl.BlockSpec((tm, d), lambda i, j: (i, 0)),
                  pl.BlockSpec((1, d), lambda i, j: (0, 0)),
                  pl.BlockSpec((d, tf), lambda i, j: (0, j)),
                  pl.BlockSpec((d, tf), lambda i, j: (0, j)),
                  pl.BlockSpec((tf, d), lambda i, j: (j, 0))],
        out_specs=pl.BlockSpec((tm, d), lambda i, j: (i, 0)),
        out_shape=jax.ShapeDtypeStruct((m, d), F32),
        scratch_shapes=[pltpu.VMEM((tm, d), BF16)],
        compiler_params=_cparams("arbitrary", "arbitrary"),
        name="ffn",
    )(x, g, wg, wu, wd)


def _router_kernel(n_experts, x_ref, g_ref, whi_ref, wlo_ref, hn_ref, meta_ref, cnt_ref, run_ref):
    @pl.when(pl.program_id(0) == 0)
    def _():
        run_ref[...] = jnp.zeros_like(run_ref)

    tm = x_ref.shape[0]
    hn = _rms_norm_f32(x_ref[...], g_ref[...])
    hn_hi = hn.astype(BF16)
    hn_hi32 = hn_hi.astype(F32)
    bits = pltpu.bitcast(hn_hi32, jnp.uint32)
    half = hn.shape[1] // 2
    hn_ref[...] = (bits[:, half:] & jnp.uint32(0xFFFF0000)) | (bits[:, :half] >> 16)
    hn_lo = (hn - hn_hi32).astype(BF16)
    logits = (jnp.dot(hn_hi, whi_ref[...], preferred_element_type=F32)
              + (jnp.dot(hn_lo, whi_ref[...], preferred_element_type=F32)
                 + jnp.dot(hn_hi, wlo_ref[...], preferred_element_type=F32)))
    lane = lax.broadcasted_iota(jnp.int32, (tm, LANES), 1).astype(F32)
    neg = jnp.float32(-jnp.inf)
    lg = jnp.where(lane < n_experts, logits, neg)
    m1 = jnp.max(lg, axis=1, keepdims=True)
    i1 = jnp.min(jnp.where(lg == m1, lane, float(LANES)), axis=1, keepdims=True)
    lg2 = jnp.where(lane == i1, neg, lg)
    m2 = jnp.max(lg2, axis=1, keepdims=True)
    i2 = jnp.min(jnp.where(lg2 == m2, lane, float(LANES)), axis=1, keepdims=True)
    ex = jnp.exp(m2 - m1)
    g1 = 1.0 / (1.0 + ex)
    g2 = ex / (1.0 + ex)

    sel1 = lane == i1
    sel2 = lane == i2
    onehot = jnp.where(sel1 | sel2, 1.0, 0.0)
    rr = lax.broadcasted_iota(jnp.int32, (tm, tm), 0)
    cc = lax.broadcasted_iota(jnp.int32, (tm, tm), 1)
    tri = jnp.where(rr > cc, 1.0, 0.0).astype(BF16)
    before = jnp.dot(tri, onehot.astype(BF16), preferred_element_type=F32) + run_ref[0:1, :]
    rank1 = jnp.sum(jnp.where(sel1, before, 0.0), axis=1, keepdims=True)
    rank2 = jnp.sum(jnp.where(sel2, before, 0.0), axis=1, keepdims=True)
    run = run_ref[0:1, :] + jnp.sum(onehot, axis=0, keepdims=True)
    run_ref[...] = jnp.broadcast_to(run, run_ref.shape)
    cnt_ref[...] = jnp.broadcast_to(run, cnt_ref.shape)

    meta = jnp.where(lane == 0, i1, 0.0)
    meta = jnp.where(lane == 1, i2, meta)
    meta = jnp.where(lane == 2, g1, meta)
    meta = jnp.where(lane == 3, g2, meta)
    meta = jnp.where(lane == 4, rank1, meta)
    meta = jnp.where(lane == 5, rank2, meta)
    meta_ref[...] = meta


def _router(x, g, w_router_padded, n_experts):
    m, d = x.shape
    tm = ROUTER_ROW_TILE
    w_hi = w_router_padded.astype(BF16)
    w_lo = (w_router_padded - w_hi.astype(F32)).astype(BF16)
    return pl.pallas_call(
        functools.partial(_router_kernel, n_experts),
        grid=(m // tm,),
        in_specs=[pl.BlockSpec((tm, d), lambda i: (i, 0)),
                  pl.BlockSpec((1, d), lambda i: (0, 0)),
                  pl.BlockSpec((d, LANES), lambda i: (0, 0)),
                  pl.BlockSpec((d, LANES), lambda i: (0, 0))],
        out_specs=[pl.BlockSpec((tm, d // 2), lambda i: (i, 0)),
                   pl.BlockSpec((tm, LANES), lambda i: (i, 0)),
                   pl.BlockSpec((SUBLANES, LANES), lambda i: (0, 0))],
        out_shape=[jax.ShapeDtypeStruct((m, d // 2), jnp.uint32),
                   jax.ShapeDtypeStruct((m, LANES), F32),
                   jax.ShapeDtypeStruct((SUBLANES, LANES), F32)],
        scratch_shapes=[pltpu.VMEM((SUBLANES, LANES), F32)],
        compiler_params=_cparams("arbitrary"),
        name="router",
    )(x, g, w_hi, w_lo)


def _row_copy(src_hbm, idx, dst_ref, r, sem):
    return pltpu.make_async_copy(src_hbm.at[pl.ds(idx, 1)], dst_ref.at[pl.ds(r, 1)], sem)


def _experts_kernel(rows_per_step, be_ref, nv_ref, nused_ref, off_ref, stok_ref,
                    hn_hbm, wg_ref, wu_ref, wd_ref, o_ref,
                    xbuf, xb_ref, sem):
    t = pl.program_id(0)
    j = pl.program_id(1)
    nused = nused_ref[0]
    tr = xb_ref.shape[0]
    gr = xbuf.shape[0]
    dh = xbuf.shape[1]

    @pl.when(j == 0)
    def _():
        o_ref[...] = jnp.zeros_like(o_ref)

    @pl.when((t == 0) & (j == 0))
    def _():
        off0 = off_ref[0]

        def issue(r, c):
            _row_copy(hn_hbm, stok_ref[off0 + r], xbuf, r, sem).start()
            return c

        lax.fori_loop(0, gr, issue, 0, unroll=8)

    @pl.when((j == 0) & (t <= nused))
    def _():
        pltpu.make_async_copy(hn_hbm.at[pl.ds(0, gr)], xbuf, sem).wait()

    @pl.when((j == 0) & (t < nused))
    def _():
        w = xbuf[0:tr, :]
        xb_ref[:, 0:dh] = pltpu.bitcast(w << 16, F32).astype(BF16)
        xb_ref[:, dh:] = pltpu.bitcast(w & jnp.uint32(0xFFFF0000), F32).astype(BF16)

    def step(n_rows):
        base = off_ref[t + 1] + j * rows_per_step
        for k in range(rows_per_step):
            _row_copy(hn_hbm, stok_ref[base + k], xbuf, j * rows_per_step + k, sem).start()
        rows = pl.ds(0, n_rows)
        x = xb_ref[rows, :]
        a = jnp.dot(x, wg_ref[...].astype(BF16), preferred_element_type=F32)
        b = jnp.dot(x, wu_ref[...].astype(BF16), preferred_element_type=F32)
        h = (jax.nn.silu(a) * b).astype(BF16)
        o_ref[rows, :] += jnp.dot(h, wd_ref[...].astype(BF16), preferred_element_type=F32)

    slab = tr // MOE_ROW_SLABS
    n_slabs = (nv_ref[t] + (slab - 1)) // slab
    for q in range(1, MOE_ROW_SLABS + 1):
        @pl.when((t < nused) & (jnp.maximum(n_slabs, 1) == q))
        def _(q=q):
            step(q * slab)


def _experts(hn_packed, sorted_tok, tile_off, block_e, n_valid, n_used, wg, wu, wd):
    dh = hn_packed.shape[1]
    d = 2 * dh
    f = wg.shape[2]
    tr = MOE_ROW_TILE
    tf = MOE_COL_TILE
    assert tr % (MOE_ROW_SLABS * 2 * SUBLANES) == 0 and f % tf == 0
    n_tiles = block_e.shape[0]
    nj = f // tf
    rows_per_step = -(-tr // nj)
    gr = rows_per_step * nj
    stok = jnp.pad(sorted_tok, (0, gr))
    offs = jnp.pad(jnp.clip(tile_off, 0, sorted_tok.shape[0]), (0, 1))

    def tile(t, nu):
        return jnp.minimum(t, jnp.maximum(nu[0] - 1, 0))

    def col(t, j, nu):
        return jnp.where(t < nu[0], j, nj - 1)

    wspec = lambda shape, imap: pl.BlockSpec(
        shape, lambda t, j, be, nv, nu, off, st: imap(be[tile(t, nu)], col(t, j, nu)))
    return pl.pallas_call(
        functools.partial(_experts_kernel, rows_per_step),
        grid_spec=pltpu.PrefetchScalarGridSpec(
            num_scalar_prefetch=5,
            grid=(n_tiles + 1, nj),
            in_specs=[pl.BlockSpec(memory_space=pl.ANY),
                      wspec((None, d, tf), lambda e, c: (e, 0, c)),
                      wspec((None, d, tf), lambda e, c: (e, 0, c)),
                      wspec((None, tf, d), lambda e, c: (e, c, 0))],
            out_specs=pl.BlockSpec((tr, d), lambda t, j, *_: (t, 0)),
            scratch_shapes=[pltpu.VMEM((gr, dh), jnp.uint32), pltpu.VMEM((tr, d), BF16),
                            pltpu.SemaphoreType.DMA(())]),
        out_shape=jax.ShapeDtypeStruct(((n_tiles + 1) * tr, d), F32),
        compiler_params=_cparams("arbitrary", "arbitrary"),
        name="moe_experts",
    )(block_e, jnp.pad(n_valid, (0, 1)), n_used, offs, stok, hn_packed, wg, wu, wd)


def _combine_kernel(n_first, d0_ref, d1_ref, d0n_ref, d1n_ref, x_ref, meta_ref, g_ref, yb_hbm,
                    oa_ref, ob_ref, y0, y1, sem):
    tm = x_ref.shape[0]
    i = pl.program_id(0)
    slot = lax.rem(i, 2)

    def issue_tile(da_ref, db_ref, s):
        def issue(r, c):
            _row_copy(yb_hbm, da_ref[0, 0, r], y0.at[s], r, sem.at[0, s]).start()
            _row_copy(yb_hbm, db_ref[0, 0, r], y1.at[s], r, sem.at[1, s]).start()
            return c

        lax.fori_loop(0, tm, issue, 0, unroll=8)

    @pl.when(i == 0)
    def _():
        issue_tile(d0_ref, d1_ref, 0)

    @pl.when(i + 1 < pl.num_programs(0))
    def _():
        issue_tile(d0n_ref, d1n_ref, 1 - slot)

    pltpu.make_async_copy(yb_hbm.at[pl.ds(0, tm)], y0.at[slot], sem.at[0, slot]).wait()
    pltpu.make_async_copy(yb_hbm.at[pl.ds(0, tm)], y1.at[slot], sem.at[1, slot]).wait()
    meta = meta_ref[...]
    x = x_ref[...] + (y0[slot] * meta[:, 2:3] + y1[slot] * meta[:, 3:4])

    @pl.when(i < n_first)
    def _():
        oa_ref[...] = _rms_norm_f32(x, g_ref[...])

    @pl.when(i >= n_first)
    def _():
        ob_ref[...] = _rms_norm_f32(x, g_ref[...])


def _combine(x, meta, g, yb, dest0, dest1, m_first):
    m, d = x.shape
    tm = COMBINE_ROW_TILE
    n_tiles = m // tm
    smem_blk = pl.BlockSpec((1, 1, tm), lambda i: (i, 0, 0), memory_space=pltpu.SMEM)
    smem_next = pl.BlockSpec((1, 1, tm), lambda i: (jnp.minimum(i + 1, n_tiles - 1), 0, 0),
                             memory_space=pltpu.SMEM)
    oa_spec, ob_spec = _two_part_specs(m_first // tm, (tm, d), lambda: 0)
    d0 = dest0.reshape(n_tiles, 1, tm)
    d1 = dest1.reshape(n_tiles, 1, tm)
    return pl.pallas_call(
        functools.partial(_combine_kernel, m_first // tm),
        grid=(n_tiles,),
        in_specs=[smem_blk, smem_blk, smem_next, smem_next,
                  pl.BlockSpec((tm, d), lambda i: (i, 0)),
                  pl.BlockSpec((tm, LANES), lambda i: (i, 0)),
                  pl.BlockSpec((1, d), lambda i: (0, 0)),
                  pl.BlockSpec(memory_space=pl.ANY)],
        out_specs=[oa_spec, ob_spec],
        out_shape=[jax.ShapeDtypeStruct((m_first, d), F32),
                   jax.ShapeDtypeStruct((m - m_first, d), F32)],
        scratch_shapes=[pltpu.VMEM((2, tm, d), F32), pltpu.VMEM((2, tm, d), F32),
                        pltpu.SemaphoreType.DMA((2, 2))],
        compiler_params=_cparams("arbitrary"),
        name="moe_combine",
    )(d0, d1, d0, d1, x, meta, g, yb)


def _sample_history(dec_seq, *states):
    hist = jnp.concatenate(states, axis=1)
    b, r, c = hist.shape
    assert r <= dec_seq
    return jnp.pad(hist, ((0, 0), (0, dec_seq - r), (0, 0))).reshape(b * dec_seq, c)


def _seq_tails(a, col0, ncol, width, batch, seq, dec_batch, dec_seq):
    mp = batch * seq
    tails = [lax.slice(a, ((b + 1) * seq - width, col0), ((b + 1) * seq, col0 + ncol))
             for b in range(batch)]
    smp = lax.slice(a, (mp, col0), (mp + dec_batch * dec_seq, col0 + ncol))
    smp = smp.reshape(dec_batch, dec_seq, ncol)[:, dec_seq - width:]
    return jnp.stack(tails, axis=0), smp


def kernel(x_prompt, x_sample, state_lru_h, state_lru_conv, state_sconv, norm_mix, norm_ffn, norm_final, lru_w_in, lru_conv_w, lru_conv_b, lru_w_rgate, lru_b_rgate, lru_w_igate, lru_b_igate, lru_lambda, lru_w_out, sc_w_in, sc_conv_w, sc_w_out, ffn_w_gate, ffn_w_up, ffn_w_down, moe_w_router, moe_w_gate, moe_w_up, moe_w_down):
    batch, seq, d = x_prompt.shape
    dec_batch, dec_seq, _ = x_sample.shape
    mp, ms = batch * seq, dec_batch * dec_seq
    m = mp + ms
    depth = norm_mix.shape[0]
    n_experts = moe_w_router.shape[2]
    conv_a = lru_conv_w.shape[1]
    conv_b = sc_conv_w.shape[1]
    assert depth == 2 and lru_w_in.shape[0] == 1 and sc_w_in.shape[0] == 1
    assert seq & (seq - 1) == 0 and dec_seq & (dec_seq - 1) == 0
    assert seq % ROW_TILE == 0 and ms == ROW_TILE and ROW_TILE % dec_seq == 0
    assert dec_seq >= conv_a - 1 and dec_seq >= conv_b - 1 and dec_seq == SUBLANES
    assert n_experts <= LANES and m % ROUTER_ROW_TILE == 0 and m % COMBINE_ROW_TILE == 0
    n_prompt_tiles = mp // ROW_TILE

    xp = x_prompt.reshape(mp, d)
    xs_in = x_sample.reshape(ms, d)
    row = lambda v: v.reshape(1, -1)

    d_rnn = lru_w_out.shape[1]
    assert d_rnn % COL_TILE == 0
    proj = _norm_matmul(xp, xs_in, row(norm_mix[0]), lru_w_in[0], n_gelu_cols=d_rnn)
    hist = _sample_history(dec_seq, state_lru_conv[0], state_lru_h[0][:, None, :])
    y, h_last = _rglru_core(proj, hist, lru_conv_w[0], row(lru_conv_b[0]), row(lru_b_rgate[0]),
                           row(lru_b_igate[0]), row(lru_lambda[0]),
                           lru_w_rgate[0].astype(BF16), lru_w_igate[0].astype(BF16),
                           n_prompt_tiles, seq // ROW_TILE, dec_seq)
    x = _matmul_residual(y, lru_w_out[0].astype(BF16), xp, xs_in)

    x = _ffn(x, row(norm_ffn[0]), ffn_w_gate[0], ffn_w_up[0], ffn_w_down[0])

    hist = _sample_history(dec_seq, state_sconv[0])
    zb, cv_tile_tails, cv_smp = _sc_mixer(x, row(norm_mix[1]), sc_w_in[0], sc_conv_w[0], hist,
                                          n_prompt_tiles, seq // ROW_TILE, dec_seq)
    x = _matmul_residual(zb, sc_w_out[0].astype(BF16), x, x)

    w_router = jnp.pad(moe_w_router[0], ((0, 0), (0, LANES - n_experts)))
    hn, meta, counts = _router(x, row(norm_ffn[1]), w_router, n_experts)
    tr = MOE_ROW_TILE
    n_tiles = -(-(m * TOP_K) // tr) + n_experts
    counts = counts[0, :n_experts].astype(jnp.int32)
    k_tiles = (counts + tr - 1) // tr
    share = jnp.maximum((counts + k_tiles - 1) // jnp.maximum(k_tiles, 1), 1)
    padded = k_tiles * tr
    pad_end = jnp.cumsum(padded)
    pad_start = pad_end - padded
    e0 = meta[:, 0].astype(jnp.int32)
    e1 = meta[:, 1].astype(jnp.int32)

    def sorted_row(e, rank):
        tile_in_expert = rank // share[e]
        return pad_start[e] + tile_in_expert * tr + (rank - tile_in_expert * share[e])

    dest0 = sorted_row(e0, meta[:, 4].astype(jnp.int32))
    dest1 = sorted_row(e1, meta[:, 5].astype(jnp.int32))
    order = jnp.argsort(jnp.stack([e0, e1], axis=1).reshape(-1), stable=True)
    sorted_tok = (order // TOP_K).astype(jnp.int32)
    start = jnp.cumsum(counts) - counts
    block_start = jnp.arange(n_tiles, dtype=jnp.int32) * tr
    block_e = jnp.sum((pad_end[None, :] <= block_start[:, None]).astype(jnp.int32), axis=1)
    block_e = jnp.minimum(block_e, n_experts - 1)
    in_expert = (block_start - pad_start[block_e]) // tr * share[block_e]
    n_valid = jnp.clip(counts[block_e] - in_expert, 0, share[block_e]).astype(jnp.int32)
    n_used = (pad_end[-1:] // tr).astype(jnp.int32)
    yb = _experts(hn, sorted_tok, (start[block_e] + in_expert).astype(jnp.int32), block_e,
                  n_valid, n_used, moe_w_gate[0], moe_w_up[0], moe_w_down[0])
    y_prompt, y_sample = _combine(x, meta, row(norm_final), yb, dest0, dest1, mp)

    tails = functools.partial(_seq_tails, batch=batch, seq=seq, dec_batch=dec_batch, dec_seq=dec_seq)
    groups_per_seq = seq // SUBLANES
    h_p = h_last[groups_per_seq - 1:mp // SUBLANES:groups_per_seq]
    h_s = h_last[mp // SUBLANES:]
    u_p, u_s = tails(proj, d_rnn, d_rnn, conv_a - 1)
    tiles_per_seq = seq // ROW_TILE
    cv_p = cv_tile_tails.reshape(m // ROW_TILE, SUBLANES, -1)[
        tiles_per_seq - 1:n_prompt_tiles:tiles_per_seq, SUBLANES - (conv_b - 1):]
    cv_s = cv_smp.reshape(dec_batch, dec_seq, -1)[:, dec_seq - (conv_b - 1):]
    return (y_prompt.reshape(batch, seq, d), y_sample.reshape(dec_batch, dec_seq, d),
            h_p[None], u_p[None], cv_p[None], h_s[None], u_s[None], cv_s[None])
```

```python
import functools

import jax
import jax.numpy as jnp
from jax import lax
from jax.experimental import pallas as pl
from jax.experimental.pallas import tpu as pltpu

F32 = jnp.float32
BF16 = jnp.bfloat16

EPS = 1e-6
RG_C = 8.0
TOP_K = 2

LANES = 128
SUBLANES = 8
ROW_TILE = 1024
OUT_ROW_TILE = 512
ROUTER_ROW_TILE = 512
COL_TILE = 512
W_COL_TILE = 256
MOE_ROW_TILE = 768
MOE_ROW_SLABS = 4
MOE_COL_TILE = 512
COMBINE_ROW_TILE = 256
SCAN_CHUNK = 64
VMEM_LIMIT = 60 * 1024 * 1024


def _cparams(*sem):
    return pltpu.CompilerParams(dimension_semantics=sem, vmem_limit_bytes=VMEM_LIMIT)


def _rms_norm_f32(x, g):
    y = x * lax.rsqrt(jnp.mean(x * x, axis=-1, keepdims=True) + EPS)
    return y * g


def _two_part_specs(n_first, block, col_of, second_mode=None):
    first = pl.BlockSpec(block, lambda i, *a: (jnp.minimum(i, n_first - 1), col_of(*a)))
    second = pl.BlockSpec(block, lambda i, *a: (jnp.maximum(i - n_first, 0), col_of(*a)),
                          pipeline_mode=second_mode)
    return first, second


def _norm_matmul_kernel(n_first, n_gelu, xa_ref, xb_ref, g_ref, w_ref, o_ref, hn_ref):
    i = pl.program_id(0)
    j = pl.program_id(1)

    def project():
        return jnp.dot(hn_ref[...], w_ref[...].astype(BF16), preferred_element_type=F32)

    @pl.when((j == 0) & (i < n_first))
    def _():
        hn_ref[...] = _rms_norm_f32(xa_ref[...], g_ref[...]).astype(BF16)

    @pl.when((j == 0) & (i >= n_first))
    def _():
        hn_ref[...] = _rms_norm_f32(xb_ref[...], g_ref[...]).astype(BF16)

    @pl.when(j < n_gelu)
    def _():
        o_ref[...] = jax.nn.gelu(project())

    @pl.when(j >= n_gelu)
    def _():
        o_ref[...] = project()


def _norm_matmul(xa, xb, g, w, n_gelu_cols):
    d = xa.shape[1]
    m = xa.shape[0] + xb.shape[0]
    n = w.shape[1]
    assert xb.shape[0] == ROW_TILE
    xa_spec, xb_spec = _two_part_specs(xa.shape[0] // ROW_TILE, (ROW_TILE, d), lambda j: 0,
                                       second_mode=pl.Buffered(1))
    return pl.pallas_call(
        functools.partial(_norm_matmul_kernel, xa.shape[0] // ROW_TILE, n_gelu_cols // COL_TILE),
        grid=(m // ROW_TILE, n // COL_TILE),
        in_specs=[xa_spec, xb_spec,
                  pl.BlockSpec((1, d), lambda i, j: (0, 0)),
                  pl.BlockSpec((d, COL_TILE), lambda i, j: (0, j))],
        out_specs=pl.BlockSpec((ROW_TILE, COL_TILE), lambda i, j: (i, j)),
        out_shape=jax.ShapeDtypeStruct((m, n), F32),
        scratch_shapes=[pltpu.VMEM((ROW_TILE, d), BF16)],
        compiler_params=_cparams("arbitrary", "arbitrary"),
        name="norm_matmul",
    )(xa, xb, g, w)


def _sc_mixer_kernel(n_prompt_tiles, tiles_per_seq, seq_s,
                     x_ref, g_ref, wb_ref, wc_ref, wv_ref, cw_ref, hist_ref,
                     zb_ref, tail_ref, cvs_ref, hn_ref, carry_ref):
    i = pl.program_id(0)
    j = pl.program_id(1)
    t_rows = x_ref.shape[0]
    is_sample = i >= n_prompt_tiles
    starts_seq = is_sample | (lax.rem(i, tiles_per_seq) == 0)

    @pl.when(j == 0)
    def _():
        hn_ref[...] = _rms_norm_f32(x_ref[...], g_ref[...]).astype(BF16)

    @pl.when((i == 0) & (j == 0))
    def _():
        carry_ref[...] = jnp.zeros_like(carry_ref)

    def mixer(masked):
        hn = hn_ref[...]
        cw = cw_ref[...]
        c = jnp.dot(hn, wc_ref[...].astype(BF16), preferred_element_type=F32)
        v = jnp.dot(hn, wv_ref[...].astype(BF16), preferred_element_type=F32)
        cv = c * v
        prev = jnp.where(starts_seq, 0.0, carry_ref[j])
        ext = jnp.concatenate([prev, cv], axis=0)
        w1 = cw.shape[0] - 1
        if masked:
            t = _pos_in_seq(0, t_rows, seq_s)
            hist = hist_ref[...]

        def tap(s):
            shifted = pltpu.roll(ext, s, axis=0)[SUBLANES:]
            if not masked:
                return shifted
            return jnp.where(t >= s, shifted, _history_rows(hist, w1 - s))

        z = tap(2) * cw[0:1] + tap(1) * cw[1:2]
        z = z + cv * cw[2:3]
        b = jnp.dot(hn, wb_ref[...].astype(BF16), preferred_element_type=F32)
        zb_ref[...] = (b * z).astype(BF16)
        last = cv[t_rows - SUBLANES:, :]
        carry_ref[j] = last
        tail_ref[...] = last
        return cv

    @pl.when(is_sample)
    def _():
        cvs_ref[...] = mixer(True)

    @pl.when(jnp.logical_not(is_sample))
    def _():
        mixer(False)


def _sc_mixer(x, g, w, conv_w, hist, n_prompt_tiles, tiles_per_seq, seq_s):
    m, d = x.shape
    dc = w.shape[1] // 3
    tn = W_COL_TILE
    nj = dc // tn
    n_tiles = m // ROW_TILE
    ms = hist.shape[0]
    assert ms == ROW_TILE
    wspec = lambda off: pl.BlockSpec((d, tn), lambda i, j: (0, off * nj + j))
    sample_col = lambda i, j: (0, jnp.where(i >= n_prompt_tiles, j, 0))
    return pl.pallas_call(
        functools.partial(_sc_mixer_kernel, n_prompt_tiles, tiles_per_seq, seq_s),
        grid=(n_tiles, nj),
        in_specs=[pl.BlockSpec((ROW_TILE, d), lambda i, j: (i, 0)),
                  pl.BlockSpec((1, d), lambda i, j: (0, 0)),
                  wspec(0), wspec(1), wspec(2),
                  pl.BlockSpec((conv_w.shape[0], tn), lambda i, j: (0, j)),
                  pl.BlockSpec((ROW_TILE, tn), sample_col)],
        out_specs=[pl.BlockSpec((ROW_TILE, tn), lambda i, j: (i, j)),
                   pl.BlockSpec((SUBLANES, tn), lambda i, j: (i, j)),
                   pl.BlockSpec((ROW_TILE, tn), sample_col)],
        out_shape=[jax.ShapeDtypeStruct((m, dc), BF16),
                   jax.ShapeDtypeStruct((n_tiles * SUBLANES, dc), F32),
                   jax.ShapeDtypeStruct((ms, dc), F32)],
        scratch_shapes=[pltpu.VMEM((ROW_TILE, d), BF16),
                        pltpu.VMEM((nj, SUBLANES, tn), F32)],
        compiler_params=_cparams("arbitrary", "arbitrary"),
        name="sc_mixer",
    )(x, g, w, w, w, conv_w, hist)


def _matmul_residual_kernel(n_first, y_ref, w_ref, ra_ref, rb_ref, o_ref):
    o_ref[...] = jnp.dot(y_ref[...], w_ref[...], preferred_element_type=F32)
    i = pl.program_id(0)

    @pl.when(i < n_first)
    def _():
        o_ref[...] += ra_ref[...]

    @pl.when(i >= n_first)
    def _():
        o_ref[...] += rb_ref[...]


def _matmul_residual(y, w, res_a, res_b):
    m, k = y.shape
    n = w.shape[1]
    tm = OUT_ROW_TILE
    ra_spec, rb_spec = _two_part_specs(res_a.shape[0] // tm, (tm, n), lambda: 0)
    return pl.pallas_call(
        functools.partial(_matmul_residual_kernel, res_a.shape[0] // tm),
        grid=(m // tm,),
        in_specs=[pl.BlockSpec((tm, k), lambda i: (i, 0)),
                  pl.BlockSpec((k, n), lambda i: (0, 0)),
                  ra_spec, rb_spec],
        out_specs=pl.BlockSpec((tm, n), lambda i: (i, 0)),
        out_shape=jax.ShapeDtypeStruct((m, n), F32),
        compiler_params=_cparams("arbitrary"),
        name="matmul_residual",
    )(y, w, res_a, res_b)


def _pos_in_seq(r0, rows, seq_len):
    return (r0 + lax.broadcasted_iota(jnp.int32, (rows, 1), 0)) & (seq_len - 1)


def _sigmoid(x):
    return 0.5 * jnp.tanh(0.5 * x) + 0.5


def _run_chunks(is_sample, n_chunks, body):
    @pl.when(is_sample)
    def _():
        lax.fori_loop(0, n_chunks, lambda k, c: body(True, k) or c, 0)

    @pl.when(jnp.logical_not(is_sample))
    def _():
        lax.fori_loop(0, n_chunks, lambda k, c: body(False, k) or c, 0)


def _history_rows(hist_chunk, back):
    n = hist_chunk.shape[0]
    return hist_chunk if back == 0 else pltpu.roll(hist_chunk, n - back, axis=0)


def _rglru_kernel(n_prompt_tiles, tiles_per_seq, seq_s,
                  gate_ref, u_ref, hist_ref,
                  cw_ref, cb_ref, br_ref, bi_ref, lam_ref, wr_ref, wi_ref,
                  y_ref, hl_ref, ubuf, uc_ref, rp_ref, ip_ref, hc_ref):
    i = pl.program_id(1)
    cblk = u_ref.shape[1]
    is_sample = i >= n_prompt_tiles
    w1 = cw_ref.shape[0] - 1

    @pl.when(is_sample | (lax.rem(i, tiles_per_seq) == 0))
    def _():
        ubuf[0:SUBLANES, :] = jnp.zeros((SUBLANES, cblk), F32)
        hc_ref[...] = jnp.zeros_like(hc_ref)

    ubuf[SUBLANES:, :] = u_ref[...]
    cw = cw_ref[...]
    cb = cb_ref[...]
    rc = SCAN_CHUNK

    def conv_chunk(masked, k):
        r0 = pl.multiple_of(k * rc, rc)
        rows = pl.ds(r0, rc)
        ext = ubuf[pl.ds(r0, rc + SUBLANES), :]
        if masked:
            t = _pos_in_seq(r0, rc, seq_s)
            hist = hist_ref[rows, :]

        def tap(s):
            shifted = pltpu.roll(ext, s, axis=0)[SUBLANES:]
            if not masked:
                return shifted
            return jnp.where(t >= s, shifted, _history_rows(hist, w1 - s))

        conv = tap(3) * cw[0:1] + tap(2) * cw[1:2]
        conv = conv + tap(1) * cw[2:3]
        conv = conv + ext[SUBLANES:] * cw[3:4]
        uc_ref[rows, :] = conv + cb

    _run_chunks(is_sample, ROW_TILE // rc, conv_chunk)

    ub = uc_ref[...].astype(BF16)
    rp_ref[...] = jnp.dot(ub, wr_ref[...], preferred_element_type=F32) + br_ref[...]
    ip_ref[...] = jnp.dot(ub, wi_ref[...], preferred_element_type=F32) + bi_ref[...]

    sp = jax.nn.softplus(-lam_ref[...])
    sub = lax.broadcasted_iota(jnp.int32, (1, SUBLANES, 1), 1)

    def scan_chunk(masked, k):
        r0 = pl.multiple_of(k * rc, rc)
        rows = pl.ds(r0, rc)
        uc = uc_ref[rows, :]
        r = _sigmoid(rp_ref[rows, :])
        ig = _sigmoid(ip_ref[rows, :])
        log_a = (-RG_C * r) * sp
        a = jnp.exp(log_a)
        mult = jnp.sqrt(-jnp.tanh(log_a) * (a * a + 1.0))
        b = mult * ig * uc
        if masked:
            first = _pos_in_seq(r0, rc, seq_s) == 0
            b = jnp.where(first, b + a * _history_rows(hist_ref[rows, :], w1), b)
            a = jnp.where(first, 0.0, a)

        a3 = a.reshape(rc // SUBLANES, SUBLANES, cblk)
        b3 = b.reshape(rc // SUBLANES, SUBLANES, cblk)
        for s in (1, 2, 4):
            a_s = pltpu.roll(a3, s, axis=1)
            b_s = pltpu.roll(b3, s, axis=1)
            m = sub >= s
            b3 = jnp.where(m, a3 * b_s + b3, b3)
            a3 = jnp.where(m, a3 * a_s, a3)

        hc = hc_ref[...]
        hs, lasts = [], []
        for g in range(rc // SUBLANES):
            hg = a3[g] * hc + b3[g]
            lasts.append(hg[SUBLANES - 1:SUBLANES])
            hc = jnp.broadcast_to(lasts[-1], (SUBLANES, cblk))
            hs.append(hg)
        hc_ref[...] = hc
        h = jnp.concatenate(hs, axis=0)
        groups = rc // SUBLANES
        hl_ref[pl.ds(pl.multiple_of(k * groups, groups), groups), :] = jnp.concatenate(lasts, axis=0)
        y_ref[rows, :] = (h * gate_ref[rows, :]).astype(BF16)

    _run_chunks(is_sample, ROW_TILE // rc, scan_chunk)
    ubuf[0:SUBLANES, :] = ubuf[ROW_TILE:ROW_TILE + SUBLANES, :]


def _rglru_core(proj, hist, conv_w, conv_b, b_r, b_i, lam, w_r, w_i, n_prompt_tiles,
                tiles_per_seq, seq_s):
    m = proj.shape[0]
    assert SCAN_CHUNK % (SUBLANES * SUBLANES) == 0
    nb, cblk = w_r.shape[0], w_r.shape[1]
    c = nb * cblk
    n_tiles = m // ROW_TILE
    row_blk = lambda off: pl.BlockSpec((ROW_TILE, cblk), lambda cb, i: (i, off + cb))
    aux_blk = pl.BlockSpec((ROW_TILE, cblk), lambda cb, i: (0, cb))
    vec_blk = lambda rows: pl.BlockSpec((rows, cblk), lambda cb, i: (0, cb))
    w_blk = pl.BlockSpec((None, cblk, cblk), lambda cb, i: (cb, 0, 0))
    out_blk = pl.BlockSpec((ROW_TILE, cblk), lambda cb, i: (i, cb))
    return pl.pallas_call(
        functools.partial(_rglru_kernel, n_prompt_tiles, tiles_per_seq, seq_s),
        grid=(nb, n_tiles),
        in_specs=[row_blk(0), row_blk(nb), aux_blk,
                  vec_blk(conv_w.shape[0]), vec_blk(1), vec_blk(1), vec_blk(1), vec_blk(1),
                  w_blk, w_blk],
        out_specs=[out_blk,
                   pl.BlockSpec((ROW_TILE // SUBLANES, cblk), lambda cb, i: (i, cb))],
        out_shape=[jax.ShapeDtypeStruct((m, c), BF16),
                   jax.ShapeDtypeStruct((m // SUBLANES, c), F32)],
        scratch_shapes=[pltpu.VMEM((ROW_TILE + SUBLANES, cblk), F32),
                        pltpu.VMEM((ROW_TILE, cblk), F32),
                        pltpu.VMEM((ROW_TILE, cblk), F32),
                        pltpu.VMEM((ROW_TILE, cblk), F32),
                        pltpu.VMEM((SUBLANES, cblk), F32)],
        compiler_params=_cparams("arbitrary", "arbitrary"),
        name="rglru_core",
    )(proj, proj, hist, conv_w, conv_b, b_r, b_i, lam, w_r, w_i)


def _ffn_kernel(x_ref, g_ref, wg_ref, wu_ref, wd_ref, o_ref, hn_ref):
    @pl.when(pl.program_id(1) == 0)
    def _():
        x = x_ref[...]
        hn_ref[...] = _rms_norm_f32(x, g_ref[...]).astype(BF16)
        o_ref[...] = x

    hn = hn_ref[...]
    a = jnp.dot(hn, wg_ref[...].astype(BF16), preferred_element_type=F32)
    b = jnp.dot(hn, wu_ref[...].astype(BF16), preferred_element_type=F32)
    h = (jax.nn.silu(a) * b).astype(BF16)
    o_ref[...] += jnp.dot(h, wd_ref[...].astype(BF16), preferred_element_type=F32)


def _ffn(x, g, wg, wu, wd):
    m, d = x.shape
    f = wg.shape[1]
    tm = ROW_TILE
    tf = W_COL_TILE
    return pl.pallas_call(
        _ffn_kernel,
        grid=(m // tm, f // tf),
        in_specs=[pl.BlockSpec((tm, d), lambda i, j: (i, 0)),
                  pl.BlockSpec((1, d), lambda i, j: (0, 0)),
                  pl.BlockSpec((d, tf), lambda i, j: (0, j)),
                  pl.BlockSpec((d, tf), lambda i, j: (0, j)),
                  pl.BlockSpec((tf, d), lambda i, j: (j, 0))],
        out_specs=pl.BlockSpec((tm, d), lambda i, j: (i, 0)),
        out_shape=jax.ShapeDtypeStruct((m, d), F32),
        scratch_shapes=[pltpu.VMEM((tm, d), BF16)],
        compiler_params=_cparams("arbitrary", "arbitrary"),
        name="ffn",
    )(x, g, wg, wu, wd)


def _router_kernel(n_experts, x_ref, g_ref, whi_ref, wlo_ref, hn_ref, meta_ref, cnt_ref, run_ref):
    @pl.when(pl.program_id(0) == 0)
    def _():
        run_ref[...] = jnp.zeros_like(run_ref)

    tm = x_ref.shape[0]
    hn = _rms_norm_f32(x_ref[...], g_ref[...])
    hn_hi = hn.astype(BF16)
    hn_hi32 = hn_hi.astype(F32)
    bits = pltpu.bitcast(hn_hi32, jnp.uint32)
    half = hn.shape[1] // 2
    hn_ref[...] = (bits[:, half:] & jnp.uint32(0xFFFF0000)) | (bits[:, :half] >> 16)
    hn_lo = (hn - hn_hi32).astype(BF16)
    logits = (jnp.dot(hn_hi, whi_ref[...], preferred_element_type=F32)
              + (jnp.dot(hn_lo, whi_ref[...], preferred_element_type=F32)
                 + jnp.dot(hn_hi, wlo_ref[...], preferred_element_type=F32)))
    lane = lax.broadcasted_iota(jnp.int32, (tm, LANES), 1).astype(F32)
    neg = jnp.float32(-jnp.inf)
    lg = jnp.where(lane < n_experts, logits, neg)
    m1 = jnp.max(lg, axis=1, keepdims=True)
    i1 = jnp.min(jnp.where(lg == m1, lane, float(LANES)), axis=1, keepdims=True)
    lg2 = jnp.where(lane == i1, neg, lg)
    m2 = jnp.max(lg2, axis=1, keepdims=True)
    i2 = jnp.min(jnp.where(lg2 == m2, lane, float(LANES)), axis=1, keepdims=True)
    ex = jnp.exp(m2 - m1)
    g1 = 1.0 / (1.0 + ex)
    g2 = ex / (1.0 + ex)

    sel1 = lane == i1
    sel2 = lane == i2
    onehot = jnp.where(sel1 | sel2, 1.0, 0.0)
    rr = lax.broadcasted_iota(jnp.int32, (tm, tm), 0)
    cc = lax.broadcasted_iota(jnp.int32, (tm, tm), 1)
    tri = jnp.where(rr > cc, 1.0, 0.0).astype(BF16)
    before = jnp.dot(tri, onehot.astype(BF16), preferred_element_type=F32) + run_ref[0:1, :]
    rank1 = jnp.sum(jnp.where(sel1, before, 0.0), axis=1, keepdims=True)
    rank2 = jnp.sum(jnp.where(sel2, before, 0.0), axis=1, keepdims=True)
    run = run_ref[0:1, :] + jnp.sum(onehot, axis=0, keepdims=True)
    run_ref[...] = jnp.broadcast_to(run, run_ref.shape)
    cnt_ref[...] = jnp.broadcast_to(run, cnt_ref.shape)

    meta = jnp.where(lane == 0, i1, 0.0)
    meta = jnp.where(lane == 1, i2, meta)
    meta = jnp.where(lane == 2, g1, meta)
    meta = jnp.where(lane == 3, g2, meta)
    meta = jnp.where(lane == 4, rank1, meta)
    meta = jnp.where(lane == 5, rank2, meta)
    meta_ref[...] = meta


def _router(x, g, w_router_padded, n_experts):
    m, d = x.shape
    tm = ROUTER_ROW_TILE
    w_hi = w_router_padded.astype(BF16)
    w_lo = (w_router_padded - w_hi.astype(F32)).astype(BF16)
    return pl.pallas_call(
        functools.partial(_router_kernel, n_experts),
        grid=(m // tm,),
        in_specs=[pl.BlockSpec((tm, d), lambda i: (i, 0)),
                  pl.BlockSpec((1, d), lambda i: (0, 0)),
                  pl.BlockSpec((d, LANES), lambda i: (0, 0)),
                  pl.BlockSpec((d, LANES), lambda i: (0, 0))],
        out_specs=[pl.BlockSpec((tm, d // 2), lambda i: (i, 0)),
                   pl.BlockSpec((tm, LANES), lambda i: (i, 0)),
                   pl.BlockSpec((SUBLANES, LANES), lambda i: (0, 0))],
        out_shape=[jax.ShapeDtypeStruct((m, d // 2), jnp.uint32),
                   jax.ShapeDtypeStruct((m, LANES), F32),
                   jax.ShapeDtypeStruct((SUBLANES, LANES), F32)],
        scratch_shapes=[pltpu.VMEM((SUBLANES, LANES), F32)],
        compiler_params=_cparams("arbitrary"),
        name="router",
    )(x, g, w_hi, w_lo)


def _row_copy(src_hbm, idx, dst_ref, r, sem):
    return pltpu.make_async_copy(src_hbm.at[pl.ds(idx, 1)], dst_ref.at[pl.ds(r, 1)], sem)


def _experts_kernel(rows_per_step, be_ref, nv_ref, nused_ref, off_ref, stok_ref,
                    hn_hbm, wg_ref, wu_ref, wd_ref, o_ref,
                    xbuf, xb_ref, sem):
    t = pl.program_id(0)
    j = pl.program_id(1)
    nused = nused_ref[0]
    tr = xb_ref.shape[0]
    gr = xbuf.shape[0]
    dh = xbuf.shape[1]

    @pl.when(j == 0)
    def _():
        o_ref[...] = jnp.zeros_like(o_ref)

    @pl.when((t == 0) & (j == 0))
    def _():
        off0 = off_ref[0]

        def issue(r, c):
            _row_copy(hn_hbm, stok_ref[off0 + r], xbuf, r, sem).start()
            return c

        lax.fori_loop(0, gr, issue, 0, unroll=8)

    @pl.when((j == 0) & (t <= nused))
    def _():
        pltpu.make_async_copy(hn_hbm.at[pl.ds(0, gr)], xbuf, sem).wait()

    @pl.when((j == 0) & (t < nused))
    def _():
        w = xbuf[0:tr, :]
        xb_ref[:, 0:dh] = pltpu.bitcast(w << 16, F32).astype(BF16)
        xb_ref[:, dh:] = pltpu.bitcast(w & jnp.uint32(0xFFFF0000), F32).astype(BF16)

    def step(n_rows):
        base = off_ref[t + 1] + j * rows_per_step
        for k in range(rows_per_step):
            _row_copy(hn_hbm, stok_ref[base + k], xbuf, j * rows_per_step + k, sem).start()
        rows = pl.ds(0, n_rows)
        x = xb_ref[rows, :]
        a = jnp.dot(x, wg_ref[...].astype(BF16), preferred_element_type=F32)
        b = jnp.dot(x, wu_ref[...].astype(BF16), preferred_element_type=F32)
        h = (jax.nn.silu(a) * b).astype(BF16)
        o_ref[rows, :] += jnp.dot(h, wd_ref[...].astype(BF16), preferred_element_type=F32)

    slab = tr // MOE_ROW_SLABS
    n_slabs = (nv_ref[t] + (slab - 1)) // slab
    for q in range(1, MOE_ROW_SLABS + 1):
        @pl.when((t < nused) & (jnp.maximum(n_slabs, 1) == q))
        def _(q=q):
            step(q * slab)


def _experts(hn_packed, sorted_tok, tile_off, block_e, n_valid, n_used, wg, wu, wd):
    dh = hn_packed.shape[1]
    d = 2 * dh
    f = wg.shape[2]
    tr = MOE_ROW_TILE
    tf = MOE_COL_TILE
    assert tr % (MOE_ROW_SLABS * 2 * SUBLANES) == 0 and f % tf == 0
    n_tiles = block_e.shape[0]
    nj = f // tf
    rows_per_step = -(-tr // nj)
    gr = rows_per_step * nj
    stok = jnp.pad(sorted_tok, (0, gr))
    offs = jnp.pad(jnp.clip(tile_off, 0, sorted_tok.shape[0]), (0, 1))

    def tile(t, nu):
        return jnp.minimum(t, jnp.maximum(nu[0] - 1, 0))

    def col(t, j, nu):
        return jnp.where(t < nu[0], j, nj - 1)

    wspec = lambda shape, imap: pl.BlockSpec(
        shape, lambda t, j, be, nv, nu, off, st: imap(be[tile(t, nu)], col(t, j, nu)))
    return pl.pallas_call(
        functools.partial(_experts_kernel, rows_per_step),
        grid_spec=pltpu.PrefetchScalarGridSpec(
            num_scalar_prefetch=5,
            grid=(n_tiles + 1, nj),
            in_specs=[pl.BlockSpec(memory_space=pl.ANY),
                      wspec((None, d, tf), lambda e, c: (e, 0, c)),
                      wspec((None, d, tf), lambda e, c: (e, 0, c)),
                      wspec((None, tf, d), lambda e, c: (e, c, 0))],
            out_specs=pl.BlockSpec((tr, d), lambda t, j, *_: (t, 0)),
            scratch_shapes=[pltpu.VMEM((gr, dh), jnp.uint32), pltpu.VMEM((tr, d), BF16),
                            pltpu.SemaphoreType.DMA(())]),
        out_shape=jax.ShapeDtypeStruct(((n_tiles + 1) * tr, d), F32),
        compiler_params=_cparams("arbitrary", "arbitrary"),
        name="moe_experts",
    )(block_e, jnp.pad(n_valid, (0, 1)), n_used, offs, stok, hn_packed, wg, wu, wd)


def _combine_kernel(n_first, d0_ref, d1_ref, d0n_ref, d1n_ref, x_ref, meta_ref, g_ref, yb_hbm,
                    oa_ref, ob_ref, y0, y1, sem):
    tm = x_ref.shape[0]
    i = pl.program_id(0)
    slot = lax.rem(i, 2)

    def row_copies(da_ref, db_ref, s, r):
        return (_row_copy(yb_hbm, da_ref[0, 0, r], y0.at[s], r, sem.at[0, s]),
                _row_copy(yb_hbm, db_ref[0, 0, r], y1.at[s], r, sem.at[1, s]))

    def wait_tile(s):
        pltpu.make_async_copy(yb_hbm.at[pl.ds(0, tm)], y0.at[s], sem.at[0, s]).wait()
        pltpu.make_async_copy(yb_hbm.at[pl.ds(0, tm)], y1.at[s], sem.at[1, s]).wait()

    @pl.when(i == 0)
    def _():
        def issue(r, c):
            for cp in row_copies(d0_ref, d1_ref, 0, r):
                cp.start()
            return c

        lax.fori_loop(0, tm, issue, 0, unroll=8)

    wait_tile(slot)
    for r in range(tm):
        for cp in row_copies(d0n_ref, d1n_ref, 1 - slot, r):
            cp.start()
    meta = meta_ref[...]
    x = x_ref[...] + (y0[slot] * meta[:, 2:3] + y1[slot] * meta[:, 3:4])
    y = _rms_norm_f32(x, g_ref[...])

    @pl.when(i < n_first)
    def _():
        oa_ref[...] = y

    @pl.when(i >= n_first)
    def _():
        ob_ref[...] = y

    @pl.when(i == pl.num_programs(0) - 1)
    def _():
        wait_tile(1 - slot)


def _combine(x, meta, g, yb, dest0, dest1, m_first):
    m, d = x.shape
    tm = COMBINE_ROW_TILE
    n_tiles = m // tm
    smem_blk = pl.BlockSpec((1, 1, tm), lambda i: (i, 0, 0), memory_space=pltpu.SMEM)
    smem_next = pl.BlockSpec((1, 1, tm), lambda i: (jnp.minimum(i + 1, n_tiles - 1), 0, 0),
                             memory_space=pltpu.SMEM)
    oa_spec, ob_spec = _two_part_specs(m_first // tm, (tm, d), lambda: 0)
    d0 = dest0.reshape(n_tiles, 1, tm)
    d1 = dest1.reshape(n_tiles, 1, tm)
    return pl.pallas_call(
        functools.partial(_combine_kernel, m_first // tm),
        grid=(n_tiles,),
        in_specs=[smem_blk, smem_blk, smem_next, smem_next,
                  pl.BlockSpec((tm, d), lambda i: (i, 0)),
                  pl.BlockSpec((tm, LANES), lambda i: (i, 0)),
                  pl.BlockSpec((1, d), lambda i: (0, 0)),
                  pl.BlockSpec(memory_space=pl.ANY)],
        out_specs=[oa_spec, ob_spec],
        out_shape=[jax.ShapeDtypeStruct((m_first, d), F32),
                   jax.ShapeDtypeStruct((m - m_first, d), F32)],
        scratch_shapes=[pltpu.VMEM((2, tm, d), F32), pltpu.VMEM((2, tm, d), F32),
                        pltpu.SemaphoreType.DMA((2, 2))],
        compiler_params=_cparams("arbitrary"),
        name="moe_combine",
    )(d0, d1, d0, d1, x, meta, g, yb)


def _sample_history(dec_seq, *states):
    hist = jnp.concatenate(states, axis=1)
    b, r, c = hist.shape
    assert r <= dec_seq
    return jnp.pad(hist, ((0, 0), (0, dec_seq - r), (0, 0))).reshape(b * dec_seq, c)


def _seq_tails(a, col0, ncol, width, batch, seq, dec_batch, dec_seq):
    mp = batch * seq
    tails = [lax.slice(a, ((b + 1) * seq - width, col0), ((b + 1) * seq, col0 + ncol))
             for b in range(batch)]
    smp = lax.slice(a, (mp, col0), (mp + dec_batch * dec_seq, col0 + ncol))
    smp = smp.reshape(dec_batch, dec_seq, ncol)[:, dec_seq - width:]
    return jnp.stack(tails, axis=0), smp


def kernel(x_prompt, x_sample, state_lru_h, state_lru_conv, state_sconv, norm_mix, norm_ffn, norm_final, lru_w_in, lru_conv_w, lru_conv_b, lru_w_rgate, lru_b_rgate, lru_w_igate, lru_b_igate, lru_lambda, lru_w_out, sc_w_in, sc_conv_w, sc_w_out, ffn_w_gate, ffn_w_up, ffn_w_down, moe_w_router, moe_w_gate, moe_w_up, moe_w_down):
    batch, seq, d = x_prompt.shape
    dec_batch, dec_seq, _ = x_sample.shape
    mp, ms = batch * seq, dec_batch * dec_seq
    m = mp + ms
    depth = norm_mix.shape[0]
    n_experts = moe_w_router.shape[2]
    conv_a = lru_conv_w.shape[1]
    conv_b = sc_conv_w.shape[1]
    assert depth == 2 and lru_w_in.shape[0] == 1 and sc_w_in.shape[0] == 1
    assert seq & (seq - 1) == 0 and dec_seq & (dec_seq - 1) == 0
    assert seq % ROW_TILE == 0 and ms == ROW_TILE and ROW_TILE % dec_seq == 0
    assert dec_seq >= conv_a - 1 and dec_seq >= conv_b - 1 and dec_seq == SUBLANES
    assert n_experts <= LANES and m % ROUTER_ROW_TILE == 0 and m % COMBINE_ROW_TILE == 0
    n_prompt_tiles = mp // ROW_TILE

    xp = x_prompt.reshape(mp, d)
    xs_in = x_sample.reshape(ms, d)
    row = lambda v: v.reshape(1, -1)

    d_rnn = lru_w_out.shape[1]
    assert d_rnn % COL_TILE == 0
    proj = _norm_matmul(xp, xs_in, row(norm_mix[0]), lru_w_in[0], n_gelu_cols=d_rnn)
    hist = _sample_history(dec_seq, state_lru_conv[0], state_lru_h[0][:, None, :])
    y, h_last = _rglru_core(proj, hist, lru_conv_w[0], row(lru_conv_b[0]), row(lru_b_rgate[0]),
                           row(lru_b_igate[0]), row(lru_lambda[0]),
                           lru_w_rgate[0].astype(BF16), lru_w_igate[0].astype(BF16),
                           n_prompt_tiles, seq // ROW_TILE, dec_seq)
    x = _matmul_residual(y, lru_w_out[0].astype(BF16), xp, xs_in)

    x = _ffn(x, row(norm_ffn[0]), ffn_w_gate[0], ffn_w_up[0], ffn_w_down[0])

    hist = _sample_history(dec_seq, state_sconv[0])
    zb, cv_tile_tails, cv_smp = _sc_mixer(x, row(norm_mix[1]), sc_w_in[0], sc_conv_w[0], hist,
                                          n_prompt_tiles, seq // ROW_TILE, dec_seq)
    x = _matmul_residual(zb, sc_w_out[0].astype(BF16), x, x)

    w_router = jnp.pad(moe_w_router[0], ((0, 0), (0, LANES - n_experts)))
    hn, meta, counts = _router(x, row(norm_ffn[1]), w_router, n_experts)
    tr = MOE_ROW_TILE
    n_tiles = -(-(m * TOP_K) // tr) + n_experts
    counts = counts[0, :n_experts].astype(jnp.int32)
    k_tiles = (counts + tr - 1) // tr
    share = jnp.maximum((counts + k_tiles - 1) // jnp.maximum(k_tiles, 1), 1)
    padded = k_tiles * tr
    pad_end = jnp.cumsum(padded)
    pad_start = pad_end - padded
    e0 = meta[:, 0].astype(jnp.int32)
    e1 = meta[:, 1].astype(jnp.int32)

    def sorted_row(e, rank):
        tile_in_expert = rank // share[e]
        return pad_start[e] + tile_in_expert * tr + (rank - tile_in_expert * share[e])

    dest0 = sorted_row(e0, meta[:, 4].astype(jnp.int32))
    dest1 = sorted_row(e1, meta[:, 5].astype(jnp.int32))
    order = jnp.argsort(jnp.stack([e0, e1], axis=1).reshape(-1), stable=True)
    sorted_tok = (order // TOP_K).astype(jnp.int32)
    start = jnp.cumsum(counts) - counts
    block_start = jnp.arange(n_tiles, dtype=jnp.int32) * tr
    block_e = jnp.sum((pad_end[None, :] <= block_start[:, None]).astype(jnp.int32), axis=1)
    block_e = jnp.minimum(block_e, n_experts - 1)
    in_expert = (block_start - pad_start[block_e]) // tr * share[block_e]
    n_valid = jnp.clip(counts[block_e] - in_expert, 0, share[block_e]).astype(jnp.int32)
    n_used = (pad_end[-1:] // tr).astype(jnp.int32)
    yb = _experts(hn, sorted_tok, (start[block_e] + in_expert).astype(jnp.int32), block_e,
                  n_valid, n_used, moe_w_gate[0], moe_w_up[0], moe_w_down[0])
    y_prompt, y_sample = _combine(x, meta, row(norm_final), yb, dest0, dest1, mp)

    tails = functools.partial(_seq_tails, batch=batch, seq=seq, dec_batch=dec_batch, dec_seq=dec_seq)
    groups_per_seq = seq // SUBLANES
    h_p = h_last[groups_per_seq - 1:mp // SUBLANES:groups_per_seq]
    h_s = h_last[mp // SUBLANES:]
    u_p, u_s = tails(proj, d_rnn, d_rnn, conv_a - 1)
    tiles_per_seq = seq // ROW_TILE
    cv_p = cv_tile_tails.reshape(m // ROW_TILE, SUBLANES, -1)[
        tiles_per_seq - 1:n_prompt_tiles:tiles_per_seq, SUBLANES - (conv_b - 1):]
    cv_s = cv_smp.reshape(dec_batch, dec_seq, -1)[:, dec_seq - (conv_b - 1):]
    return (y_prompt.reshape(batch, seq, d), y_sample.reshape(dec_batch, dec_seq, d),
            h_p[None], u_p[None], cv_p[None], h_s[None], u_s[None], cv_s[None])
```

```python
import functools

import jax
import jax.numpy as jnp
from jax import lax
from jax.experimental import pallas as pl
from jax.experimental.pallas import tpu as pltpu

F32 = jnp.float32
BF16 = jnp.bfloat16

EPS = 1e-6
RG_C = 8.0
TOP_K = 2

LANES = 128
SUBLANES = 8
ROW_TILE = 1024
OUT_ROW_TILE = 512
ROUTER_ROW_TILE = 512
COL_TILE = 512
W_COL_TILE = 256
MOE_ROW_TILE = 768
MOE_ROW_SLABS = 4
MOE_COL_TILE = 512
COMBINE_ROW_TILE = 256
SCAN_CHUNK = 64
VMEM_LIMIT = 60 * 1024 * 1024


def _cparams(*sem):
    return pltpu.CompilerParams(dimension_semantics=sem, vmem_limit_bytes=VMEM_LIMIT)


def _rms_norm_f32(x, g):
    y = x * lax.rsqrt(jnp.mean(x * x, axis=-1, keepdims=True) + EPS)
    return y * g


BF16_HIGH_HALF = 0xFFFF0000
BF16_BITS = 16


def _pack_bf16_pairs(x_rounded):
    bits = pltpu.bitcast(x_rounded, jnp.uint32)
    half = x_rounded.shape[1] // 2
    return (bits[:, half:] & jnp.uint32(BF16_HIGH_HALF)) | (bits[:, :half] >> BF16_BITS)


def _unpack_bf16_pairs(w):
    return (pltpu.bitcast(w << BF16_BITS, F32),
            pltpu.bitcast(w & jnp.uint32(BF16_HIGH_HALF), F32))


def _two_part_specs(n_first, block, col_of, second_mode=None):
    first = pl.BlockSpec(block, lambda i, *a: (jnp.minimum(i, n_first - 1), col_of(*a)))
    second = pl.BlockSpec(block, lambda i, *a: (jnp.maximum(i - n_first, 0), col_of(*a)),
                          pipeline_mode=second_mode)
    return first, second


def _norm_matmul_kernel(n_first, n_gelu, xa_ref, xb_ref, g_ref, w_ref, o_ref, hn_ref):
    i = pl.program_id(0)
    j = pl.program_id(1)

    def project():
        return jnp.dot(hn_ref[...], w_ref[...].astype(BF16), preferred_element_type=F32)

    @pl.when((j == 0) & (i < n_first))
    def _():
        hn_ref[...] = _rms_norm_f32(xa_ref[...], g_ref[...]).astype(BF16)

    @pl.when((j == 0) & (i >= n_first))
    def _():
        hn_ref[...] = _rms_norm_f32(xb_ref[...], g_ref[...]).astype(BF16)

    @pl.when(j < n_gelu)
    def _():
        o_ref[...] = jax.nn.gelu(project())

    @pl.when(j >= n_gelu)
    def _():
        o_ref[...] = project()


def _norm_matmul(xa, xb, g, w, n_gelu_cols):
    d = xa.shape[1]
    m = xa.shape[0] + xb.shape[0]
    n = w.shape[1]
    assert xb.shape[0] == ROW_TILE
    xa_spec, xb_spec = _two_part_specs(xa.shape[0] // ROW_TILE, (ROW_TILE, d), lambda j: 0,
                                       second_mode=pl.Buffered(1))
    return pl.pallas_call(
        functools.partial(_norm_matmul_kernel, xa.shape[0] // ROW_TILE, n_gelu_cols // COL_TILE),
        grid=(m // ROW_TILE, n // COL_TILE),
        in_specs=[xa_spec, xb_spec,
                  pl.BlockSpec((1, d), lambda i, j: (0, 0)),
                  pl.BlockSpec((d, COL_TILE), lambda i, j: (0, j))],
        out_specs=pl.BlockSpec((ROW_TILE, COL_TILE), lambda i, j: (i, j)),
        out_shape=jax.ShapeDtypeStruct((m, n), F32),
        scratch_shapes=[pltpu.VMEM((ROW_TILE, d), BF16)],
        compiler_params=_cparams("arbitrary", "arbitrary"),
        name="norm_matmul",
    )(xa, xb, g, w)


def _sc_mixer_kernel(n_prompt_tiles, tiles_per_seq, seq_s,
                     x_ref, g_ref, wb_ref, wc_ref, wv_ref, cw_ref, hist_ref,
                     zb_ref, tail_ref, cvs_ref, hn_ref, carry_ref):
    i = pl.program_id(0)
    j = pl.program_id(1)
    t_rows = x_ref.shape[0]
    is_sample = i >= n_prompt_tiles
    starts_seq = is_sample | (lax.rem(i, tiles_per_seq) == 0)

    @pl.when(j == 0)
    def _():
        hn_ref[...] = _rms_norm_f32(x_ref[...], g_ref[...]).astype(BF16)

    @pl.when((i == 0) & (j == 0))
    def _():
        carry_ref[...] = jnp.zeros_like(carry_ref)

    def mixer(masked):
        hn = hn_ref[...]
        cw = cw_ref[...]
        c = jnp.dot(hn, wc_ref[...].astype(BF16), preferred_element_type=F32)
        v = jnp.dot(hn, wv_ref[...].astype(BF16), preferred_element_type=F32)
        cv = c * v
        prev = jnp.where(starts_seq, 0.0, carry_ref[j])
        ext = jnp.concatenate([prev, cv], axis=0)
        w1 = cw.shape[0] - 1
        if masked:
            t = _pos_in_seq(0, t_rows, seq_s)
            hist = hist_ref[...]

        def tap(s):
            shifted = pltpu.roll(ext, s, axis=0)[SUBLANES:]
            if not masked:
                return shifted
            return jnp.where(t >= s, shifted, _history_rows(hist, w1 - s))

        z = tap(2) * cw[0:1] + tap(1) * cw[1:2]
        z = z + cv * cw[2:3]
        b = jnp.dot(hn, wb_ref[...].astype(BF16), preferred_element_type=F32)
        zb_ref[...] = (b * z).astype(BF16)
        last = cv[t_rows - SUBLANES:, :]
        carry_ref[j] = last
        tail_ref[...] = last
        return cv

    @pl.when(is_sample)
    def _():
        cvs_ref[...] = mixer(True)

    @pl.when(jnp.logical_not(is_sample))
    def _():
        mixer(False)


def _sc_mixer(x, g, w, conv_w, hist, n_prompt_tiles, tiles_per_seq, seq_s):
    m, d = x.shape
    dc = w.shape[1] // 3
    tn = W_COL_TILE
    nj = dc // tn
    n_tiles = m // ROW_TILE
    ms = hist.shape[0]
    assert ms == ROW_TILE
    wspec = lambda off: pl.BlockSpec((d, tn), lambda i, j: (0, off * nj + j))
    sample_col = lambda i, j: (0, jnp.where(i >= n_prompt_tiles, j, 0))
    return pl.pallas_call(
        functools.partial(_sc_mixer_kernel, n_prompt_tiles, tiles_per_seq, seq_s),
        grid=(n_tiles, nj),
        in_specs=[pl.BlockSpec((ROW_TILE, d), lambda i, j: (i, 0)),
                  pl.BlockSpec((1, d), lambda i, j: (0, 0)),
                  wspec(0), wspec(1), wspec(2),
                  pl.BlockSpec((conv_w.shape[0], tn), lambda i, j: (0, j)),
                  pl.BlockSpec((ROW_TILE, tn), sample_col)],
        out_specs=[pl.BlockSpec((ROW_TILE, tn), lambda i, j: (i, j)),
                   pl.BlockSpec((SUBLANES, tn), lambda i, j: (i, j)),
                   pl.BlockSpec((ROW_TILE, tn), sample_col)],
        out_shape=[jax.ShapeDtypeStruct((m, dc), BF16),
                   jax.ShapeDtypeStruct((n_tiles * SUBLANES, dc), F32),
                   jax.ShapeDtypeStruct((ms, dc), F32)],
        scratch_shapes=[pltpu.VMEM((ROW_TILE, d), BF16),
                        pltpu.VMEM((nj, SUBLANES, tn), F32)],
        compiler_params=_cparams("arbitrary", "arbitrary"),
        name="sc_mixer",
    )(x, g, w, w, w, conv_w, hist)


def _matmul_residual_kernel(n_first, y_ref, w_ref, ra_ref, rb_ref, o_ref):
    o_ref[...] = jnp.dot(y_ref[...], w_ref[...], preferred_element_type=F32)
    i = pl.program_id(0)

    @pl.when(i < n_first)
    def _():
        o_ref[...] += ra_ref[...]

    @pl.when(i >= n_first)
    def _():
        o_ref[...] += rb_ref[...]


def _matmul_residual(y, w, res_a, res_b):
    m, k = y.shape
    n = w.shape[1]
    tm = OUT_ROW_TILE
    ra_spec, rb_spec = _two_part_specs(res_a.shape[0] // tm, (tm, n), lambda: 0)
    return pl.pallas_call(
        functools.partial(_matmul_residual_kernel, res_a.shape[0] // tm),
        grid=(m // tm,),
        in_specs=[pl.BlockSpec((tm, k), lambda i: (i, 0)),
                  pl.BlockSpec((k, n), lambda i: (0, 0)),
                  ra_spec, rb_spec],
        out_specs=pl.BlockSpec((tm, n), lambda i: (i, 0)),
        out_shape=jax.ShapeDtypeStruct((m, n), F32),
        compiler_params=_cparams("arbitrary"),
        name="matmul_residual",
    )(y, w, res_a, res_b)


def _pos_in_seq(r0, rows, seq_len):
    return (r0 + lax.broadcasted_iota(jnp.int32, (rows, 1), 0)) & (seq_len - 1)


def _sigmoid(x):
    return 0.5 * jnp.tanh(0.5 * x) + 0.5


def _run_chunks(is_sample, n_chunks, body):
    @pl.when(is_sample)
    def _():
        lax.fori_loop(0, n_chunks, lambda k, c: body(True, k) or c, 0)

    @pl.when(jnp.logical_not(is_sample))
    def _():
        lax.fori_loop(0, n_chunks, lambda k, c: body(False, k) or c, 0)


def _history_rows(hist_chunk, back):
    n = hist_chunk.shape[0]
    return hist_chunk if back == 0 else pltpu.roll(hist_chunk, n - back, axis=0)


def _rglru_kernel(n_prompt_tiles, tiles_per_seq, seq_s,
                  gate_ref, u_ref, hist_ref,
                  cw_ref, cb_ref, br_ref, bi_ref, lam_ref, wr_ref, wi_ref,
                  y_ref, hl_ref, ubuf, uc_ref, rp_ref, ip_ref, hc_ref):
    i = pl.program_id(1)
    cblk = u_ref.shape[1]
    is_sample = i >= n_prompt_tiles
    w1 = cw_ref.shape[0] - 1

    @pl.when(is_sample | (lax.rem(i, tiles_per_seq) == 0))
    def _():
        ubuf[0:SUBLANES, :] = jnp.zeros((SUBLANES, cblk), F32)
        hc_ref[...] = jnp.zeros_like(hc_ref)

    ubuf[SUBLANES:, :] = u_ref[...]
    cw = cw_ref[...]
    cb = cb_ref[...]
    rc = SCAN_CHUNK

    def conv_chunk(masked, k):
        r0 = pl.multiple_of(k * rc, rc)
        rows = pl.ds(r0, rc)
        ext = ubuf[pl.ds(r0, rc + SUBLANES), :]
        if masked:
            t = _pos_in_seq(r0, rc, seq_s)
            hist = hist_ref[rows, :]

        def tap(s):
            shifted = pltpu.roll(ext, s, axis=0)[SUBLANES:]
            if not masked:
                return shifted
            return jnp.where(t >= s, shifted, _history_rows(hist, w1 - s))

        conv = tap(3) * cw[0:1] + tap(2) * cw[1:2]
        conv = conv + tap(1) * cw[2:3]
        conv = conv + ext[SUBLANES:] * cw[3:4]
        uc_ref[rows, :] = conv + cb

    _run_chunks(is_sample, ROW_TILE // rc, conv_chunk)

    ub = uc_ref[...].astype(BF16)
    rp_ref[...] = jnp.dot(ub, wr_ref[...], preferred_element_type=F32) + br_ref[...]
    ip_ref[...] = jnp.dot(ub, wi_ref[...], preferred_element_type=F32) + bi_ref[...]

    sp = jax.nn.softplus(-lam_ref[...])
    sub = lax.broadcasted_iota(jnp.int32, (1, SUBLANES, 1), 1)

    def scan_chunk(masked, k):
        r0 = pl.multiple_of(k * rc, rc)
        rows = pl.ds(r0, rc)
        uc = uc_ref[rows, :]
        r = _sigmoid(rp_ref[rows, :])
        ig = _sigmoid(ip_ref[rows, :])
        log_a = (-RG_C * r) * sp
        a = jnp.exp(log_a)
        mult = jnp.sqrt(-jnp.tanh(log_a) * (a * a + 1.0))
        b = mult * ig * uc
        if masked:
            first = _pos_in_seq(r0, rc, seq_s) == 0
            b = jnp.where(first, b + a * _history_rows(hist_ref[rows, :], w1), b)
            a = jnp.where(first, 0.0, a)

        a3 = a.reshape(rc // SUBLANES, SUBLANES, cblk)
        b3 = b.reshape(rc // SUBLANES, SUBLANES, cblk)
        for s in (1, 2, 4):
            a_s = pltpu.roll(a3, s, axis=1)
            b_s = pltpu.roll(b3, s, axis=1)
            m = sub >= s
            b3 = jnp.where(m, a3 * b_s + b3, b3)
            a3 = jnp.where(m, a3 * a_s, a3)

        hc = hc_ref[...]
        hs, lasts = [], []
        for g in range(rc // SUBLANES):
            hg = a3[g] * hc + b3[g]
            lasts.append(hg[SUBLANES - 1:SUBLANES])
            hc = jnp.broadcast_to(lasts[-1], (SUBLANES, cblk))
            hs.append(hg)
        hc_ref[...] = hc
        h = jnp.concatenate(hs, axis=0)
        groups = rc // SUBLANES
        hl_ref[pl.ds(pl.multiple_of(k * groups, groups), groups), :] = jnp.concatenate(lasts, axis=0)
        y_ref[rows, :] = (h * gate_ref[rows, :]).astype(BF16)

    _run_chunks(is_sample, ROW_TILE // rc, scan_chunk)
    ubuf[0:SUBLANES, :] = ubuf[ROW_TILE:ROW_TILE + SUBLANES, :]


def _rglru_core(proj, hist, conv_w, conv_b, b_r, b_i, lam, w_r, w_i, n_prompt_tiles,
                tiles_per_seq, seq_s):
    m = proj.shape[0]
    assert SCAN_CHUNK % (SUBLANES * SUBLANES) == 0
    nb, cblk = w_r.shape[0], w_r.shape[1]
    c = nb * cblk
    n_tiles = m // ROW_TILE
    row_blk = lambda off: pl.BlockSpec((ROW_TILE, cblk), lambda cb, i: (i, off + cb))
    aux_blk = pl.BlockSpec((ROW_TILE, cblk), lambda cb, i: (0, cb))
    vec_blk = lambda rows: pl.BlockSpec((rows, cblk), lambda cb, i: (0, cb))
    w_blk = pl.BlockSpec((None, cblk, cblk), lambda cb, i: (cb, 0, 0))
    out_blk = pl.BlockSpec((ROW_TILE, cblk), lambda cb, i: (i, cb))
    return pl.pallas_call(
        functools.partial(_rglru_kernel, n_prompt_tiles, tiles_per_seq, seq_s),
        grid=(nb, n_tiles),
        in_specs=[row_blk(0), row_blk(nb), aux_blk,
                  vec_blk(conv_w.shape[0]), vec_blk(1), vec_blk(1), vec_blk(1), vec_blk(1),
                  w_blk, w_blk],
        out_specs=[out_blk,
                   pl.BlockSpec((ROW_TILE // SUBLANES, cblk), lambda cb, i: (i, cb))],
        out_shape=[jax.ShapeDtypeStruct((m, c), BF16),
                   jax.ShapeDtypeStruct((m // SUBLANES, c), F32)],
        scratch_shapes=[pltpu.VMEM((ROW_TILE + SUBLANES, cblk), F32),
                        pltpu.VMEM((ROW_TILE, cblk), F32),
                        pltpu.VMEM((ROW_TILE, cblk), F32),
                        pltpu.VMEM((ROW_TILE, cblk), F32),
                        pltpu.VMEM((SUBLANES, cblk), F32)],
        compiler_params=_cparams("arbitrary", "arbitrary"),
        name="rglru_core",
    )(proj, proj, hist, conv_w, conv_b, b_r, b_i, lam, w_r, w_i)


def _ffn_kernel(x_ref, g_ref, wg_ref, wu_ref, wd_ref, o_ref, hn_ref):
    @pl.when(pl.program_id(1) == 0)
    def _():
        x = x_ref[...]
        hn_ref[...] = _rms_norm_f32(x, g_ref[...]).astype(BF16)
        o_ref[...] = x

    hn = hn_ref[...]
    a = jnp.dot(hn, wg_ref[...].astype(BF16), preferred_element_type=F32)
    b = jnp.dot(hn, wu_ref[...].astype(BF16), preferred_element_type=F32)
    h = (jax.nn.silu(a) * b).astype(BF16)
    o_ref[...] += jnp.dot(h, wd_ref[...].astype(BF16), preferred_element_type=F32)


def _ffn(x, g, wg, wu, wd):
    m, d = x.shape
    f = wg.shape[1]
    tm = ROW_TILE
    tf = W_COL_TILE
    return pl.pallas_call(
        _ffn_kernel,
        grid=(m // tm, f // tf),
        in_specs=[pl.BlockSpec((tm, d), lambda i, j: (i, 0)),
                  pl.BlockSpec((1, d), lambda i, j: (0, 0)),
                  pl.BlockSpec((d, tf), lambda i, j: (0, j)),
                  pl.BlockSpec((d, tf), lambda i, j: (0, j)),
                  pl.BlockSpec((tf, d), lambda i, j: (j, 0))],
        out_specs=pl.BlockSpec((tm, d), lambda i, j: (i, 0)),
        out_shape=jax.ShapeDtypeStruct((m, d), F32),
        scratch_shapes=[pltpu.VMEM((tm, d), BF16)],
        compiler_params=_cparams("arbitrary", "arbitrary"),
        name="ffn",
    )(x, g, wg, wu, wd)


def _router_kernel(n_experts, x_ref, g_ref, whi_ref, wlo_ref, hn_ref, meta_ref, cnt_ref, run_ref):
    @pl.when(pl.program_id(0) == 0)
    def _():
        run_ref[...] = jnp.zeros_like(run_ref)

    tm = x_ref.shape[0]
    hn = _rms_norm_f32(x_ref[...], g_ref[...])
    hn_hi = hn.astype(BF16)
    hn_hi32 = hn_hi.astype(F32)
    hn_ref[...] = _pack_bf16_pairs(hn_hi32)
    hn_lo = (hn - hn_hi32).astype(BF16)
    logits = (jnp.dot(hn_hi, whi_ref[...], preferred_element_type=F32)
              + (jnp.dot(hn_lo, whi_ref[...], preferred_element_type=F32)
                 + jnp.dot(hn_hi, wlo_ref[...], preferred_element_type=F32)))
    lane = lax.broadcasted_iota(jnp.int32, (tm, LANES), 1).astype(F32)
    neg = jnp.float32(-jnp.inf)
    lg = jnp.where(lane < n_experts, logits, neg)
    m1 = jnp.max(lg, axis=1, keepdims=True)
    i1 = jnp.min(jnp.where(lg == m1, lane, float(LANES)), axis=1, keepdims=True)
    lg2 = jnp.where(lane == i1, neg, lg)
    m2 = jnp.max(lg2, axis=1, keepdims=True)
    i2 = jnp.min(jnp.where(lg2 == m2, lane, float(LANES)), axis=1, keepdims=True)
    ex = jnp.exp(m2 - m1)
    g1 = 1.0 / (1.0 + ex)
    g2 = ex / (1.0 + ex)

    sel1 = lane == i1
    sel2 = lane == i2
    onehot = jnp.where(sel1 | sel2, 1.0, 0.0)
    rr = lax.broadcasted_iota(jnp.int32, (tm, tm), 0)
    cc = lax.broadcasted_iota(jnp.int32, (tm, tm), 1)
    tri = jnp.where(rr > cc, 1.0, 0.0).astype(BF16)
    before = jnp.dot(tri, onehot.astype(BF16), preferred_element_type=F32) + run_ref[0:1, :]
    rank1 = jnp.sum(jnp.where(sel1, before, 0.0), axis=1, keepdims=True)
    rank2 = jnp.sum(jnp.where(sel2, before, 0.0), axis=1, keepdims=True)
    run = run_ref[0:1, :] + jnp.sum(onehot, axis=0, keepdims=True)
    run_ref[...] = jnp.broadcast_to(run, run_ref.shape)
    cnt_ref[...] = jnp.broadcast_to(run, cnt_ref.shape)

    meta = jnp.where(lane == 0, i1, 0.0)
    meta = jnp.where(lane == 1, i2, meta)
    meta = jnp.where(lane == 2, g1, meta)
    meta = jnp.where(lane == 3, g2, meta)
    meta = jnp.where(lane == 4, rank1, meta)
    meta = jnp.where(lane == 5, rank2, meta)
    meta_ref[...] = meta


def _router(x, g, w_router_padded, n_experts):
    m, d = x.shape
    tm = ROUTER_ROW_TILE
    w_hi = w_router_padded.astype(BF16)
    w_lo = (w_router_padded - w_hi.astype(F32)).astype(BF16)
    return pl.pallas_call(
        functools.partial(_router_kernel, n_experts),
        grid=(m // tm,),
        in_specs=[pl.BlockSpec((tm, d), lambda i: (i, 0)),
                  pl.BlockSpec((1, d), lambda i: (0, 0)),
                  pl.BlockSpec((d, LANES), lambda i: (0, 0)),
                  pl.BlockSpec((d, LANES), lambda i: (0, 0))],
        out_specs=[pl.BlockSpec((tm, d // 2), lambda i: (i, 0)),
                   pl.BlockSpec((tm, LANES), lambda i: (i, 0)),
                   pl.BlockSpec((SUBLANES, LANES), lambda i: (0, 0))],
        out_shape=[jax.ShapeDtypeStruct((m, d // 2), jnp.uint32),
                   jax.ShapeDtypeStruct((m, LANES), F32),
                   jax.ShapeDtypeStruct((SUBLANES, LANES), F32)],
        scratch_shapes=[pltpu.VMEM((SUBLANES, LANES), F32)],
        compiler_params=_cparams("arbitrary"),
        name="router",
    )(x, g, w_hi, w_lo)


def _row_copy(src_hbm, idx, dst_ref, r, sem):
    return pltpu.make_async_copy(src_hbm.at[pl.ds(idx, 1)], dst_ref.at[pl.ds(r, 1)], sem)


def _experts_kernel(rows_per_step, be_ref, nv_ref, nused_ref, off_ref, stok_ref,
                    hn_hbm, wg_ref, wu_ref, wd_ref, o_ref,
                    xbuf, xb_ref, acc_ref, sem):
    t = pl.program_id(0)
    j = pl.program_id(1)
    nused = nused_ref[0]
    tr = xb_ref.shape[0]
    gr = xbuf.shape[0]
    dh = xbuf.shape[1]

    @pl.when(j == 0)
    def _():
        acc_ref[...] = jnp.zeros_like(acc_ref)

    @pl.when((t == 0) & (j == 0))
    def _():
        off0 = off_ref[0]

        def issue(r, c):
            _row_copy(hn_hbm, stok_ref[off0 + r], xbuf, r, sem).start()
            return c

        lax.fori_loop(0, gr, issue, 0, unroll=8)

    @pl.when((j == 0) & (t <= nused))
    def _():
        pltpu.make_async_copy(hn_hbm.at[pl.ds(0, gr)], xbuf, sem).wait()

    @pl.when((j == 0) & (t < nused))
    def _():
        lo, hi = _unpack_bf16_pairs(xbuf[0:tr, :])
        xb_ref[:, 0:dh] = lo.astype(BF16)
        xb_ref[:, dh:] = hi.astype(BF16)

    def step(n_rows):
        base = off_ref[t + 1] + j * rows_per_step
        for k in range(rows_per_step):
            _row_copy(hn_hbm, stok_ref[base + k], xbuf, j * rows_per_step + k, sem).start()
        rows = pl.ds(0, n_rows)
        x = xb_ref[rows, :]
        a = jnp.dot(x, wg_ref[...].astype(BF16), preferred_element_type=F32)
        b = jnp.dot(x, wu_ref[...].astype(BF16), preferred_element_type=F32)
        h = (jax.nn.silu(a) * b).astype(BF16)
        acc_ref[rows, :] += jnp.dot(h, wd_ref[...].astype(BF16), preferred_element_type=F32)

    slab = tr // MOE_ROW_SLABS
    n_slabs = (nv_ref[t] + (slab - 1)) // slab
    for q in range(1, MOE_ROW_SLABS + 1):
        @pl.when((t < nused) & (jnp.maximum(n_slabs, 1) == q))
        def _(q=q):
            step(q * slab)

    @pl.when(j == pl.num_programs(1) - 1)
    def _():
        o_ref[...] = _pack_bf16_pairs(acc_ref[...].astype(BF16).astype(F32))


def _experts(hn_packed, sorted_tok, tile_off, block_e, n_valid, n_used, wg, wu, wd):
    dh = hn_packed.shape[1]
    d = 2 * dh
    f = wg.shape[2]
    tr = MOE_ROW_TILE
    tf = MOE_COL_TILE
    assert tr % (MOE_ROW_SLABS * 2 * SUBLANES) == 0 and f % tf == 0
    n_tiles = block_e.shape[0]
    nj = f // tf
    rows_per_step = -(-tr // nj)
    gr = rows_per_step * nj
    stok = jnp.pad(sorted_tok, (0, gr))
    offs = jnp.pad(jnp.clip(tile_off, 0, sorted_tok.shape[0]), (0, 1))

    def tile(t, nu):
        return jnp.minimum(t, jnp.maximum(nu[0] - 1, 0))

    def col(t, j, nu):
        return jnp.where(t < nu[0], j, nj - 1)

    wspec = lambda shape, imap: pl.BlockSpec(
        shape, lambda t, j, be, nv, nu, off, st: imap(be[tile(t, nu)], col(t, j, nu)))
    return pl.pallas_call(
        functools.partial(_experts_kernel, rows_per_step),
        grid_spec=pltpu.PrefetchScalarGridSpec(
            num_scalar_prefetch=5,
            grid=(n_tiles + 1, nj),
            in_specs=[pl.BlockSpec(memory_space=pl.ANY),
                      wspec((None, d, tf), lambda e, c: (e, 0, c)),
                      wspec((None, d, tf), lambda e, c: (e, 0, c)),
                      wspec((None, tf, d), lambda e, c: (e, c, 0))],
            out_specs=pl.BlockSpec((tr, dh), lambda t, j, *_: (t, 0)),
            scratch_shapes=[pltpu.VMEM((gr, dh), jnp.uint32), pltpu.VMEM((tr, d), BF16),
                            pltpu.VMEM((tr, d), F32),
                            pltpu.SemaphoreType.DMA(())]),
        out_shape=jax.ShapeDtypeStruct(((n_tiles + 1) * tr, dh), jnp.uint32),
        compiler_params=_cparams("arbitrary", "arbitrary"),
        name="moe_experts",
    )(block_e, jnp.pad(n_valid, (0, 1)), n_used, offs, stok, hn_packed, wg, wu, wd)


def _combine_kernel(n_first, d0_ref, d1_ref, d0n_ref, d1n_ref, x_ref, meta_ref, g_ref, yb_hbm,
                    oa_ref, ob_ref, y0, y1, sem):
    tm = x_ref.shape[0]
    i = pl.program_id(0)
    slot = lax.rem(i, 2)

    def row_copies(da_ref, db_ref, s, r):
        return (_row_copy(yb_hbm, da_ref[0, 0, r], y0.at[s], r, sem.at[0, s]),
                _row_copy(yb_hbm, db_ref[0, 0, r], y1.at[s], r, sem.at[1, s]))

    def wait_tile(s):
        pltpu.make_async_copy(yb_hbm.at[pl.ds(0, tm)], y0.at[s], sem.at[0, s]).wait()
        pltpu.make_async_copy(yb_hbm.at[pl.ds(0, tm)], y1.at[s], sem.at[1, s]).wait()

    @pl.when(i == 0)
    def _():
        def issue(r, c):
            for cp in row_copies(d0_ref, d1_ref, 0, r):
                cp.start()
            return c

        lax.fori_loop(0, tm, issue, 0, unroll=8)

    wait_tile(slot)
    for r in range(tm):
        for cp in row_copies(d0n_ref, d1n_ref, 1 - slot, r):
            cp.start()
    meta = meta_ref[...]
    g0, g1 = meta[:, 2:3], meta[:, 3:4]
    lo0, hi0 = _unpack_bf16_pairs(y0[slot])
    lo1, hi1 = _unpack_bf16_pairs(y1[slot])
    moe = jnp.concatenate([lo0 * g0 + lo1 * g1, hi0 * g0 + hi1 * g1], axis=1)
    y = _rms_norm_f32(x_ref[...] + moe, g_ref[...])

    @pl.when(i < n_first)
    def _():
        oa_ref[...] = y

    @pl.when(i >= n_first)
    def _():
        ob_ref[...] = y

    @pl.when(i == pl.num_programs(0) - 1)
    def _():
        wait_tile(1 - slot)


def _combine(x, meta, g, yb, dest0, dest1, m_first):
    m, d = x.shape
    tm = COMBINE_ROW_TILE
    n_tiles = m // tm
    smem_blk = pl.BlockSpec((1, 1, tm), lambda i: (i, 0, 0), memory_space=pltpu.SMEM)
    smem_next = pl.BlockSpec((1, 1, tm), lambda i: (jnp.minimum(i + 1, n_tiles - 1), 0, 0),
                             memory_space=pltpu.SMEM)
    oa_spec, ob_spec = _two_part_specs(m_first // tm, (tm, d), lambda: 0)
    d0 = dest0.reshape(n_tiles, 1, tm)
    d1 = dest1.reshape(n_tiles, 1, tm)
    return pl.pallas_call(
        functools.partial(_combine_kernel, m_first // tm),
        grid=(n_tiles,),
        in_specs=[smem_blk, smem_blk, smem_next, smem_next,
                  pl.BlockSpec((tm, d), lambda i: (i, 0)),
                  pl.BlockSpec((tm, LANES), lambda i: (i, 0)),
                  pl.BlockSpec((1, d), lambda i: (0, 0)),
                  pl.BlockSpec(memory_space=pl.ANY)],
        out_specs=[oa_spec, ob_spec],
        out_shape=[jax.ShapeDtypeStruct((m_first, d), F32),
                   jax.ShapeDtypeStruct((m - m_first, d), F32)],
        scratch_shapes=[pltpu.VMEM((2, tm, d // 2), jnp.uint32),
                        pltpu.VMEM((2, tm, d // 2), jnp.uint32),
                        pltpu.SemaphoreType.DMA((2, 2))],
        compiler_params=_cparams("arbitrary"),
        name="moe_combine",
    )(d0, d1, d0, d1, x, meta, g, yb)


def _sample_history(dec_seq, *states):
    hist = jnp.concatenate(states, axis=1)
    b, r, c = hist.shape
    assert r <= dec_seq
    return jnp.pad(hist, ((0, 0), (0, dec_seq - r), (0, 0))).reshape(b * dec_seq, c)


def _seq_tails(a, col0, ncol, width, batch, seq, dec_batch, dec_seq):
    mp = batch * seq
    tails = [lax.slice(a, ((b + 1) * seq - width, col0), ((b + 1) * seq, col0 + ncol))
             for b in range(batch)]
    smp = lax.slice(a, (mp, col0), (mp + dec_batch * dec_seq, col0 + ncol))
    smp = smp.reshape(dec_batch, dec_seq, ncol)[:, dec_seq - width:]
    return jnp.stack(tails, axis=0), smp


def kernel(x_prompt, x_sample, state_lru_h, state_lru_conv, state_sconv, norm_mix, norm_ffn, norm_final, lru_w_in, lru_conv_w, lru_conv_b, lru_w_rgate, lru_b_rgate, lru_w_igate, lru_b_igate, lru_lambda, lru_w_out, sc_w_in, sc_conv_w, sc_w_out, ffn_w_gate, ffn_w_up, ffn_w_down, moe_w_router, moe_w_gate, moe_w_up, moe_w_down):
    batch, seq, d = x_prompt.shape
    dec_batch, dec_seq, _ = x_sample.shape
    mp, ms = batch * seq, dec_batch * dec_seq
    m = mp + ms
    depth = norm_mix.shape[0]
    n_experts = moe_w_router.shape[2]
    conv_a = lru_conv_w.shape[1]
    conv_b = sc_conv_w.shape[1]
    assert depth == 2 and lru_w_in.shape[0] == 1 and sc_w_in.shape[0] == 1
    assert seq & (seq - 1) == 0 and dec_seq & (dec_seq - 1) == 0
    assert seq % ROW_TILE == 0 and ms == ROW_TILE and ROW_TILE % dec_seq == 0
    assert dec_seq >= conv_a - 1 and dec_seq >= conv_b - 1 and dec_seq == SUBLANES
    assert n_experts <= LANES and m % ROUTER_ROW_TILE == 0 and m % COMBINE_ROW_TILE == 0
    n_prompt_tiles = mp // ROW_TILE

    xp = x_prompt.reshape(mp, d)
    xs_in = x_sample.reshape(ms, d)
    row = lambda v: v.reshape(1, -1)

    d_rnn = lru_w_out.shape[1]
    assert d_rnn % COL_TILE == 0
    proj = _norm_matmul(xp, xs_in, row(norm_mix[0]), lru_w_in[0], n_gelu_cols=d_rnn)
    hist = _sample_history(dec_seq, state_lru_conv[0], state_lru_h[0][:, None, :])
    y, h_last = _rglru_core(proj, hist, lru_conv_w[0], row(lru_conv_b[0]), row(lru_b_rgate[0]),
                           row(lru_b_igate[0]), row(lru_lambda[0]),
                           lru_w_rgate[0].astype(BF16), lru_w_igate[0].astype(BF16),
                           n_prompt_tiles, seq // ROW_TILE, dec_seq)
    x = _matmul_residual(y, lru_w_out[0].astype(BF16), xp, xs_in)

    x = _ffn(x, row(norm_ffn[0]), ffn_w_gate[0], ffn_w_up[0], ffn_w_down[0])

    hist = _sample_history(dec_seq, state_sconv[0])
    zb, cv_tile_tails, cv_smp = _sc_mixer(x, row(norm_mix[1]), sc_w_in[0], sc_conv_w[0], hist,
                                          n_prompt_tiles, seq // ROW_TILE, dec_seq)
    x = _matmul_residual(zb, sc_w_out[0].astype(BF16), x, x)

    w_router = jnp.pad(moe_w_router[0], ((0, 0), (0, LANES - n_experts)))
    hn, meta, counts = _router(x, row(norm_ffn[1]), w_router, n_experts)
    tr = MOE_ROW_TILE
    n_tiles = -(-(m * TOP_K) // tr) + n_experts
    counts = counts[0, :n_experts].astype(jnp.int32)
    k_tiles = (counts + tr - 1) // tr
    share = jnp.maximum((counts + k_tiles - 1) // jnp.maximum(k_tiles, 1), 1)
    padded = k_tiles * tr
    pad_end = jnp.cumsum(padded)
    pad_start = pad_end - padded
    e0 = meta[:, 0].astype(jnp.int32)
    e1 = meta[:, 1].astype(jnp.int32)

    def sorted_row(e, rank):
        tile_in_expert = rank // share[e]
        return pad_start[e] + tile_in_expert * tr + (rank - tile_in_expert * share[e])

    dest0 = sorted_row(e0, meta[:, 4].astype(jnp.int32))
    dest1 = sorted_row(e1, meta[:, 5].astype(jnp.int32))
    order = jnp.argsort(jnp.stack([e0, e1], axis=1).reshape(-1), stable=True)
    sorted_tok = (order // TOP_K).astype(jnp.int32)
    start = jnp.cumsum(counts) - counts
    block_start = jnp.arange(n_tiles, dtype=jnp.int32) * tr
    block_e = jnp.sum((pad_end[None, :] <= block_start[:, None]).astype(jnp.int32), axis=1)
    block_e = jnp.minimum(block_e, n_experts - 1)
    in_expert = (block_start - pad_start[block_e]) // tr * share[block_e]
    n_valid = jnp.clip(counts[block_e] - in_expert, 0, share[block_e]).astype(jnp.int32)
    n_used = (pad_end[-1:] // tr).astype(jnp.int32)
    yb = _experts(hn, sorted_tok, (start[block_e] + in_expert).astype(jnp.int32), block_e,
                  n_valid, n_used, moe_w_gate[0], moe_w_up[0], moe_w_down[0])
    y_prompt, y_sample = _combine(x, meta, row(norm_final), yb, dest0, dest1, mp)

    tails = functools.partial(_seq_tails, batch=batch, seq=seq, dec_batch=dec_batch, dec_seq=dec_seq)
    groups_per_seq = seq // SUBLANES
    h_p = h_last[groups_per_seq - 1:mp // SUBLANES:groups_per_seq]
    h_s = h_last[mp // SUBLANES:]
    u_p, u_s = tails(proj, d_rnn, d_rnn, conv_a - 1)
    tiles_per_seq = seq // ROW_TILE
    cv_p = cv_tile_tails.reshape(m // ROW_TILE, SUBLANES, -1)[
        tiles_per_seq - 1:n_prompt_tiles:tiles_per_seq, SUBLANES - (conv_b - 1):]
    cv_s = cv_smp.reshape(dec_batch, dec_seq, -1)[:, dec_seq - (conv_b - 1):]
    return (y_prompt.reshape(batch, seq, d), y_sample.reshape(dec_batch, dec_seq, d),
            h_p[None], u_p[None], cv_p[None], h_s[None], u_s[None], cv_s[None])
```

```python
import functools

import jax
import jax.numpy as jnp
from jax import lax
from jax.experimental import pallas as pl
from jax.experimental.pallas import tpu as pltpu

F32 = jnp.float32
BF16 = jnp.bfloat16

EPS = 1e-6
RG_C = 8.0
TOP_K = 2

LANES = 128
SUBLANES = 8
ROW_TILE = 1024
OUT_ROW_TILE = 512
ROUTER_ROW_TILE = 512
COL_TILE = 512
W_COL_TILE = 256
MOE_ROW_TILE = 768
MOE_ROW_SLABS = 4
MOE_COL_TILE = 512
COMBINE_ROW_TILE = 256
SCAN_CHUNK = 128
VMEM_LIMIT = 60 * 1024 * 1024


def _cparams(*sem):
    return pltpu.CompilerParams(dimension_semantics=sem, vmem_limit_bytes=VMEM_LIMIT)


def _rms_norm_f32(x, g):
    y = x * lax.rsqrt(jnp.mean(x * x, axis=-1, keepdims=True) + EPS)
    return y * g


BF16_HIGH_HALF = 0xFFFF0000
BF16_BITS = 16


def _pack_bf16_pairs(x_rounded):
    bits = pltpu.bitcast(x_rounded, jnp.uint32)
    half = x_rounded.shape[1] // 2
    return (bits[:, half:] & jnp.uint32(BF16_HIGH_HALF)) | (bits[:, :half] >> BF16_BITS)


def _unpack_bf16_pairs(w):
    return (pltpu.bitcast(w << BF16_BITS, F32),
            pltpu.bitcast(w & jnp.uint32(BF16_HIGH_HALF), F32))


def _two_part_specs(n_first, block, col_of, second_mode=None):
    first = pl.BlockSpec(block, lambda i, *a: (jnp.minimum(i, n_first - 1), col_of(*a)))
    second = pl.BlockSpec(block, lambda i, *a: (jnp.maximum(i - n_first, 0), col_of(*a)),
                          pipeline_mode=second_mode)
    return first, second


def _norm_matmul_kernel(n_first, n_gelu, xa_ref, xb_ref, g_ref, w_ref, o_ref, hn_ref):
    i = pl.program_id(0)
    j = pl.program_id(1)

    def project():
        return jnp.dot(hn_ref[...], w_ref[...].astype(BF16), preferred_element_type=F32)

    @pl.when((j == 0) & (i < n_first))
    def _():
        hn_ref[...] = _rms_norm_f32(xa_ref[...], g_ref[...]).astype(BF16)

    @pl.when((j == 0) & (i >= n_first))
    def _():
        hn_ref[...] = _rms_norm_f32(xb_ref[...], g_ref[...]).astype(BF16)

    @pl.when(j < n_gelu)
    def _():
        o_ref[...] = jax.nn.gelu(project())

    @pl.when(j >= n_gelu)
    def _():
        o_ref[...] = project()


def _norm_matmul(xa, xb, g, w, n_gelu_cols):
    d = xa.shape[1]
    m = xa.shape[0] + xb.shape[0]
    n = w.shape[1]
    assert xb.shape[0] == ROW_TILE
    xa_spec, xb_spec = _two_part_specs(xa.shape[0] // ROW_TILE, (ROW_TILE, d), lambda j: 0,
                                       second_mode=pl.Buffered(1))
    return pl.pallas_call(
        functools.partial(_norm_matmul_kernel, xa.shape[0] // ROW_TILE, n_gelu_cols // COL_TILE),
        grid=(m // ROW_TILE, n // COL_TILE),
        in_specs=[xa_spec, xb_spec,
                  pl.BlockSpec((1, d), lambda i, j: (0, 0)),
                  pl.BlockSpec((d, COL_TILE), lambda i, j: (0, j))],
        out_specs=pl.BlockSpec((ROW_TILE, COL_TILE), lambda i, j: (i, j)),
        out_shape=jax.ShapeDtypeStruct((m, n), F32),
        scratch_shapes=[pltpu.VMEM((ROW_TILE, d), BF16)],
        compiler_params=_cparams("arbitrary", "arbitrary"),
        name="norm_matmul",
    )(xa, xb, g, w)


def _sc_mixer_kernel(n_prompt_tiles, tiles_per_seq, seq_s,
                     x_ref, g_ref, wb_ref, wc_ref, wv_ref, cw_ref, hist_ref,
                     zb_ref, tail_ref, cvs_ref, hn_ref, carry_ref):
    i = pl.program_id(0)
    j = pl.program_id(1)
    t_rows = x_ref.shape[0]
    is_sample = i >= n_prompt_tiles
    starts_seq = is_sample | (lax.rem(i, tiles_per_seq) == 0)

    @pl.when(j == 0)
    def _():
        hn_ref[...] = _rms_norm_f32(x_ref[...], g_ref[...]).astype(BF16)

    @pl.when((i == 0) & (j == 0))
    def _():
        carry_ref[...] = jnp.zeros_like(carry_ref)

    def mixer(masked):
        hn = hn_ref[...]
        cw = cw_ref[...]
        c = jnp.dot(hn, wc_ref[...].astype(BF16), preferred_element_type=F32)
        v = jnp.dot(hn, wv_ref[...].astype(BF16), preferred_element_type=F32)
        cv = c * v
        prev = jnp.where(starts_seq, 0.0, carry_ref[j])
        ext = jnp.concatenate([prev, cv], axis=0)
        w1 = cw.shape[0] - 1
        if masked:
            t = _pos_in_seq(0, t_rows, seq_s)
            hist = hist_ref[...]

        def tap(s):
            shifted = pltpu.roll(ext, s, axis=0)[SUBLANES:]
            if not masked:
                return shifted
            return jnp.where(t >= s, shifted, _history_rows(hist, w1 - s))

        z = tap(2) * cw[0:1] + tap(1) * cw[1:2]
        z = z + cv * cw[2:3]
        b = jnp.dot(hn, wb_ref[...].astype(BF16), preferred_element_type=F32)
        zb_ref[...] = (b * z).astype(BF16)
        last = cv[t_rows - SUBLANES:, :]
        carry_ref[j] = last
        tail_ref[...] = last
        return cv

    @pl.when(is_sample)
    def _():
        cvs_ref[...] = mixer(True)

    @pl.when(jnp.logical_not(is_sample))
    def _():
        mixer(False)


def _sc_mixer(x, g, w, conv_w, hist, n_prompt_tiles, tiles_per_seq, seq_s):
    m, d = x.shape
    dc = w.shape[1] // 3
    tn = W_COL_TILE
    nj = dc // tn
    n_tiles = m // ROW_TILE
    ms = hist.shape[0]
    assert ms == ROW_TILE
    wspec = lambda off: pl.BlockSpec((d, tn), lambda i, j: (0, off * nj + j))
    sample_col = lambda i, j: (0, jnp.where(i >= n_prompt_tiles, j, 0))
    return pl.pallas_call(
        functools.partial(_sc_mixer_kernel, n_prompt_tiles, tiles_per_seq, seq_s),
        grid=(n_tiles, nj),
        in_specs=[pl.BlockSpec((ROW_TILE, d), lambda i, j: (i, 0)),
                  pl.BlockSpec((1, d), lambda i, j: (0, 0)),
                  wspec(0), wspec(1), wspec(2),
                  pl.BlockSpec((conv_w.shape[0], tn), lambda i, j: (0, j)),
                  pl.BlockSpec((ROW_TILE, tn), sample_col)],
        out_specs=[pl.BlockSpec((ROW_TILE, tn), lambda i, j: (i, j)),
                   pl.BlockSpec((SUBLANES, tn), lambda i, j: (i, j)),
                   pl.BlockSpec((ROW_TILE, tn), sample_col)],
        out_shape=[jax.ShapeDtypeStruct((m, dc), BF16),
                   jax.ShapeDtypeStruct((n_tiles * SUBLANES, dc), F32),
                   jax.ShapeDtypeStruct((ms, dc), F32)],
        scratch_shapes=[pltpu.VMEM((ROW_TILE, d), BF16),
                        pltpu.VMEM((nj, SUBLANES, tn), F32)],
        compiler_params=_cparams("arbitrary", "arbitrary"),
        name="sc_mixer",
    )(x, g, w, w, w, conv_w, hist)


def _matmul_residual_kernel(n_first, y_ref, w_ref, ra_ref, rb_ref, o_ref):
    o_ref[...] = jnp.dot(y_ref[...], w_ref[...], preferred_element_type=F32)
    i = pl.program_id(0)

    @pl.when(i < n_first)
    def _():
        o_ref[...] += ra_ref[...]

    @pl.when(i >= n_first)
    def _():
        o_ref[...] += rb_ref[...]


def _matmul_residual(y, w, res_a, res_b):
    m, k = y.shape
    n = w.shape[1]
    tm = OUT_ROW_TILE
    ra_spec, rb_spec = _two_part_specs(res_a.shape[0] // tm, (tm, n), lambda: 0)
    return pl.pallas_call(
        functools.partial(_matmul_residual_kernel, res_a.shape[0] // tm),
        grid=(m // tm,),
        in_specs=[pl.BlockSpec((tm, k), lambda i: (i, 0)),
                  pl.BlockSpec((k, n), lambda i: (0, 0)),
                  ra_spec, rb_spec],
        out_specs=pl.BlockSpec((tm, n), lambda i: (i, 0)),
        out_shape=jax.ShapeDtypeStruct((m, n), F32),
        compiler_params=_cparams("arbitrary"),
        name="matmul_residual",
    )(y, w, res_a, res_b)


def _pos_in_seq(r0, rows, seq_len):
    return (r0 + lax.broadcasted_iota(jnp.int32, (rows, 1), 0)) & (seq_len - 1)


def _sigmoid(x):
    return 0.5 * jnp.tanh(0.5 * x) + 0.5


def _run_chunks(is_sample, n_chunks, body):
    @pl.when(is_sample)
    def _():
        lax.fori_loop(0, n_chunks, lambda k, c: body(True, k) or c, 0)

    @pl.when(jnp.logical_not(is_sample))
    def _():
        lax.fori_loop(0, n_chunks, lambda k, c: body(False, k) or c, 0)


def _history_rows(hist_chunk, back):
    n = hist_chunk.shape[0]
    return hist_chunk if back == 0 else pltpu.roll(hist_chunk, n - back, axis=0)


def _rglru_kernel(n_prompt_tiles, tiles_per_seq, seq_s,
                  gate_ref, u_ref, hist_ref,
                  cw_ref, cb_ref, br_ref, bi_ref, lam_ref, wr_ref, wi_ref,
                  y_ref, hl_ref, ubuf, uc_ref, rp_ref, ip_ref, hc_ref):
    i = pl.program_id(1)
    cblk = u_ref.shape[1]
    is_sample = i >= n_prompt_tiles
    w1 = cw_ref.shape[0] - 1

    @pl.when(is_sample | (lax.rem(i, tiles_per_seq) == 0))
    def _():
        ubuf[0:SUBLANES, :] = jnp.zeros((SUBLANES, cblk), F32)
        hc_ref[...] = jnp.zeros_like(hc_ref)

    ubuf[SUBLANES:, :] = u_ref[...]
    cw = cw_ref[...]
    cb = cb_ref[...]
    rc = SCAN_CHUNK

    def conv_chunk(masked, k):
        r0 = pl.multiple_of(k * rc, rc)
        rows = pl.ds(r0, rc)
        ext = ubuf[pl.ds(r0, rc + SUBLANES), :]
        if masked:
            t = _pos_in_seq(r0, rc, seq_s)
            hist = hist_ref[rows, :]

        def tap(s):
            shifted = pltpu.roll(ext, s, axis=0)[SUBLANES:]
            if not masked:
                return shifted
            return jnp.where(t >= s, shifted, _history_rows(hist, w1 - s))

        conv = tap(3) * cw[0:1] + tap(2) * cw[1:2]
        conv = conv + tap(1) * cw[2:3]
        conv = conv + ext[SUBLANES:] * cw[3:4]
        uc_ref[rows, :] = conv + cb

    _run_chunks(is_sample, ROW_TILE // rc, conv_chunk)

    ub = uc_ref[...].astype(BF16)
    rp_ref[...] = jnp.dot(ub, wr_ref[...], preferred_element_type=F32) + br_ref[...]
    ip_ref[...] = jnp.dot(ub, wi_ref[...], preferred_element_type=F32) + bi_ref[...]

    sp = jax.nn.softplus(-lam_ref[...])
    sub = lax.broadcasted_iota(jnp.int32, (1, SUBLANES, 1), 1)

    def scan_chunk(masked, k):
        r0 = pl.multiple_of(k * rc, rc)
        rows = pl.ds(r0, rc)
        uc = uc_ref[rows, :]
        r = _sigmoid(rp_ref[rows, :])
        ig = _sigmoid(ip_ref[rows, :])
        log_a = (-RG_C * r) * sp
        a = jnp.exp(log_a)
        mult = jnp.sqrt(-jnp.tanh(log_a) * (a * a + 1.0))
        b = mult * ig * uc
        if masked:
            first = _pos_in_seq(r0, rc, seq_s) == 0
            b = jnp.where(first, b + a * _history_rows(hist_ref[rows, :], w1), b)
            a = jnp.where(first, 0.0, a)

        a3 = a.reshape(rc // SUBLANES, SUBLANES, cblk)
        b3 = b.reshape(rc // SUBLANES, SUBLANES, cblk)
        for s in (1, 2, 4):
            a_s = pltpu.roll(a3, s, axis=1)
            b_s = pltpu.roll(b3, s, axis=1)
            m = sub >= s
            b3 = jnp.where(m, a3 * b_s + b3, b3)
            a3 = jnp.where(m, a3 * a_s, a3)

        hc = hc_ref[...]
        hs, lasts = [], []
        for g in range(rc // SUBLANES):
            hg = a3[g] * hc + b3[g]
            lasts.append(hg[SUBLANES - 1:SUBLANES])
            hc = jnp.broadcast_to(lasts[-1], (SUBLANES, cblk))
            hs.append(hg)
        hc_ref[...] = hc
        h = jnp.concatenate(hs, axis=0)
        groups = rc // SUBLANES
        hl_ref[pl.ds(pl.multiple_of(k * groups, groups), groups), :] = jnp.concatenate(lasts, axis=0)
        y_ref[rows, :] = (h * gate_ref[rows, :]).astype(BF16)

    _run_chunks(is_sample, ROW_TILE // rc, scan_chunk)
    ubuf[0:SUBLANES, :] = ubuf[ROW_TILE:ROW_TILE + SUBLANES, :]


def _rglru_core(proj, hist, conv_w, conv_b, b_r, b_i, lam, w_r, w_i, n_prompt_tiles,
                tiles_per_seq, seq_s):
    m = proj.shape[0]
    assert SCAN_CHUNK % (SUBLANES * SUBLANES) == 0
    nb, cblk = w_r.shape[0], w_r.shape[1]
    c = nb * cblk
    n_tiles = m // ROW_TILE
    row_blk = lambda off: pl.BlockSpec((ROW_TILE, cblk), lambda cb, i: (i, off + cb))
    aux_blk = pl.BlockSpec((ROW_TILE, cblk), lambda cb, i: (0, cb))
    vec_blk = lambda rows: pl.BlockSpec((rows, cblk), lambda cb, i: (0, cb))
    w_blk = pl.BlockSpec((None, cblk, cblk), lambda cb, i: (cb, 0, 0))
    out_blk = pl.BlockSpec((ROW_TILE, cblk), lambda cb, i: (i, cb))
    return pl.pallas_call(
        functools.partial(_rglru_kernel, n_prompt_tiles, tiles_per_seq, seq_s),
        grid=(nb, n_tiles),
        in_specs=[row_blk(0), row_blk(nb), aux_blk,
                  vec_blk(conv_w.shape[0]), vec_blk(1), vec_blk(1), vec_blk(1), vec_blk(1),
                  w_blk, w_blk],
        out_specs=[out_blk,
                   pl.BlockSpec((ROW_TILE // SUBLANES, cblk), lambda cb, i: (i, cb))],
        out_shape=[jax.ShapeDtypeStruct((m, c), BF16),
                   jax.ShapeDtypeStruct((m // SUBLANES, c), F32)],
        scratch_shapes=[pltpu.VMEM((ROW_TILE + SUBLANES, cblk), F32),
                        pltpu.VMEM((ROW_TILE, cblk), F32),
                        pltpu.VMEM((ROW_TILE, cblk), F32),
                        pltpu.VMEM((ROW_TILE, cblk), F32),
                        pltpu.VMEM((SUBLANES, cblk), F32)],
        compiler_params=_cparams("arbitrary", "arbitrary"),
        name="rglru_core",
    )(proj, proj, hist, conv_w, conv_b, b_r, b_i, lam, w_r, w_i)


def _ffn_kernel(x_ref, g_ref, wg_ref, wu_ref, wd_ref, o_ref, hn_ref):
    @pl.when(pl.program_id(1) == 0)
    def _():
        x = x_ref[...]
        hn_ref[...] = _rms_norm_f32(x, g_ref[...]).astype(BF16)
        o_ref[...] = x

    hn = hn_ref[...]
    a = jnp.dot(hn, wg_ref[...].astype(BF16), preferred_element_type=F32)
    b = jnp.dot(hn, wu_ref[...].astype(BF16), preferred_element_type=F32)
    h = (jax.nn.silu(a) * b).astype(BF16)
    o_ref[...] += jnp.dot(h, wd_ref[...].astype(BF16), preferred_element_type=F32)


def _ffn(x, g, wg, wu, wd):
    m, d = x.shape
    f = wg.shape[1]
    tm = ROW_TILE
    tf = W_COL_TILE
    return pl.pallas_call(
        _ffn_kernel,
        grid=(m // tm, f // tf),
        in_specs=[pl.BlockSpec((tm, d), lambda i, j: (i, 0)),
                  pl.BlockSpec((1, d), lambda i, j: (0, 0)),
                  pl.BlockSpec((d, tf), lambda i, j: (0, j)),
                  pl.BlockSpec((d, tf), lambda i, j: (0, j)),
                  pl.BlockSpec((tf, d), lambda i, j: (j, 0))],
        out_specs=pl.BlockSpec((tm, d), lambda i, j: (i, 0)),
        out_shape=jax.ShapeDtypeStruct((m, d), F32),
        scratch_shapes=[pltpu.VMEM((tm, d), BF16)],
        compiler_params=_cparams("arbitrary", "arbitrary"),
        name="ffn",
    )(x, g, wg, wu, wd)


def _router_kernel(n_experts, x_ref, g_ref, whi_ref, wlo_ref, hn_ref, meta_ref, cnt_ref, run_ref):
    @pl.when(pl.program_id(0) == 0)
    def _():
        run_ref[...] = jnp.zeros_like(run_ref)

    tm = x_ref.shape[0]
    hn = _rms_norm_f32(x_ref[...], g_ref[...])
    hn_hi = hn.astype(BF16)
    hn_hi32 = hn_hi.astype(F32)
    hn_ref[...] = _pack_bf16_pairs(hn_hi32)
    hn_lo = (hn - hn_hi32).astype(BF16)
    logits = (jnp.dot(hn_hi, whi_ref[...], preferred_element_type=F32)
              + (jnp.dot(hn_lo, whi_ref[...], preferred_element_type=F32)
                 + jnp.dot(hn_hi, wlo_ref[...], preferred_element_type=F32)))
    lane = lax.broadcasted_iota(jnp.int32, (tm, LANES), 1).astype(F32)
    neg = jnp.float32(-jnp.inf)
    lg = jnp.where(lane < n_experts, logits, neg)
    m1 = jnp.max(lg, axis=1, keepdims=True)
    i1 = jnp.min(jnp.where(lg == m1, lane, float(LANES)), axis=1, keepdims=True)
    lg2 = jnp.where(lane == i1, neg, lg)
    m2 = jnp.max(lg2, axis=1, keepdims=True)
    i2 = jnp.min(jnp.where(lg2 == m2, lane, float(LANES)), axis=1, keepdims=True)
    ex = jnp.exp(m2 - m1)
    g1 = 1.0 / (1.0 + ex)
    g2 = ex / (1.0 + ex)

    sel1 = lane == i1
    sel2 = lane == i2
    onehot = jnp.where(sel1 | sel2, 1.0, 0.0)
    rr = lax.broadcasted_iota(jnp.int32, (tm, tm), 0)
    cc = lax.broadcasted_iota(jnp.int32, (tm, tm), 1)
    tri = jnp.where(rr > cc, 1.0, 0.0).astype(BF16)
    before = jnp.dot(tri, onehot.astype(BF16), preferred_element_type=F32) + run_ref[0:1, :]
    rank1 = jnp.sum(jnp.where(sel1, before, 0.0), axis=1, keepdims=True)
    rank2 = jnp.sum(jnp.where(sel2, before, 0.0), axis=1, keepdims=True)
    run = run_ref[0:1, :] + jnp.sum(onehot, axis=0, keepdims=True)
    run_ref[...] = jnp.broadcast_to(run, run_ref.shape)
    cnt_ref[...] = jnp.broadcast_to(run, cnt_ref.shape)

    meta = jnp.where(lane == 0, i1, 0.0)
    meta = jnp.where(lane == 1, i2, meta)
    meta = jnp.where(lane == 2, g1, meta)
    meta = jnp.where(lane == 3, g2, meta)
    meta = jnp.where(lane == 4, rank1, meta)
    meta = jnp.where(lane == 5, rank2, meta)
    meta_ref[...] = meta


def _router(x, g, w_router_padded, n_experts):
    m, d = x.shape
    tm = ROUTER_ROW_TILE
    w_hi = w_router_padded.astype(BF16)
    w_lo = (w_router_padded - w_hi.astype(F32)).astype(BF16)
    return pl.pallas_call(
        functools.partial(_router_kernel, n_experts),
        grid=(m // tm,),
        in_specs=[pl.BlockSpec((tm, d), lambda i: (i, 0)),
                  pl.BlockSpec((1, d), lambda i: (0, 0)),
                  pl.BlockSpec((d, LANES), lambda i: (0, 0)),
                  pl.BlockSpec((d, LANES), lambda i: (0, 0))],
        out_specs=[pl.BlockSpec((tm, d // 2), lambda i: (i, 0)),
                   pl.BlockSpec((tm, LANES), lambda i: (i, 0)),
                   pl.BlockSpec((SUBLANES, LANES), lambda i: (0, 0))],
        out_shape=[jax.ShapeDtypeStruct((m, d // 2), jnp.uint32),
                   jax.ShapeDtypeStruct((m, LANES), F32),
                   jax.ShapeDtypeStruct((SUBLANES, LANES), F32)],
        scratch_shapes=[pltpu.VMEM((SUBLANES, LANES), F32)],
        compiler_params=_cparams("arbitrary"),
        name="router",
    )(x, g, w_hi, w_lo)


def _row_copy(src_hbm, idx, dst_ref, r, sem):
    return pltpu.make_async_copy(src_hbm.at[pl.ds(idx, 1)], dst_ref.at[pl.ds(r, 1)], sem)


def _experts_kernel(rows_per_step, be_ref, nv_ref, nused_ref, off_ref, stok_ref,
                    hn_hbm, wg_ref, wu_ref, wd_ref, o_ref,
                    xbuf, xb_ref, acc_ref, sem):
    t = pl.program_id(0)
    j = pl.program_id(1)
    nused = nused_ref[0]
    tr = xb_ref.shape[0]
    gr = xbuf.shape[0]
    dh = xbuf.shape[1]

    @pl.when(j == 0)
    def _():
        acc_ref[...] = jnp.zeros_like(acc_ref)

    @pl.when((t == 0) & (j == 0))
    def _():
        off0 = off_ref[0]

        def issue(r, c):
            _row_copy(hn_hbm, stok_ref[off0 + r], xbuf, r, sem).start()
            return c

        lax.fori_loop(0, gr, issue, 0, unroll=8)

    @pl.when((j == 0) & (t <= nused))
    def _():
        pltpu.make_async_copy(hn_hbm.at[pl.ds(0, gr)], xbuf, sem).wait()

    @pl.when((j == 0) & (t < nused))
    def _():
        lo, hi = _unpack_bf16_pairs(xbuf[0:tr, :])
        xb_ref[:, 0:dh] = lo.astype(BF16)
        xb_ref[:, dh:] = hi.astype(BF16)

    def step(n_rows):
        base = off_ref[t + 1] + j * rows_per_step
        for k in range(rows_per_step):
            _row_copy(hn_hbm, stok_ref[base + k], xbuf, j * rows_per_step + k, sem).start()
        rows = pl.ds(0, n_rows)
        x = xb_ref[rows, :]
        a = jnp.dot(x, wg_ref[...].astype(BF16), preferred_element_type=F32)
        b = jnp.dot(x, wu_ref[...].astype(BF16), preferred_element_type=F32)
        h = (jax.nn.silu(a) * b).astype(BF16)
        acc_ref[rows, :] += jnp.dot(h, wd_ref[...].astype(BF16), preferred_element_type=F32)

    slab = tr // MOE_ROW_SLABS
    n_slabs = (nv_ref[t] + (slab - 1)) // slab
    for q in range(1, MOE_ROW_SLABS + 1):
        @pl.when((t < nused) & (jnp.maximum(n_slabs, 1) == q))
        def _(q=q):
            step(q * slab)

    @pl.when(j == pl.num_programs(1) - 1)
    def _():
        o_ref[...] = _pack_bf16_pairs(acc_ref[...].astype(BF16).astype(F32))


def _experts(hn_packed, sorted_tok, tile_off, block_e, n_valid, n_used, wg, wu, wd):
    dh = hn_packed.shape[1]
    d = 2 * dh
    f = wg.shape[2]
    tr = MOE_ROW_TILE
    tf = MOE_COL_TILE
    assert tr % (MOE_ROW_SLABS * 2 * SUBLANES) == 0 and f % tf == 0
    n_tiles = block_e.shape[0]
    nj = f // tf
    rows_per_step = -(-tr // nj)
    gr = rows_per_step * nj
    stok = jnp.pad(sorted_tok, (0, gr))
    offs = jnp.pad(jnp.clip(tile_off, 0, sorted_tok.shape[0]), (0, 1))

    def tile(t, nu):
        return jnp.minimum(t, jnp.maximum(nu[0] - 1, 0))

    def col(t, j, nu):
        return jnp.where(t < nu[0], j, nj - 1)

    wspec = lambda shape, imap: pl.BlockSpec(
        shape, lambda t, j, be, nv, nu, off, st: imap(be[tile(t, nu)], col(t, j, nu)))
    return pl.pallas_call(
        functools.partial(_experts_kernel, rows_per_step),
        grid_spec=pltpu.PrefetchScalarGridSpec(
            num_scalar_prefetch=5,
            grid=(n_tiles + 1, nj),
            in_specs=[pl.BlockSpec(memory_space=pl.ANY),
                      wspec((None, d, tf), lambda e, c: (e, 0, c)),
                      wspec((None, d, tf), lambda e, c: (e, 0, c)),
                      wspec((None, tf, d), lambda e, c: (e, c, 0))],
            out_specs=pl.BlockSpec((tr, dh), lambda t, j, *_: (t, 0)),
            scratch_shapes=[pltpu.VMEM((gr, dh), jnp.uint32), pltpu.VMEM((tr, d), BF16),
                            pltpu.VMEM((tr, d), F32),
                            pltpu.SemaphoreType.DMA(())]),
        out_shape=jax.ShapeDtypeStruct(((n_tiles + 1) * tr, dh), jnp.uint32),
        compiler_params=_cparams("arbitrary", "arbitrary"),
        name="moe_experts",
    )(block_e, jnp.pad(n_valid, (0, 1)), n_used, offs, stok, hn_packed, wg, wu, wd)


def _combine_kernel(n_first, d0_ref, d1_ref, d0n_ref, d1n_ref, x_ref, meta_ref, g_ref, yb_hbm,
                    oa_ref, ob_ref, y0, y1, sem):
    tm = x_ref.shape[0]
    i = pl.program_id(0)
    slot = lax.rem(i, 2)

    def row_copies(da_ref, db_ref, s, r):
        return (_row_copy(yb_hbm, da_ref[0, 0, r], y0.at[s], r, sem.at[0, s]),
                _row_copy(yb_hbm, db_ref[0, 0, r], y1.at[s], r, sem.at[1, s]))

    def wait_tile(s):
        pltpu.make_async_copy(yb_hbm.at[pl.ds(0, tm)], y0.at[s], sem.at[0, s]).wait()
        pltpu.make_async_copy(yb_hbm.at[pl.ds(0, tm)], y1.at[s], sem.at[1, s]).wait()

    @pl.when(i == 0)
    def _():
        def issue(r, c):
            for cp in row_copies(d0_ref, d1_ref, 0, r):
                cp.start()
            return c

        lax.fori_loop(0, tm, issue, 0, unroll=8)

    wait_tile(slot)
    for r in range(tm):
        for cp in row_copies(d0n_ref, d1n_ref, 1 - slot, r):
            cp.start()
    meta = meta_ref[...]
    g0, g1 = meta[:, 2:3], meta[:, 3:4]
    lo0, hi0 = _unpack_bf16_pairs(y0[slot])
    lo1, hi1 = _unpack_bf16_pairs(y1[slot])
    moe = jnp.concatenate([lo0 * g0 + lo1 * g1, hi0 * g0 + hi1 * g1], axis=1)
    y = _rms_norm_f32(x_ref[...] + moe, g_ref[...])

    @pl.when(i < n_first)
    def _():
        oa_ref[...] = y

    @pl.when(i >= n_first)
    def _():
        ob_ref[...] = y

    @pl.when(i == pl.num_programs(0) - 1)
    def _():
        wait_tile(1 - slot)


def _combine(x, meta, g, yb, dest0, dest1, m_first):
    m, d = x.shape
    tm = COMBINE_ROW_TILE
    n_tiles = m // tm
    smem_blk = pl.BlockSpec((1, 1, tm), lambda i: (i, 0, 0), memory_space=pltpu.SMEM)
    smem_next = pl.BlockSpec((1, 1, tm), lambda i: (jnp.minimum(i + 1, n_tiles - 1), 0, 0),
                             memory_space=pltpu.SMEM)
    oa_spec, ob_spec = _two_part_specs(m_first // tm, (tm, d), lambda: 0)
    d0 = dest0.reshape(n_tiles, 1, tm)
    d1 = dest1.reshape(n_tiles, 1, tm)
    return pl.pallas_call(
        functools.partial(_combine_kernel, m_first // tm),
        grid=(n_tiles,),
        in_specs=[smem_blk, smem_blk, smem_next, smem_next,
                  pl.BlockSpec((tm, d), lambda i: (i, 0)),
                  pl.BlockSpec((tm, LANES), lambda i: (i, 0)),
                  pl.BlockSpec((1, d), lambda i: (0, 0)),
                  pl.BlockSpec(memory_space=pl.ANY)],
        out_specs=[oa_spec, ob_spec],
        out_shape=[jax.ShapeDtypeStruct((m_first, d), F32),
                   jax.ShapeDtypeStruct((m - m_first, d), F32)],
        scratch_shapes=[pltpu.VMEM((2, tm, d // 2), jnp.uint32),
                        pltpu.VMEM((2, tm, d // 2), jnp.uint32),
                        pltpu.SemaphoreType.DMA((2, 2))],
        compiler_params=_cparams("arbitrary"),
        name="moe_combine",
    )(d0, d1, d0, d1, x, meta, g, yb)


def _sample_history(dec_seq, *states):
    hist = jnp.concatenate(states, axis=1)
    b, r, c = hist.shape
    assert r <= dec_seq
    return jnp.pad(hist, ((0, 0), (0, dec_seq - r), (0, 0))).reshape(b * dec_seq, c)


def _seq_tails(a, col0, ncol, width, batch, seq, dec_batch, dec_seq):
    mp = batch * seq
    tails = [lax.slice(a, ((b + 1) * seq - width, col0), ((b + 1) * seq, col0 + ncol))
             for b in range(batch)]
    smp = lax.slice(a, (mp, col0), (mp + dec_batch * dec_seq, col0 + ncol))
    smp = smp.reshape(dec_batch, dec_seq, ncol)[:, dec_seq - width:]
    return jnp.stack(tails, axis=0), smp


def kernel(x_prompt, x_sample, state_lru_h, state_lru_conv, state_sconv, norm_mix, norm_ffn, norm_final, lru_w_in, lru_conv_w, lru_conv_b, lru_w_rgate, lru_b_rgate, lru_w_igate, lru_b_igate, lru_lambda, lru_w_out, sc_w_in, sc_conv_w, sc_w_out, ffn_w_gate, ffn_w_up, ffn_w_down, moe_w_router, moe_w_gate, moe_w_up, moe_w_down):
    batch, seq, d = x_prompt.shape
    dec_batch, dec_seq, _ = x_sample.shape
    mp, ms = batch * seq, dec_batch * dec_seq
    m = mp + ms
    depth = norm_mix.shape[0]
    n_experts = moe_w_router.shape[2]
    conv_a = lru_conv_w.shape[1]
    conv_b = sc_conv_w.shape[1]
    assert depth == 2 and lru_w_in.shape[0] == 1 and sc_w_in.shape[0] == 1
    assert seq & (seq - 1) == 0 and dec_seq & (dec_seq - 1) == 0
    assert seq % ROW_TILE == 0 and ms == ROW_TILE and ROW_TILE % dec_seq == 0
    assert dec_seq >= conv_a - 1 and dec_seq >= conv_b - 1 and dec_seq == SUBLANES
    assert n_experts <= LANES and m % ROUTER_ROW_TILE == 0 and m % COMBINE_ROW_TILE == 0
    n_prompt_tiles = mp // ROW_TILE

    xp = x_prompt.reshape(mp, d)
    xs_in = x_sample.reshape(ms, d)
    row = lambda v: v.reshape(1, -1)

    d_rnn = lru_w_out.shape[1]
    assert d_rnn % COL_TILE == 0
    proj = _norm_matmul(xp, xs_in, row(norm_mix[0]), lru_w_in[0], n_gelu_cols=d_rnn)
    hist = _sample_history(dec_seq, state_lru_conv[0], state_lru_h[0][:, None, :])
    y, h_last = _rglru_core(proj, hist, lru_conv_w[0], row(lru_conv_b[0]), row(lru_b_rgate[0]),
                           row(lru_b_igate[0]), row(lru_lambda[0]),
                           lru_w_rgate[0].astype(BF16), lru_w_igate[0].astype(BF16),
                           n_prompt_tiles, seq // ROW_TILE, dec_seq)
    x = _matmul_residual(y, lru_w_out[0].astype(BF16), xp, xs_in)

    x = _ffn(x, row(norm_ffn[0]), ffn_w_gate[0], ffn_w_up[0], ffn_w_down[0])

    hist = _sample_history(dec_seq, state_sconv[0])
    zb, cv_tile_tails, cv_smp = _sc_mixer(x, row(norm_mix[1]), sc_w_in[0], sc_conv_w[0], hist,
                                          n_prompt_tiles, seq // ROW_TILE, dec_seq)
    x = _matmul_residual(zb, sc_w_out[0].astype(BF16), x, x)

    w_router = jnp.pad(moe_w_router[0], ((0, 0), (0, LANES - n_experts)))
    hn, meta, counts = _router(x, row(norm_ffn[1]), w_router, n_experts)
    tr = MOE_ROW_TILE
    n_tiles = -(-(m * TOP_K) // tr) + n_experts
    counts = counts[0, :n_experts].astype(jnp.int32)
    k_tiles = (counts + tr - 1) // tr
    share = jnp.maximum((counts + k_tiles - 1) // jnp.maximum(k_tiles, 1), 1)
    padded = k_tiles * tr
    pad_end = jnp.cumsum(padded)
    pad_start = pad_end - padded
    e0 = meta[:, 0].astype(jnp.int32)
    e1 = meta[:, 1].astype(jnp.int32)

    def sorted_row(e, rank):
        tile_in_expert = rank // share[e]
        return pad_start[e] + tile_in_expert * tr + (rank - tile_in_expert * share[e])

    dest0 = sorted_row(e0, meta[:, 4].astype(jnp.int32))
    dest1 = sorted_row(e1, meta[:, 5].astype(jnp.int32))
    order = jnp.argsort(jnp.stack([e0, e1], axis=1).reshape(-1), stable=True)
    sorted_tok = (order // TOP_K).astype(jnp.int32)
    start = jnp.cumsum(counts) - counts
    block_start = jnp.arange(n_tiles, dtype=jnp.int32) * tr
    block_e = jnp.sum((pad_end[None, :] <= block_start[:, None]).astype(jnp.int32), axis=1)
    block_e = jnp.minimum(block_e, n_experts - 1)
    in_expert = (block_start - pad_start[block_e]) // tr * share[block_e]
    n_valid = jnp.clip(counts[block_e] - in_expert, 0, share[block_e]).astype(jnp.int32)
    n_used = (pad_end[-1:] // tr).astype(jnp.int32)
    yb = _experts(hn, sorted_tok, (start[block_e] + in_expert).astype(jnp.int32), block_e,
                  n_valid, n_used, moe_w_gate[0], moe_w_up[0], moe_w_down[0])
    y_prompt, y_sample = _combine(x, meta, row(norm_final), yb, dest0, dest1, mp)

    tails = functools.partial(_seq_tails, batch=batch, seq=seq, dec_batch=dec_batch, dec_seq=dec_seq)
    groups_per_seq = seq // SUBLANES
    h_p = h_last[groups_per_seq - 1:mp // SUBLANES:groups_per_seq]
    h_s = h_last[mp // SUBLANES:]
    u_p, u_s = tails(proj, d_rnn, d_rnn, conv_a - 1)
    tiles_per_seq = seq // ROW_TILE
    cv_p = cv_tile_tails.reshape(m // ROW_TILE, SUBLANES, -1)[
        tiles_per_seq - 1:n_prompt_tiles:tiles_per_seq, SUBLANES - (conv_b - 1):]
    cv_s = cv_smp.reshape(dec_batch, dec_seq, -1)[:, dec_seq - (conv_b - 1):]
    return (y_prompt.reshape(batch, seq, d), y_sample.reshape(dec_batch, dec_seq, d),
            h_p[None], u_p[None], cv_p[None], h_s[None], u_s[None], cv_s[None])
```

```python
import functools

import jax
import jax.numpy as jnp
from jax import lax
from jax.experimental import pallas as pl
from jax.experimental.pallas import tpu as pltpu

F32 = jnp.float32
BF16 = jnp.bfloat16

EPS = 1e-6
RG_C = 8.0
TOP_K = 2

LANES = 128
SUBLANES = 8
ROW_TILE = 1024
OUT_ROW_TILE = 512
ROUTER_ROW_TILE = 512
COL_TILE = 512
W_COL_TILE = 256
MOE_ROW_TILE = 768
MOE_ROW_SLABS = 4
MOE_COL_TILE = 512
COMBINE_ROW_TILE = 256
SCAN_CHUNK = 128
VMEM_LIMIT = 60 * 1024 * 1024


def _cparams(*sem):
    return pltpu.CompilerParams(dimension_semantics=sem, vmem_limit_bytes=VMEM_LIMIT)


def _rms_norm_f32(x, g):
    y = x * lax.rsqrt(jnp.mean(x * x, axis=-1, keepdims=True) + EPS)
    return y * g


BF16_HIGH_HALF = 0xFFFF0000
BF16_BITS = 16


def _pack_bf16_pairs(x_rounded):
    bits = pltpu.bitcast(x_rounded, jnp.uint32)
    half = x_rounded.shape[1] // 2
    return (bits[:, half:] & jnp.uint32(BF16_HIGH_HALF)) | (bits[:, :half] >> BF16_BITS)


def _unpack_bf16_pairs(w):
    return (pltpu.bitcast(w << BF16_BITS, F32),
            pltpu.bitcast(w & jnp.uint32(BF16_HIGH_HALF), F32))


def _two_part_specs(n_first, block, col_of, second_mode=None):
    first = pl.BlockSpec(block, lambda i, *a: (jnp.minimum(i, n_first - 1), col_of(*a)))
    second = pl.BlockSpec(block, lambda i, *a: (jnp.maximum(i - n_first, 0), col_of(*a)),
                          pipeline_mode=second_mode)
    return first, second


def _norm_matmul_kernel(n_first, n_gelu, xa_ref, xb_ref, g_ref, w_ref, o_ref, hn_ref):
    i = pl.program_id(0)
    j = pl.program_id(1)

    def project():
        return jnp.dot(hn_ref[...], w_ref[...].astype(BF16), preferred_element_type=F32)

    @pl.when((j == 0) & (i < n_first))
    def _():
        hn_ref[...] = _rms_norm_f32(xa_ref[...], g_ref[...]).astype(BF16)

    @pl.when((j == 0) & (i >= n_first))
    def _():
        hn_ref[...] = _rms_norm_f32(xb_ref[...], g_ref[...]).astype(BF16)

    @pl.when(j < n_gelu)
    def _():
        o_ref[...] = jax.nn.gelu(project())

    @pl.when(j >= n_gelu)
    def _():
        o_ref[...] = project()


def _norm_matmul(xa, xb, g, w, n_gelu_cols):
    d = xa.shape[1]
    m = xa.shape[0] + xb.shape[0]
    n = w.shape[1]
    assert xb.shape[0] == ROW_TILE
    xa_spec, xb_spec = _two_part_specs(xa.shape[0] // ROW_TILE, (ROW_TILE, d), lambda j: 0,
                                       second_mode=pl.Buffered(1))
    return pl.pallas_call(
        functools.partial(_norm_matmul_kernel, xa.shape[0] // ROW_TILE, n_gelu_cols // COL_TILE),
        grid=(m // ROW_TILE, n // COL_TILE),
        in_specs=[xa_spec, xb_spec,
                  pl.BlockSpec((1, d), lambda i, j: (0, 0)),
                  pl.BlockSpec((d, COL_TILE), lambda i, j: (0, j))],
        out_specs=pl.BlockSpec((ROW_TILE, COL_TILE), lambda i, j: (i, j)),
        out_shape=jax.ShapeDtypeStruct((m, n), F32),
        scratch_shapes=[pltpu.VMEM((ROW_TILE, d), BF16)],
        compiler_params=_cparams("arbitrary", "arbitrary"),
        name="norm_matmul",
    )(xa, xb, g, w)


def _sc_mixer_kernel(n_prompt_tiles, tiles_per_seq, seq_s,
                     x_ref, g_ref, wb_ref, wc_ref, wv_ref, cw_ref, hist_ref,
                     zb_ref, tail_ref, cvs_ref, hn_ref, carry_ref):
    i = pl.program_id(0)
    j = pl.program_id(1)
    t_rows = x_ref.shape[0]
    is_sample = i >= n_prompt_tiles
    starts_seq = is_sample | (lax.rem(i, tiles_per_seq) == 0)

    @pl.when(j == 0)
    def _():
        hn_ref[...] = _rms_norm_f32(x_ref[...], g_ref[...]).astype(BF16)

    @pl.when((i == 0) & (j == 0))
    def _():
        carry_ref[...] = jnp.zeros_like(carry_ref)

    def mixer(masked):
        hn = hn_ref[...]
        cw = cw_ref[...]
        c = jnp.dot(hn, wc_ref[...].astype(BF16), preferred_element_type=F32)
        v = jnp.dot(hn, wv_ref[...].astype(BF16), preferred_element_type=F32)
        cv = c * v
        prev = jnp.where(starts_seq, 0.0, carry_ref[j])
        ext = jnp.concatenate([prev, cv], axis=0)
        w1 = cw.shape[0] - 1
        if masked:
            t = _pos_in_seq(0, t_rows, seq_s)
            hist = hist_ref[...]

        def tap(s):
            shifted = pltpu.roll(ext, s, axis=0)[SUBLANES:]
            if not masked:
                return shifted
            return jnp.where(t >= s, shifted, _history_rows(hist, w1 - s))

        z = tap(2) * cw[0:1] + tap(1) * cw[1:2]
        z = z + cv * cw[2:3]
        b = jnp.dot(hn, wb_ref[...].astype(BF16), preferred_element_type=F32)
        zb_ref[...] = (b * z).astype(BF16)
        last = cv[t_rows - SUBLANES:, :]
        carry_ref[j] = last
        tail_ref[...] = last
        return cv

    @pl.when(is_sample)
    def _():
        cvs_ref[...] = mixer(True)

    @pl.when(jnp.logical_not(is_sample))
    def _():
        mixer(False)


def _sc_mixer(x, g, w, conv_w, hist, n_prompt_tiles, tiles_per_seq, seq_s):
    m, d = x.shape
    dc = w.shape[1] // 3
    tn = W_COL_TILE
    nj = dc // tn
    n_tiles = m // ROW_TILE
    ms = hist.shape[0]
    assert ms == ROW_TILE
    wspec = lambda off: pl.BlockSpec((d, tn), lambda i, j: (0, off * nj + j))
    sample_col = lambda i, j: (0, jnp.where(i >= n_prompt_tiles, j, 0))
    return pl.pallas_call(
        functools.partial(_sc_mixer_kernel, n_prompt_tiles, tiles_per_seq, seq_s),
        grid=(n_tiles, nj),
        in_specs=[pl.BlockSpec((ROW_TILE, d), lambda i, j: (i, 0)),
                  pl.BlockSpec((1, d), lambda i, j: (0, 0)),
                  wspec(0), wspec(1), wspec(2),
                  pl.BlockSpec((conv_w.shape[0], tn), lambda i, j: (0, j)),
                  pl.BlockSpec((ROW_TILE, tn), sample_col)],
        out_specs=[pl.BlockSpec((ROW_TILE, tn), lambda i, j: (i, j)),
                   pl.BlockSpec((SUBLANES, tn), lambda i, j: (i, j)),
                   pl.BlockSpec((ROW_TILE, tn), sample_col)],
        out_shape=[jax.ShapeDtypeStruct((m, dc), BF16),
                   jax.ShapeDtypeStruct((n_tiles * SUBLANES, dc), F32),
                   jax.ShapeDtypeStruct((ms, dc), F32)],
        scratch_shapes=[pltpu.VMEM((ROW_TILE, d), BF16),
                        pltpu.VMEM((nj, SUBLANES, tn), F32)],
        compiler_params=_cparams("arbitrary", "arbitrary"),
        name="sc_mixer",
    )(x, g, w, w, w, conv_w, hist)


def _matmul_residual_kernel(n_first, y_ref, w_ref, ra_ref, rb_ref, o_ref):
    o_ref[...] = jnp.dot(y_ref[...], w_ref[...], preferred_element_type=F32)
    i = pl.program_id(0)

    @pl.when(i < n_first)
    def _():
        o_ref[...] += ra_ref[...]

    @pl.when(i >= n_first)
    def _():
        o_ref[...] += rb_ref[...]


def _matmul_residual(y, w, res_a, res_b):
    m, k = y.shape
    n = w.shape[1]
    tm = OUT_ROW_TILE
    ra_spec, rb_spec = _two_part_specs(res_a.shape[0] // tm, (tm, n), lambda: 0)
    return pl.pallas_call(
        functools.partial(_matmul_residual_kernel, res_a.shape[0] // tm),
        grid=(m // tm,),
        in_specs=[pl.BlockSpec((tm, k), lambda i: (i, 0)),
                  pl.BlockSpec((k, n), lambda i: (0, 0)),
                  ra_spec, rb_spec],
        out_specs=pl.BlockSpec((tm, n), lambda i: (i, 0)),
        out_shape=jax.ShapeDtypeStruct((m, n), F32),
        compiler_params=_cparams("arbitrary"),
        name="matmul_residual",
    )(y, w, res_a, res_b)


def _pos_in_seq(r0, rows, seq_len):
    return (r0 + lax.broadcasted_iota(jnp.int32, (rows, 1), 0)) & (seq_len - 1)


def _run_chunks(is_sample, n_chunks, body):
    @pl.when(is_sample)
    def _():
        lax.fori_loop(0, n_chunks, lambda k, c: body(True, k) or c, 0)

    @pl.when(jnp.logical_not(is_sample))
    def _():
        lax.fori_loop(0, n_chunks, lambda k, c: body(False, k) or c, 0)


def _history_rows(hist_chunk, back):
    n = hist_chunk.shape[0]
    return hist_chunk if back == 0 else pltpu.roll(hist_chunk, n - back, axis=0)


def _rglru_kernel(n_prompt_tiles, tiles_per_seq, seq_s,
                  gate_ref, u_ref, hist_ref,
                  cw_ref, cb_ref, br_ref, bi_ref, lam_ref, wr_ref, wi_ref,
                  y_ref, hl_ref, ubuf, uc_ref, rp_ref, ip_ref, hc_ref):
    i = pl.program_id(1)
    cblk = u_ref.shape[1]
    is_sample = i >= n_prompt_tiles
    w1 = cw_ref.shape[0] - 1

    @pl.when(is_sample | (lax.rem(i, tiles_per_seq) == 0))
    def _():
        ubuf[0:SUBLANES, :] = jnp.zeros((SUBLANES, cblk), F32)
        hc_ref[...] = jnp.zeros_like(hc_ref)

    ubuf[SUBLANES:, :] = u_ref[...]
    cw = cw_ref[...]
    cb = cb_ref[...]
    rc = SCAN_CHUNK

    def conv_chunk(masked, k):
        r0 = pl.multiple_of(k * rc, rc)
        rows = pl.ds(r0, rc)
        ext = ubuf[pl.ds(r0, rc + SUBLANES), :]
        if masked:
            t = _pos_in_seq(r0, rc, seq_s)
            hist = hist_ref[rows, :]

        def tap(s):
            shifted = pltpu.roll(ext, s, axis=0)[SUBLANES:]
            if not masked:
                return shifted
            return jnp.where(t >= s, shifted, _history_rows(hist, w1 - s))

        conv = tap(3) * cw[0:1] + tap(2) * cw[1:2]
        conv = conv + tap(1) * cw[2:3]
        conv = conv + ext[SUBLANES:] * cw[3:4]
        uc_ref[rows, :] = conv + cb

    _run_chunks(is_sample, ROW_TILE // rc, conv_chunk)

    ub = uc_ref[...].astype(BF16)
    rp_ref[...] = 0.5 * (jnp.dot(ub, wr_ref[...], preferred_element_type=F32) + br_ref[...])
    ip_ref[...] = 0.5 * (jnp.dot(ub, wi_ref[...], preferred_element_type=F32) + bi_ref[...])

    neg_half_c_sp = (-0.5 * RG_C) * jax.nn.softplus(-lam_ref[...])
    sub = lax.broadcasted_iota(jnp.int32, (1, SUBLANES, 1), 1)

    def scan_chunk(masked, k):
        r0 = pl.multiple_of(k * rc, rc)
        rows = pl.ds(r0, rc)
        uc = uc_ref[rows, :]
        ig = 0.5 * jnp.tanh(ip_ref[rows, :]) + 0.5
        log_a = neg_half_c_sp * (jnp.tanh(rp_ref[rows, :]) + 1.0)
        a = jnp.exp(log_a)
        mult = jnp.sqrt(-jnp.tanh(log_a) * (a * a + 1.0))
        b = mult * ig * uc
        if masked:
            first = _pos_in_seq(r0, rc, seq_s) == 0
            b = jnp.where(first, b + a * _history_rows(hist_ref[rows, :], w1), b)
            a = jnp.where(first, 0.0, a)

        a3 = a.reshape(rc // SUBLANES, SUBLANES, cblk)
        b3 = b.reshape(rc // SUBLANES, SUBLANES, cblk)
        for s in (1, 2, 4):
            a_s = pltpu.roll(a3, s, axis=1)
            b_s = pltpu.roll(b3, s, axis=1)
            m = sub >= s
            b3 = jnp.where(m, a3 * b_s + b3, b3)
            a3 = jnp.where(m, a3 * a_s, a3)

        hc = hc_ref[...]
        hs, lasts = [], []
        for g in range(rc // SUBLANES):
            hg = a3[g] * hc + b3[g]
            lasts.append(hg[SUBLANES - 1:SUBLANES])
            hc = jnp.broadcast_to(lasts[-1], (SUBLANES, cblk))
            hs.append(hg)
        hc_ref[...] = hc
        h = jnp.concatenate(hs, axis=0)
        groups = rc // SUBLANES
        hl_ref[pl.ds(pl.multiple_of(k * groups, groups), groups), :] = jnp.concatenate(lasts, axis=0)
        y_ref[rows, :] = (h * gate_ref[rows, :]).astype(BF16)

    _run_chunks(is_sample, ROW_TILE // rc, scan_chunk)
    ubuf[0:SUBLANES, :] = ubuf[ROW_TILE:ROW_TILE + SUBLANES, :]


def _rglru_core(proj, hist, conv_w, conv_b, b_r, b_i, lam, w_r, w_i, n_prompt_tiles,
                tiles_per_seq, seq_s):
    m = proj.shape[0]
    assert SCAN_CHUNK % (SUBLANES * SUBLANES) == 0
    nb, cblk = w_r.shape[0], w_r.shape[1]
    c = nb * cblk
    n_tiles = m // ROW_TILE
    row_blk = lambda off: pl.BlockSpec((ROW_TILE, cblk), lambda cb, i: (i, off + cb))
    aux_blk = pl.BlockSpec((ROW_TILE, cblk), lambda cb, i: (0, cb))
    vec_blk = lambda rows: pl.BlockSpec((rows, cblk), lambda cb, i: (0, cb))
    w_blk = pl.BlockSpec((None, cblk, cblk), lambda cb, i: (cb, 0, 0))
    out_blk = pl.BlockSpec((ROW_TILE, cblk), lambda cb, i: (i, cb))
    return pl.pallas_call(
        functools.partial(_rglru_kernel, n_prompt_tiles, tiles_per_seq, seq_s),
        grid=(nb, n_tiles),
        in_specs=[row_blk(0), row_blk(nb), aux_blk,
                  vec_blk(conv_w.shape[0]), vec_blk(1), vec_blk(1), vec_blk(1), vec_blk(1),
                  w_blk, w_blk],
        out_specs=[out_blk,
                   pl.BlockSpec((ROW_TILE // SUBLANES, cblk), lambda cb, i: (i, cb))],
        out_shape=[jax.ShapeDtypeStruct((m, c), BF16),
                   jax.ShapeDtypeStruct((m // SUBLANES, c), F32)],
        scratch_shapes=[pltpu.VMEM((ROW_TILE + SUBLANES, cblk), F32),
                        pltpu.VMEM((ROW_TILE, cblk), F32),
                        pltpu.VMEM((ROW_TILE, cblk), F32),
                        pltpu.VMEM((ROW_TILE, cblk), F32),
                        pltpu.VMEM((SUBLANES, cblk), F32)],
        compiler_params=_cparams("arbitrary", "arbitrary"),
        name="rglru_core",
    )(proj, proj, hist, conv_w, conv_b, b_r, b_i, lam, w_r, w_i)


def _ffn_kernel(x_ref, g_ref, wg_ref, wu_ref, wd_ref, o_ref, hn_ref):
    @pl.when(pl.program_id(1) == 0)
    def _():
        x = x_ref[...]
        hn_ref[...] = _rms_norm_f32(x, g_ref[...]).astype(BF16)
        o_ref[...] = x

    hn = hn_ref[...]
    a = jnp.dot(hn, wg_ref[...].astype(BF16), preferred_element_type=F32)
    b = jnp.dot(hn, wu_ref[...].astype(BF16), preferred_element_type=F32)
    h = (jax.nn.silu(a) * b).astype(BF16)
    o_ref[...] += jnp.dot(h, wd_ref[...].astype(BF16), preferred_element_type=F32)


def _ffn(x, g, wg, wu, wd):
    m, d = x.shape
    f = wg.shape[1]
    tm = ROW_TILE
    tf = W_COL_TILE
    return pl.pallas_call(
        _ffn_kernel,
        grid=(m // tm, f // tf),
        in_specs=[pl.BlockSpec((tm, d), lambda i, j: (i, 0)),
                  pl.BlockSpec((1, d), lambda i, j: (0, 0)),
                  pl.BlockSpec((d, tf), lambda i, j: (0, j)),
                  pl.BlockSpec((d, tf), lambda i, j: (0, j)),
                  pl.BlockSpec((tf, d), lambda i, j: (j, 0))],
        out_specs=pl.BlockSpec((tm, d), lambda i, j: (i, 0)),
        out_shape=jax.ShapeDtypeStruct((m, d), F32),
        scratch_shapes=[pltpu.VMEM((tm, d), BF16)],
        compiler_params=_cparams("arbitrary", "arbitrary"),
        name="ffn",
    )(x, g, wg, wu, wd)


def _router_kernel(n_experts, x_ref, g_ref, whi_ref, wlo_ref, hn_ref, meta_ref, cnt_ref, run_ref):
    @pl.when(pl.program_id(0) == 0)
    def _():
        run_ref[...] = jnp.zeros_like(run_ref)

    tm = x_ref.shape[0]
    hn = _rms_norm_f32(x_ref[...], g_ref[...])
    hn_hi = hn.astype(BF16)
    hn_hi32 = hn_hi.astype(F32)
    hn_ref[...] = _pack_bf16_pairs(hn_hi32)
    hn_lo = (hn - hn_hi32).astype(BF16)
    logits = (jnp.dot(hn_hi, whi_ref[...], preferred_element_type=F32)
              + (jnp.dot(hn_lo, whi_ref[...], preferred_element_type=F32)
                 + jnp.dot(hn_hi, wlo_ref[...], preferred_element_type=F32)))
    lane = lax.broadcasted_iota(jnp.int32, (tm, LANES), 1).astype(F32)
    neg = jnp.float32(-jnp.inf)
    lg = jnp.where(lane < n_experts, logits, neg)
    m1 = jnp.max(lg, axis=1, keepdims=True)
    i1 = jnp.min(jnp.where(lg == m1, lane, float(LANES)), axis=1, keepdims=True)
    lg2 = jnp.where(lane == i1, neg, lg)
    m2 = jnp.max(lg2, axis=1, keepdims=True)
    i2 = jnp.min(jnp.where(lg2 == m2, lane, float(LANES)), axis=1, keepdims=True)
    ex = jnp.exp(m2 - m1)
    g1 = 1.0 / (1.0 + ex)
    g2 = ex / (1.0 + ex)

    sel1 = lane == i1
    sel2 = lane == i2
    onehot = jnp.where(sel1 | sel2, 1.0, 0.0)
    rr = lax.broadcasted_iota(jnp.int32, (tm, tm), 0)
    cc = lax.broadcasted_iota(jnp.int32, (tm, tm), 1)
    tri = jnp.where(rr > cc, 1.0, 0.0).astype(BF16)
    before = jnp.dot(tri, onehot.astype(BF16), preferred_element_type=F32) + run_ref[0:1, :]
    rank1 = jnp.sum(jnp.where(sel1, before, 0.0), axis=1, keepdims=True)
    rank2 = jnp.sum(jnp.where(sel2, before, 0.0), axis=1, keepdims=True)
    run = run_ref[0:1, :] + jnp.sum(onehot, axis=0, keepdims=True)
    run_ref[...] = jnp.broadcast_to(run, run_ref.shape)
    cnt_ref[...] = jnp.broadcast_to(run, cnt_ref.shape)

    meta = jnp.where(lane == 0, i1, 0.0)
    meta = jnp.where(lane == 1, i2, meta)
    meta = jnp.where(lane == 2, g1, meta)
    meta = jnp.where(lane == 3, g2, meta)
    meta = jnp.where(lane == 4, rank1, meta)
    meta = jnp.where(lane == 5, rank2, meta)
    meta_ref[...] = meta


def _router(x, g, w_router_padded, n_experts):
    m, d = x.shape
    tm = ROUTER_ROW_TILE
    w_hi = w_router_padded.astype(BF16)
    w_lo = (w_router_padded - w_hi.astype(F32)).astype(BF16)
    return pl.pallas_call(
        functools.partial(_router_kernel, n_experts),
        grid=(m // tm,),
        in_specs=[pl.BlockSpec((tm, d), lambda i: (i, 0)),
                  pl.BlockSpec((1, d), lambda i: (0, 0)),
                  pl.BlockSpec((d, LANES), lambda i: (0, 0)),
                  pl.BlockSpec((d, LANES), lambda i: (0, 0))],
        out_specs=[pl.BlockSpec((tm, d // 2), lambda i: (i, 0)),
                   pl.BlockSpec((tm, LANES), lambda i: (i, 0)),
                   pl.BlockSpec((SUBLANES, LANES), lambda i: (0, 0))],
        out_shape=[jax.ShapeDtypeStruct((m, d // 2), jnp.uint32),
                   jax.ShapeDtypeStruct((m, LANES), F32),
                   jax.ShapeDtypeStruct((SUBLANES, LANES), F32)],
        scratch_shapes=[pltpu.VMEM((SUBLANES, LANES), F32)],
        compiler_params=_cparams("arbitrary"),
        name="router",
    )(x, g, w_hi, w_lo)


def _row_copy(src_hbm, idx, dst_ref, r, sem):
    return pltpu.make_async_copy(src_hbm.at[pl.ds(idx, 1)], dst_ref.at[pl.ds(r, 1)], sem)


def _experts_kernel(rows_per_step, be_ref, nv_ref, nused_ref, off_ref, stok_ref,
                    hn_hbm, wg_ref, wu_ref, wd_ref, o_ref,
                    xbuf, xb_ref, acc_ref, sem):
    t = pl.program_id(0)
    j = pl.program_id(1)
    nused = nused_ref[0]
    tr = xb_ref.shape[0]
    gr = xbuf.shape[0]
    dh = xbuf.shape[1]

    @pl.when(j == 0)
    def _():
        acc_ref[...] = jnp.zeros_like(acc_ref)

    @pl.when((t == 0) & (j == 0))
    def _():
        off0 = off_ref[0]

        def issue(r, c):
            _row_copy(hn_hbm, stok_ref[off0 + r], xbuf, r, sem).start()
            return c

        lax.fori_loop(0, gr, issue, 0, unroll=8)

    @pl.when((j == 0) & (t <= nused))
    def _():
        pltpu.make_async_copy(hn_hbm.at[pl.ds(0, gr)], xbuf, sem).wait()

    @pl.when((j == 0) & (t < nused))
    def _():
        lo, hi = _unpack_bf16_pairs(xbuf[0:tr, :])
        xb_ref[:, 0:dh] = lo.astype(BF16)
        xb_ref[:, dh:] = hi.astype(BF16)

    def step(n_rows):
        base = off_ref[t + 1] + j * rows_per_step
        for k in range(rows_per_step):
            _row_copy(hn_hbm, stok_ref[base + k], xbuf, j * rows_per_step + k, sem).start()
        rows = pl.ds(0, n_rows)
        x = xb_ref[rows, :]
        a = jnp.dot(x, wg_ref[...].astype(BF16), preferred_element_type=F32)
        b = jnp.dot(x, wu_ref[...].astype(BF16), preferred_element_type=F32)
        h = (jax.nn.silu(a) * b).astype(BF16)
        acc_ref[rows, :] += jnp.dot(h, wd_ref[...].astype(BF16), preferred_element_type=F32)

    slab = tr // MOE_ROW_SLABS
    n_slabs = (nv_ref[t] + (slab - 1)) // slab
    for q in range(1, MOE_ROW_SLABS + 1):
        @pl.when((t < nused) & (jnp.maximum(n_slabs, 1) == q))
        def _(q=q):
            step(q * slab)

    @pl.when(j == pl.num_programs(1) - 1)
    def _():
        o_ref[...] = _pack_bf16_pairs(acc_ref[...].astype(BF16).astype(F32))


def _experts(hn_packed, sorted_tok, tile_off, block_e, n_valid, n_used, wg, wu, wd):
    dh = hn_packed.shape[1]
    d = 2 * dh
    f = wg.shape[2]
    tr = MOE_ROW_TILE
    tf = MOE_COL_TILE
    assert tr % (MOE_ROW_SLABS * 2 * SUBLANES) == 0 and f % tf == 0
    n_tiles = block_e.shape[0]
    nj = f // tf
    rows_per_step = -(-tr // nj)
    gr = rows_per_step * nj
    stok = jnp.pad(sorted_tok, (0, gr))
    offs = jnp.pad(jnp.clip(tile_off, 0, sorted_tok.shape[0]), (0, 1))

    def tile(t, nu):
        return jnp.minimum(t, jnp.maximum(nu[0] - 1, 0))

    def col(t, j, nu):
        return jnp.where(t < nu[0], j, nj - 1)

    wspec = lambda shape, imap: pl.BlockSpec(
        shape, lambda t, j, be, nv, nu, off, st: imap(be[tile(t, nu)], col(t, j, nu)))
    return pl.pallas_call(
        functools.partial(_experts_kernel, rows_per_step),
        grid_spec=pltpu.PrefetchScalarGridSpec(
            num_scalar_prefetch=5,
            grid=(n_tiles + 1, nj),
            in_specs=[pl.BlockSpec(memory_space=pl.ANY),
                      wspec((None, d, tf), lambda e, c: (e, 0, c)),
                      wspec((None, d, tf), lambda e, c: (e, 0, c)),
                      wspec((None, tf, d), lambda e, c: (e, c, 0))],
            out_specs=pl.BlockSpec((tr, dh), lambda t, j, *_: (t, 0)),
            scratch_shapes=[pltpu.VMEM((gr, dh), jnp.uint32), pltpu.VMEM((tr, d), BF16),
                            pltpu.VMEM((tr, d), F32),
                            pltpu.SemaphoreType.DMA(())]),
        out_shape=jax.ShapeDtypeStruct(((n_tiles + 1) * tr, dh), jnp.uint32),
        compiler_params=_cparams("arbitrary", "arbitrary"),
        name="moe_experts",
    )(block_e, jnp.pad(n_valid, (0, 1)), n_used, offs, stok, hn_packed, wg, wu, wd)


def _combine_kernel(n_first, d0_ref, d1_ref, d0n_ref, d1n_ref, x_ref, meta_ref, g_ref, yb_hbm,
                    oa_ref, ob_ref, y0, y1, sem):
    tm = x_ref.shape[0]
    i = pl.program_id(0)
    slot = lax.rem(i, 2)

    def row_copies(da_ref, db_ref, s, r):
        return (_row_copy(yb_hbm, da_ref[0, 0, r], y0.at[s], r, sem.at[0, s]),
                _row_copy(yb_hbm, db_ref[0, 0, r], y1.at[s], r, sem.at[1, s]))

    def wait_tile(s):
        pltpu.make_async_copy(yb_hbm.at[pl.ds(0, tm)], y0.at[s], sem.at[0, s]).wait()
        pltpu.make_async_copy(yb_hbm.at[pl.ds(0, tm)], y1.at[s], sem.at[1, s]).wait()

    @pl.when(i == 0)
    def _():
        def issue(r, c):
            for cp in row_copies(d0_ref, d1_ref, 0, r):
                cp.start()
            return c

        lax.fori_loop(0, tm, issue, 0, unroll=8)

    wait_tile(slot)
    for r in range(tm):
        for cp in row_copies(d0n_ref, d1n_ref, 1 - slot, r):
            cp.start()
    meta = meta_ref[...]
    g0, g1 = meta[:, 2:3], meta[:, 3:4]
    lo0, hi0 = _unpack_bf16_pairs(y0[slot])
    lo1, hi1 = _unpack_bf16_pairs(y1[slot])
    moe = jnp.concatenate([lo0 * g0 + lo1 * g1, hi0 * g0 + hi1 * g1], axis=1)
    y = _rms_norm_f32(x_ref[...] + moe, g_ref[...])

    @pl.when(i < n_first)
    def _():
        oa_ref[...] = y

    @pl.when(i >= n_first)
    def _():
        ob_ref[...] = y

    @pl.when(i == pl.num_programs(0) - 1)
    def _():
        wait_tile(1 - slot)


def _combine(x, meta, g, yb, dest0, dest1, m_first):
    m, d = x.shape
    tm = COMBINE_ROW_TILE
    n_tiles = m // tm
    smem_blk = pl.BlockSpec((1, 1, tm), lambda i: (i, 0, 0), memory_space=pltpu.SMEM)
    smem_next = pl.BlockSpec((1, 1, tm), lambda i: (jnp.minimum(i + 1, n_tiles - 1), 0, 0),
                             memory_space=pltpu.SMEM)
    oa_spec, ob_spec = _two_part_specs(m_first // tm, (tm, d), lambda: 0)
    d0 = dest0.reshape(n_tiles, 1, tm)
    d1 = dest1.reshape(n_tiles, 1, tm)
    return pl.pallas_call(
        functools.partial(_combine_kernel, m_first // tm),
        grid=(n_tiles,),
        in_specs=[smem_blk, smem_blk, smem_next, smem_next,
                  pl.BlockSpec((tm, d), lambda i: (i, 0)),
                  pl.BlockSpec((tm, LANES), lambda i: (i, 0)),
                  pl.BlockSpec((1, d), lambda i: (0, 0)),
                  pl.BlockSpec(memory_space=pl.ANY)],
        out_specs=[oa_spec, ob_spec],
        out_shape=[jax.ShapeDtypeStruct((m_first, d), F32),
                   jax.ShapeDtypeStruct((m - m_first, d), F32)],
        scratch_shapes=[pltpu.VMEM((2, tm, d // 2), jnp.uint32),
                        pltpu.VMEM((2, tm, d // 2), jnp.uint32),
                        pltpu.SemaphoreType.DMA((2, 2))],
        compiler_params=_cparams("arbitrary"),
        name="moe_combine",
    )(d0, d1, d0, d1, x, meta, g, yb)


def _sample_history(dec_seq, *states):
    hist = jnp.concatenate(states, axis=1)
    b, r, c = hist.shape
    assert r <= dec_seq
    return jnp.pad(hist, ((0, 0), (0, dec_seq - r), (0, 0))).reshape(b * dec_seq, c)


def _seq_tails(a, col0, ncol, width, batch, seq, dec_batch, dec_seq):
    mp = batch * seq
    tails = [lax.slice(a, ((b + 1) * seq - width, col0), ((b + 1) * seq, col0 + ncol))
             for b in range(batch)]
    smp = lax.slice(a, (mp, col0), (mp + dec_batch * dec_seq, col0 + ncol))
    smp = smp.reshape(dec_batch, dec_seq, ncol)[:, dec_seq - width:]
    return jnp.stack(tails, axis=0), smp


def kernel(x_prompt, x_sample, state_lru_h, state_lru_conv, state_sconv, norm_mix, norm_ffn, norm_final, lru_w_in, lru_conv_w, lru_conv_b, lru_w_rgate, lru_b_rgate, lru_w_igate, lru_b_igate, lru_lambda, lru_w_out, sc_w_in, sc_conv_w, sc_w_out, ffn_w_gate, ffn_w_up, ffn_w_down, moe_w_router, moe_w_gate, moe_w_up, moe_w_down):
    batch, seq, d = x_prompt.shape
    dec_batch, dec_seq, _ = x_sample.shape
    mp, ms = batch * seq, dec_batch * dec_seq
    m = mp + ms
    depth = norm_mix.shape[0]
    n_experts = moe_w_router.shape[2]
    conv_a = lru_conv_w.shape[1]
    conv_b = sc_conv_w.shape[1]
    assert depth == 2 and lru_w_in.shape[0] == 1 and sc_w_in.shape[0] == 1
    assert seq & (seq - 1) == 0 and dec_seq & (dec_seq - 1) == 0
    assert seq % ROW_TILE == 0 and ms == ROW_TILE and ROW_TILE % dec_seq == 0
    assert dec_seq >= conv_a - 1 and dec_seq >= conv_b - 1 and dec_seq == SUBLANES
    assert n_experts <= LANES and m % ROUTER_ROW_TILE == 0 and m % COMBINE_ROW_TILE == 0
    n_prompt_tiles = mp // ROW_TILE

    xp = x_prompt.reshape(mp, d)
    xs_in = x_sample.reshape(ms, d)
    row = lambda v: v.reshape(1, -1)

    d_rnn = lru_w_out.shape[1]
    assert d_rnn % COL_TILE == 0
    proj = _norm_matmul(xp, xs_in, row(norm_mix[0]), lru_w_in[0], n_gelu_cols=d_rnn)
    hist = _sample_history(dec_seq, state_lru_conv[0], state_lru_h[0][:, None, :])
    y, h_last = _rglru_core(proj, hist, lru_conv_w[0], row(lru_conv_b[0]), row(lru_b_rgate[0]),
                           row(lru_b_igate[0]), row(lru_lambda[0]),
                           lru_w_rgate[0].astype(BF16), lru_w_igate[0].astype(BF16),
                           n_prompt_tiles, seq // ROW_TILE, dec_seq)
    x = _matmul_residual(y, lru_w_out[0].astype(BF16), xp, xs_in)

    x = _ffn(x, row(norm_ffn[0]), ffn_w_gate[0], ffn_w_up[0], ffn_w_down[0])

    hist = _sample_history(dec_seq, state_sconv[0])
    zb, cv_tile_tails, cv_smp = _sc_mixer(x, row(norm_mix[1]), sc_w_in[0], sc_conv_w[0], hist,
                                          n_prompt_tiles, seq // ROW_TILE, dec_seq)
    x = _matmul_residual(zb, sc_w_out[0].astype(BF16), x, x)

    w_router = jnp.pad(moe_w_router[0], ((0, 0), (0, LANES - n_experts)))
    hn, meta, counts = _router(x, row(norm_ffn[1]), w_router, n_experts)
    tr = MOE_ROW_TILE
    n_tiles = -(-(m * TOP_K) // tr) + n_experts
    counts = counts[0, :n_experts].astype(jnp.int32)
    k_tiles = (counts + tr - 1) // tr
    share = jnp.maximum((counts + k_tiles - 1) // jnp.maximum(k_tiles, 1), 1)
    padded = k_tiles * tr
    pad_end = jnp.cumsum(padded)
    pad_start = pad_end - padded
    e0 = meta[:, 0].astype(jnp.int32)
    e1 = meta[:, 1].astype(jnp.int32)

    def sorted_row(e, rank):
        tile_in_expert = rank // share[e]
        return pad_start[e] + tile_in_expert * tr + (rank - tile_in_expert * share[e])

    dest0 = sorted_row(e0, meta[:, 4].astype(jnp.int32))
    dest1 = sorted_row(e1, meta[:, 5].astype(jnp.int32))
    order = jnp.argsort(jnp.stack([e0, e1], axis=1).reshape(-1), stable=True)
    sorted_tok = (order // TOP_K).astype(jnp.int32)
    start = jnp.cumsum(counts) - counts
    block_start = jnp.arange(n_tiles, dtype=jnp.int32) * tr
    block_e = jnp.sum((pad_end[None, :] <= block_start[:, None]).astype(jnp.int32), axis=1)
    block_e = jnp.minimum(block_e, n_experts - 1)
    in_expert = (block_start - pad_start[block_e]) // tr * share[block_e]
    n_valid = jnp.clip(counts[block_e] - in_expert, 0, share[block_e]).astype(jnp.int32)
    n_used = (pad_end[-1:] // tr).astype(jnp.int32)
    yb = _experts(hn, sorted_tok, (start[block_e] + in_expert).astype(jnp.int32), block_e,
                  n_valid, n_used, moe_w_gate[0], moe_w_up[0], moe_w_down[0])
    y_prompt, y_sample = _combine(x, meta, row(norm_final), yb, dest0, dest1, mp)

    tails = functools.partial(_seq_tails, batch=batch, seq=seq, dec_batch=dec_batch, dec_seq=dec_seq)
    groups_per_seq = seq // SUBLANES
    h_p = h_last[groups_per_seq - 1:mp // SUBLANES:groups_per_seq]
    h_s = h_last[mp // SUBLANES:]
    u_p, u_s = tails(proj, d_rnn, d_rnn, conv_a - 1)
    tiles_per_seq = seq // ROW_TILE
    cv_p = cv_tile_tails.reshape(m // ROW_TILE, SUBLANES, -1)[
        tiles_per_seq - 1:n_prompt_tiles:tiles_per_seq, SUBLANES - (conv_b - 1):]
    cv_s = cv_smp.reshape(dec_batch, dec_seq, -1)[:, dec_seq - (conv_b - 1):]
    return (y_prompt.reshape(batch, seq, d), y_sample.reshape(dec_batch, dec_seq, d),
            h_p[None], u_p[None], cv_p[None], h_s[None], u_s[None], cv_s[None])
```

```python
import functools

import jax
import jax.numpy as jnp
from jax import lax
from jax.experimental import pallas as pl
from jax.experimental.pallas import tpu as pltpu

F32 = jnp.float32
BF16 = jnp.bfloat16

EPS = 1e-6
RG_C = 8.0
TOP_K = 2

LANES = 128
SUBLANES = 8
ROW_TILE = 1024
OUT_ROW_TILE = 512
ROUTER_ROW_TILE = 512
COL_TILE = 512
W_COL_TILE = 256
MOE_ROW_TILE = 768
MOE_ROW_SLABS = 6
MOE_COL_TILE = 512
COMBINE_ROW_TILE = 256
SCAN_CHUNK = 128
VMEM_LIMIT = 60 * 1024 * 1024


def _cparams(*sem):
    return pltpu.CompilerParams(dimension_semantics=sem, vmem_limit_bytes=VMEM_LIMIT)


def _rms_norm_f32(x, g):
    y = x * lax.rsqrt(jnp.mean(x * x, axis=-1, keepdims=True) + EPS)
    return y * g


BF16_HIGH_HALF = 0xFFFF0000
BF16_BITS = 16


def _pack_bf16_pairs(x_rounded):
    bits = pltpu.bitcast(x_rounded, jnp.uint32)
    half = x_rounded.shape[1] // 2
    return (bits[:, half:] & jnp.uint32(BF16_HIGH_HALF)) | (bits[:, :half] >> BF16_BITS)


def _unpack_bf16_pairs(w):
    return (pltpu.bitcast(w << BF16_BITS, F32),
            pltpu.bitcast(w & jnp.uint32(BF16_HIGH_HALF), F32))


def _two_part_specs(n_first, block, col_of, second_mode=None):
    first = pl.BlockSpec(block, lambda i, *a: (jnp.minimum(i, n_first - 1), col_of(*a)))
    second = pl.BlockSpec(block, lambda i, *a: (jnp.maximum(i - n_first, 0), col_of(*a)),
                          pipeline_mode=second_mode)
    return first, second


def _norm_matmul_kernel(n_first, n_gelu, xa_ref, xb_ref, g_ref, w_ref, o_ref, hn_ref):
    i = pl.program_id(0)
    j = pl.program_id(1)

    def project():
        return jnp.dot(hn_ref[...], w_ref[...].astype(BF16), preferred_element_type=F32)

    @pl.when((j == 0) & (i < n_first))
    def _():
        hn_ref[...] = _rms_norm_f32(xa_ref[...], g_ref[...]).astype(BF16)

    @pl.when((j == 0) & (i >= n_first))
    def _():
        hn_ref[...] = _rms_norm_f32(xb_ref[...], g_ref[...]).astype(BF16)

    @pl.when(j < n_gelu)
    def _():
        o_ref[...] = jax.nn.gelu(project())

    @pl.when(j >= n_gelu)
    def _():
        o_ref[...] = project()


def _norm_matmul(xa, xb, g, w, n_gelu_cols):
    d = xa.shape[1]
    m = xa.shape[0] + xb.shape[0]
    n = w.shape[1]
    assert xb.shape[0] == ROW_TILE
    xa_spec, xb_spec = _two_part_specs(xa.shape[0] // ROW_TILE, (ROW_TILE, d), lambda j: 0,
                                       second_mode=pl.Buffered(1))
    return pl.pallas_call(
        functools.partial(_norm_matmul_kernel, xa.shape[0] // ROW_TILE, n_gelu_cols // COL_TILE),
        grid=(m // ROW_TILE, n // COL_TILE),
        in_specs=[xa_spec, xb_spec,
                  pl.BlockSpec((1, d), lambda i, j: (0, 0)),
                  pl.BlockSpec((d, COL_TILE), lambda i, j: (0, j))],
        out_specs=pl.BlockSpec((ROW_TILE, COL_TILE), lambda i, j: (i, j)),
        out_shape=jax.ShapeDtypeStruct((m, n), F32),
        scratch_shapes=[pltpu.VMEM((ROW_TILE, d), BF16)],
        compiler_params=_cparams("arbitrary", "arbitrary"),
        name="norm_matmul",
    )(xa, xb, g, w)


def _sc_mixer_kernel(n_prompt_tiles, tiles_per_seq, seq_s,
                     x_ref, g_ref, wb_ref, wc_ref, wv_ref, cw_ref, hist_ref,
                     zb_ref, tail_ref, cvs_ref, hn_ref, carry_ref):
    i = pl.program_id(0)
    j = pl.program_id(1)
    t_rows = x_ref.shape[0]
    is_sample = i >= n_prompt_tiles
    starts_seq = is_sample | (lax.rem(i, tiles_per_seq) == 0)

    @pl.when(j == 0)
    def _():
        hn_ref[...] = _rms_norm_f32(x_ref[...], g_ref[...]).astype(BF16)

    @pl.when((i == 0) & (j == 0))
    def _():
        carry_ref[...] = jnp.zeros_like(carry_ref)

    def mixer(masked):
        hn = hn_ref[...]
        cw = cw_ref[...]
        c = jnp.dot(hn, wc_ref[...].astype(BF16), preferred_element_type=F32)
        v = jnp.dot(hn, wv_ref[...].astype(BF16), preferred_element_type=F32)
        cv = c * v
        prev = jnp.where(starts_seq, 0.0, carry_ref[j])
        ext = jnp.concatenate([prev, cv], axis=0)
        w1 = cw.shape[0] - 1
        if masked:
            t = _pos_in_seq(0, t_rows, seq_s)
            hist = hist_ref[...]

        def tap(s):
            shifted = pltpu.roll(ext, s, axis=0)[SUBLANES:]
            if not masked:
                return shifted
            return jnp.where(t >= s, shifted, _history_rows(hist, w1 - s))

        z = tap(2) * cw[0:1] + tap(1) * cw[1:2]
        z = z + cv * cw[2:3]
        b = jnp.dot(hn, wb_ref[...].astype(BF16), preferred_element_type=F32)
        zb_ref[...] = (b * z).astype(BF16)
        last = cv[t_rows - SUBLANES:, :]
        carry_ref[j] = last
        tail_ref[...] = last
        return cv

    @pl.when(is_sample)
    def _():
        cvs_ref[...] = mixer(True)

    @pl.when(jnp.logical_not(is_sample))
    def _():
        mixer(False)


def _sc_mixer(x, g, w, conv_w, hist, n_prompt_tiles, tiles_per_seq, seq_s):
    m, d = x.shape
    dc = w.shape[1] // 3
    tn = W_COL_TILE
    nj = dc // tn
    n_tiles = m // ROW_TILE
    ms = hist.shape[0]
    assert ms == ROW_TILE
    wspec = lambda off: pl.BlockSpec((d, tn), lambda i, j: (0, off * nj + j))
    sample_col = lambda i, j: (0, jnp.where(i >= n_prompt_tiles, j, 0))
    return pl.pallas_call(
        functools.partial(_sc_mixer_kernel, n_prompt_tiles, tiles_per_seq, seq_s),
        grid=(n_tiles, nj),
        in_specs=[pl.BlockSpec((ROW_TILE, d), lambda i, j: (i, 0)),
                  pl.BlockSpec((1, d), lambda i, j: (0, 0)),
                  wspec(0), wspec(1), wspec(2),
                  pl.BlockSpec((conv_w.shape[0], tn), lambda i, j: (0, j)),
                  pl.BlockSpec((ROW_TILE, tn), sample_col)],
        out_specs=[pl.BlockSpec((ROW_TILE, tn), lambda i, j: (i, j)),
                   pl.BlockSpec((SUBLANES, tn), lambda i, j: (i, j)),
                   pl.BlockSpec((ROW_TILE, tn), sample_col)],
        out_shape=[jax.ShapeDtypeStruct((m, dc), BF16),
                   jax.ShapeDtypeStruct((n_tiles * SUBLANES, dc), F32),
                   jax.ShapeDtypeStruct((ms, dc), F32)],
        scratch_shapes=[pltpu.VMEM((ROW_TILE, d), BF16),
                        pltpu.VMEM((nj, SUBLANES, tn), F32)],
        compiler_params=_cparams("arbitrary", "arbitrary"),
        name="sc_mixer",
    )(x, g, w, w, w, conv_w, hist)


def _matmul_residual_kernel(n_first, y_ref, w_ref, ra_ref, rb_ref, o_ref):
    o_ref[...] = jnp.dot(y_ref[...], w_ref[...], preferred_element_type=F32)
    i = pl.program_id(0)

    @pl.when(i < n_first)
    def _():
        o_ref[...] += ra_ref[...]

    @pl.when(i >= n_first)
    def _():
        o_ref[...] += rb_ref[...]


def _matmul_residual(y, w, res_a, res_b):
    m, k = y.shape
    n = w.shape[1]
    tm = OUT_ROW_TILE
    ra_spec, rb_spec = _two_part_specs(res_a.shape[0] // tm, (tm, n), lambda: 0)
    return pl.pallas_call(
        functools.partial(_matmul_residual_kernel, res_a.shape[0] // tm),
        grid=(m // tm,),
        in_specs=[pl.BlockSpec((tm, k), lambda i: (i, 0)),
                  pl.BlockSpec((k, n), lambda i: (0, 0)),
                  ra_spec, rb_spec],
        out_specs=pl.BlockSpec((tm, n), lambda i: (i, 0)),
        out_shape=jax.ShapeDtypeStruct((m, n), F32),
        compiler_params=_cparams("arbitrary"),
        name="matmul_residual",
    )(y, w, res_a, res_b)


def _pos_in_seq(r0, rows, seq_len):
    return (r0 + lax.broadcasted_iota(jnp.int32, (rows, 1), 0)) & (seq_len - 1)


def _run_chunks(is_sample, n_chunks, body):
    @pl.when(is_sample)
    def _():
        lax.fori_loop(0, n_chunks, lambda k, c: body(True, k) or c, 0)

    @pl.when(jnp.logical_not(is_sample))
    def _():
        lax.fori_loop(0, n_chunks, lambda k, c: body(False, k) or c, 0)


def _history_rows(hist_chunk, back):
    n = hist_chunk.shape[0]
    return hist_chunk if back == 0 else pltpu.roll(hist_chunk, n - back, axis=0)


def _rglru_kernel(n_prompt_tiles, tiles_per_seq, seq_s,
                  gate_ref, u_ref, hist_ref,
                  cw_ref, cb_ref, br_ref, bi_ref, lam_ref, wr_ref, wi_ref,
                  y_ref, hl_ref, ubuf, uc_ref, rp_ref, ip_ref, hc_ref):
    i = pl.program_id(1)
    cblk = u_ref.shape[1]
    is_sample = i >= n_prompt_tiles
    w1 = cw_ref.shape[0] - 1

    @pl.when(is_sample | (lax.rem(i, tiles_per_seq) == 0))
    def _():
        ubuf[0:SUBLANES, :] = jnp.zeros((SUBLANES, cblk), F32)
        hc_ref[...] = jnp.zeros_like(hc_ref)

    ubuf[SUBLANES:, :] = u_ref[...]
    cw = cw_ref[...]
    cb = cb_ref[...]
    rc = SCAN_CHUNK

    def conv_chunk(masked, k):
        r0 = pl.multiple_of(k * rc, rc)
        rows = pl.ds(r0, rc)
        ext = ubuf[pl.ds(r0, rc + SUBLANES), :]
        if masked:
            t = _pos_in_seq(r0, rc, seq_s)
            hist = hist_ref[rows, :]

        def tap(s):
            shifted = pltpu.roll(ext, s, axis=0)[SUBLANES:]
            if not masked:
                return shifted
            return jnp.where(t >= s, shifted, _history_rows(hist, w1 - s))

        conv = tap(3) * cw[0:1] + tap(2) * cw[1:2]
        conv = conv + tap(1) * cw[2:3]
        conv = conv + ext[SUBLANES:] * cw[3:4]
        uc_ref[rows, :] = conv + cb

    _run_chunks(is_sample, ROW_TILE // rc, conv_chunk)

    ub = uc_ref[...].astype(BF16)
    rp_ref[...] = 0.5 * (jnp.dot(ub, wr_ref[...], preferred_element_type=F32) + br_ref[...])
    ip_ref[...] = 0.5 * (jnp.dot(ub, wi_ref[...], preferred_element_type=F32) + bi_ref[...])

    neg_half_c_sp = (-0.5 * RG_C) * jax.nn.softplus(-lam_ref[...])
    sub = lax.broadcasted_iota(jnp.int32, (1, SUBLANES, 1), 1)

    def scan_chunk(masked, k):
        r0 = pl.multiple_of(k * rc, rc)
        rows = pl.ds(r0, rc)
        uc = uc_ref[rows, :]
        ig = 0.5 * jnp.tanh(ip_ref[rows, :]) + 0.5
        log_a = neg_half_c_sp * (jnp.tanh(rp_ref[rows, :]) + 1.0)
        a = jnp.exp(log_a)
        mult = jnp.sqrt(-jnp.tanh(log_a) * (a * a + 1.0))
        b = mult * ig * uc
        if masked:
            first = _pos_in_seq(r0, rc, seq_s) == 0
            b = jnp.where(first, b + a * _history_rows(hist_ref[rows, :], w1), b)
            a = jnp.where(first, 0.0, a)

        a3 = a.reshape(rc // SUBLANES, SUBLANES, cblk)
        b3 = b.reshape(rc // SUBLANES, SUBLANES, cblk)
        for s in (1, 2, 4):
            a_s = pltpu.roll(a3, s, axis=1)
            b_s = pltpu.roll(b3, s, axis=1)
            m = sub >= s
            b3 = jnp.where(m, a3 * b_s + b3, b3)
            a3 = jnp.where(m, a3 * a_s, a3)

        hc = hc_ref[...]
        hs, lasts = [], []
        for g in range(rc // SUBLANES):
            hg = a3[g] * hc + b3[g]
            lasts.append(hg[SUBLANES - 1:SUBLANES])
            hc = jnp.broadcast_to(lasts[-1], (SUBLANES, cblk))
            hs.append(hg)
        hc_ref[...] = hc
        h = jnp.concatenate(hs, axis=0)
        groups = rc // SUBLANES
        hl_ref[pl.ds(pl.multiple_of(k * groups, groups), groups), :] = jnp.concatenate(lasts, axis=0)
        y_ref[rows, :] = (h * gate_ref[rows, :]).astype(BF16)

    _run_chunks(is_sample, ROW_TILE // rc, scan_chunk)
    ubuf[0:SUBLANES, :] = ubuf[ROW_TILE:ROW_TILE + SUBLANES, :]


def _rglru_core(proj, hist, conv_w, conv_b, b_r, b_i, lam, w_r, w_i, n_prompt_tiles,
                tiles_per_seq, seq_s):
    m = proj.shape[0]
    assert SCAN_CHUNK % (SUBLANES * SUBLANES) == 0
    nb, cblk = w_r.shape[0], w_r.shape[1]
    c = nb * cblk
    n_tiles = m // ROW_TILE
    row_blk = lambda off: pl.BlockSpec((ROW_TILE, cblk), lambda cb, i: (i, off + cb))
    aux_blk = pl.BlockSpec((ROW_TILE, cblk), lambda cb, i: (0, cb))
    vec_blk = lambda rows: pl.BlockSpec((rows, cblk), lambda cb, i: (0, cb))
    w_blk = pl.BlockSpec((None, cblk, cblk), lambda cb, i: (cb, 0, 0))
    out_blk = pl.BlockSpec((ROW_TILE, cblk), lambda cb, i: (i, cb))
    return pl.pallas_call(
        functools.partial(_rglru_kernel, n_prompt_tiles, tiles_per_seq, seq_s),
        grid=(nb, n_tiles),
        in_specs=[row_blk(0), row_blk(nb), aux_blk,
                  vec_blk(conv_w.shape[0]), vec_blk(1), vec_blk(1), vec_blk(1), vec_blk(1),
                  w_blk, w_blk],
        out_specs=[out_blk,
                   pl.BlockSpec((ROW_TILE // SUBLANES, cblk), lambda cb, i: (i, cb))],
        out_shape=[jax.ShapeDtypeStruct((m, c), BF16),
                   jax.ShapeDtypeStruct((m // SUBLANES, c), F32)],
        scratch_shapes=[pltpu.VMEM((ROW_TILE + SUBLANES, cblk), F32),
                        pltpu.VMEM((ROW_TILE, cblk), F32),
                        pltpu.VMEM((ROW_TILE, cblk), F32),
                        pltpu.VMEM((ROW_TILE, cblk), F32),
                        pltpu.VMEM((SUBLANES, cblk), F32)],
        compiler_params=_cparams("arbitrary", "arbitrary"),
        name="rglru_core",
    )(proj, proj, hist, conv_w, conv_b, b_r, b_i, lam, w_r, w_i)


def _ffn_kernel(x_ref, g_ref, wg_ref, wu_ref, wd_ref, o_ref, hn_ref):
    @pl.when(pl.program_id(1) == 0)
    def _():
        x = x_ref[...]
        hn_ref[...] = _rms_norm_f32(x, g_ref[...]).astype(BF16)
        o_ref[...] = x

    hn = hn_ref[...]
    a = jnp.dot(hn, wg_ref[...].astype(BF16), preferred_element_type=F32)
    b = jnp.dot(hn, wu_ref[...].astype(BF16), preferred_element_type=F32)
    h = (jax.nn.silu(a) * b).astype(BF16)
    o_ref[...] += jnp.dot(h, wd_ref[...].astype(BF16), preferred_element_type=F32)


def _ffn(x, g, wg, wu, wd):
    m, d = x.shape
    f = wg.shape[1]
    tm = ROW_TILE
    tf = W_COL_TILE
    return pl.pallas_call(
        _ffn_kernel,
        grid=(m // tm, f // tf),
        in_specs=[pl.BlockSpec((tm, d), lambda i, j: (i, 0)),
                  pl.BlockSpec((1, d), lambda i, j: (0, 0)),
                  pl.BlockSpec((d, tf), lambda i, j: (0, j)),
                  pl.BlockSpec((d, tf), lambda i, j: (0, j)),
                  pl.BlockSpec((tf, d), lambda i, j: (j, 0))],
        out_specs=pl.BlockSpec((tm, d), lambda i, j: (i, 0)),
        out_shape=jax.ShapeDtypeStruct((m, d), F32),
        scratch_shapes=[pltpu.VMEM((tm, d), BF16)],
        compiler_params=_cparams("arbitrary", "arbitrary"),
        name="ffn",
    )(x, g, wg, wu, wd)


def _router_kernel(n_experts, x_ref, g_ref, whi_ref, wlo_ref, hn_ref, meta_ref, cnt_ref, run_ref):
    @pl.when(pl.program_id(0) == 0)
    def _():
        run_ref[...] = jnp.zeros_like(run_ref)

    tm = x_ref.shape[0]
    hn = _rms_norm_f32(x_ref[...], g_ref[...])
    hn_hi = hn.astype(BF16)
    hn_hi32 = hn_hi.astype(F32)
    hn_ref[...] = _pack_bf16_pairs(hn_hi32)
    hn_lo = (hn - hn_hi32).astype(BF16)
    logits = (jnp.dot(hn_hi, whi_ref[...], preferred_element_type=F32)
              + (jnp.dot(hn_lo, whi_ref[...], preferred_element_type=F32)
                 + jnp.dot(hn_hi, wlo_ref[...], preferred_element_type=F32)))
    lane = lax.broadcasted_iota(jnp.int32, (tm, LANES), 1).astype(F32)
    neg = jnp.float32(-jnp.inf)
    lg = jnp.where(lane < n_experts, logits, neg)
    m1 = jnp.max(lg, axis=1, keepdims=True)
    i1 = jnp.min(jnp.where(lg == m1, lane, float(LANES)), axis=1, keepdims=True)
    lg2 = jnp.where(lane == i1, neg, lg)
    m2 = jnp.max(lg2, axis=1, keepdims=True)
    i2 = jnp.min(jnp.where(lg2 == m2, lane, float(LANES)), axis=1, keepdims=True)
    ex = jnp.exp(m2 - m1)
    g1 = 1.0 / (1.0 + ex)
    g2 = ex / (1.0 + ex)

    sel1 = lane == i1
    sel2 = lane == i2
    onehot = jnp.where(sel1 | sel2, 1.0, 0.0)
    rr = lax.broadcasted_iota(jnp.int32, (tm, tm), 0)
    cc = lax.broadcasted_iota(jnp.int32, (tm, tm), 1)
    tri = jnp.where(rr > cc, 1.0, 0.0).astype(BF16)
    before = jnp.dot(tri, onehot.astype(BF16), preferred_element_type=F32) + run_ref[0:1, :]
    rank1 = jnp.sum(jnp.where(sel1, before, 0.0), axis=1, keepdims=True)
    rank2 = jnp.sum(jnp.where(sel2, before, 0.0), axis=1, keepdims=True)
    run = run_ref[0:1, :] + jnp.sum(onehot, axis=0, keepdims=True)
    run_ref[...] = jnp.broadcast_to(run, run_ref.shape)
    cnt_ref[...] = jnp.broadcast_to(run, cnt_ref.shape)

    meta = jnp.where(lane == 0, i1, 0.0)
    meta = jnp.where(lane == 1, i2, meta)
    meta = jnp.where(lane == 2, g1, meta)
    meta = jnp.where(lane == 3, g2, meta)
    meta = jnp.where(lane == 4, rank1, meta)
    meta = jnp.where(lane == 5, rank2, meta)
    meta_ref[...] = meta


def _router(x, g, w_router_padded, n_experts):
    m, d = x.shape
    tm = ROUTER_ROW_TILE
    w_hi = w_router_padded.astype(BF16)
    w_lo = (w_router_padded - w_hi.astype(F32)).astype(BF16)
    return pl.pallas_call(
        functools.partial(_router_kernel, n_experts),
        grid=(m // tm,),
        in_specs=[pl.BlockSpec((tm, d), lambda i: (i, 0)),
                  pl.BlockSpec((1, d), lambda i: (0, 0)),
                  pl.BlockSpec((d, LANES), lambda i: (0, 0)),
                  pl.BlockSpec((d, LANES), lambda i: (0, 0))],
        out_specs=[pl.BlockSpec((tm, d // 2), lambda i: (i, 0)),
                   pl.BlockSpec((tm, LANES), lambda i: (i, 0)),
                   pl.BlockSpec((SUBLANES, LANES), lambda i: (0, 0))],
        out_shape=[jax.ShapeDtypeStruct((m, d // 2), jnp.uint32),
                   jax.ShapeDtypeStruct((m, LANES), F32),
                   jax.ShapeDtypeStruct((SUBLANES, LANES), F32)],
        scratch_shapes=[pltpu.VMEM((SUBLANES, LANES), F32)],
        compiler_params=_cparams("arbitrary"),
        name="router",
    )(x, g, w_hi, w_lo)


def _row_copy(src_hbm, idx, dst_ref, r, sem):
    return pltpu.make_async_copy(src_hbm.at[pl.ds(idx, 1)], dst_ref.at[pl.ds(r, 1)], sem)


def _experts_kernel(rows_per_step, be_ref, nv_ref, nused_ref, off_ref, stok_ref,
                    hn_hbm, wg_ref, wu_ref, wd_ref, o_ref,
                    xbuf, xb_ref, acc_ref, sem):
    t = pl.program_id(0)
    j = pl.program_id(1)
    nused = nused_ref[0]
    tr = xb_ref.shape[0]
    gr = xbuf.shape[0]
    dh = xbuf.shape[1]

    @pl.when(j == 0)
    def _():
        acc_ref[...] = jnp.zeros_like(acc_ref)

    @pl.when((t == 0) & (j == 0))
    def _():
        off0 = off_ref[0]

        def issue(r, c):
            _row_copy(hn_hbm, stok_ref[off0 + r], xbuf, r, sem).start()
            return c

        lax.fori_loop(0, gr, issue, 0, unroll=8)

    @pl.when((j == 0) & (t <= nused))
    def _():
        pltpu.make_async_copy(hn_hbm.at[pl.ds(0, gr)], xbuf, sem).wait()

    @pl.when((j == 0) & (t < nused))
    def _():
        lo, hi = _unpack_bf16_pairs(xbuf[0:tr, :])
        xb_ref[:, 0:dh] = lo.astype(BF16)
        xb_ref[:, dh:] = hi.astype(BF16)

    def step(n_rows):
        base = off_ref[t + 1] + j * rows_per_step
        for k in range(rows_per_step):
            _row_copy(hn_hbm, stok_ref[base + k], xbuf, j * rows_per_step + k, sem).start()
        rows = pl.ds(0, n_rows)
        x = xb_ref[rows, :]
        a = jnp.dot(x, wg_ref[...].astype(BF16), preferred_element_type=F32)
        b = jnp.dot(x, wu_ref[...].astype(BF16), preferred_element_type=F32)
        h = (jax.nn.silu(a) * b).astype(BF16)
        acc_ref[rows, :] += jnp.dot(h, wd_ref[...].astype(BF16), preferred_element_type=F32)

    slab = tr // MOE_ROW_SLABS
    n_slabs = (nv_ref[t] + (slab - 1)) // slab
    for q in range(1, MOE_ROW_SLABS + 1):
        @pl.when((t < nused) & (jnp.maximum(n_slabs, 1) == q))
        def _(q=q):
            step(q * slab)

    @pl.when(j == pl.num_programs(1) - 1)
    def _():
        o_ref[...] = _pack_bf16_pairs(acc_ref[...].astype(BF16).astype(F32))


def _experts(hn_packed, sorted_tok, tile_off, block_e, n_valid, n_used, wg, wu, wd):
    dh = hn_packed.shape[1]
    d = 2 * dh
    f = wg.shape[2]
    tr = MOE_ROW_TILE
    tf = MOE_COL_TILE
    assert tr % (MOE_ROW_SLABS * 2 * SUBLANES) == 0 and f % tf == 0
    n_tiles = block_e.shape[0]
    nj = f // tf
    rows_per_step = -(-tr // nj)
    gr = rows_per_step * nj
    stok = jnp.pad(sorted_tok, (0, gr))
    offs = jnp.pad(jnp.clip(tile_off, 0, sorted_tok.shape[0]), (0, 1))

    def tile(t, nu):
        return jnp.minimum(t, jnp.maximum(nu[0] - 1, 0))

    def col(t, j, nu):
        return jnp.where(t < nu[0], j, nj - 1)

    wspec = lambda shape, imap: pl.BlockSpec(
        shape, lambda t, j, be, nv, nu, off, st: imap(be[tile(t, nu)], col(t, j, nu)))
    return pl.pallas_call(
        functools.partial(_experts_kernel, rows_per_step),
        grid_spec=pltpu.PrefetchScalarGridSpec(
            num_scalar_prefetch=5,
            grid=(n_tiles + 1, nj),
            in_specs=[pl.BlockSpec(memory_space=pl.ANY),
                      wspec((None, d, tf), lambda e, c: (e, 0, c)),
                      wspec((None, d, tf), lambda e, c: (e, 0, c)),
                      wspec((None, tf, d), lambda e, c: (e, c, 0))],
            out_specs=pl.BlockSpec((tr, dh), lambda t, j, *_: (t, 0)),
            scratch_shapes=[pltpu.VMEM((gr, dh), jnp.uint32), pltpu.VMEM((tr, d), BF16),
                            pltpu.VMEM((tr, d), F32),
                            pltpu.SemaphoreType.DMA(())]),
        out_shape=jax.ShapeDtypeStruct(((n_tiles + 1) * tr, dh), jnp.uint32),
        compiler_params=_cparams("arbitrary", "arbitrary"),
        name="moe_experts",
    )(block_e, jnp.pad(n_valid, (0, 1)), n_used, offs, stok, hn_packed, wg, wu, wd)


def _combine_kernel(n_first, d0_ref, d1_ref, d0n_ref, d1n_ref, x_ref, meta_ref, g_ref, yb_hbm,
                    oa_ref, ob_ref, y0, y1, sem):
    tm = x_ref.shape[0]
    i = pl.program_id(0)
    slot = lax.rem(i, 2)

    def row_copies(da_ref, db_ref, s, r):
        return (_row_copy(yb_hbm, da_ref[0, 0, r], y0.at[s], r, sem.at[0, s]),
                _row_copy(yb_hbm, db_ref[0, 0, r], y1.at[s], r, sem.at[1, s]))

    def wait_tile(s):
        pltpu.make_async_copy(yb_hbm.at[pl.ds(0, tm)], y0.at[s], sem.at[0, s]).wait()
        pltpu.make_async_copy(yb_hbm.at[pl.ds(0, tm)], y1.at[s], sem.at[1, s]).wait()

    @pl.when(i == 0)
    def _():
        def issue(r, c):
            for cp in row_copies(d0_ref, d1_ref, 0, r):
                cp.start()
            return c

        lax.fori_loop(0, tm, issue, 0, unroll=8)

    wait_tile(slot)
    for r in range(tm):
        for cp in row_copies(d0n_ref, d1n_ref, 1 - slot, r):
            cp.start()
    meta = meta_ref[...]
    g0, g1 = meta[:, 2:3], meta[:, 3:4]
    lo0, hi0 = _unpack_bf16_pairs(y0[slot])
    lo1, hi1 = _unpack_bf16_pairs(y1[slot])
    moe = jnp.concatenate([lo0 * g0 + lo1 * g1, hi0 * g0 + hi1 * g1], axis=1)
    y = _rms_norm_f32(x_ref[...] + moe, g_ref[...])

    @pl.when(i < n_first)
    def _():
        oa_ref[...] = y

    @pl.when(i >= n_first)
    def _():
        ob_ref[...] = y

    @pl.when(i == pl.num_programs(0) - 1)
    def _():
        wait_tile(1 - slot)


def _combine(x, meta, g, yb, dest0, dest1, m_first):
    m, d = x.shape
    tm = COMBINE_ROW_TILE
    n_tiles = m // tm
    smem_blk = pl.BlockSpec((1, 1, tm), lambda i: (i, 0, 0), memory_space=pltpu.SMEM)
    smem_next = pl.BlockSpec((1, 1, tm), lambda i: (jnp.minimum(i + 1, n_tiles - 1), 0, 0),
                             memory_space=pltpu.SMEM)
    oa_spec, ob_spec = _two_part_specs(m_first // tm, (tm, d), lambda: 0)
    d0 = dest0.reshape(n_tiles, 1, tm)
    d1 = dest1.reshape(n_tiles, 1, tm)
    return pl.pallas_call(
        functools.partial(_combine_kernel, m_first // tm),
        grid=(n_tiles,),
        in_specs=[smem_blk, smem_blk, smem_next, smem_next,
                  pl.BlockSpec((tm, d), lambda i: (i, 0)),
                  pl.BlockSpec((tm, LANES), lambda i: (i, 0)),
                  pl.BlockSpec((1, d), lambda i: (0, 0)),
                  pl.BlockSpec(memory_space=pl.ANY)],
        out_specs=[oa_spec, ob_spec],
        out_shape=[jax.ShapeDtypeStruct((m_first, d), F32),
                   jax.ShapeDtypeStruct((m - m_first, d), F32)],
        scratch_shapes=[pltpu.VMEM((2, tm, d // 2), jnp.uint32),
                        pltpu.VMEM((2, tm, d // 2), jnp.uint32),
                        pltpu.SemaphoreType.DMA((2, 2))],
        compiler_params=_cparams("arbitrary"),
        name="moe_combine",
    )(d0, d1, d0, d1, x, meta, g, yb)


def _sample_history(dec_seq, *states):
    hist = jnp.concatenate(states, axis=1)
    b, r, c = hist.shape
    assert r <= dec_seq
    return jnp.pad(hist, ((0, 0), (0, dec_seq - r), (0, 0))).reshape(b * dec_seq, c)


def _seq_tails(a, col0, ncol, width, batch, seq, dec_batch, dec_seq):
    mp = batch * seq
    tails = [lax.slice(a, ((b + 1) * seq - width, col0), ((b + 1) * seq, col0 + ncol))
             for b in range(batch)]
    smp = lax.slice(a, (mp, col0), (mp + dec_batch * dec_seq, col0 + ncol))
    smp = smp.reshape(dec_batch, dec_seq, ncol)[:, dec_seq - width:]
    return jnp.stack(tails, axis=0), smp


def kernel(x_prompt, x_sample, state_lru_h, state_lru_conv, state_sconv, norm_mix, norm_ffn, norm_final, lru_w_in, lru_conv_w, lru_conv_b, lru_w_rgate, lru_b_rgate, lru_w_igate, lru_b_igate, lru_lambda, lru_w_out, sc_w_in, sc_conv_w, sc_w_out, ffn_w_gate, ffn_w_up, ffn_w_down, moe_w_router, moe_w_gate, moe_w_up, moe_w_down):
    batch, seq, d = x_prompt.shape
    dec_batch, dec_seq, _ = x_sample.shape
    mp, ms = batch * seq, dec_batch * dec_seq
    m = mp + ms
    depth = norm_mix.shape[0]
    n_experts = moe_w_router.shape[2]
    conv_a = lru_conv_w.shape[1]
    conv_b = sc_conv_w.shape[1]
    assert depth == 2 and lru_w_in.shape[0] == 1 and sc_w_in.shape[0] == 1
    assert seq & (seq - 1) == 0 and dec_seq & (dec_seq - 1) == 0
    assert seq % ROW_TILE == 0 and ms == ROW_TILE and ROW_TILE % dec_seq == 0
    assert dec_seq >= conv_a - 1 and dec_seq >= conv_b - 1 and dec_seq == SUBLANES
    assert n_experts <= LANES and m % ROUTER_ROW_TILE == 0 and m % COMBINE_ROW_TILE == 0
    n_prompt_tiles = mp // ROW_TILE

    xp = x_prompt.reshape(mp, d)
    xs_in = x_sample.reshape(ms, d)
    row = lambda v: v.reshape(1, -1)

    d_rnn = lru_w_out.shape[1]
    assert d_rnn % COL_TILE == 0
    proj = _norm_matmul(xp, xs_in, row(norm_mix[0]), lru_w_in[0], n_gelu_cols=d_rnn)
    hist = _sample_history(dec_seq, state_lru_conv[0], state_lru_h[0][:, None, :])
    y, h_last = _rglru_core(proj, hist, lru_conv_w[0], row(lru_conv_b[0]), row(lru_b_rgate[0]),
                           row(lru_b_igate[0]), row(lru_lambda[0]),
                           lru_w_rgate[0].astype(BF16), lru_w_igate[0].astype(BF16),
                           n_prompt_tiles, seq // ROW_TILE, dec_seq)
    x = _matmul_residual(y, lru_w_out[0].astype(BF16), xp, xs_in)

    x = _ffn(x, row(norm_ffn[0]), ffn_w_gate[0], ffn_w_up[0], ffn_w_down[0])

    hist = _sample_history(dec_seq, state_sconv[0])
    zb, cv_tile_tails, cv_smp = _sc_mixer(x, row(norm_mix[1]), sc_w_in[0], sc_conv_w[0], hist,
                                          n_prompt_tiles, seq // ROW_TILE, dec_seq)
    x = _matmul_residual(zb, sc_w_out[0].astype(BF16), x, x)

    w_router = jnp.pad(moe_w_router[0], ((0, 0), (0, LANES - n_experts)))
    hn, meta, counts = _router(x, row(norm_ffn[1]), w_router, n_experts)
    tr = MOE_ROW_TILE
    n_tiles = -(-(m * TOP_K) // tr) + n_experts
    counts = counts[0, :n_experts].astype(jnp.int32)
    k_tiles = (counts + tr - 1) // tr
    share = jnp.maximum((counts + k_tiles - 1) // jnp.maximum(k_tiles, 1), 1)
    padded = k_tiles * tr
    pad_end = jnp.cumsum(padded)
    pad_start = pad_end - padded
    e0 = meta[:, 0].astype(jnp.int32)
    e1 = meta[:, 1].astype(jnp.int32)

    def sorted_row(e, rank):
        tile_in_expert = rank // share[e]
        return pad_start[e] + tile_in_expert * tr + (rank - tile_in_expert * share[e])

    dest0 = sorted_row(e0, meta[:, 4].astype(jnp.int32))
    dest1 = sorted_row(e1, meta[:, 5].astype(jnp.int32))
    order = jnp.argsort(jnp.stack([e0, e1], axis=1).reshape(-1), stable=True)
    sorted_tok = (order // TOP_K).astype(jnp.int32)
    start = jnp.cumsum(counts) - counts
    block_start = jnp.arange(n_tiles, dtype=jnp.int32) * tr
    block_e = jnp.sum((pad_end[None, :] <= block_start[:, None]).astype(jnp.int32), axis=1)
    block_e = jnp.minimum(block_e, n_experts - 1)
    in_expert = (block_start - pad_start[block_e]) // tr * share[block_e]
    n_valid = jnp.clip(counts[block_e] - in_expert, 0, share[block_e]).astype(jnp.int32)
    n_used = (pad_end[-1:] // tr).astype(jnp.int32)
    yb = _experts(hn, sorted_tok, (start[block_e] + in_expert).astype(jnp.int32), block_e,
                  n_valid, n_used, moe_w_gate[0], moe_w_up[0], moe_w_down[0])
    y_prompt, y_sample = _combine(x, meta, row(norm_final), yb, dest0, dest1, mp)

    tails = functools.partial(_seq_tails, batch=batch, seq=seq, dec_batch=dec_batch, dec_seq=dec_seq)
    groups_per_seq = seq // SUBLANES
    h_p = h_last[groups_per_seq - 1:mp // SUBLANES:groups_per_seq]
    h_s = h_last[mp // SUBLANES:]
    u_p, u_s = tails(proj, d_rnn, d_rnn, conv_a - 1)
    tiles_per_seq = seq // ROW_TILE
    cv_p = cv_tile_tails.reshape(m // ROW_TILE, SUBLANES, -1)[
        tiles_per_seq - 1:n_prompt_tiles:tiles_per_seq, SUBLANES - (conv_b - 1):]
    cv_s = cv_smp.reshape(dec_batch, dec_seq, -1)[:, dec_seq - (conv_b - 1):]
    return (y_prompt.reshape(batch, seq, d), y_sample.reshape(dec_batch, dec_seq, d),
            h_p[None], u_p[None], cv_p[None], h_s[None], u_s[None], cv_s[None])
```

```python
import functools

import jax
import jax.numpy as jnp
from jax import lax
from jax.experimental import pallas as pl
from jax.experimental.pallas import tpu as pltpu

F32 = jnp.float32
BF16 = jnp.bfloat16

EPS = 1e-6
RG_C = 8.0
TOP_K = 2

LANES = 128
SUBLANES = 8
ROW_TILE = 1024
OUT_ROW_TILE = 512
ROUTER_ROW_TILE = 512
COL_TILE = 512
W_COL_TILE = 256
MOE_ROW_TILE = 800
MOE_ROW_SLABS = 5
MOE_COL_TILE = 512
COMBINE_ROW_TILE = 256
SCAN_CHUNK = 128
VMEM_LIMIT = 60 * 1024 * 1024


def _cparams(*sem):
    return pltpu.CompilerParams(dimension_semantics=sem, vmem_limit_bytes=VMEM_LIMIT)


def _rms_norm_f32(x, g):
    y = x * lax.rsqrt(jnp.mean(x * x, axis=-1, keepdims=True) + EPS)
    return y * g


BF16_HIGH_HALF = 0xFFFF0000
BF16_BITS = 16


def _pack_bf16_pairs(x_rounded):
    bits = pltpu.bitcast(x_rounded, jnp.uint32)
    half = x_rounded.shape[1] // 2
    return (bits[:, half:] & jnp.uint32(BF16_HIGH_HALF)) | (bits[:, :half] >> BF16_BITS)


def _unpack_bf16_pairs(w):
    return (pltpu.bitcast(w << BF16_BITS, F32),
            pltpu.bitcast(w & jnp.uint32(BF16_HIGH_HALF), F32))


def _two_part_specs(n_first, block, col_of, second_mode=None):
    first = pl.BlockSpec(block, lambda i, *a: (jnp.minimum(i, n_first - 1), col_of(*a)))
    second = pl.BlockSpec(block, lambda i, *a: (jnp.maximum(i - n_first, 0), col_of(*a)),
                          pipeline_mode=second_mode)
    return first, second


def _norm_matmul_kernel(n_first, n_gelu, xa_ref, xb_ref, g_ref, w_ref, o_ref, hn_ref):
    i = pl.program_id(0)
    j = pl.program_id(1)

    def project():
        return jnp.dot(hn_ref[...], w_ref[...].astype(BF16), preferred_element_type=F32)

    @pl.when((j == 0) & (i < n_first))
    def _():
        hn_ref[...] = _rms_norm_f32(xa_ref[...], g_ref[...]).astype(BF16)

    @pl.when((j == 0) & (i >= n_first))
    def _():
        hn_ref[...] = _rms_norm_f32(xb_ref[...], g_ref[...]).astype(BF16)

    @pl.when(j < n_gelu)
    def _():
        o_ref[...] = jax.nn.gelu(project())

    @pl.when(j >= n_gelu)
    def _():
        o_ref[...] = project()


def _norm_matmul(xa, xb, g, w, n_gelu_cols):
    d = xa.shape[1]
    m = xa.shape[0] + xb.shape[0]
    n = w.shape[1]
    assert xb.shape[0] == ROW_TILE
    xa_spec, xb_spec = _two_part_specs(xa.shape[0] // ROW_TILE, (ROW_TILE, d), lambda j: 0,
                                       second_mode=pl.Buffered(1))
    return pl.pallas_call(
        functools.partial(_norm_matmul_kernel, xa.shape[0] // ROW_TILE, n_gelu_cols // COL_TILE),
        grid=(m // ROW_TILE, n // COL_TILE),
        in_specs=[xa_spec, xb_spec,
                  pl.BlockSpec((1, d), lambda i, j: (0, 0)),
                  pl.BlockSpec((d, COL_TILE), lambda i, j: (0, j))],
        out_specs=pl.BlockSpec((ROW_TILE, COL_TILE), lambda i, j: (i, j)),
        out_shape=jax.ShapeDtypeStruct((m, n), F32),
        scratch_shapes=[pltpu.VMEM((ROW_TILE, d), BF16)],
        compiler_params=_cparams("arbitrary", "arbitrary"),
        name="norm_matmul",
    )(xa, xb, g, w)


def _sc_mixer_kernel(n_prompt_tiles, tiles_per_seq, seq_s,
                     x_ref, g_ref, wb_ref, wc_ref, wv_ref, cw_ref, hist_ref,
                     zb_ref, tail_ref, cvs_ref, hn_ref, carry_ref):
    i = pl.program_id(0)
    j = pl.program_id(1)
    t_rows = x_ref.shape[0]
    is_sample = i >= n_prompt_tiles
    starts_seq = is_sample | (lax.rem(i, tiles_per_seq) == 0)

    @pl.when(j == 0)
    def _():
        hn_ref[...] = _rms_norm_f32(x_ref[...], g_ref[...]).astype(BF16)

    @pl.when((i == 0) & (j == 0))
    def _():
        carry_ref[...] = jnp.zeros_like(carry_ref)

    def mixer(masked):
        hn = hn_ref[...]
        cw = cw_ref[...]
        c = jnp.dot(hn, wc_ref[...].astype(BF16), preferred_element_type=F32)
        v = jnp.dot(hn, wv_ref[...].astype(BF16), preferred_element_type=F32)
        cv = c * v
        prev = jnp.where(starts_seq, 0.0, carry_ref[j])
        ext = jnp.concatenate([prev, cv], axis=0)
        w1 = cw.shape[0] - 1
        if masked:
            t = _pos_in_seq(0, t_rows, seq_s)
            hist = hist_ref[...]

        def tap(s):
            shifted = pltpu.roll(ext, s, axis=0)[SUBLANES:]
            if not masked:
                return shifted
            return jnp.where(t >= s, shifted, _history_rows(hist, w1 - s))

        z = tap(2) * cw[0:1] + tap(1) * cw[1:2]
        z = z + cv * cw[2:3]
        b = jnp.dot(hn, wb_ref[...].astype(BF16), preferred_element_type=F32)
        zb_ref[...] = (b * z).astype(BF16)
        last = cv[t_rows - SUBLANES:, :]
        carry_ref[j] = last
        tail_ref[...] = last
        return cv

    @pl.when(is_sample)
    def _():
        cvs_ref[...] = mixer(True)

    @pl.when(jnp.logical_not(is_sample))
    def _():
        mixer(False)


def _sc_mixer(x, g, w, conv_w, hist, n_prompt_tiles, tiles_per_seq, seq_s):
    m, d = x.shape
    dc = w.shape[1] // 3
    tn = W_COL_TILE
    nj = dc // tn
    n_tiles = m // ROW_TILE
    ms = hist.shape[0]
    assert ms == ROW_TILE
    wspec = lambda off: pl.BlockSpec((d, tn), lambda i, j: (0, off * nj + j))
    sample_col = lambda i, j: (0, jnp.where(i >= n_prompt_tiles, j, 0))
    return pl.pallas_call(
        functools.partial(_sc_mixer_kernel, n_prompt_tiles, tiles_per_seq, seq_s),
        grid=(n_tiles, nj),
        in_specs=[pl.BlockSpec((ROW_TILE, d), lambda i, j: (i, 0)),
                  pl.BlockSpec((1, d), lambda i, j: (0, 0)),
                  wspec(0), wspec(1), wspec(2),
                  pl.BlockSpec((conv_w.shape[0], tn), lambda i, j: (0, j)),
                  pl.BlockSpec((ROW_TILE, tn), sample_col)],
        out_specs=[pl.BlockSpec((ROW_TILE, tn), lambda i, j: (i, j)),
                   pl.BlockSpec((SUBLANES, tn), lambda i, j: (i, j)),
                   pl.BlockSpec((ROW_TILE, tn), sample_col)],
        out_shape=[jax.ShapeDtypeStruct((m, dc), BF16),
                   jax.ShapeDtypeStruct((n_tiles * SUBLANES, dc), F32),
                   jax.ShapeDtypeStruct((ms, dc), F32)],
        scratch_shapes=[pltpu.VMEM((ROW_TILE, d), BF16),
                        pltpu.VMEM((nj, SUBLANES, tn), F32)],
        compiler_params=_cparams("arbitrary", "arbitrary"),
        name="sc_mixer",
    )(x, g, w, w, w, conv_w, hist)


def _matmul_residual_kernel(n_first, y_ref, w_ref, ra_ref, rb_ref, o_ref):
    o_ref[...] = jnp.dot(y_ref[...], w_ref[...], preferred_element_type=F32)
    i = pl.program_id(0)

    @pl.when(i < n_first)
    def _():
        o_ref[...] += ra_ref[...]

    @pl.when(i >= n_first)
    def _():
        o_ref[...] += rb_ref[...]


def _matmul_residual(y, w, res_a, res_b):
    m, k = y.shape
    n = w.shape[1]
    tm = OUT_ROW_TILE
    ra_spec, rb_spec = _two_part_specs(res_a.shape[0] // tm, (tm, n), lambda: 0)
    return pl.pallas_call(
        functools.partial(_matmul_residual_kernel, res_a.shape[0] // tm),
        grid=(m // tm,),
        in_specs=[pl.BlockSpec((tm, k), lambda i: (i, 0)),
                  pl.BlockSpec((k, n), lambda i: (0, 0)),
                  ra_spec, rb_spec],
        out_specs=pl.BlockSpec((tm, n), lambda i: (i, 0)),
        out_shape=jax.ShapeDtypeStruct((m, n), F32),
        compiler_params=_cparams("arbitrary"),
        name="matmul_residual",
    )(y, w, res_a, res_b)


def _pos_in_seq(r0, rows, seq_len):
    return (r0 + lax.broadcasted_iota(jnp.int32, (rows, 1), 0)) & (seq_len - 1)


def _run_chunks(is_sample, n_chunks, body):
    @pl.when(is_sample)
    def _():
        lax.fori_loop(0, n_chunks, lambda k, c: body(True, k) or c, 0)

    @pl.when(jnp.logical_not(is_sample))
    def _():
        lax.fori_loop(0, n_chunks, lambda k, c: body(False, k) or c, 0)


def _history_rows(hist_chunk, back):
    n = hist_chunk.shape[0]
    return hist_chunk if back == 0 else pltpu.roll(hist_chunk, n - back, axis=0)


def _rglru_kernel(n_prompt_tiles, tiles_per_seq, seq_s,
                  gate_ref, u_ref, hist_ref,
                  cw_ref, cb_ref, br_ref, bi_ref, lam_ref, wr_ref, wi_ref,
                  y_ref, hl_ref, ubuf, uc_ref, rp_ref, ip_ref, hc_ref):
    i = pl.program_id(1)
    cblk = u_ref.shape[1]
    is_sample = i >= n_prompt_tiles
    w1 = cw_ref.shape[0] - 1

    @pl.when(is_sample | (lax.rem(i, tiles_per_seq) == 0))
    def _():
        ubuf[0:SUBLANES, :] = jnp.zeros((SUBLANES, cblk), F32)
        hc_ref[...] = jnp.zeros_like(hc_ref)

    ubuf[SUBLANES:, :] = u_ref[...]
    cw = cw_ref[...]
    cb = cb_ref[...]
    rc = SCAN_CHUNK

    def conv_chunk(masked, k):
        r0 = pl.multiple_of(k * rc, rc)
        rows = pl.ds(r0, rc)
        ext = ubuf[pl.ds(r0, rc + SUBLANES), :]
        if masked:
            t = _pos_in_seq(r0, rc, seq_s)
            hist = hist_ref[rows, :]

        def tap(s):
            shifted = pltpu.roll(ext, s, axis=0)[SUBLANES:]
            if not masked:
                return shifted
            return jnp.where(t >= s, shifted, _history_rows(hist, w1 - s))

        conv = tap(3) * cw[0:1] + tap(2) * cw[1:2]
        conv = conv + tap(1) * cw[2:3]
        conv = conv + ext[SUBLANES:] * cw[3:4]
        uc_ref[rows, :] = conv + cb

    _run_chunks(is_sample, ROW_TILE // rc, conv_chunk)

    ub = uc_ref[...].astype(BF16)
    rp_ref[...] = 0.5 * (jnp.dot(ub, wr_ref[...], preferred_element_type=F32) + br_ref[...])
    ip_ref[...] = 0.5 * (jnp.dot(ub, wi_ref[...], preferred_element_type=F32) + bi_ref[...])

    neg_half_c_sp = (-0.5 * RG_C) * jax.nn.softplus(-lam_ref[...])
    sub = lax.broadcasted_iota(jnp.int32, (1, SUBLANES, 1), 1)

    def scan_chunk(masked, k):
        r0 = pl.multiple_of(k * rc, rc)
        rows = pl.ds(r0, rc)
        uc = uc_ref[rows, :]
        ig = 0.5 * jnp.tanh(ip_ref[rows, :]) + 0.5
        log_a = neg_half_c_sp * (jnp.tanh(rp_ref[rows, :]) + 1.0)
        a = jnp.exp(log_a)
        mult = jnp.sqrt(-jnp.tanh(log_a) * (a * a + 1.0))
        b = mult * ig * uc
        if masked:
            first = _pos_in_seq(r0, rc, seq_s) == 0
            b = jnp.where(first, b + a * _history_rows(hist_ref[rows, :], w1), b)
            a = jnp.where(first, 0.0, a)

        a3 = a.reshape(rc // SUBLANES, SUBLANES, cblk)
        b3 = b.reshape(rc // SUBLANES, SUBLANES, cblk)
        for s in (1, 2, 4):
            a_s = pltpu.roll(a3, s, axis=1)
            b_s = pltpu.roll(b3, s, axis=1)
            m = sub >= s
            b3 = jnp.where(m, a3 * b_s + b3, b3)
            a3 = jnp.where(m, a3 * a_s, a3)

        hc = hc_ref[...]
        hs, lasts = [], []
        for g in range(rc // SUBLANES):
            hg = a3[g] * hc + b3[g]
            lasts.append(hg[SUBLANES - 1:SUBLANES])
            hc = jnp.broadcast_to(lasts[-1], (SUBLANES, cblk))
            hs.append(hg)
        hc_ref[...] = hc
        h = jnp.concatenate(hs, axis=0)
        groups = rc // SUBLANES
        hl_ref[pl.ds(pl.multiple_of(k * groups, groups), groups), :] = jnp.concatenate(lasts, axis=0)
        y_ref[rows, :] = (h * gate_ref[rows, :]).astype(BF16)

    _run_chunks(is_sample, ROW_TILE // rc, scan_chunk)
    ubuf[0:SUBLANES, :] = ubuf[ROW_TILE:ROW_TILE + SUBLANES, :]


def _rglru_core(proj, hist, conv_w, conv_b, b_r, b_i, lam, w_r, w_i, n_prompt_tiles,
                tiles_per_seq, seq_s):
    m = proj.shape[0]
    assert SCAN_CHUNK % (SUBLANES * SUBLANES) == 0
    nb, cblk = w_r.shape[0], w_r.shape[1]
    c = nb * cblk
    n_tiles = m // ROW_TILE
    row_blk = lambda off: pl.BlockSpec((ROW_TILE, cblk), lambda cb, i: (i, off + cb))
    aux_blk = pl.BlockSpec((ROW_TILE, cblk), lambda cb, i: (0, cb))
    vec_blk = lambda rows: pl.BlockSpec((rows, cblk), lambda cb, i: (0, cb))
    w_blk = pl.BlockSpec((None, cblk, cblk), lambda cb, i: (cb, 0, 0))
    out_blk = pl.BlockSpec((ROW_TILE, cblk), lambda cb, i: (i, cb))
    return pl.pallas_call(
        functools.partial(_rglru_kernel, n_prompt_tiles, tiles_per_seq, seq_s),
        grid=(nb, n_tiles),
        in_specs=[row_blk(0), row_blk(nb), aux_blk,
                  vec_blk(conv_w.shape[0]), vec_blk(1), vec_blk(1), vec_blk(1), vec_blk(1),
                  w_blk, w_blk],
        out_specs=[out_blk,
                   pl.BlockSpec((ROW_TILE // SUBLANES, cblk), lambda cb, i: (i, cb))],
        out_shape=[jax.ShapeDtypeStruct((m, c), BF16),
                   jax.ShapeDtypeStruct((m // SUBLANES, c), F32)],
        scratch_shapes=[pltpu.VMEM((ROW_TILE + SUBLANES, cblk), F32),
                        pltpu.VMEM((ROW_TILE, cblk), F32),
                        pltpu.VMEM((ROW_TILE, cblk), F32),
                        pltpu.VMEM((ROW_TILE, cblk), F32),
                        pltpu.VMEM((SUBLANES, cblk), F32)],
        compiler_params=_cparams("arbitrary", "arbitrary"),
        name="rglru_core",
    )(proj, proj, hist, conv_w, conv_b, b_r, b_i, lam, w_r, w_i)


def _ffn_kernel(x_ref, g_ref, wg_ref, wu_ref, wd_ref, o_ref, hn_ref):
    @pl.when(pl.program_id(1) == 0)
    def _():
        x = x_ref[...]
        hn_ref[...] = _rms_norm_f32(x, g_ref[...]).astype(BF16)
        o_ref[...] = x

    hn = hn_ref[...]
    a = jnp.dot(hn, wg_ref[...].astype(BF16), preferred_element_type=F32)
    b = jnp.dot(hn, wu_ref[...].astype(BF16), preferred_element_type=F32)
    h = (jax.nn.silu(a) * b).astype(BF16)
    o_ref[...] += jnp.dot(h, wd_ref[...].astype(BF16), preferred_element_type=F32)


def _ffn(x, g, wg, wu, wd):
    m, d = x.shape
    f = wg.shape[1]
    tm = ROW_TILE
    tf = W_COL_TILE
    return pl.pallas_call(
        _ffn_kernel,
        grid=(m // tm, f // tf),
        in_specs=[pl.BlockSpec((tm, d), lambda i, j: (i, 0)),
                  pl.BlockSpec((1, d), lambda i, j: (0, 0)),
                  pl.BlockSpec((d, tf), lambda i, j: (0, j)),
                  pl.BlockSpec((d, tf), lambda i, j: (0, j)),
                  pl.BlockSpec((tf, d), lambda i, j: (j, 0))],
        out_specs=pl.BlockSpec((tm, d), lambda i, j: (i, 0)),
        out_shape=jax.ShapeDtypeStruct((m, d), F32),
        scratch_shapes=[pltpu.VMEM((tm, d), BF16)],
        compiler_params=_cparams("arbitrary", "arbitrary"),
        name="ffn",
    )(x, g, wg, wu, wd)


def _router_kernel(n_experts, x_ref, g_ref, whi_ref, wlo_ref, hn_ref, meta_ref, cnt_ref, run_ref):
    @pl.when(pl.program_id(0) == 0)
    def _():
        run_ref[...] = jnp.zeros_like(run_ref)

    tm = x_ref.shape[0]
    hn = _rms_norm_f32(x_ref[...], g_ref[...])
    hn_hi = hn.astype(BF16)
    hn_hi32 = hn_hi.astype(F32)
    hn_ref[...] = _pack_bf16_pairs(hn_hi32)
    hn_lo = (hn - hn_hi32).astype(BF16)
    logits = (jnp.dot(hn_hi, whi_ref[...], preferred_element_type=F32)
              + (jnp.dot(hn_lo, whi_ref[...], preferred_element_type=F32)
                 + jnp.dot(hn_hi, wlo_ref[...], preferred_element_type=F32)))
    lane = lax.broadcasted_iota(jnp.int32, (tm, LANES), 1).astype(F32)
    neg = jnp.float32(-jnp.inf)
    lg = jnp.where(lane < n_experts, logits, neg)
    m1 = jnp.max(lg, axis=1, keepdims=True)
    i1 = jnp.min(jnp.where(lg == m1, lane, float(LANES)), axis=1, keepdims=True)
    lg2 = jnp.where(lane == i1, neg, lg)
    m2 = jnp.max(lg2, axis=1, keepdims=True)
    i2 = jnp.min(jnp.where(lg2 == m2, lane, float(LANES)), axis=1, keepdims=True)
    ex = jnp.exp(m2 - m1)
    g1 = 1.0 / (1.0 + ex)
    g2 = ex / (1.0 + ex)

    sel1 = lane == i1
    sel2 = lane == i2
    onehot = jnp.where(sel1 | sel2, 1.0, 0.0)
    rr = lax.broadcasted_iota(jnp.int32, (tm, tm), 0)
    cc = lax.broadcasted_iota(jnp.int32, (tm, tm), 1)
    tri = jnp.where(rr > cc, 1.0, 0.0).astype(BF16)
    before = jnp.dot(tri, onehot.astype(BF16), preferred_element_type=F32) + run_ref[0:1, :]
    rank1 = jnp.sum(jnp.where(sel1, before, 0.0), axis=1, keepdims=True)
    rank2 = jnp.sum(jnp.where(sel2, before, 0.0), axis=1, keepdims=True)
    run = run_ref[0:1, :] + jnp.sum(onehot, axis=0, keepdims=True)
    run_ref[...] = jnp.broadcast_to(run, run_ref.shape)
    cnt_ref[...] = jnp.broadcast_to(run, cnt_ref.shape)

    meta = jnp.where(lane == 0, i1, 0.0)
    meta = jnp.where(lane == 1, i2, meta)
    meta = jnp.where(lane == 2, g1, meta)
    meta = jnp.where(lane == 3, g2, meta)
    meta = jnp.where(lane == 4, rank1, meta)
    meta = jnp.where(lane == 5, rank2, meta)
    meta_ref[...] = meta


def _router(x, g, w_router_padded, n_experts):
    m, d = x.shape
    tm = ROUTER_ROW_TILE
    w_hi = w_router_padded.astype(BF16)
    w_lo = (w_router_padded - w_hi.astype(F32)).astype(BF16)
    return pl.pallas_call(
        functools.partial(_router_kernel, n_experts),
        grid=(m // tm,),
        in_specs=[pl.BlockSpec((tm, d), lambda i: (i, 0)),
                  pl.BlockSpec((1, d), lambda i: (0, 0)),
                  pl.BlockSpec((d, LANES), lambda i: (0, 0)),
                  pl.BlockSpec((d, LANES), lambda i: (0, 0))],
        out_specs=[pl.BlockSpec((tm, d // 2), lambda i: (i, 0)),
                   pl.BlockSpec((tm, LANES), lambda i: (i, 0)),
                   pl.BlockSpec((SUBLANES, LANES), lambda i: (0, 0))],
        out_shape=[jax.ShapeDtypeStruct((m, d // 2), jnp.uint32),
                   jax.ShapeDtypeStruct((m, LANES), F32),
                   jax.ShapeDtypeStruct((SUBLANES, LANES), F32)],
        scratch_shapes=[pltpu.VMEM((SUBLANES, LANES), F32)],
        compiler_params=_cparams("arbitrary"),
        name="router",
    )(x, g, w_hi, w_lo)


def _row_copy(src_hbm, idx, dst_ref, r, sem):
    return pltpu.make_async_copy(src_hbm.at[pl.ds(idx, 1)], dst_ref.at[pl.ds(r, 1)], sem)


def _experts_kernel(rows_per_step, be_ref, nv_ref, nused_ref, off_ref, stok_ref,
                    hn_hbm, wg_ref, wu_ref, wd_ref, o_ref,
                    xbuf, xb_ref, acc_ref, sem):
    t = pl.program_id(0)
    j = pl.program_id(1)
    nused = nused_ref[0]
    tr = xb_ref.shape[0]
    gr = xbuf.shape[0]
    dh = xbuf.shape[1]

    @pl.when(j == 0)
    def _():
        acc_ref[...] = jnp.zeros_like(acc_ref)

    @pl.when((t == 0) & (j == 0))
    def _():
        off0 = off_ref[0]

        def issue(r, c):
            _row_copy(hn_hbm, stok_ref[off0 + r], xbuf, r, sem).start()
            return c

        lax.fori_loop(0, gr, issue, 0, unroll=8)

    @pl.when((j == 0) & (t <= nused))
    def _():
        pltpu.make_async_copy(hn_hbm.at[pl.ds(0, gr)], xbuf, sem).wait()

    @pl.when((j == 0) & (t < nused))
    def _():
        lo, hi = _unpack_bf16_pairs(xbuf[0:tr, :])
        xb_ref[:, 0:dh] = lo.astype(BF16)
        xb_ref[:, dh:] = hi.astype(BF16)

    def step(n_rows):
        base = off_ref[t + 1] + j * rows_per_step
        for k in range(rows_per_step):
            _row_copy(hn_hbm, stok_ref[base + k], xbuf, j * rows_per_step + k, sem).start()
        rows = pl.ds(0, n_rows)
        x = xb_ref[rows, :]
        a = jnp.dot(x, wg_ref[...].astype(BF16), preferred_element_type=F32)
        b = jnp.dot(x, wu_ref[...].astype(BF16), preferred_element_type=F32)
        h = (jax.nn.silu(a) * b).astype(BF16)
        acc_ref[rows, :] += jnp.dot(h, wd_ref[...].astype(BF16), preferred_element_type=F32)

    slab = tr // MOE_ROW_SLABS
    n_slabs = (nv_ref[t] + (slab - 1)) // slab
    for q in range(1, MOE_ROW_SLABS + 1):
        @pl.when((t < nused) & (jnp.maximum(n_slabs, 1) == q))
        def _(q=q):
            step(q * slab)

    @pl.when(j == pl.num_programs(1) - 1)
    def _():
        o_ref[...] = _pack_bf16_pairs(acc_ref[...].astype(BF16).astype(F32))


def _experts(hn_packed, sorted_tok, tile_off, block_e, n_valid, n_used, wg, wu, wd):
    dh = hn_packed.shape[1]
    d = 2 * dh
    f = wg.shape[2]
    tr = MOE_ROW_TILE
    tf = MOE_COL_TILE
    assert tr % (MOE_ROW_SLABS * 2 * SUBLANES) == 0 and f % tf == 0
    n_tiles = block_e.shape[0]
    nj = f // tf
    rows_per_step = -(-tr // (nj * SUBLANES)) * SUBLANES
    gr = rows_per_step * nj
    stok = jnp.pad(sorted_tok, (0, gr))
    offs = jnp.pad(jnp.clip(tile_off, 0, sorted_tok.shape[0]), (0, 1))

    def tile(t, nu):
        return jnp.minimum(t, jnp.maximum(nu[0] - 1, 0))

    def col(t, j, nu):
        return jnp.where(t < nu[0], j, nj - 1)

    wspec = lambda shape, imap: pl.BlockSpec(
        shape, lambda t, j, be, nv, nu, off, st: imap(be[tile(t, nu)], col(t, j, nu)))
    return pl.pallas_call(
        functools.partial(_experts_kernel, rows_per_step),
        grid_spec=pltpu.PrefetchScalarGridSpec(
            num_scalar_prefetch=5,
            grid=(n_tiles + 1, nj),
            in_specs=[pl.BlockSpec(memory_space=pl.ANY),
                      wspec((None, d, tf), lambda e, c: (e, 0, c)),
                      wspec((None, d, tf), lambda e, c: (e, 0, c)),
                      wspec((None, tf, d), lambda e, c: (e, c, 0))],
            out_specs=pl.BlockSpec((tr, dh), lambda t, j, *_: (t, 0)),
            scratch_shapes=[pltpu.VMEM((gr, dh), jnp.uint32), pltpu.VMEM((tr, d), BF16),
                            pltpu.VMEM((tr, d), F32),
                            pltpu.SemaphoreType.DMA(())]),
        out_shape=jax.ShapeDtypeStruct(((n_tiles + 1) * tr, dh), jnp.uint32),
        compiler_params=_cparams("arbitrary", "arbitrary"),
        name="moe_experts",
    )(block_e, jnp.pad(n_valid, (0, 1)), n_used, offs, stok, hn_packed, wg, wu, wd)


def _combine_kernel(n_first, d0_ref, d1_ref, d0n_ref, d1n_ref, x_ref, meta_ref, g_ref, yb_hbm,
                    oa_ref, ob_ref, y0, y1, sem):
    tm = x_ref.shape[0]
    i = pl.program_id(0)
    slot = lax.rem(i, 2)

    def row_copies(da_ref, db_ref, s, r):
        return (_row_copy(yb_hbm, da_ref[0, 0, r], y0.at[s], r, sem.at[0, s]),
                _row_copy(yb_hbm, db_ref[0, 0, r], y1.at[s], r, sem.at[1, s]))

    def wait_tile(s):
        pltpu.make_async_copy(yb_hbm.at[pl.ds(0, tm)], y0.at[s], sem.at[0, s]).wait()
        pltpu.make_async_copy(yb_hbm.at[pl.ds(0, tm)], y1.at[s], sem.at[1, s]).wait()

    @pl.when(i == 0)
    def _():
        def issue(r, c):
            for cp in row_copies(d0_ref, d1_ref, 0, r):
                cp.start()
            return c

        lax.fori_loop(0, tm, issue, 0, unroll=8)

    wait_tile(slot)
    for r in range(tm):
        for cp in row_copies(d0n_ref, d1n_ref, 1 - slot, r):
            cp.start()
    meta = meta_ref[...]
    g0, g1 = meta[:, 2:3], meta[:, 3:4]
    lo0, hi0 = _unpack_bf16_pairs(y0[slot])
    lo1, hi1 = _unpack_bf16_pairs(y1[slot])
    moe = jnp.concatenate([lo0 * g0 + lo1 * g1, hi0 * g0 + hi1 * g1], axis=1)
    y = _rms_norm_f32(x_ref[...] + moe, g_ref[...])

    @pl.when(i < n_first)
    def _():
        oa_ref[...] = y

    @pl.when(i >= n_first)
    def _():
        ob_ref[...] = y

    @pl.when(i == pl.num_programs(0) - 1)
    def _():
        wait_tile(1 - slot)


def _combine(x, meta, g, yb, dest0, dest1, m_first):
    m, d = x.shape
    tm = COMBINE_ROW_TILE
    n_tiles = m // tm
    smem_blk = pl.BlockSpec((1, 1, tm), lambda i: (i, 0, 0), memory_space=pltpu.SMEM)
    smem_next = pl.BlockSpec((1, 1, tm), lambda i: (jnp.minimum(i + 1, n_tiles - 1), 0, 0),
                             memory_space=pltpu.SMEM)
    oa_spec, ob_spec = _two_part_specs(m_first // tm, (tm, d), lambda: 0)
    d0 = dest0.reshape(n_tiles, 1, tm)
    d1 = dest1.reshape(n_tiles, 1, tm)
    return pl.pallas_call(
        functools.partial(_combine_kernel, m_first // tm),
        grid=(n_tiles,),
        in_specs=[smem_blk, smem_blk, smem_next, smem_next,
                  pl.BlockSpec((tm, d), lambda i: (i, 0)),
                  pl.BlockSpec((tm, LANES), lambda i: (i, 0)),
                  pl.BlockSpec((1, d), lambda i: (0, 0)),
                  pl.BlockSpec(memory_space=pl.ANY)],
        out_specs=[oa_spec, ob_spec],
        out_shape=[jax.ShapeDtypeStruct((m_first, d), F32),
                   jax.ShapeDtypeStruct((m - m_first, d), F32)],
        scratch_shapes=[pltpu.VMEM((2, tm, d // 2), jnp.uint32),
                        pltpu.VMEM((2, tm, d // 2), jnp.uint32),
                        pltpu.SemaphoreType.DMA((2, 2))],
        compiler_params=_cparams("arbitrary"),
        name="moe_combine",
    )(d0, d1, d0, d1, x, meta, g, yb)


def _sample_history(dec_seq, *states):
    hist = jnp.concatenate(states, axis=1)
    b, r, c = hist.shape
    assert r <= dec_seq
    return jnp.pad(hist, ((0, 0), (0, dec_seq - r), (0, 0))).reshape(b * dec_seq, c)


def _seq_tails(a, col0, ncol, width, batch, seq, dec_batch, dec_seq):
    mp = batch * seq
    tails = [lax.slice(a, ((b + 1) * seq - width, col0), ((b + 1) * seq, col0 + ncol))
             for b in range(batch)]
    smp = lax.slice(a, (mp, col0), (mp + dec_batch * dec_seq, col0 + ncol))
    smp = smp.reshape(dec_batch, dec_seq, ncol)[:, dec_seq - width:]
    return jnp.stack(tails, axis=0), smp


def kernel(x_prompt, x_sample, state_lru_h, state_lru_conv, state_sconv, norm_mix, norm_ffn, norm_final, lru_w_in, lru_conv_w, lru_conv_b, lru_w_rgate, lru_b_rgate, lru_w_igate, lru_b_igate, lru_lambda, lru_w_out, sc_w_in, sc_conv_w, sc_w_out, ffn_w_gate, ffn_w_up, ffn_w_down, moe_w_router, moe_w_gate, moe_w_up, moe_w_down):
    batch, seq, d = x_prompt.shape
    dec_batch, dec_seq, _ = x_sample.shape
    mp, ms = batch * seq, dec_batch * dec_seq
    m = mp + ms
    depth = norm_mix.shape[0]
    n_experts = moe_w_router.shape[2]
    conv_a = lru_conv_w.shape[1]
    conv_b = sc_conv_w.shape[1]
    assert depth == 2 and lru_w_in.shape[0] == 1 and sc_w_in.shape[0] == 1
    assert seq & (seq - 1) == 0 and dec_seq & (dec_seq - 1) == 0
    assert seq % ROW_TILE == 0 and ms == ROW_TILE and ROW_TILE % dec_seq == 0
    assert dec_seq >= conv_a - 1 and dec_seq >= conv_b - 1 and dec_seq == SUBLANES
    assert n_experts <= LANES and m % ROUTER_ROW_TILE == 0 and m % COMBINE_ROW_TILE == 0
    n_prompt_tiles = mp // ROW_TILE

    xp = x_prompt.reshape(mp, d)
    xs_in = x_sample.reshape(ms, d)
    row = lambda v: v.reshape(1, -1)

    d_rnn = lru_w_out.shape[1]
    assert d_rnn % COL_TILE == 0
    proj = _norm_matmul(xp, xs_in, row(norm_mix[0]), lru_w_in[0], n_gelu_cols=d_rnn)
    hist = _sample_history(dec_seq, state_lru_conv[0], state_lru_h[0][:, None, :])
    y, h_last = _rglru_core(proj, hist, lru_conv_w[0], row(lru_conv_b[0]), row(lru_b_rgate[0]),
                           row(lru_b_igate[0]), row(lru_lambda[0]),
                           lru_w_rgate[0].astype(BF16), lru_w_igate[0].astype(BF16),
                           n_prompt_tiles, seq // ROW_TILE, dec_seq)
    x = _matmul_residual(y, lru_w_out[0].astype(BF16), xp, xs_in)

    x = _ffn(x, row(norm_ffn[0]), ffn_w_gate[0], ffn_w_up[0], ffn_w_down[0])

    hist = _sample_history(dec_seq, state_sconv[0])
    zb, cv_tile_tails, cv_smp = _sc_mixer(x, row(norm_mix[1]), sc_w_in[0], sc_conv_w[0], hist,
                                          n_prompt_tiles, seq // ROW_TILE, dec_seq)
    x = _matmul_residual(zb, sc_w_out[0].astype(BF16), x, x)

    w_router = jnp.pad(moe_w_router[0], ((0, 0), (0, LANES - n_experts)))
    hn, meta, counts = _router(x, row(norm_ffn[1]), w_router, n_experts)
    tr = MOE_ROW_TILE
    n_tiles = -(-(m * TOP_K) // tr) + n_experts
    counts = counts[0, :n_experts].astype(jnp.int32)
    k_tiles = (counts + tr - 1) // tr
    share = jnp.maximum((counts + k_tiles - 1) // jnp.maximum(k_tiles, 1), 1)
    padded = k_tiles * tr
    pad_end = jnp.cumsum(padded)
    pad_start = pad_end - padded
    e0 = meta[:, 0].astype(jnp.int32)
    e1 = meta[:, 1].astype(jnp.int32)

    def sorted_row(e, rank):
        tile_in_expert = rank // share[e]
        return pad_start[e] + tile_in_expert * tr + (rank - tile_in_expert * share[e])

    dest0 = sorted_row(e0, meta[:, 4].astype(jnp.int32))
    dest1 = sorted_row(e1, meta[:, 5].astype(jnp.int32))
    order = jnp.argsort(jnp.stack([e0, e1], axis=1).reshape(-1), stable=True)
    sorted_tok = (order // TOP_K).astype(jnp.int32)
    start = jnp.cumsum(counts) - counts
    block_start = jnp.arange(n_tiles, dtype=jnp.int32) * tr
    block_e = jnp.sum((pad_end[None, :] <= block_start[:, None]).astype(jnp.int32), axis=1)
    block_e = jnp.minimum(block_e, n_experts - 1)
    in_expert = (block_start - pad_start[block_e]) // tr * share[block_e]
    n_valid = jnp.clip(counts[block_e] - in_expert, 0, share[block_e]).astype(jnp.int32)
    n_used = (pad_end[-1:] // tr).astype(jnp.int32)
    yb = _experts(hn, sorted_tok, (start[block_e] + in_expert).astype(jnp.int32), block_e,
                  n_valid, n_used, moe_w_gate[0], moe_w_up[0], moe_w_down[0])
    y_prompt, y_sample = _combine(x, meta, row(norm_final), yb, dest0, dest1, mp)

    tails = functools.partial(_seq_tails, batch=batch, seq=seq, dec_batch=dec_batch, dec_seq=dec_seq)
    groups_per_seq = seq // SUBLANES
    h_p = h_last[groups_per_seq - 1:mp // SUBLANES:groups_per_seq]
    h_s = h_last[mp // SUBLANES:]
    u_p, u_s = tails(proj, d_rnn, d_rnn, conv_a - 1)
    tiles_per_seq = seq // ROW_TILE
    cv_p = cv_tile_tails.reshape(m // ROW_TILE, SUBLANES, -1)[
        tiles_per_seq - 1:n_prompt_tiles:tiles_per_seq, SUBLANES - (conv_b - 1):]
    cv_s = cv_smp.reshape(dec_batch, dec_seq, -1)[:, dec_seq - (conv_b - 1):]
    return (y_prompt.reshape(batch, seq, d), y_sample.reshape(dec_batch, dec_seq, d),
            h_p[None], u_p[None], cv_p[None], h_s[None], u_s[None], cv_s[None])
```
